```python
import math
import jax, jax.numpy as jnp
from jax import lax
import numpy as np

D_MODEL = 2048
BATCH = 8
SEQ = 8192
DEPTH = 4

C_A = 512
H_B = 16
HEAD_DIM = 64
C_B = H_B * HEAD_DIM
C_C = 512
POOL_WINDOWS = (2, 4, 8, 16)
N_POOL_GROUPS = len(POOL_WINDOWS)
C_G = C_C // N_POOL_GROUPS
MIX_WIDTH = C_A + C_B + C_C
IN_WIDTH = 2 * C_A + 3 * C_B + C_C

CONV_WIDTH = 31
CONV_HALF = CONV_WIDTH // 2

DILATED_PATTERNS = ((128, 1), (512, 4), (2048, 16))
ATTN_BLOCK = 64
ROT_DIM = HEAD_DIM // 4
ROPE_THETA = 500000.0

FFN_HIDDEN = int(math.ceil((8 * D_MODEL / 3) / 256) * 256)
EPS = 1e-6
NEG = -1e30

kernel_name = "hybrid_conv_dilatedattn_pool_encoder"


def rms_normalize(t):
    tf = t.astype(jnp.float32)
    return tf * lax.rsqrt(jnp.mean(tf * tf, axis=-1, keepdims=True) + EPS)


def rmsnorm(t, g):
    return (rms_normalize(t) * g.astype(jnp.float32)).astype(t.dtype)


def rope_tables(S):
    pos = jnp.arange(S, dtype=jnp.float32)
    inv = ROPE_THETA ** (-jnp.arange(0, ROT_DIM, 2, dtype=jnp.float32) / ROT_DIM)
    ang = pos[:, None] * inv[None, :]
    return jnp.cos(ang), jnp.sin(ang)


def apply_partial_rope(t, cos, sin):
    half = ROT_DIM // 2
    t1 = t[..., :half]
    t2 = t[..., half:ROT_DIM]
    c = cos[None, :, None, :]
    s = sin[None, :, None, :]
    return jnp.concatenate([t1 * c - t2 * s, t2 * c + t1 * s, t[..., ROT_DIM:]], axis=-1)


def conformer_conv(u, w, b, ln_g, ln_b):
    a, gate = jnp.split(u, 2, axis=-1)
    h = a * jax.nn.sigmoid(gate)
    h = lax.conv_general_dilated(
        h, w[:, None, :], window_strides=(1,),
        padding=[(CONV_HALF, CONV_HALF)],
        dimension_numbers=('NWC', 'WIO', 'NWC'),
        feature_group_count=C_A) + b
    hf = h.astype(jnp.float32)
    mu = jnp.mean(hf, axis=-1, keepdims=True)
    var = jnp.mean(jnp.square(hf - mu), axis=-1, keepdims=True)
    hf = (hf - mu) * lax.rsqrt(var + EPS) * ln_g.astype(jnp.float32) + ln_b.astype(jnp.float32)
    return jax.nn.silu(hf).astype(u.dtype)


def banded_dilated_stats(q, k, v, dilation, half):
    B, S, H, E = q.shape
    L = S // dilation
    nb = -(-L // ATTN_BLOCK)
    Lp = nb * ATTN_BLOCK
    qd = q.reshape(B, L, dilation, H, E)
    kd = k.reshape(B, L, dilation, H, E)
    vd = v.reshape(B, L, dilation, H, E)
    pad_q = ((0, 0), (0, Lp - L), (0, 0), (0, 0), (0, 0))
    pad_kv = ((0, 0), (ATTN_BLOCK, Lp - L + ATTN_BLOCK), (0, 0), (0, 0), (0, 0))
    qb = jnp.pad(qd, pad_q).reshape(B, nb, ATTN_BLOCK, dilation, H, E)
    kb = jnp.pad(kd, pad_kv).reshape(B, nb + 2, ATTN_BLOCK, dilation, H, E)
    vb = jnp.pad(vd, pad_kv).reshape(B, nb + 2, ATTN_BLOCK, dilation, H, E)
    kwin = jnp.concatenate([kb[:, :-2], kb[:, 1:-1], kb[:, 2:]], axis=2)
    vwin = jnp.concatenate([vb[:, :-2], vb[:, 1:-1], vb[:, 2:]], axis=2)
    blk = jnp.arange(nb)[:, None]
    jq = blk * ATTN_BLOCK + jnp.arange(ATTN_BLOCK)[None, :]
    jk = (blk - 1) * ATTN_BLOCK + jnp.arange(3 * ATTN_BLOCK)[None, :]
    mask = ((jnp.abs(jk[:, None, :] - jq[:, :, None]) <= half)
            & (jk[:, None, :] >= 0) & (jk[:, None, :] < L))
    scores = jnp.einsum('bnqrhe,bnkrhe->bnrhqk', qb, kwin)
    maskb = mask[None, :, None, None, :, :]
    scores = jnp.where(maskb, scores, NEG)
    m = jnp.max(scores, axis=-1)
    p = jnp.where(maskb, jnp.exp(scores - m[..., None]), 0.0)
    s = jnp.sum(p, axis=-1)
    o = jnp.einsum('bnrhqk,bnkrhe->bnqrhe', p, vwin)
    m = jnp.transpose(m, (0, 1, 4, 2, 3)).reshape(B, Lp, dilation, H)[:, :L].reshape(B, S, H)
    s = jnp.transpose(s, (0, 1, 4, 2, 3)).reshape(B, Lp, dilation, H)[:, :L].reshape(B, S, H)
    o = o.reshape(B, Lp, dilation, H, E)[:, :L].reshape(B, S, H, E)
    return m, s, o


def dilated_attention(qkv, cos, sin):
    B, S, _ = qkv.shape
    q, k, v = jnp.split(qkv.astype(jnp.float32), 3, axis=-1)
    q = apply_partial_rope(q.reshape(B, S, H_B, HEAD_DIM), cos, sin) * (HEAD_DIM ** -0.5)
    k = apply_partial_rope(k.reshape(B, S, H_B, HEAD_DIM), cos, sin)
    v = v.reshape(B, S, H_B, HEAD_DIM)
    ms, ss, os_ = [], [], []
    for window, dilation in DILATED_PATTERNS:
        m, s, o = banded_dilated_stats(q, k, v, dilation, window // (2 * dilation))
        ms.append(m); ss.append(s); os_.append(o)
    m_all = jnp.stack(ms)
    s_all = jnp.stack(ss)
    o_all = jnp.stack(os_)
    wgt = jnp.exp(m_all - jnp.max(m_all, axis=0, keepdims=True))
    num = jnp.sum(wgt[..., None] * o_all, axis=0)
    den = jnp.sum(wgt * s_all, axis=0)
    out = num / den[..., None]
    return out.reshape(B, S, C_B).astype(qkv.dtype)


def pool_mixer(u, w, scale):
    B, S, _ = u.shape
    uf = u.astype(jnp.float32)
    cs = jnp.concatenate([jnp.zeros((B, 1, C_C), jnp.float32), jnp.cumsum(uf, axis=1)], axis=1)
    pos = jnp.arange(S)
    outs = []
    for gi, win in enumerate(POOL_WINDOWS):
        seg = cs[..., gi * C_G:(gi + 1) * C_G]
        lo = jnp.clip(pos - win // 2, 0, S)
        hi = jnp.clip(pos + win - win // 2, 0, S)
        mean = (seg[:, hi] - seg[:, lo]) / (hi - lo).astype(jnp.float32)[None, :, None]
        outs.append(mean - uf[..., gi * C_G:(gi + 1) * C_G])
    pooled = jnp.stack(outs, axis=2)
    mixed = jnp.einsum('bsgc,gcd->bsgd', pooled, w.astype(jnp.float32))
    return (mixed.reshape(B, S, C_C) * scale.astype(jnp.float32)).astype(u.dtype)


def _fwd_setup_inputs(seed: int = 0) -> dict:
    key = jax.random.key(seed)
    ks = jax.random.split(key, 20)
    f32 = jnp.float32

    def nrm(k, shape, scale):
        return jax.random.normal(k, shape, f32) * scale

    def gain(k, shape):
        return 1.0 + 0.05 * jax.random.normal(k, shape, f32)

    return {
        "x": jax.random.normal(ks[0], (BATCH, SEQ, D_MODEL), f32),
        "w_in": nrm(ks[1], (DEPTH, D_MODEL, IN_WIDTH), D_MODEL ** -0.5),
        "conv_w": nrm(ks[2], (DEPTH, CONV_WIDTH, C_A), CONV_WIDTH ** -0.5),
        "conv_b": nrm(ks[3], (DEPTH, C_A), 0.02),
        "conv_ln_g": gain(ks[4], (DEPTH, C_A)),
        "conv_ln_b": nrm(ks[5], (DEPTH, C_A), 0.02),
        "pool_w": nrm(ks[6], (DEPTH, N_POOL_GROUPS, C_G, C_G), C_G ** -0.5),
        "pool_scale": gain(ks[7], (DEPTH, C_C)),
        "g_mix": gain(ks[8], (DEPTH, MIX_WIDTH)),
        "w_out": nrm(ks[9], (DEPTH, MIX_WIDTH, D_MODEL), MIX_WIDTH ** -0.5),
        "g_pre_mix": gain(ks[10], (DEPTH, D_MODEL)),
        "g_post_mix": gain(ks[11], (DEPTH, D_MODEL)),
        "g_pre_ffn": gain(ks[12], (DEPTH, D_MODEL)),
        "g_post_ffn": gain(ks[13], (DEPTH, D_MODEL)),
        "w_gate": nrm(ks[14], (DEPTH, D_MODEL, FFN_HIDDEN), D_MODEL ** -0.5),
        "w_up": nrm(ks[15], (DEPTH, D_MODEL, FFN_HIDDEN), D_MODEL ** -0.5),
        "w_down": nrm(ks[16], (DEPTH, FFN_HIDDEN, D_MODEL), FFN_HIDDEN ** -0.5),
    }


def _fwd_reference(x, w_in, conv_w, conv_b, conv_ln_g, conv_ln_b, pool_w, pool_scale, g_mix,
              w_out, g_pre_mix, g_post_mix, g_pre_ffn, g_post_ffn, w_gate, w_up, w_down):
    S = x.shape[1]
    cos, sin = rope_tables(S)
    a_end = 2 * C_A
    b_end = a_end + 3 * C_B
    for l in range(DEPTH):
        h = rmsnorm(x, g_pre_mix[l])
        proj = h @ w_in[l]
        y_a = conformer_conv(proj[..., :a_end], conv_w[l], conv_b[l], conv_ln_g[l], conv_ln_b[l])
        y_b = dilated_attention(proj[..., a_end:b_end], cos, sin)
        y_c = pool_mixer(proj[..., b_end:], pool_w[l], pool_scale[l])
        y = jnp.concatenate([rms_normalize(y_a), rms_normalize(y_b), rms_normalize(y_c)], axis=-1)
        y = (y * g_mix[l].astype(jnp.float32)).astype(x.dtype)
        x = x + rmsnorm(y @ w_out[l], g_post_mix[l])
        h = rmsnorm(x, g_pre_ffn[l])
        f = (jax.nn.silu(h @ w_gate[l]) * (h @ w_up[l])) @ w_down[l]
        x = x + rmsnorm(f, g_post_ffn[l])
    return x


import jax as _jax
import jax.numpy as _jnp

TWIN_FORMAT = 'train_step'
FWD_PARAMS = ['x', 'w_in', 'conv_w', 'conv_b', 'conv_ln_g', 'conv_ln_b', 'pool_w', 'pool_scale', 'g_mix', 'w_out', 'g_pre_mix', 'g_post_mix', 'g_pre_ffn', 'g_post_ffn', 'w_gate', 'w_up', 'w_down']
TWIN_WEIGHTS = ['w_in', 'conv_w', 'conv_b', 'conv_ln_g', 'conv_ln_b', 'pool_w', 'pool_scale', 'g_mix', 'w_out', 'g_pre_mix', 'g_post_mix', 'g_pre_ffn', 'g_post_ffn', 'w_gate', 'w_up', 'w_down']
TWIN_DIFF_INPUT = 'x'
TWIN_INPUTS = ['x', 'w_in', 'conv_w', 'conv_b', 'conv_ln_g', 'conv_ln_b', 'pool_w', 'pool_scale', 'g_mix', 'w_out', 'g_pre_mix', 'g_post_mix', 'g_pre_ffn', 'g_post_ffn', 'w_gate', 'w_up', 'w_down', 'loss_target', 'm_w_in', 'm_conv_w', 'm_conv_b', 'm_conv_ln_g', 'm_conv_ln_b', 'm_pool_w', 'm_pool_scale', 'm_g_mix', 'm_w_out', 'm_g_pre_mix', 'm_g_post_mix', 'm_g_pre_ffn', 'm_g_post_ffn', 'm_w_gate', 'm_w_up', 'm_w_down', 'v_w_in', 'v_conv_w', 'v_conv_b', 'v_conv_ln_g', 'v_conv_ln_b', 'v_pool_w', 'v_pool_scale', 'v_g_mix', 'v_w_out', 'v_g_pre_mix', 'v_g_post_mix', 'v_g_pre_ffn', 'v_g_post_ffn', 'v_w_gate', 'v_w_up', 'v_w_down']
TWIN_OUTPUTS = ['loss', 'grad_x', 'grad_w_in', 'grad_conv_w', 'grad_conv_b', 'grad_conv_ln_g', 'grad_conv_ln_b', 'grad_pool_w', 'grad_pool_scale', 'grad_g_mix', 'grad_w_out', 'grad_g_pre_mix', 'grad_g_post_mix', 'grad_g_pre_ffn', 'grad_g_post_ffn', 'grad_w_gate', 'grad_w_up', 'grad_w_down', 'delta_w_in', 'delta_conv_w', 'delta_conv_b', 'delta_conv_ln_g', 'delta_conv_ln_b', 'delta_pool_w', 'delta_pool_scale', 'delta_g_mix', 'delta_w_out', 'delta_g_pre_mix', 'delta_g_post_mix', 'delta_g_pre_ffn', 'delta_g_post_ffn', 'delta_w_gate', 'delta_w_up', 'delta_w_down', 'new_m_w_in', 'new_m_conv_w', 'new_m_conv_b', 'new_m_conv_ln_g', 'new_m_conv_ln_b', 'new_m_pool_w', 'new_m_pool_scale', 'new_m_g_mix', 'new_m_w_out', 'new_m_g_pre_mix', 'new_m_g_post_mix', 'new_m_g_pre_ffn', 'new_m_g_post_ffn', 'new_m_w_gate', 'new_m_w_up', 'new_m_w_down', 'new_v_w_in', 'new_v_conv_w', 'new_v_conv_b', 'new_v_conv_ln_g', 'new_v_conv_ln_b', 'new_v_pool_w', 'new_v_pool_scale', 'new_v_g_mix', 'new_v_w_out', 'new_v_g_pre_mix', 'new_v_g_post_mix', 'new_v_g_pre_ffn', 'new_v_g_post_ffn', 'new_v_w_gate', 'new_v_w_up', 'new_v_w_down']
TWIN_LEAF_KINDS = {'loss': 'loss', 'grad_x': 'grad_x', 'grad_w_in': 'grad_w', 'grad_conv_w': 'grad_w', 'grad_conv_b': 'grad_w', 'grad_conv_ln_g': 'grad_w', 'grad_conv_ln_b': 'grad_w', 'grad_pool_w': 'grad_w', 'grad_pool_scale': 'grad_w', 'grad_g_mix': 'grad_w', 'grad_w_out': 'grad_w', 'grad_g_pre_mix': 'grad_w', 'grad_g_post_mix': 'grad_w', 'grad_g_pre_ffn': 'grad_w', 'grad_g_post_ffn': 'grad_w', 'grad_w_gate': 'grad_w', 'grad_w_up': 'grad_w', 'grad_w_down': 'grad_w', 'delta_w_in': 'delta_w', 'delta_conv_w': 'delta_w', 'delta_conv_b': 'delta_w', 'delta_conv_ln_g': 'delta_w', 'delta_conv_ln_b': 'delta_w', 'delta_pool_w': 'delta_w', 'delta_pool_scale': 'delta_w', 'delta_g_mix': 'delta_w', 'delta_w_out': 'delta_w', 'delta_g_pre_mix': 'delta_w', 'delta_g_post_mix': 'delta_w', 'delta_g_pre_ffn': 'delta_w', 'delta_g_post_ffn': 'delta_w', 'delta_w_gate': 'delta_w', 'delta_w_up': 'delta_w', 'delta_w_down': 'delta_w', 'new_m_w_in': 'new_m', 'new_m_conv_w': 'new_m', 'new_m_conv_b': 'new_m', 'new_m_conv_ln_g': 'new_m', 'new_m_conv_ln_b': 'new_m', 'new_m_pool_w': 'new_m', 'new_m_pool_scale': 'new_m', 'new_m_g_mix': 'new_m', 'new_m_w_out': 'new_m', 'new_m_g_pre_mix': 'new_m', 'new_m_g_post_mix': 'new_m', 'new_m_g_pre_ffn': 'new_m', 'new_m_g_post_ffn': 'new_m', 'new_m_w_gate': 'new_m', 'new_m_w_up': 'new_m', 'new_m_w_down': 'new_m', 'new_v_w_in': 'new_v', 'new_v_conv_w': 'new_v', 'new_v_conv_b': 'new_v', 'new_v_conv_ln_g': 'new_v', 'new_v_conv_ln_b': 'new_v', 'new_v_pool_w': 'new_v', 'new_v_pool_scale': 'new_v', 'new_v_g_mix': 'new_v', 'new_v_w_out': 'new_v', 'new_v_g_pre_mix': 'new_v', 'new_v_g_post_mix': 'new_v', 'new_v_g_pre_ffn': 'new_v', 'new_v_g_post_ffn': 'new_v', 'new_v_w_gate': 'new_v', 'new_v_w_up': 'new_v', 'new_v_w_down': 'new_v'}


def _forward(args):
    return _fwd_reference(*[args[k] for k in FWD_PARAMS])


def _output_shape():
    def fwd():
        inp = _fwd_setup_inputs(0)
        return _fwd_reference(*[inp[k] for k in FWD_PARAMS])
    out = _jax.eval_shape(fwd)
    return out.shape, out.dtype

N_MICROBATCH = 1
ADAM_LR = 0.001
ADAM_B1 = 0.9
ADAM_B2 = 0.999
ADAM_EPS = 1e-08
ADAM_WD = 0.01
ADAM_STEP = 10
PER_EXAMPLE_BATCH_AXIS = {'x': 0, 'loss_target': 0}
SHARED_INPUTS = []
_WEIGHT_DTYPES = {'w_in': _jnp.float32, 'conv_w': _jnp.float32, 'conv_b': _jnp.float32, 'conv_ln_g': _jnp.float32, 'conv_ln_b': _jnp.float32, 'pool_w': _jnp.float32, 'pool_scale': _jnp.float32, 'g_mix': _jnp.float32, 'w_out': _jnp.float32, 'g_pre_mix': _jnp.float32, 'g_post_mix': _jnp.float32, 'g_pre_ffn': _jnp.float32, 'g_post_ffn': _jnp.float32, 'w_gate': _jnp.float32, 'w_up': _jnp.float32, 'w_down': _jnp.float32}
MOMENT_SCALE = {'w_in': 7.372568e+00, 'conv_w': 6.089476e+00, 'conv_b': 7.152287e+01, 'conv_ln_g': 2.710748e+01, 'conv_ln_b': 3.812543e+01, 'pool_w': 1.490072e+00, 'pool_scale': 1.643142e+00, 'g_mix': 1.557024e+01, 'w_out': 1.570124e+01, 'g_pre_mix': 1.070702e+01, 'g_post_mix': 3.667153e+01, 'g_pre_ffn': 3.900162e+00, 'g_post_ffn': 3.213633e+01, 'w_gate': 1.084041e+00, 'w_up': 2.129229e+00, 'w_down': 3.506233e+00}


def _to_microbatches(a, axis):
    t = _jnp.moveaxis(a, axis, 0)
    t = t.reshape((N_MICROBATCH, t.shape[0] // N_MICROBATCH) + t.shape[1:])
    return _jnp.moveaxis(t, 1, axis + 1)


def setup_inputs(seed: int = 0) -> dict:
    inp = _fwd_setup_inputs(seed)
    key = _jax.random.fold_in(_jax.random.key(seed), 7919)
    shape, _ = _output_shape()
    out = dict(inp)
    out["loss_target"] = _jax.random.normal(_jax.random.fold_in(key, 0), shape, _jnp.float32)
    for i, name in enumerate(TWIN_WEIGHTS):
        w = inp[name].astype(_jnp.float32)
        if MOMENT_SCALE is None:
            s = _jnp.sqrt(_jnp.mean(_jnp.square(w)) + 1e-30)
        else:
            s = MOMENT_SCALE[name]
        km, kv = _jax.random.split(_jax.random.fold_in(key, i + 1))
        out[name] = w
        out["m_" + name] = s * _jax.random.normal(km, w.shape, _jnp.float32)
        out["v_" + name] = (s * s) * _jax.random.uniform(kv, w.shape, _jnp.float32, 0.5, 1.5)
    if N_MICROBATCH > 1:
        for name, axis in PER_EXAMPLE_BATCH_AXIS.items():
            out[name] = _to_microbatches(out[name], axis)
    return {'x': out['x'], 'w_in': out['w_in'], 'conv_w': out['conv_w'], 'conv_b': out['conv_b'], 'conv_ln_g': out['conv_ln_g'], 'conv_ln_b': out['conv_ln_b'], 'pool_w': out['pool_w'], 'pool_scale': out['pool_scale'], 'g_mix': out['g_mix'], 'w_out': out['w_out'], 'g_pre_mix': out['g_pre_mix'], 'g_post_mix': out['g_post_mix'], 'g_pre_ffn': out['g_pre_ffn'], 'g_post_ffn': out['g_post_ffn'], 'w_gate': out['w_gate'], 'w_up': out['w_up'], 'w_down': out['w_down'], 'loss_target': out['loss_target'], 'm_w_in': out['m_w_in'], 'm_conv_w': out['m_conv_w'], 'm_conv_b': out['m_conv_b'], 'm_conv_ln_g': out['m_conv_ln_g'], 'm_conv_ln_b': out['m_conv_ln_b'], 'm_pool_w': out['m_pool_w'], 'm_pool_scale': out['m_pool_scale'], 'm_g_mix': out['m_g_mix'], 'm_w_out': out['m_w_out'], 'm_g_pre_mix': out['m_g_pre_mix'], 'm_g_post_mix': out['m_g_post_mix'], 'm_g_pre_ffn': out['m_g_pre_ffn'], 'm_g_post_ffn': out['m_g_post_ffn'], 'm_w_gate': out['m_w_gate'], 'm_w_up': out['m_w_up'], 'm_w_down': out['m_w_down'], 'v_w_in': out['v_w_in'], 'v_conv_w': out['v_conv_w'], 'v_conv_b': out['v_conv_b'], 'v_conv_ln_g': out['v_conv_ln_g'], 'v_conv_ln_b': out['v_conv_ln_b'], 'v_pool_w': out['v_pool_w'], 'v_pool_scale': out['v_pool_scale'], 'v_g_mix': out['v_g_mix'], 'v_w_out': out['v_w_out'], 'v_g_pre_mix': out['v_g_pre_mix'], 'v_g_post_mix': out['v_g_post_mix'], 'v_g_pre_ffn': out['v_g_pre_ffn'], 'v_g_post_ffn': out['v_g_post_ffn'], 'v_w_gate': out['v_w_gate'], 'v_w_up': out['v_w_up'], 'v_w_down': out['v_w_down']}


def _loss(weights, diff, rest, loss_target):
    with _jax.named_scope("forward"):
        args = {**rest, TWIN_DIFF_INPUT: diff, **{k: w.astype(_WEIGHT_DTYPES[k]) for k, w in weights.items()}}
        y = _forward(args)
    with _jax.named_scope("loss_head"):
        err = _jnp.square(y.astype(_jnp.float32) - loss_target)
        return 0.5 * _jnp.sum(_jnp.mean(err, axis=-1)) if err.ndim else 0.5 * err


def _adamw(w, g, m, v):
    m = ADAM_B1 * m + (1.0 - ADAM_B1) * g
    v = ADAM_B2 * v + (1.0 - ADAM_B2) * _jnp.square(g)
    m_hat = m / (1.0 - ADAM_B1 ** ADAM_STEP)
    v_hat = v / (1.0 - ADAM_B2 ** ADAM_STEP)
    delta = -ADAM_LR * (m_hat / (_jnp.sqrt(v_hat) + ADAM_EPS) + ADAM_WD * w)
    return delta, m, v


def reference(x, w_in, conv_w, conv_b, conv_ln_g, conv_ln_b, pool_w, pool_scale, g_mix, w_out, g_pre_mix, g_post_mix, g_pre_ffn, g_post_ffn, w_gate, w_up, w_down, loss_target, m_w_in, m_conv_w, m_conv_b, m_conv_ln_g, m_conv_ln_b, m_pool_w, m_pool_scale, m_g_mix, m_w_out, m_g_pre_mix, m_g_post_mix, m_g_pre_ffn, m_g_post_ffn, m_w_gate, m_w_up, m_w_down, v_w_in, v_conv_w, v_conv_b, v_conv_ln_g, v_conv_ln_b, v_pool_w, v_pool_scale, v_g_mix, v_w_out, v_g_pre_mix, v_g_post_mix, v_g_pre_ffn, v_g_post_ffn, v_w_gate, v_w_up, v_w_down):
    given = dict(x=x, w_in=w_in, conv_w=conv_w, conv_b=conv_b, conv_ln_g=conv_ln_g, conv_ln_b=conv_ln_b, pool_w=pool_w, pool_scale=pool_scale, g_mix=g_mix, w_out=w_out, g_pre_mix=g_pre_mix, g_post_mix=g_post_mix, g_pre_ffn=g_pre_ffn, g_post_ffn=g_post_ffn, w_gate=w_gate, w_up=w_up, w_down=w_down, loss_target=loss_target, m_w_in=m_w_in, m_conv_w=m_conv_w, m_conv_b=m_conv_b, m_conv_ln_g=m_conv_ln_g, m_conv_ln_b=m_conv_ln_b, m_pool_w=m_pool_w, m_pool_scale=m_pool_scale, m_g_mix=m_g_mix, m_w_out=m_w_out, m_g_pre_mix=m_g_pre_mix, m_g_post_mix=m_g_post_mix, m_g_pre_ffn=m_g_pre_ffn, m_g_post_ffn=m_g_post_ffn, m_w_gate=m_w_gate, m_w_up=m_w_up, m_w_down=m_w_down, v_w_in=v_w_in, v_conv_w=v_conv_w, v_conv_b=v_conv_b, v_conv_ln_g=v_conv_ln_g, v_conv_ln_b=v_conv_ln_b, v_pool_w=v_pool_w, v_pool_scale=v_pool_scale, v_g_mix=v_g_mix, v_w_out=v_w_out, v_g_pre_mix=v_g_pre_mix, v_g_post_mix=v_g_post_mix, v_g_pre_ffn=v_g_pre_ffn, v_g_post_ffn=v_g_post_ffn, v_w_gate=v_w_gate, v_w_up=v_w_up, v_w_down=v_w_down)
    weights = {n: given[n] for n in TWIN_WEIGHTS}
    shared = {n: given[n] for n in SHARED_INPUTS}
    per_example = {n: given[n] for n in ['x']}
    grad_fn = _jax.value_and_grad(_loss, argnums=(0, 1))

    def one_microbatch(ex, loss_target):
        ex = dict(ex)
        diff = ex.pop(TWIN_DIFF_INPUT)
        return grad_fn(weights, diff, {**shared, **ex}, loss_target)

    if N_MICROBATCH == 1:
        loss, (grad_w, grad_x) = one_microbatch(per_example, given["loss_target"])
    else:
        def body(carry, xs):
            loss_sum, grad_sum = carry
            l_k, (gw_k, gx_k) = one_microbatch(xs[0], xs[1])
            with _jax.named_scope("update"):
                return (loss_sum + l_k, _jax.tree.map(_jnp.add, grad_sum, gw_k)), gx_k

        init = (_jnp.zeros((), _jnp.float32), _jax.tree.map(_jnp.zeros_like, weights))
        (loss, grad_w), grad_x = _jax.lax.scan(body, init, (per_example, given["loss_target"]))
    with _jax.named_scope("update"):
        delta_w, new_m, new_v = {}, {}, {}
        for n in TWIN_WEIGHTS:
            delta_w[n], new_m[n], new_v[n] = _adamw(weights[n], grad_w[n], given["m_" + n], given["v_" + n])
    return (loss, grad_x, *[grad_w[n] for n in TWIN_WEIGHTS], *[delta_w[n] for n in TWIN_WEIGHTS],
            *[new_m[n] for n in TWIN_WEIGHTS], *[new_v[n] for n in TWIN_WEIGHTS])
```

```python
import functools
import math

import jax
import jax.numpy as jnp
from jax import lax
from jax.experimental import pallas as pl
from jax.experimental.pallas import tpu as pltpu

F32 = jnp.float32
BF16 = jnp.bfloat16
MESH = pl.DeviceIdType.MESH

EPS = 1e-6
NEG = -1e30
C_A = 512
C_B = 1024
C_C = 512
HEAD_DIM = 64
POOL_WINDOWS = (2, 4, 8, 16)
C_G = 128
CONV_WIDTH = 31
CONV_HALF = 15
DILATIONS = (1, 4, 16)
ATT_HALF = 64
ROT_DIM = 16
ROPE_THETA = 500000.0
ADAM_LR, ADAM_B1, ADAM_B2, ADAM_EPS, ADAM_WD, ADAM_STEP = 0.001, 0.9, 0.999, 1e-08, 0.01, 10
N_GROUPS = 4
HALO = 32
LANES = 128
VMEM_LIMIT = 48 * 1024 * 1024


def _params(*sem):
    return pltpu.CompilerParams(dimension_semantics=sem, vmem_limit_bytes=VMEM_LIMIT)


def _pick(n, prefs):
    for p in prefs:
        if p <= n and n % p == 0:
            return p
    return n


def _rms_scale(t):
    return lax.rsqrt(jnp.mean(t * t, axis=-1, keepdims=True) + EPS)


def _rms_bwd(d, x, g):
    r = _rms_scale(x)
    u = d * g
    dx = r * u - x * (r * r * r) * jnp.mean(u * x, axis=-1, keepdims=True)
    return dx, d * x * r


def _sum8(t):
    rows, cols = t.shape
    return jnp.sum(t.reshape(rows // 8, 8, cols), axis=0)


def _dot(a, b, mode):
    dn = {"nn": (((1,), (0,)), ((), ())), "nt": (((1,), (1,)), ((), ())), "tn": (((0,), (0,)), ((), ()))}[mode]
    return lax.dot_general(a, b, dn, preferred_element_type=F32)


def _row_spec(tm, cols, col_block=0):
    return pl.BlockSpec((tm, cols), lambda i, cb=col_block: (i, cb))


def _const_spec(shape):
    return pl.BlockSpec(shape, lambda i: tuple(0 for _ in shape))


def norm_fwd(name, x, z, g_post, g_next):
    s, d = x.shape
    tm = _pick(s, (512, 256, 128))
    has_z, has_h = z is not None, g_next is not None

    def body(*refs):
        refs = list(refs)
        x_ref = refs.pop(0)
        xn = x_ref[...]
        if has_z:
            z_ref, gp_ref = refs.pop(0), refs.pop(0)
            zz = z_ref[...]
            xn = xn + zz * _rms_scale(zz) * gp_ref[...]
        if has_h:
            gn_ref = refs.pop(0)
        if has_z:
            refs.pop(0)[...] = xn
        if has_h:
            refs.pop(0)[...] = (xn * _rms_scale(xn) * gn_ref[...]).astype(BF16)

    ins, specs, outs, ospecs = [x], [_row_spec(tm, d)], [], []
    if has_z:
        ins += [z, g_post.reshape(1, d)]
        specs += [_row_spec(tm, d), _const_spec((1, d))]
        outs.append(jax.ShapeDtypeStruct((s, d), F32))
        ospecs.append(_row_spec(tm, d))
    if has_h:
        ins.append(g_next.reshape(1, d))
        specs.append(_const_spec((1, d)))
        outs.append(jax.ShapeDtypeStruct((s, d), BF16))
        ospecs.append(_row_spec(tm, d))
    res = pl.pallas_call(body, name=name, grid=(s // tm,), in_specs=specs, out_specs=ospecs, out_shape=outs,
                         compiler_params=_params("parallel"))(*ins)
    return list(res)


def norm_bwd(name, dres, dh, xin, g_pre, zin, g_post):
    s, d = dres.shape
    tm = _pick(s, (256, 128))
    has_pre, has_post = dh is not None, zin is not None

    def body(*refs):
        refs = list(refs)
        i = pl.program_id(0)
        dx = refs.pop(0)[...]
        if has_pre:
            dh_ref, x_ref, g_ref = refs.pop(0), refs.pop(0), refs.pop(0)
            ddx, dg = _rms_bwd(dh_ref[...].astype(F32), x_ref[...], g_ref[...])
            dx = dx + ddx
        if has_post:
            z_ref, gp_ref = refs.pop(0), refs.pop(0)
            dz, dgp = _rms_bwd(dx, z_ref[...], gp_ref[...])
        if has_pre:
            refs.pop(0)[...] = dx
            dg_ref = refs.pop(0)

            @pl.when(i == 0)
            def _():
                dg_ref[...] = jnp.zeros_like(dg_ref)

            dg_ref[...] += _sum8(dg)
        if has_post:
            refs.pop(0)[...] = dz.astype(BF16)
            dgp_ref = refs.pop(0)

            @pl.when(i == 0)
            def _():
                dgp_ref[...] = jnp.zeros_like(dgp_ref)

            dgp_ref[...] += _sum8(dgp)

    ins, specs, outs, ospecs = [dres], [_row_spec(tm, d)], [], []
    if has_pre:
        ins += [dh, xin, g_pre.reshape(1, d)]
        specs += [_row_spec(tm, d), _row_spec(tm, d), _const_spec((1, d))]
        outs += [jax.ShapeDtypeStruct((s, d), F32), jax.ShapeDtypeStruct((8, d), F32)]
        ospecs += [_row_spec(tm, d), _const_spec((8, d))]
    if has_post:
        ins += [zin, g_post.reshape(1, d)]
        specs += [_row_spec(tm, d), _const_spec((1, d))]
        outs += [jax.ShapeDtypeStruct((s, d), BF16), jax.ShapeDtypeStruct((8, d), F32)]
        ospecs += [_row_spec(tm, d), _const_spec((8, d))]
    res = pl.pallas_call(body, name=name, grid=(s // tm,), in_specs=specs, out_specs=ospecs, out_shape=outs,
                         compiler_params=_params("arbitrary"))(*ins)
    return list(res)


def loss_head(x, target):
    s, d = x.shape
    tm = _pick(s, (512, 256, 128))

    def body(x_ref, t_ref, dx_ref, sq_ref):
        i = pl.program_id(0)
        e = x_ref[...] - t_ref[...]
        dx_ref[...] = e * (1.0 / d)

        @pl.when(i == 0)
        def _():
            sq_ref[...] = jnp.zeros_like(sq_ref)

        sq_ref[...] += _sum8(e * e)

    return pl.pallas_call(body, name="loss_head", grid=(s // tm,),
                          in_specs=[_row_spec(tm, d), _row_spec(tm, d)],
                          out_specs=[_row_spec(tm, d), _const_spec((8, d))],
                          out_shape=[jax.ShapeDtypeStruct((s, d), F32), jax.ShapeDtypeStruct((8, d), F32)],
                          compiler_params=_params("arbitrary"))(x, target)


def matmul(name, mode, a, b, *, grid, tk_steps, a_spec, b_spec, o_spec, out_shape, acc_shape):
    nk = tk_steps

    def body(a_ref, b_ref, o_ref, *scratch):
        if nk == 1:
            o_ref[...] = _dot(a_ref[...], b_ref[...], mode).astype(o_ref.dtype)
            return
        acc = scratch[0]
        kk = pl.program_id(2)

        @pl.when(kk == 0)
        def _():
            acc[...] = jnp.zeros_like(acc)

        acc[...] += _dot(a_ref[...], b_ref[...], mode)

        @pl.when(kk == nk - 1)
        def _():
            o_ref[...] = acc[...].astype(o_ref.dtype)

    return pl.pallas_call(body, name=name, grid=grid, in_specs=[a_spec, b_spec], out_specs=o_spec, out_shape=out_shape,
                          scratch_shapes=[] if nk == 1 else [pltpu.VMEM(acc_shape, F32)],
                          compiler_params=_params("parallel", "parallel", "arbitrary"))(a, b)


TM_PREFS = (1024, 512, 256, 128)
TW_PREFS = (1408, 1152, 1024, 512, 384, 256, 128)
TK_PREFS = (512, 384, 256, 128)


def mm_act_wcols(name, a, wg, layer, out_dtype):
    s, k = a.shape
    g, _, _, ng = wg.shape
    tm, tk, tn = _pick(s, TM_PREFS), _pick(k, TK_PREFS), _pick(ng, TW_PREFS)
    per = ng // tn
    return matmul(name, "nn", a, wg, grid=(s // tm, g * per, k // tk), tk_steps=k // tk,
                  a_spec=pl.BlockSpec((tm, tk), lambda i, j, kk: (i, kk)),
                  b_spec=pl.BlockSpec((None, None, tk, tn), lambda i, j, kk: (j // per, layer, kk, j % per)),
                  o_spec=pl.BlockSpec((tm, tn), lambda i, j, kk: (i, j)),
                  out_shape=jax.ShapeDtypeStruct((s, g * ng), out_dtype), acc_shape=(tm, tn))


def mm_act_wcols_t(name, a, wg, layer, out_dtype):
    s, n = a.shape
    g, _, k, ng = wg.shape
    tm, tn, tk = _pick(s, TM_PREFS), _pick(k, TM_PREFS), _pick(ng, (384, 128))
    per = ng // tk
    return matmul(name, "nt", a, wg, grid=(s // tm, k // tn, g * per), tk_steps=g * per,
                  a_spec=pl.BlockSpec((tm, tk), lambda i, j, kk: (i, kk)),
                  b_spec=pl.BlockSpec((None, None, tn, tk), lambda i, j, kk: (kk // per, layer, j, kk % per)),
                  o_spec=pl.BlockSpec((tm, tn), lambda i, j, kk: (i, j)),
                  out_shape=jax.ShapeDtypeStruct((s, k), out_dtype), acc_shape=(tm, tn))


def mm_act_wrows(name, a, wr, layer, out_dtype):
    s, k = a.shape
    g, _, kg, n = wr.shape
    tm, tn, tk = _pick(s, TM_PREFS), _pick(n, TM_PREFS), _pick(kg, (1408, 512, 256, 128))
    per = kg // tk
    return matmul(name, "nn", a, wr, grid=(s // tm, n // tn, g * per), tk_steps=g * per,
                  a_spec=pl.BlockSpec((tm, tk), lambda i, j, kk: (i, kk)),
                  b_spec=pl.BlockSpec((None, None, tk, tn), lambda i, j, kk: (kk // per, layer, kk % per, j)),
                  o_spec=pl.BlockSpec((tm, tn), lambda i, j, kk: (i, j)),
                  out_shape=jax.ShapeDtypeStruct((s, n), out_dtype), acc_shape=(tm, tn))


def mm_act_wrows_t(name, a, wr, layer, out_dtype):
    s, n = a.shape
    g, _, kg, _ = wr.shape
    tm, tn, tk = _pick(s, TM_PREFS), _pick(kg, (1408, 512, 256, 128)), _pick(n, TK_PREFS)
    per = kg // tn
    return matmul(name, "nt", a, wr, grid=(s // tm, g * per, n // tk), tk_steps=n // tk,
                  a_spec=pl.BlockSpec((tm, tk), lambda i, j, kk: (i, kk)),
                  b_spec=pl.BlockSpec((None, None, tn, tk), lambda i, j, kk: (j // per, layer, j % per, kk)),
                  o_spec=pl.BlockSpec((tm, tn), lambda i, j, kk: (i, j)),
                  out_shape=jax.ShapeDtypeStruct((s, g * kg), out_dtype), acc_shape=(tm, tn))


def mm_wgrad_cols(name, a, dy, ng):
    s, k = a.shape
    n = dy.shape[1]
    g = n // ng
    tm = _pick(k, (1024, 512, 256, 128))
    tk = _pick(s, (512, 256, 128))
    tn = _pick(ng, (1408, 1152, 1024, 512, 384, 256, 128))
    per = ng // tn
    return matmul(name, "tn", a, dy, grid=(k // tm, g * per, s // tk), tk_steps=s // tk,
                  a_spec=pl.BlockSpec((tk, tm), lambda i, j, kk: (kk, i)),
                  b_spec=pl.BlockSpec((tk, tn), lambda i, j, kk: (kk, j)),
                  o_spec=pl.BlockSpec((None, tm, tn), lambda i, j, kk: (j // per, i, j % per)),
                  out_shape=jax.ShapeDtypeStruct((g, k, ng), BF16), acc_shape=(tm, tn))


def mm_plain(name, mode, a, b, out_dtype):
    if mode == "nn":
        (m, k), n = a.shape, b.shape[1]
    elif mode == "nt":
        (m, k), n = a.shape, b.shape[0]
    else:
        (k, m), n = a.shape, b.shape[1]
    tm = _pick(m, (1408,) + TM_PREFS)
    tn = _pick(n, TM_PREFS)
    tk = _pick(k, (512, 256, 128))
    a_spec = (pl.BlockSpec((tk, tm), lambda i, j, kk: (kk, i)) if mode == "tn"
              else pl.BlockSpec((tm, tk), lambda i, j, kk: (i, kk)))
    b_spec = (pl.BlockSpec((tn, tk), lambda i, j, kk: (j, kk)) if mode == "nt"
              else pl.BlockSpec((tk, tn), lambda i, j, kk: (kk, j)))
    return matmul(name, mode, a, b, grid=(m // tm, n // tn, k // tk), tk_steps=k // tk, a_spec=a_spec, b_spec=b_spec,
                  o_spec=pl.BlockSpec((tm, tn), lambda i, j, kk: (i, j)),
                  out_shape=jax.ShapeDtypeStruct((m, n), out_dtype), acc_shape=(tm, tn))


def _silu_parts(gt):
    sg = jax.nn.sigmoid(gt)
    return gt * sg, sg


def ffn_up(name, h, wgate, wup, layer):
    s, d = h.shape
    g, _, _, ng = wgate.shape
    tm = _pick(s, (512, 256, 128))
    tn = _pick(ng, TW_PREFS)
    tk = _pick(d, (512, 256, 128))
    per, nk = ng // tn, d // tk

    def body(h_ref, wg_ref, wu_ref, gt_ref, up_ref, act_ref, accg, accu):
        kk = pl.program_id(2)

        @pl.when(kk == 0)
        def _():
            accg[...] = jnp.zeros_like(accg)
            accu[...] = jnp.zeros_like(accu)

        hh = h_ref[...]
        accg[...] += _dot(hh, wg_ref[...], "nn")
        accu[...] += _dot(hh, wu_ref[...], "nn")

        @pl.when(kk == nk - 1)
        def _():
            gt, up = accg[...], accu[...]
            gt_ref[...] = gt.astype(BF16)
            up_ref[...] = up.astype(BF16)
            act_ref[...] = (_silu_parts(gt)[0] * up).astype(BF16)

    wspec = pl.BlockSpec((None, None, tk, tn), lambda i, j, kk: (j // per, layer, kk, j % per))
    ospec = pl.BlockSpec((tm, tn), lambda i, j, kk: (i, j))
    osh = jax.ShapeDtypeStruct((s, g * ng), BF16)
    return pl.pallas_call(body, name=name, grid=(s // tm, g * per, nk),
                          in_specs=[pl.BlockSpec((tm, tk), lambda i, j, kk: (i, kk)), wspec, wspec],
                          out_specs=[ospec, ospec, ospec], out_shape=[osh, osh, osh],
                          scratch_shapes=[pltpu.VMEM((tm, tn), F32), pltpu.VMEM((tm, tn), F32)],
                          compiler_params=_params("parallel", "parallel", "arbitrary"))(h, wgate, wup)


def ffn_down_bwd(name, df, wdown, layer, gt, up):
    s, d = df.shape
    g, _, fg, _ = wdown.shape
    f = g * fg
    tm = _pick(s, (512, 256, 128))
    tn = _pick(fg, TW_PREFS)
    tk = _pick(d, (512, 256, 128))
    nk = d // tk
    per = fg // tn

    def body(df_ref, wd_ref, gt_ref, up_ref, dgt_ref, dup_ref, acc):
        kk = pl.program_id(2)

        @pl.when(kk == 0)
        def _():
            acc[...] = jnp.zeros_like(acc)

        acc[...] += _dot(df_ref[...], wd_ref[...], "nt")

        @pl.when(kk == nk - 1)
        def _():
            da = acc[...]
            gt = gt_ref[...].astype(F32)
            up = up_ref[...].astype(F32)
            silu, sg = _silu_parts(gt)
            dgt_ref[...] = (da * up * (sg * (1.0 + gt * (1.0 - sg)))).astype(BF16)
            dup_ref[...] = (da * silu).astype(BF16)

    ospec = pl.BlockSpec((tm, tn), lambda i, j, kk: (i, j))
    osh = jax.ShapeDtypeStruct((s, f), BF16)
    return pl.pallas_call(body, name=name, grid=(s // tm, f // tn, nk),
                          in_specs=[pl.BlockSpec((tm, tk), lambda i, j, kk: (i, kk)),
                                    pl.BlockSpec((None, None, tn, tk), lambda i, j, kk: (j // per, layer, j % per, kk)),
                                    ospec, ospec],
                          out_specs=[ospec, ospec], out_shape=[osh, osh],
                          scratch_shapes=[pltpu.VMEM((tm, tn), F32)],
                          compiler_params=_params("parallel", "parallel", "arbitrary"))(df, wdown, gt, up)


def ffn_in_bwd(name, dgt, dup, wgate, wup, layer):
    s, f = dgt.shape
    g, _, d, ng = wgate.shape
    tm = _pick(s, (512, 256, 128))
    tn = _pick(d, TM_PREFS)
    tk = _pick(ng, TW_PREFS)
    per = ng // tk
    nk = g * per

    def body(dg_ref, du_ref, wg_ref, wu_ref, o_ref, acc):
        kk = pl.program_id(2)

        @pl.when(kk == 0)
        def _():
            acc[...] = jnp.zeros_like(acc)

        acc[...] += _dot(dg_ref[...], wg_ref[...], "nt") + _dot(du_ref[...], wu_ref[...], "nt")

        @pl.when(kk == nk - 1)
        def _():
            o_ref[...] = acc[...]

    aspec = pl.BlockSpec((tm, tk), lambda i, j, kk: (i, kk))
    wspec = pl.BlockSpec((None, None, tn, tk), lambda i, j, kk: (kk // per, layer, j, kk % per))
    return pl.pallas_call(body, name=name, grid=(s // tm, d // tn, nk), in_specs=[aspec, aspec, wspec, wspec],
                          out_specs=pl.BlockSpec((tm, tn), lambda i, j, kk: (i, j)),
                          out_shape=jax.ShapeDtypeStruct((s, d), F32), scratch_shapes=[pltpu.VMEM((tm, tn), F32)],
                          compiler_params=_params("parallel", "parallel", "arbitrary"))(dgt, dup, wgate, wup)


def _halo_specs(tm, s, cols, cb, halo=HALO):
    r = tm // halo
    last = s // halo - 1
    return [pl.BlockSpec((halo, cols), lambda i: (jnp.maximum(i * r - 1, 0), cb)),
            pl.BlockSpec((tm, cols), lambda i: (i, cb)),
            pl.BlockSpec((halo, cols), lambda i: (jnp.minimum((i + 1) * r, last), cb))]


def _edge_masked(prev_ref, next_ref, i, nt):
    return jnp.where(i > 0, prev_ref[...], 0.0), jnp.where(i < nt - 1, next_ref[...], 0.0)


def _glu(a, gate):
    return a * jax.nn.sigmoid(gate)


def _conv_post(conv, lg, lb, gm):
    mu = jnp.mean(conv, axis=-1, keepdims=True)
    xc = conv - mu
    rs = lax.rsqrt(jnp.mean(xc * xc, axis=-1, keepdims=True) + EPS)
    xhat = xc * rs
    yl = xhat * lg + lb
    ya, sg = _silu_parts(yl)
    return xhat, rs, yl, sg, ya


def _fill_glu(hg, ap, am, an, gp, gm_, gn, i, nt, tm):
    a0, a2 = _edge_masked(ap, an, i, nt)
    g0, g2 = _edge_masked(gp, gn, i, nt)
    hg[pl.ds(0, HALO), :] = _glu(a0, g0)
    hg[pl.ds(HALO, tm), :] = _glu(am[...], gm_[...])
    hg[pl.ds(HALO + tm, HALO), :] = _glu(a2, g2)


def conv_fwd(name, proj, cw, cb, lg, lb, gmix):
    s = proj.shape[0]
    tm = _pick(s, (256, 128))
    nt = s // tm

    def body(ap, am, an, gp, gm_, gn, cw_ref, cb_ref, lg_ref, lb_ref, gx_ref, o_ref, hg):
        i = pl.program_id(0)
        _fill_glu(hg, ap, am, an, gp, gm_, gn, i, nt, tm)
        acc = jnp.zeros((tm, C_A), F32) + cb_ref[...]
        for t in range(CONV_WIDTH):
            acc = acc + cw_ref[pl.ds(t, 1), :] * hg[pl.ds(HALO - CONV_HALF + t, tm), :]
        ya = _conv_post(acc, lg_ref[...], lb_ref[...], None)[4]
        o_ref[...] = (ya * _rms_scale(ya) * gx_ref[...]).astype(BF16)

    vec = _const_spec((1, C_A))
    return pl.pallas_call(body, name=name, grid=(nt,),
                          in_specs=_halo_specs(tm, s, C_A, 0) + _halo_specs(tm, s, C_A, 1)
                          + [_const_spec((32, C_A)), vec, vec, vec, vec],
                          out_specs=_row_spec(tm, C_A), out_shape=jax.ShapeDtypeStruct((s, C_A), BF16),
                          scratch_shapes=[pltpu.VMEM((tm + 2 * HALO, C_A), F32)],
                          compiler_params=_params("parallel"))(proj, proj, proj, proj, proj, proj, cw, cb, lg, lb, gmix)


def conv_bwd(name, proj, dy, cw, cb, lg, lb, gmix):
    s = proj.shape[0]
    tm = _pick(s, (256, 128))
    nt = s // tm
    te = tm + HALO
    off = HALO // 2

    def body(ap, am, an, gp, gm_, gn, dp, dm, dn, cw_ref, cb_ref, lg_ref, lb_ref, gx_ref,
             dag_ref, dcw_ref, dcb_ref, dlg_ref, dlb_ref, dgx_ref, hg, dc, dyx):
        i = pl.program_id(0)
        _fill_glu(hg, ap, am, an, gp, gm_, gn, i, nt, tm)
        d0, d2 = _edge_masked(dp, dn, i, nt)
        dyx[pl.ds(0, HALO), :] = d0
        dyx[pl.ds(HALO, tm), :] = dm[...]
        dyx[pl.ds(HALO + tm, HALO), :] = d2
        lg, lb, gx = lg_ref[...], lb_ref[...], gx_ref[...]
        conv = jnp.zeros((te, C_A), F32) + cb_ref[...]
        for t in range(CONV_WIDTH):
            conv = conv + cw_ref[pl.ds(t, 1), :] * hg[pl.ds(1 + t, te), :]
        xhat, rs, yl, sg, ya = _conv_post(conv, lg, lb, gx)
        dout = dyx[pl.ds(off, te), :]
        dya, dgx_rows = _rms_bwd(dout, ya, gx)
        dyl = dya * (sg * (1.0 + yl * (1.0 - sg)))
        dxh = dyl * lg
        dconv = rs * (dxh - jnp.mean(dxh, axis=-1, keepdims=True)
                      - xhat * jnp.mean(dxh * xhat, axis=-1, keepdims=True))
        dc[...] = dconv

        @pl.when(i == 0)
        def _():
            for r in (dcw_ref, dcb_ref, dlg_ref, dlb_ref, dgx_ref):
                r[...] = jnp.zeros_like(r)

        dcb_ref[...] += _sum8(dconv[off:off + tm])
        dlg_ref[...] += _sum8((dyl * xhat)[off:off + tm])
        dlb_ref[...] += _sum8(dyl[off:off + tm])
        dgx_ref[...] += _sum8(dgx_rows[off:off + tm])
        dcm = dconv[off:off + tm]
        dhg = jnp.zeros((tm, C_A), F32)
        for t in range(CONV_WIDTH):
            dhg = dhg + cw_ref[pl.ds(t, 1), :] * dc[pl.ds(HALO - 1 - t, tm), :]
            dcw_ref[pl.ds(8 * t, 8), :] += _sum8(dcm * hg[pl.ds(HALO - CONV_HALF + t, tm), :])
        a, gate = am[...], gm_[...]
        sgate = jax.nn.sigmoid(gate)
        dag_ref[:, pl.ds(0, C_A)] = (dhg * sgate).astype(BF16)
        dag_ref[:, pl.ds(C_A, C_A)] = (dhg * a * sgate * (1.0 - sgate)).astype(BF16)

    vec = _const_spec((1, C_A))
    acc8 = _const_spec((8, C_A))
    sh8 = jax.ShapeDtypeStruct((8, C_A), F32)
    return pl.pallas_call(body, name=name, grid=(nt,),
                          in_specs=_halo_specs(tm, s, C_A, 0) + _halo_specs(tm, s, C_A, 1) + _halo_specs(tm, s, C_A, 0)
                          + [_const_spec((32, C_A)), vec, vec, vec, vec],
                          out_specs=[_row_spec(tm, 2 * C_A), _const_spec((32 * 8, C_A)), acc8, acc8, acc8, acc8],
                          out_shape=[jax.ShapeDtypeStruct((s, 2 * C_A), BF16), jax.ShapeDtypeStruct((32 * 8, C_A), F32),
                                     sh8, sh8, sh8, sh8],
                          scratch_shapes=[pltpu.VMEM((tm + 2 * HALO, C_A), F32), pltpu.VMEM((te, C_A), F32),
                                          pltpu.VMEM((tm + 2 * HALO, C_A), F32)],
                          compiler_params=_params("arbitrary"))(proj, proj, proj, proj, proj, proj, dy, dy, dy,
                                                                cw, cb, lg, lb, gmix)


def _pool_counts(pos, win, s):
    lo = jnp.maximum(pos - win // 2, 0)
    hi = jnp.minimum(pos + win - win // 2, s)
    return jnp.maximum(hi - lo, 1).astype(F32)


def _pooled(uext, base, rows, pos, s):
    outs = []
    for gi, win in enumerate(POOL_WINDOWS):
        lanes = pl.ds(gi * C_G, C_G)
        acc = jnp.zeros((rows, C_G), F32)
        for o in range(-(win // 2), win - win // 2):
            acc = acc + uext[pl.ds(base + o, rows), lanes]
        outs.append(acc / _pool_counts(pos, win, s) - uext[pl.ds(base, rows), lanes])
    return outs


def pool_fwd(name, proj, pw, scale, gmix):
    s = proj.shape[0]
    tm = _pick(s, (256, 128))
    nt = s // tm
    ucol = (proj.shape[1] - C_C) // C_C

    def body(up, um, un, pw_ref, sc_ref, gx_ref, o_ref, uext):
        i = pl.program_id(0)
        u0, u2 = _edge_masked(up, un, i, nt)
        uext[pl.ds(0, HALO), :] = u0
        uext[pl.ds(HALO, tm), :] = um[...]
        uext[pl.ds(HALO + tm, HALO), :] = u2
        pos = i * tm + lax.broadcasted_iota(jnp.int32, (tm, 1), 0)
        pooled = _pooled(uext, HALO, tm, pos, s)
        mixed = jnp.concatenate([_dot(pooled[g].astype(BF16), pw_ref[g], "nn") for g in range(4)], axis=1)
        yc = mixed * sc_ref[...]
        o_ref[...] = (yc * _rms_scale(yc) * gx_ref[...]).astype(BF16)

    vec = _const_spec((1, C_C))
    return pl.pallas_call(body, name=name, grid=(nt,),
                          in_specs=_halo_specs(tm, s, C_C, ucol) + [_const_spec((4, C_G, C_G)), vec, vec],
                          out_specs=_row_spec(tm, C_C), out_shape=jax.ShapeDtypeStruct((s, C_C), BF16),
                          scratch_shapes=[pltpu.VMEM((tm + 2 * HALO, C_C), F32)],
                          compiler_params=_params("parallel"))(proj, proj, proj, pw, scale, gmix)


def pool_bwd(name, proj, dy, pw, scale, gmix):
    s = proj.shape[0]
    tm = _pick(s, (256, 128))
    nt = s // tm
    te = tm + HALO
    off = HALO // 2
    ucol = (proj.shape[1] - C_C) // C_C
    dcol = (dy.shape[1] - C_C) // C_C

    def body(up, um, un, dp, dm, dn, pw_ref, sc_ref, gx_ref, du_ref, dpw_ref, dsc_ref, dgx_ref, uext, dyx, qs, dps):
        i = pl.program_id(0)
        u0, u2 = _edge_masked(up, un, i, nt)
        uext[pl.ds(0, HALO), :] = u0
        uext[pl.ds(HALO, tm), :] = um[...]
        uext[pl.ds(HALO + tm, HALO), :] = u2
        d0, d2 = _edge_masked(dp, dn, i, nt)
        dyx[pl.ds(0, HALO), :] = d0
        dyx[pl.ds(HALO, tm), :] = dm[...]
        dyx[pl.ds(HALO + tm, HALO), :] = d2
        pos = i * tm - off + lax.broadcasted_iota(jnp.int32, (te, 1), 0)
        pooled = _pooled(uext, off, te, pos, s)
        mixed = jnp.concatenate([_dot(pooled[g].astype(BF16), pw_ref[g], "nn") for g in range(4)], axis=1)
        sc = sc_ref[...]
        yc = mixed * sc
        dyc, dgx_rows = _rms_bwd(dyx[pl.ds(off, te), :], yc, gx_ref[...])
        dmixed = dyc * sc

        @pl.when(i == 0)
        def _():
            for r in (dpw_ref, dsc_ref, dgx_ref):
                r[...] = jnp.zeros_like(r)

        dsc_ref[...] += _sum8((dyc * mixed)[off:off + tm])
        dgx_ref[...] += _sum8(dgx_rows[off:off + tm])
        for gi, win in enumerate(POOL_WINDOWS):
            lanes = pl.ds(gi * C_G, C_G)
            dmg = dmixed[:, gi * C_G:(gi + 1) * C_G].astype(BF16)
            dpw_ref[gi] += _dot(pooled[gi][off:off + tm].astype(BF16), dmg[off:off + tm], "tn")
            dpl = _dot(dmg, pw_ref[gi], "nt")
            dps[:, lanes] = dpl
            qs[:, lanes] = dpl / _pool_counts(pos, win, s)
        for gi, win in enumerate(POOL_WINDOWS):
            lanes = pl.ds(gi * C_G, C_G)
            acc = jnp.zeros((tm, C_G), F32) - dps[pl.ds(off, tm), lanes]
            for o in range(-(win // 2) + 1, win // 2 + 1):
                acc = acc + qs[pl.ds(off + o, tm), lanes]
            du_ref[:, lanes] = acc.astype(BF16)

    vec = _const_spec((1, C_C))
    acc8 = _const_spec((8, C_C))
    sh8 = jax.ShapeDtypeStruct((8, C_C), F32)
    return pl.pallas_call(body, name=name, grid=(nt,),
                          in_specs=_halo_specs(tm, s, C_C, ucol) + _halo_specs(tm, s, C_C, dcol)
                          + [_const_spec((4, C_G, C_G)), vec, vec],
                          out_specs=[_row_spec(tm, C_C), _const_spec((4, C_G, C_G)), acc8, acc8],
                          out_shape=[jax.ShapeDtypeStruct((s, C_C), BF16), jax.ShapeDtypeStruct((4, C_G, C_G), F32),
                                     sh8, sh8],
                          scratch_shapes=[pltpu.VMEM((tm + 2 * HALO, C_C), F32), pltpu.VMEM((tm + 2 * HALO, C_C), F32),
                                          pltpu.VMEM((te, C_C), F32), pltpu.VMEM((te, C_C), F32)],
                          compiler_params=_params("arbitrary"))(proj, proj, proj, dy, dy, dy, pw, scale, gmix)


def rope_tables(s):
    pos = jnp.arange(s, dtype=F32)
    inv = ROPE_THETA ** (-jnp.arange(0, ROT_DIM, 2, dtype=F32) / ROT_DIM)
    ang = pos[:, None] * inv[None, :]
    cos, sin = jnp.cos(ang), jnp.sin(ang)
    half = ROT_DIM // 2
    rest = HEAD_DIM - ROT_DIM
    c = jnp.concatenate([cos, cos, jnp.ones((s, rest), F32)], axis=1)
    sa = jnp.concatenate([-sin, jnp.zeros((s, HEAD_DIM - half), F32)], axis=1)
    sb = jnp.concatenate([jnp.zeros((s, half), F32), sin, jnp.zeros((s, rest), F32)], axis=1)
    return tuple(jnp.concatenate([t, t], axis=1) for t in (c, sa, sb))


def rope_fwd(name, proj, tabs):
    s = proj.shape[0]
    tm = _pick(s, (256, 128))
    scale = HEAD_DIM ** -0.5

    def body(q_ref, k_ref, v_ref, c_ref, sa_ref, sb_ref, qo, ko, vo):
        c, sa, sb = c_ref[...], sa_ref[...], sb_ref[...]
        for j in range(C_B // LANES):
            lanes = pl.ds(j * LANES, LANES)
            for src, dst, mul in ((q_ref, qo, scale), (k_ref, ko, 1.0)):
                t = src[:, lanes]
                r = t * c + pltpu.roll(t, LANES - ROT_DIM // 2, 1) * sa + pltpu.roll(t, ROT_DIM // 2, 1) * sb
                dst[:, lanes] = (r * mul).astype(BF16)
        vo[...] = v_ref[...].astype(BF16)

    tab = _row_spec(tm, LANES)
    osh = jax.ShapeDtypeStruct((s, C_B), BF16)
    return pl.pallas_call(body, name=name, grid=(s // tm,),
                          in_specs=[_row_spec(tm, C_B, 1), _row_spec(tm, C_B, 2), _row_spec(tm, C_B, 3), tab, tab, tab],
                          out_specs=[_row_spec(tm, C_B)] * 3, out_shape=[osh, osh, osh],
                          compiler_params=_params("parallel"))(proj, proj, proj, *tabs)


def rope_bwd(name, dqs, dks, dvs, tabs):
    s = dqs[0].shape[0]
    tm = _pick(s, (256, 128))
    scale = HEAD_DIM ** -0.5

    def body(q1, q2, q3, k1, k2, k3, v1, v2, v3, c_ref, sa_ref, sb_ref, o_ref):
        c, sa, sb = c_ref[...], sa_ref[...], sb_ref[...]
        for j in range(C_B // LANES):
            lanes = pl.ds(j * LANES, LANES)
            for which, (r1, r2, r3), mul in ((0, (q1, q2, q3), scale), (1, (k1, k2, k3), 1.0)):
                d = r1[:, lanes].astype(F32) + r2[:, lanes].astype(F32) + r3[:, lanes].astype(F32)
                t = d * c + pltpu.roll(d * sa, ROT_DIM // 2, 1) + pltpu.roll(d * sb, LANES - ROT_DIM // 2, 1)
                o_ref[:, pl.ds(which * C_B + j * LANES, LANES)] = (t * mul).astype(BF16)
        o_ref[:, pl.ds(2 * C_B, C_B)] = (v1[...].astype(F32) + v2[...].astype(F32) + v3[...].astype(F32)).astype(BF16)

    tab = _row_spec(tm, LANES)
    return pl.pallas_call(body, name=name, grid=(s // tm,), in_specs=[_row_spec(tm, C_B)] * 9 + [tab, tab, tab],
                          out_specs=_row_spec(tm, 3 * C_B), out_shape=jax.ShapeDtypeStruct((s, 3 * C_B), BF16),
                          compiler_params=_params("parallel"))(*dqs, *dks, *dvs, *tabs)


def to_dilated(t, d):
    if d == 1:
        return t
    s, c = t.shape
    return t.reshape(s // d, d, c).transpose(1, 0, 2).reshape(s, c)


def from_dilated(t, d):
    if d == 1:
        return t
    s, c = t.shape
    return t.reshape(d, s // d, c).transpose(1, 0, 2).reshape(s, c)


def _window_specs(tq, s):
    r = tq // ATT_HALF
    last = s // ATT_HALF - 1
    return [pl.BlockSpec((ATT_HALF, LANES), lambda i, p: (jnp.maximum(i * r - 1, 0), p)),
            pl.BlockSpec((tq, LANES), lambda i, p: (i, p)),
            pl.BlockSpec((ATT_HALF, LANES), lambda i, p: (jnp.minimum((i + 1) * r, last), p))]


def _window(prev_ref, main_ref, next_ref):
    return jnp.concatenate([prev_ref[...], main_ref[...], next_ref[...]], axis=0)


def _band_valid(tile_rows, window_rows, j0, seg, tile_is_rows):
    shape = (tile_rows, window_rows) if tile_is_rows else (window_rows, tile_rows)
    ti = lax.broadcasted_iota(jnp.int32, shape, 0 if tile_is_rows else 1)
    wi = lax.broadcasted_iota(jnp.int32, shape, 1 if tile_is_rows else 0)
    jw = j0 - ATT_HALF + wi
    return (jnp.abs(wi - ATT_HALF - ti) <= ATT_HALF) & (jw >= 0) & (jw < seg)


def _first_head():
    return lax.broadcasted_iota(jnp.int32, (1, LANES), 1) < HEAD_DIM


def band_attn_fwd(name, q, k, v, seg):
    s, c = q.shape
    tq = _pick(seg, (256, 128))
    per_seg = seg // tq

    def body(q_ref, kp, km, kn, vp, vm, vn, o_ref, lse_ref):
        j0 = (pl.program_id(0) % per_seg) * tq
        q2 = q_ref[...]
        kw, vw = _window(kp, km, kn), _window(vp, vm, vn)
        valid = _band_valid(tq, tq + 2 * ATT_HALF, j0, seg, True)
        first = _first_head()
        res = []
        for hm in (first, jnp.logical_not(first)):
            sc = jnp.where(valid, _dot(jnp.where(hm, q2, jnp.zeros_like(q2)), kw, "nt"), NEG)
            m = jnp.max(sc, axis=-1, keepdims=True)
            p = jnp.exp(sc - m)
            l = jnp.sum(p, axis=-1, keepdims=True)
            res.append((_dot(p.astype(BF16), vw, "nn") / l, m + jnp.log(l)))
        o_ref[...] = jnp.where(first, res[0][0], res[1][0]).astype(BF16)
        lse_ref[...] = jnp.where(first, res[0][1], res[1][1])

    win = _window_specs(tq, s)
    tile = pl.BlockSpec((tq, LANES), lambda i, p: (i, p))
    return pl.pallas_call(body, name=name, grid=(s // tq, c // LANES), in_specs=[tile] + win + win,
                          out_specs=[tile, tile],
                          out_shape=[jax.ShapeDtypeStruct((s, c), BF16), jax.ShapeDtypeStruct((s, c), F32)],
                          compiler_params=_params("parallel", "parallel"))(q, k, k, k, v, v, v)


def band_attn_dq(name, q, k, v, do, lse, delta, seg):
    s, c = q.shape
    tq = _pick(seg, (256, 128))
    per_seg = seg // tq

    def body(q_ref, kp, km, kn, vp, vm, vn, do_ref, lse_ref, dl_ref, dq_ref):
        j0 = (pl.program_id(0) % per_seg) * tq
        q2, do2 = q_ref[...], do_ref[...]
        kw, vw = _window(kp, km, kn), _window(vp, vm, vn)
        valid = _band_valid(tq, tq + 2 * ATT_HALF, j0, seg, True)
        first = _first_head()
        res = []
        for hm, lane0 in ((first, 0), (jnp.logical_not(first), HEAD_DIM)):
            sc = _dot(jnp.where(hm, q2, jnp.zeros_like(q2)), kw, "nt")
            p = jnp.where(valid, jnp.exp(sc - lse_ref[:, pl.ds(lane0, 1)]), 0.0)
            dp = _dot(jnp.where(hm, do2, jnp.zeros_like(do2)), vw, "nt")
            ds = p * (dp - dl_ref[:, pl.ds(lane0, 1)])
            res.append(_dot(ds.astype(BF16), kw, "nn"))
        dq_ref[...] = jnp.where(first, res[0], res[1]).astype(BF16)

    win = _window_specs(tq, s)
    tile = pl.BlockSpec((tq, LANES), lambda i, p: (i, p))
    return pl.pallas_call(body, name=name, grid=(s // tq, c // LANES), in_specs=[tile] + win + win + [tile, tile, tile],
                          out_specs=tile, out_shape=jax.ShapeDtypeStruct((s, c), BF16),
                          compiler_params=_params("parallel", "parallel"))(q, k, k, k, v, v, v, do, lse, delta)


def band_attn_dkv(name, q, k, v, do, lse, delta, seg):
    s, c = q.shape
    tk = _pick(seg, (256, 128))
    per_seg = seg // tk

    def body(k_ref, v_ref, qp, qm, qn, dop, dom, don, lp, lm, ln, dlp, dlm, dln, dk_ref, dv_ref, lsew, dlw):
        j0 = (pl.program_id(0) % per_seg) * tk
        k2, v2 = k_ref[...], v_ref[...]
        qw, dow = _window(qp, qm, qn), _window(dop, dom, don)
        lsew[...] = _window(lp, lm, ln)
        dlw[...] = _window(dlp, dlm, dln)
        valid = _band_valid(tk, tk + 2 * ATT_HALF, j0, seg, False)
        first = _first_head()
        res = []
        for hm, lane0 in ((first, 0), (jnp.logical_not(first), HEAD_DIM)):
            sc = _dot(jnp.where(hm, qw, jnp.zeros_like(qw)), k2, "nt")
            p = jnp.where(valid, jnp.exp(sc - lsew[:, pl.ds(lane0, 1)]), 0.0)
            dv = _dot(p.astype(BF16), dow, "tn")
            dp = _dot(jnp.where(hm, dow, jnp.zeros_like(dow)), v2, "nt")
            ds = p * (dp - dlw[:, pl.ds(lane0, 1)])
            res.append((_dot(ds.astype(BF16), qw, "tn"), dv))
        dk_ref[...] = jnp.where(first, res[0][0], res[1][0]).astype(BF16)
        dv_ref[...] = jnp.where(first, res[0][1], res[1][1]).astype(BF16)

    win = _window_specs(tk, s)
    tile = pl.BlockSpec((tk, LANES), lambda i, p: (i, p))
    osh = jax.ShapeDtypeStruct((s, c), BF16)
    return pl.pallas_call(body, name=name, grid=(s // tk, c // LANES), in_specs=[tile, tile] + win * 4,
                          out_specs=[tile, tile], out_shape=[osh, osh],
                          scratch_shapes=[pltpu.VMEM((tk + 2 * ATT_HALF, LANES), F32)] * 2,
                          compiler_params=_params("parallel", "parallel"))(k, v, q, q, q, do, do, do, lse, lse, lse,
                                                                          delta, delta, delta)


def attn_combine_fwd(name, os_, lses, gmix):
    s, c = os_[0].shape
    tm = _pick(s, (256, 128))

    def body(o1, o2, o3, l1, l2, l3, gx_ref, yb_ref, lse_ref, y_ref):
        la, lb, lc = l1[...], l2[...], l3[...]
        m = jnp.maximum(jnp.maximum(la, lb), lc)
        ea, eb, ec = jnp.exp(la - m), jnp.exp(lb - m), jnp.exp(lc - m)
        den = ea + eb + ec
        yb = (ea * o1[...].astype(F32) + eb * o2[...].astype(F32) + ec * o3[...].astype(F32)) / den
        yb_ref[...] = yb
        lse_ref[...] = m + jnp.log(den)
        y_ref[...] = (yb * _rms_scale(yb) * gx_ref[...]).astype(BF16)

    row = _row_spec(tm, c)
    return pl.pallas_call(body, name=name, grid=(s // tm,), in_specs=[row] * 6 + [_const_spec((1, c))],
                          out_specs=[row, row, row],
                          out_shape=[jax.ShapeDtypeStruct((s, c), F32), jax.ShapeDtypeStruct((s, c), F32),
                                     jax.ShapeDtypeStruct((s, c), BF16)],
                          compiler_params=_params("parallel"))(*os_, *lses, gmix)


def attn_combine_bwd(name, dy, yb, gmix):
    s, c = yb.shape
    tm = _pick(s, (256, 128))
    half = c // 2

    def body(d1, d2, yb_ref, gx_ref, dyb_ref, dl_ref, dgx_ref):
        i = pl.program_id(0)
        d = jnp.concatenate([d1[...], d2[...]], axis=1)
        yb_ = yb_ref[...]
        dyb, dgx_rows = _rms_bwd(d, yb_, gx_ref[...])
        dyb_ref[...] = dyb.astype(BF16)

        @pl.when(i == 0)
        def _():
            dgx_ref[...] = jnp.zeros_like(dgx_ref)

        dgx_ref[...] += _sum8(dgx_rows)
        ri = lax.broadcasted_iota(jnp.int32, (LANES, LANES), 0) // HEAD_DIM
        ci = lax.broadcasted_iota(jnp.int32, (LANES, LANES), 1) // HEAD_DIM
        same_head = jnp.where(ri == ci, 1.0, 0.0).astype(BF16)
        prod = dyb * yb_
        for j in range(c // LANES):
            pj = prod[:, j * LANES:(j + 1) * LANES]
            hi = pj.astype(BF16)
            lo = (pj - hi.astype(F32)).astype(BF16)
            dl_ref[:, pl.ds(j * LANES, LANES)] = _dot(hi, same_head, "nn") + _dot(lo, same_head, "nn")

    row = _row_spec(tm, c)
    return pl.pallas_call(body, name=name, grid=(s // tm,),
                          in_specs=[_row_spec(tm, half, 1), _row_spec(tm, half, 2), row, _const_spec((1, c))],
                          out_specs=[row, row, _const_spec((8, c))],
                          out_shape=[jax.ShapeDtypeStruct((s, c), BF16), jax.ShapeDtypeStruct((s, c), F32),
                                     jax.ShapeDtypeStruct((8, c), F32)],
                          compiler_params=_params("arbitrary"))(dy, dy, yb, gmix)


def _ew_rows(rows, cols, n_bufs):
    budget = 24 * 1024 * 1024 // (8 * n_bufs * cols)
    return _pick(rows, tuple(t for t in (2048, 1024, 512, 256, 128, 64, 32, 16, 8) if t <= max(budget, 8)))


def elementwise(name, fn, ins, out_dtypes):
    shape = ins[0].shape
    cols = shape[-1]
    rows = math.prod(shape[:-1])
    tr = _ew_rows(rows, cols, len(ins) + len(out_dtypes))
    n_in = len(ins)

    def body(*refs):
        outs = fn(*[r[...] for r in refs[:n_in]])
        for o_ref, o in zip(refs[n_in:], outs):
            o_ref[...] = o.astype(o_ref.dtype)

    spec = _row_spec(tr, cols)
    res = pl.pallas_call(body, name=name, grid=(rows // tr,), in_specs=[spec] * n_in, out_specs=[spec] * len(out_dtypes),
                         out_shape=[jax.ShapeDtypeStruct((rows, cols), dt) for dt in out_dtypes],
                         compiler_params=_params("parallel"))(*[t.reshape(rows, cols) for t in ins])
    return [r.reshape(shape) for r in res]


def _adamw(g, w, m, v):
    m = ADAM_B1 * m + (1.0 - ADAM_B1) * g
    v = ADAM_B2 * v + (1.0 - ADAM_B2) * (g * g)
    m_hat = m / (1.0 - ADAM_B1 ** ADAM_STEP)
    v_hat = v / (1.0 - ADAM_B2 ** ADAM_STEP)
    delta = -ADAM_LR * (m_hat / (jnp.sqrt(v_hat) + ADAM_EPS) + ADAM_WD * w)
    return delta, m, v


def sum_chip_partials(name, own, recv):
    h, cols = own.shape
    tr = _ew_rows(h, cols, 5)

    def body(o_ref, r0, r1, r2, out_ref):
        out_ref[...] = ((o_ref[...].astype(F32) + r0[...].astype(F32)) + r1[...].astype(F32)) + r2[...].astype(F32)

    rspecs = [pl.BlockSpec((None, tr, cols), lambda i, j=j: (j, i, 0)) for j in range(3)]
    return pl.pallas_call(body, name=name, grid=(h // tr,), in_specs=[_row_spec(tr, cols)] + rspecs,
                          out_specs=_row_spec(tr, cols), out_shape=jax.ShapeDtypeStruct((h, cols), F32),
                          compiler_params=_params("parallel"))(own, recv, recv, recv)


def small_update(name, gall, w, m, v):
    _, rows, cols = gall.shape
    tr = _ew_rows(rows, cols, 16)

    def body(g_ref, w_ref, m_ref, v_ref, go, do, mo, vo):
        g = g_ref[0]
        for dev in range(1, 8):
            g = g + g_ref[dev]
        delta, mn, vn = _adamw(g, w_ref[...], m_ref[...], v_ref[...])
        go[...] = g
        do[...] = delta
        mo[...] = mn
        vo[...] = vn

    spec = _row_spec(tr, cols)
    sh = jax.ShapeDtypeStruct((rows, cols), F32)
    return pl.pallas_call(body, name=name, grid=(rows // tr,),
                          in_specs=[pl.BlockSpec((8, tr, cols), lambda i: (0, i, 0)), spec, spec, spec],
                          out_specs=[spec] * 4, out_shape=[sh] * 4, compiler_params=_params("parallel"))(gall, w, m, v)


def _place():
    x, y, c = lax.axis_index("x"), lax.axis_index("y"), lax.axis_index("c")
    chips = [(1 - x, y), (x, 1 - y), (1 - x, 1 - y)]
    return x, y, c, chips


def _remote(src, dst, send_sems, recv_sems, k, to):
    return pltpu.make_async_remote_copy(src_ref=src, dst_ref=dst, send_sem=send_sems.at[k], recv_sem=recv_sems.at[k],
                                        device_id=to, device_id_type=MESH)


ANY = pl.BlockSpec(memory_space=pl.ANY)


def allgather_small(name, block):
    m_per, n = block.shape

    def body(x_ref, out_ref, send_sems, recv_sems, local_sem):
        x, y, c, chips = _place()
        me, sibling = (x, y, c), (x, y, 1 - c)

        def rows(px, py, pc):
            return out_ref.at[pl.ds((4 * px + 2 * py + pc) * m_per, m_per), :]

        def copy(k, blk, to, src=None):
            return _remote(rows(*blk) if src is None else src, rows(*blk), send_sems, recv_sems, k, to)

        mine = pltpu.make_async_copy(x_ref, rows(*me), local_sem)
        mine.start()
        first = [copy(0, me, sibling, src=x_ref)]
        first += [copy(1 + j, me, (*chip, c), src=x_ref) for j, chip in enumerate(chips)]
        for cp in first:
            cp.start()
        passed = [copy(4 + j, (*chip, c), sibling) for j, chip in enumerate(chips)]
        for j, chip in enumerate(chips):
            copy(1 + j, (*chip, c), me).wait_recv()
            passed[j].start()
        copy(0, sibling, me).wait_recv()
        for j, chip in enumerate(chips):
            copy(4 + j, (*chip, 1 - c), me).wait_recv()
        for cp in first + passed:
            cp.wait_send()
        mine.wait()

    return pl.pallas_call(body, name=name, out_shape=jax.ShapeDtypeStruct((8 * m_per, n), block.dtype),
                          in_specs=[pl.BlockSpec(memory_space=pltpu.VMEM)], out_specs=pl.BlockSpec(memory_space=pltpu.VMEM),
                          scratch_shapes=[pltpu.SemaphoreType.DMA((7,)), pltpu.SemaphoreType.DMA((7,)),
                                          pltpu.SemaphoreType.DMA],
                          compiler_params=pltpu.CompilerParams(vmem_limit_bytes=VMEM_LIMIT))(block)


def gather_weights(shards):
    n = len(shards)
    halves = [t.shape[1] // 2 for t in shards]

    def body(*refs):
        ins, outs = refs[:n], refs[n:2 * n]
        send_sems, recv_sems, local_sems = refs[2 * n:]
        x, y, c, chips = _place()
        g0 = 2 * x + y
        sibling = (x, y, 1 - c)
        groups = [2 * cx + cy for cx, cy in chips]

        def half(a, cc):
            return pl.ds(cc * halves[a], halves[a])

        def own(a):
            return pltpu.make_async_copy(ins[a], outs[a].at[g0], local_sems.at[a])

        def ici(a, j):
            return _remote(ins[a].at[:, half(a, c), :], outs[a].at[g0, :, half(a, c), :], send_sems, recv_sems,
                           3 * a + j, (*chips[j], c))

        def ici_arrival(a, j):
            landed = outs[a].at[groups[j], :, half(a, c), :]
            return _remote(landed, landed, send_sems, recv_sems, 3 * a + j, (*chips[j], c))

        def passed(a, j, cc):
            landed = outs[a].at[groups[j], :, half(a, cc), :]
            return _remote(landed, landed, send_sems, recv_sems, 3 * n + 3 * a + j, sibling)

        for a in range(n):
            own(a).start()
            for j in range(3):
                ici(a, j).start()
        for a in range(n):
            for j in range(3):
                ici_arrival(a, j).wait_recv()
                passed(a, j, c).start()
        for a in range(n):
            for j in range(3):
                passed(a, j, 1 - c).wait_recv()
        for a in range(n):
            for j in range(3):
                ici(a, j).wait_send()
                passed(a, j, c).wait_send()
            own(a).wait()

    return pl.pallas_call(body, name="gather_weights",
                          out_shape=[jax.ShapeDtypeStruct((N_GROUPS,) + t.shape, t.dtype) for t in shards],
                          in_specs=[ANY] * n, out_specs=[ANY] * n,
                          scratch_shapes=[pltpu.SemaphoreType.DMA((6 * n,)), pltpu.SemaphoreType.DMA((6 * n,)),
                                          pltpu.SemaphoreType.DMA((n,))])(*shards)


def exchange_core_halves(name, grads):
    n = len(grads)
    halves = [t.shape[1] // 2 for t in grads]

    def body(*refs):
        ins, mine, theirs = refs[:n], refs[n:2 * n], refs[2 * n:3 * n]
        send_sems, recv_sems, local_sems = refs[3 * n:]
        x, y, c, _ = _place()
        sibling = (x, y, 1 - c)

        def keep(a):
            return pltpu.make_async_copy(ins[a].at[:, pl.ds(c * halves[a], halves[a]), :], mine[a], local_sems.at[a])

        def give(a):
            return _remote(ins[a].at[:, pl.ds((1 - c) * halves[a], halves[a]), :], theirs[a], send_sems, recv_sems,
                           a, sibling)

        for a in range(n):
            keep(a).start()
            give(a).start()
        for a in range(n):
            give(a).wait_recv()
        for a in range(n):
            give(a).wait_send()
            keep(a).wait()

    shapes = [jax.ShapeDtypeStruct((t.shape[0], h, t.shape[2]), t.dtype) for t, h in zip(grads, halves)]
    res = pl.pallas_call(body, name=name, out_shape=shapes + shapes, in_specs=[ANY] * n, out_specs=[ANY] * (2 * n),
                         scratch_shapes=[pltpu.SemaphoreType.DMA((n,)), pltpu.SemaphoreType.DMA((n,)),
                                         pltpu.SemaphoreType.DMA((n,))])(*grads)
    return res[:n], res[n:]


def exchange_chip_partials(name, parts):
    n = len(parts)

    def body(*refs):
        ins, own, recv = refs[:n], refs[n:2 * n], refs[2 * n:3 * n]
        send_sems, recv_sems, local_sems = refs[3 * n:]
        x, y, c, chips = _place()
        g0 = 2 * x + y

        def keep(a):
            return pltpu.make_async_copy(ins[a].at[g0], own[a], local_sems.at[a])

        def give(a, j):
            return _remote(ins[a].at[2 * chips[j][0] + chips[j][1]], recv[a].at[j], send_sems, recv_sems, 3 * a + j,
                           (*chips[j], c))

        for a in range(n):
            keep(a).start()
            for j in range(3):
                give(a, j).start()
        for a in range(n):
            for j in range(3):
                give(a, j).wait_recv()
        for a in range(n):
            for j in range(3):
                give(a, j).wait_send()
            keep(a).wait()

    own_sh = [jax.ShapeDtypeStruct(t.shape[1:], t.dtype) for t in parts]
    recv_sh = [jax.ShapeDtypeStruct((3,) + t.shape[1:], t.dtype) for t in parts]
    res = pl.pallas_call(body, name=name, out_shape=own_sh + recv_sh, in_specs=[ANY] * n, out_specs=[ANY] * (2 * n),
                         scratch_shapes=[pltpu.SemaphoreType.DMA((3 * n,)), pltpu.SemaphoreType.DMA((3 * n,)),
                                         pltpu.SemaphoreType.DMA((n,))])(*parts)
    return res[:n], res[n:]


def share_reduced_halves(reduced, n_layers):
    n = len(reduced[0])
    flat = [reduced[l][a] for l in range(n_layers) for a in range(n)]
    halves = [reduced[0][a].shape[0] for a in range(n)]

    def body(*refs):
        ins, outs = refs[:n_layers * n], refs[n_layers * n:n_layers * n + n]
        send_sems, recv_sems, local_sems = refs[n_layers * n + n:]
        x, y, c, _ = _place()
        sibling = (x, y, 1 - c)

        def place(l, a, cc):
            return outs[a].at[l, pl.ds(cc * halves[a], halves[a]), :]

        def keep(l, a):
            return pltpu.make_async_copy(ins[l * n + a], place(l, a, c), local_sems.at[l * n + a])

        def give(l, a):
            return _remote(ins[l * n + a], place(l, a, c), send_sems, recv_sems, l * n + a, sibling)

        def arrival(l, a):
            return _remote(place(l, a, 1 - c), place(l, a, 1 - c), send_sems, recv_sems, l * n + a, sibling)

        for l in range(n_layers):
            for a in range(n):
                keep(l, a).start()
                give(l, a).start()
        for l in range(n_layers):
            for a in range(n):
                arrival(l, a).wait_recv()
        for l in range(n_layers):
            for a in range(n):
                give(l, a).wait_send()
                keep(l, a).wait()

    shapes = [jax.ShapeDtypeStruct((n_layers, 2 * halves[a], reduced[0][a].shape[1]), F32) for a in range(n)]
    k = n_layers * n
    return pl.pallas_call(body, name="share_reduced_halves", out_shape=shapes, in_specs=[ANY] * k, out_specs=[ANY] * n,
                          scratch_shapes=[pltpu.SemaphoreType.DMA((k,)), pltpu.SemaphoreType.DMA((k,)),
                                          pltpu.SemaphoreType.DMA((k,))])(*flat)


def reduce_layer(grads):
    mine, theirs = exchange_core_halves("exchange_core_halves", grads)
    parts = [elementwise("sum_core_halves", lambda a, b: (a.astype(F32) + b.astype(F32),), [m, t], [BF16])[0]
             for m, t in zip(mine, theirs)]
    own, recv = exchange_chip_partials("exchange_chip_partials", parts)
    return [sum_chip_partials("sum_chip_partials", o, r) for o, r in zip(own, recv)]


def _pack(parts):
    flat = jnp.concatenate([p.reshape(-1) for p in parts])
    pad = (-flat.shape[0]) % (8 * LANES)
    if pad:
        flat = jnp.concatenate([flat, jnp.zeros((pad,), flat.dtype)])
    return flat.reshape(-1, LANES)


def _unpack(buf, shapes):
    flat = buf.reshape(-1)
    out, at = [], 0
    for sh in shapes:
        size = math.prod(sh)
        out.append(flat[at:at + size].reshape(sh))
        at += size
    return out


def kernel(x, w_in, conv_w, conv_b, conv_ln_g, conv_ln_b, pool_w, pool_scale, g_mix, w_out, g_pre_mix, g_post_mix, g_pre_ffn, g_post_ffn, w_gate, w_up, w_down, loss_target, m_w_in, m_conv_w, m_conv_b, m_conv_ln_g, m_conv_ln_b, m_pool_w, m_pool_scale, m_g_mix, m_w_out, m_g_pre_mix, m_g_post_mix, m_g_pre_ffn, m_g_post_ffn, m_w_gate, m_w_up, m_w_down, v_w_in, v_conv_w, v_conv_b, v_conv_ln_g, v_conv_ln_b, v_pool_w, v_pool_scale, v_g_mix, v_w_out, v_g_pre_mix, v_g_post_mix, v_g_pre_ffn, v_g_post_ffn, v_w_gate, v_w_up, v_w_down):
    xs, target = x[0], loss_target[0]
    s, d = xs.shape
    n_layers = w_in.shape[0]
    my_group = 2 * lax.axis_index("x") + lax.axis_index("y")

    big = [w_in, w_out, w_gate, w_up, w_down]
    big_bf16 = [elementwise("cast_bf16", lambda t: (t,), [w], [BF16])[0] for w in big]
    win_g, wout_g, wgate_g, wup_g, wdown_g = gather_weights(big_bf16)
    cw_pad = jnp.pad(conv_w, ((0, 0), (0, 32 - CONV_WIDTH), (0, 0)))
    cw_all = allgather_small("allgather_conv_w", cw_pad.reshape(n_layers * 32, LANES))
    cw_full = cw_all.reshape(N_GROUPS, 2, n_layers, 32, LANES)[:, 0].transpose(1, 2, 0, 3).reshape(n_layers, 32, C_A)
    pw_bf16 = pool_w.astype(BF16)
    tabs = rope_tables(s)
    gmix_a, gmix_b, gmix_c = g_mix[:, :C_A], g_mix[:, C_A:C_A + C_B], g_mix[:, C_A + C_B:]
    row = lambda t: t.reshape(1, -1)

    saved = []
    (h,) = norm_fwd("norm_first", xs, None, None, g_pre_mix[0])
    xin = xs
    for l in range(n_layers):
        proj = mm_act_wcols("proj", h, win_g, l, F32)
        ya = conv_fwd("conv_fwd", proj, cw_full[l], row(conv_b[l]), row(conv_ln_g[l]), row(conv_ln_b[l]), row(gmix_a[l]))
        qr, kr, vr = rope_fwd("rope_fwd", proj, tabs)
        qkv_d, os_, lses = [], [], []
        for dil in DILATIONS:
            qd, kd, vd = (to_dilated(t, dil) for t in (qr, kr, vr))
            o, lse = band_attn_fwd(f"band_attn_fwd_d{dil}", qd, kd, vd, s // dil)
            qkv_d.append((qd, kd, vd))
            os_.append(from_dilated(o, dil))
            lses.append(from_dilated(lse, dil))
        yb, lse_joint, ybn = attn_combine_fwd("attn_combine_fwd", os_, lses, row(gmix_b[l]))
        yc = pool_fwd("pool_fwd", proj, pw_bf16[l], row(pool_scale[l]), row(gmix_c[l]))
        y = jnp.concatenate([ya, ybn, yc], axis=1)
        z = mm_act_wrows("mix_out", y, wout_g, l, F32)
        x1, h2 = norm_fwd("norm_mid", xin, z, g_post_mix[l], g_pre_ffn[l])
        gt, up, act = ffn_up("ffn_up", h2, wgate_g, wup_g, l)
        f = mm_act_wrows("ffn_down", act, wdown_g, l, F32)
        if l + 1 < n_layers:
            x2, h_next = norm_fwd("norm_next", x1, f, g_post_ffn[l], g_pre_mix[l + 1])
        else:
            (x2,), h_next = norm_fwd("norm_last", x1, f, g_post_ffn[l], None), None
        saved.append(dict(xin=xin, h=h, proj=proj, qkv_d=qkv_d, yb=yb, lse=lse_joint, y=y, z=z, x1=x1, h2=h2,
                          gt=gt, up=up, act=act, f=f))
        xin, h = x2, h_next

    dx, sq = loss_head(xin, target)
    loss = lax.psum(0.5 * jnp.sum(sq) / d, ("x", "y", "c"))

    small = {k: [None] * n_layers for k in ("cw", "cb", "lg", "lb", "pw", "ps", "gmix", "gpre_mix", "gpost_mix",
                                            "gpre_ffn", "gpost_ffn")}
    reduced = [None] * n_layers
    df, dg = norm_bwd("norm_bwd_top", dx, None, None, None, saved[-1]["f"], g_post_ffn[n_layers - 1])
    small["gpost_ffn"][n_layers - 1] = dg
    for l in reversed(range(n_layers)):
        sv = saved[l]
        dgt, dup = ffn_down_bwd("ffn_down_bwd", df, wdown_g, l, sv["gt"], sv["up"])
        d_wdown = mm_plain("wgrad_rows_ffn", "tn", sv["act"], df, BF16)
        dh2 = ffn_in_bwd("ffn_in_bwd", dgt, dup, wgate_g, wup_g, l)
        d_wgate = mm_wgrad_cols("wgrad_cols_ffn", sv["h2"], dgt, wgate_g.shape[3])
        d_wup = mm_wgrad_cols("wgrad_cols_ffn", sv["h2"], dup, wup_g.shape[3])
        dx1, small["gpre_ffn"][l], dz, small["gpost_mix"][l] = norm_bwd(
            "norm_bwd_mid", dx, dh2, sv["x1"], g_pre_ffn[l], sv["z"], g_post_mix[l])
        dy = mm_act_wrows_t("mix_out_bwd", dz, wout_g, l, F32)
        d_wout = mm_plain("wgrad_rows_mix", "tn", sv["y"], dz, BF16)
        proj = sv["proj"]
        dag, dcw, small["cb"][l], small["lg"][l], small["lb"][l], dgx_a = conv_bwd(
            "conv_bwd", proj, dy, cw_full[l], row(conv_b[l]), row(conv_ln_g[l]), row(conv_ln_b[l]), row(gmix_a[l]))
        small["cw"][l] = dcw.reshape(32, 8, C_A).sum(axis=1)
        dyb, delta, dgx_b = attn_combine_bwd("attn_combine_bwd", dy, sv["yb"], row(gmix_b[l]))
        dqs, dks, dvs = [], [], []
        for dil, (qd, kd, vd) in zip(DILATIONS, sv["qkv_d"]):
            do_d, lse_d, delta_d = (to_dilated(t, dil) for t in (dyb, sv["lse"], delta))
            dq = band_attn_dq(f"band_attn_dq_d{dil}", qd, kd, vd, do_d, lse_d, delta_d, s // dil)
            dk, dv = band_attn_dkv(f"band_attn_dkv_d{dil}", qd, kd, vd, do_d, lse_d, delta_d, s // dil)
            dqs.append(from_dilated(dq, dil))
            dks.append(from_dilated(dk, dil))
            dvs.append(from_dilated(dv, dil))
        dqkv = rope_bwd("rope_bwd", dqs, dks, dvs, tabs)
        du, small["pw"][l], small["ps"][l], dgx_c = pool_bwd("pool_bwd", proj, dy, pw_bf16[l], row(pool_scale[l]),
                                                              row(gmix_c[l]))
        small["gmix"][l] = jnp.concatenate([dgx_a, dgx_b, dgx_c], axis=1)
        dproj = jnp.concatenate([dag, dqkv, du], axis=1)
        dh1 = mm_act_wcols_t("proj_bwd", dproj, win_g, l, F32)
        d_win = mm_wgrad_cols("wgrad_cols_proj", sv["h"], dproj, win_g.shape[3])
        if l > 0:
            dx, small["gpre_mix"][l], df, small["gpost_ffn"][l - 1] = norm_bwd(
                "norm_bwd_next", dx1, dh1, sv["xin"], g_pre_mix[l], saved[l - 1]["f"], g_post_ffn[l - 1])
        else:
            dx, small["gpre_mix"][l] = norm_bwd("norm_bwd_first", dx1, dh1, sv["xin"], g_pre_mix[l], None, None)
        d_wout = d_wout.reshape(N_GROUPS, -1, d_wout.shape[1])
        d_wdown = d_wdown.reshape(N_GROUPS, -1, d_wdown.shape[1])
        reduced[l] = reduce_layer([d_win, d_wout, d_wgate, d_wup, d_wdown])

    big_grads = share_reduced_halves(reduced, n_layers)
    big_m = [m_w_in, m_w_out, m_w_gate, m_w_up, m_w_down]
    big_v = [v_w_in, v_w_out, v_w_gate, v_w_up, v_w_down]
    big_upd = [elementwise("adamw", _adamw, [g, w, m, v], [F32, F32, F32])
               for g, w, m, v in zip(big_grads, big, big_m, big_v)]

    def stack8(k):
        return jnp.stack([t.sum(axis=0) if t.shape[0] == 8 and t.ndim == 2 else t for t in small[k]])

    order = ("cw", "cb", "lg", "lb", "pw", "ps", "gmix", "gpre_mix", "gpost_mix", "gpre_ffn", "gpost_ffn")
    partial = [jnp.stack(small["cw"])] + [stack8(k) for k in order[1:]]
    rep_w = [conv_b, conv_ln_g, conv_ln_b, pool_w, pool_scale, g_mix, g_pre_mix, g_post_mix, g_pre_ffn, g_post_ffn]
    rep_m = [m_conv_b, m_conv_ln_g, m_conv_ln_b, m_pool_w, m_pool_scale, m_g_mix, m_g_pre_mix, m_g_post_mix,
             m_g_pre_ffn, m_g_post_ffn]
    rep_v = [v_conv_b, v_conv_ln_g, v_conv_ln_b, v_pool_w, v_pool_scale, v_g_mix, v_g_pre_mix, v_g_post_mix,
             v_g_pre_ffn, v_g_post_ffn]
    packed = _pack(partial)
    rows = packed.shape[0]
    gall = allgather_small("allgather_small_grads", packed).reshape(8, rows, LANES)
    cw_hole = jnp.zeros((n_layers, 32, C_A), F32)
    rep_out = small_update("small_update", gall, _pack([cw_hole] + rep_w), _pack([cw_hole] + rep_m),
                           _pack([cw_hole] + rep_v))
    shapes = [cw_hole.shape] + [t.shape for t in rep_w]
    rep_g, rep_d, rep_mn, rep_vn = ([t for t in _unpack(buf, shapes)[1:]] for buf in rep_out)
    cw_rows = n_layers * 32
    gall_cw = gall[:, :cw_rows * C_A // LANES].reshape(8, cw_rows, C_A // LANES, LANES)
    gall_cw = lax.dynamic_index_in_dim(gall_cw, my_group, axis=2, keepdims=False)
    pad_cw = lambda t: jnp.pad(t, ((0, 0), (0, 32 - CONV_WIDTH), (0, 0))).reshape(cw_rows, LANES)
    cw_out = small_update("conv_w_update", gall_cw, pad_cw(conv_w), pad_cw(m_conv_w), pad_cw(v_conv_w))
    cw_g, cw_d, cw_mn, cw_vn = (t.reshape(n_layers, 32, LANES)[:, :CONV_WIDTH] for t in cw_out)

    def assemble(bigs, cw, reps):
        return [bigs[0], cw] + list(reps[:6]) + [bigs[1]] + list(reps[6:]) + list(bigs[2:])

    grads = assemble(big_grads, cw_g, rep_g)
    deltas = assemble([u[0] for u in big_upd], cw_d, rep_d)
    new_m = assemble([u[1] for u in big_upd], cw_mn, rep_mn)
    new_v = assemble([u[2] for u in big_upd], cw_vn, rep_vn)
    return (loss, dx[None], *grads, *deltas, *new_m, *new_v)
```

```python
import functools
import math

import jax
import jax.numpy as jnp
from jax import lax
from jax.experimental import pallas as pl
from jax.experimental.pallas import tpu as pltpu

F32 = jnp.float32
BF16 = jnp.bfloat16
MESH = pl.DeviceIdType.MESH

EPS = 1e-6
NEG = -1e30
C_A = 512
C_B = 1024
C_C = 512
HEAD_DIM = 64
POOL_WINDOWS = (2, 4, 8, 16)
C_G = 128
CONV_WIDTH = 31
CONV_HALF = 15
DILATIONS = (1, 4, 16)
ATT_HALF = 64
ROT_DIM = 16
ROPE_THETA = 500000.0
ADAM_LR, ADAM_B1, ADAM_B2, ADAM_EPS, ADAM_WD, ADAM_STEP = 0.001, 0.9, 0.999, 1e-08, 0.01, 10
N_GROUPS = 4
HALO = 32
LANES = 128
VMEM_LIMIT = 48 * 1024 * 1024


def _params(*sem):
    return pltpu.CompilerParams(dimension_semantics=sem, vmem_limit_bytes=VMEM_LIMIT)


def _pick(n, prefs):
    for p in prefs:
        if p <= n and n % p == 0:
            return p
    return n


def _rms_scale(t):
    return lax.rsqrt(jnp.mean(t * t, axis=-1, keepdims=True) + EPS)


def _rms_bwd(d, x, g):
    r = _rms_scale(x)
    u = d * g
    dx = r * u - x * (r * r * r) * jnp.mean(u * x, axis=-1, keepdims=True)
    return dx, d * x * r


def _sum8(t):
    rows, cols = t.shape
    return jnp.sum(t.reshape(rows // 8, 8, cols), axis=0)


def _dot(a, b, mode):
    dn = {"nn": (((1,), (0,)), ((), ())), "nt": (((1,), (1,)), ((), ())), "tn": (((0,), (0,)), ((), ()))}[mode]
    return lax.dot_general(a, b, dn, preferred_element_type=F32)


def _row_spec(tm, cols, col_block=0):
    return pl.BlockSpec((tm, cols), lambda i, cb=col_block: (i, cb))


def _const_spec(shape):
    return pl.BlockSpec(shape, lambda i: tuple(0 for _ in shape))


def norm_fwd(name, x, z, g_post, g_next):
    s, d = x.shape
    tm = _pick(s, (512, 256, 128))
    has_z, has_h = z is not None, g_next is not None

    def body(*refs):
        refs = list(refs)
        x_ref = refs.pop(0)
        xn = x_ref[...]
        if has_z:
            z_ref, gp_ref = refs.pop(0), refs.pop(0)
            zz = z_ref[...]
            xn = xn + zz * _rms_scale(zz) * gp_ref[...]
        if has_h:
            gn_ref = refs.pop(0)
        if has_z:
            refs.pop(0)[...] = xn
        if has_h:
            refs.pop(0)[...] = (xn * _rms_scale(xn) * gn_ref[...]).astype(BF16)

    ins, specs, outs, ospecs = [x], [_row_spec(tm, d)], [], []
    if has_z:
        ins += [z, g_post.reshape(1, d)]
        specs += [_row_spec(tm, d), _const_spec((1, d))]
        outs.append(jax.ShapeDtypeStruct((s, d), F32))
        ospecs.append(_row_spec(tm, d))
    if has_h:
        ins.append(g_next.reshape(1, d))
        specs.append(_const_spec((1, d)))
        outs.append(jax.ShapeDtypeStruct((s, d), BF16))
        ospecs.append(_row_spec(tm, d))
    res = pl.pallas_call(body, name=name, grid=(s // tm,), in_specs=specs, out_specs=ospecs, out_shape=outs,
                         compiler_params=_params("parallel"))(*ins)
    return list(res)


def norm_bwd(name, dres, dh, xin, g_pre, zin, g_post):
    s, d = dres.shape
    tm = _pick(s, (256, 128))
    has_pre, has_post = dh is not None, zin is not None

    def body(*refs):
        refs = list(refs)
        i = pl.program_id(0)
        dx = refs.pop(0)[...]
        if has_pre:
            dh_ref, x_ref, g_ref = refs.pop(0), refs.pop(0), refs.pop(0)
            ddx, dg = _rms_bwd(dh_ref[...].astype(F32), x_ref[...], g_ref[...])
            dx = dx + ddx
        if has_post:
            z_ref, gp_ref = refs.pop(0), refs.pop(0)
            dz, dgp = _rms_bwd(dx, z_ref[...], gp_ref[...])
        if has_pre:
            refs.pop(0)[...] = dx
            dg_ref = refs.pop(0)

            @pl.when(i == 0)
            def _():
                dg_ref[...] = jnp.zeros_like(dg_ref)

            dg_ref[...] += _sum8(dg)
        if has_post:
            refs.pop(0)[...] = dz.astype(BF16)
            dgp_ref = refs.pop(0)

            @pl.when(i == 0)
            def _():
                dgp_ref[...] = jnp.zeros_like(dgp_ref)

            dgp_ref[...] += _sum8(dgp)

    ins, specs, outs, ospecs = [dres], [_row_spec(tm, d)], [], []
    if has_pre:
        ins += [dh, xin, g_pre.reshape(1, d)]
        specs += [_row_spec(tm, d), _row_spec(tm, d), _const_spec((1, d))]
        outs += [jax.ShapeDtypeStruct((s, d), F32), jax.ShapeDtypeStruct((8, d), F32)]
        ospecs += [_row_spec(tm, d), _const_spec((8, d))]
    if has_post:
        ins += [zin, g_post.reshape(1, d)]
        specs += [_row_spec(tm, d), _const_spec((1, d))]
        outs += [jax.ShapeDtypeStruct((s, d), BF16), jax.ShapeDtypeStruct((8, d), F32)]
        ospecs += [_row_spec(tm, d), _const_spec((8, d))]
    res = pl.pallas_call(body, name=name, grid=(s // tm,), in_specs=specs, out_specs=ospecs, out_shape=outs,
                         compiler_params=_params("arbitrary"))(*ins)
    return list(res)


def loss_head(x, target):
    s, d = x.shape
    tm = _pick(s, (512, 256, 128))

    def body(x_ref, t_ref, dx_ref, sq_ref):
        i = pl.program_id(0)
        e = x_ref[...] - t_ref[...]
        dx_ref[...] = e * (1.0 / d)

        @pl.when(i == 0)
        def _():
            sq_ref[...] = jnp.zeros_like(sq_ref)

        sq_ref[...] += _sum8(e * e)

    return pl.pallas_call(body, name="loss_head", grid=(s // tm,),
                          in_specs=[_row_spec(tm, d), _row_spec(tm, d)],
                          out_specs=[_row_spec(tm, d), _const_spec((8, d))],
                          out_shape=[jax.ShapeDtypeStruct((s, d), F32), jax.ShapeDtypeStruct((8, d), F32)],
                          compiler_params=_params("arbitrary"))(x, target)


def matmul(name, mode, a, b, *, grid, tk_steps, a_spec, b_spec, o_spec, out_shape, acc_shape, b_flat=None):
    nk = tk_steps

    def rhs(b_ref):
        return b_ref[...] if b_flat is None else b_ref[...].reshape(b_flat)

    def body(a_ref, b_ref, o_ref, *scratch):
        if nk == 1:
            o_ref[...] = _dot(a_ref[...], rhs(b_ref), mode).astype(o_ref.dtype)
            return
        acc = scratch[0]
        kk = pl.program_id(2)

        @pl.when(kk == 0)
        def _():
            acc[...] = jnp.zeros_like(acc)

        acc[...] += _dot(a_ref[...], rhs(b_ref), mode)

        @pl.when(kk == nk - 1)
        def _():
            o_ref[...] = acc[...].astype(o_ref.dtype)

    return pl.pallas_call(body, name=name, grid=grid, in_specs=[a_spec, b_spec], out_specs=o_spec, out_shape=out_shape,
                          scratch_shapes=[] if nk == 1 else [pltpu.VMEM(acc_shape, F32)],
                          compiler_params=_params("parallel", "parallel", "arbitrary"))(a, b)


TM_PREFS = (1024, 512, 256, 128)
TW_PREFS = (1408, 1152, 1024, 512, 384, 256, 128)
TK_PREFS = (512, 384, 256, 128)
FULL_K_PREFS = (2048, 1024) + TK_PREFS


def mm_act_wcols(name, a, wg, layer, out_dtype):
    s, k = a.shape
    g, _, _, ng = wg.shape
    tm, tk, tn = _pick(s, TM_PREFS), _pick(k, FULL_K_PREFS), _pick(ng, TW_PREFS)
    per = ng // tn
    return matmul(name, "nn", a, wg, grid=(g * per, s // tm, k // tk), tk_steps=k // tk,
                  a_spec=pl.BlockSpec((tm, tk), lambda j, i, kk: (i, kk)),
                  b_spec=pl.BlockSpec((None, None, tk, tn), lambda j, i, kk: (j // per, layer, kk, j % per)),
                  o_spec=pl.BlockSpec((tm, tn), lambda j, i, kk: (i, j)),
                  out_shape=jax.ShapeDtypeStruct((s, g * ng), out_dtype), acc_shape=(tm, tn))


def mm_act_wcols_t(name, a, wg, layer, out_dtype):
    s, n = a.shape
    g, _, k, ng = wg.shape
    tm, tn, tk = _pick(s, TM_PREFS), _pick(k, TM_PREFS), _pick(ng, (1152, 384, 128))
    per = ng // tk
    return matmul(name, "nt", a, wg, grid=(s // tm, k // tn, g * per), tk_steps=g * per,
                  a_spec=pl.BlockSpec((tm, tk), lambda i, j, kk: (i, kk)),
                  b_spec=pl.BlockSpec((None, None, tn, tk), lambda i, j, kk: (kk // per, layer, j, kk % per)),
                  o_spec=pl.BlockSpec((tm, tn), lambda i, j, kk: (i, j)),
                  out_shape=jax.ShapeDtypeStruct((s, k), out_dtype), acc_shape=(tm, tn))


def mm_act_wrows(name, a, wr, layer, out_dtype):
    s, k = a.shape
    g, _, kg, n = wr.shape
    tm, tn, tk = _pick(s, TM_PREFS), _pick(n, TM_PREFS), _pick(kg, (1408, 512, 256, 128))
    if k <= FULL_K_PREFS[0]:
        return matmul(name, "nn", a, wr, grid=(n // tn, s // tm, 1), tk_steps=1,
                      a_spec=pl.BlockSpec((tm, k), lambda j, i, kk: (i, 0)),
                      b_spec=pl.BlockSpec((g, None, kg, tn), lambda j, i, kk: (0, layer, 0, j)),
                      o_spec=pl.BlockSpec((tm, tn), lambda j, i, kk: (i, j)),
                      out_shape=jax.ShapeDtypeStruct((s, n), out_dtype), acc_shape=(tm, tn), b_flat=(k, tn))
    per = kg // tk
    return matmul(name, "nn", a, wr, grid=(s // tm, n // tn, g * per), tk_steps=g * per,
                  a_spec=pl.BlockSpec((tm, tk), lambda i, j, kk: (i, kk)),
                  b_spec=pl.BlockSpec((None, None, tk, tn), lambda i, j, kk: (kk // per, layer, kk % per, j)),
                  o_spec=pl.BlockSpec((tm, tn), lambda i, j, kk: (i, j)),
                  out_shape=jax.ShapeDtypeStruct((s, n), out_dtype), acc_shape=(tm, tn))


def mm_act_wrows_t(name, a, wr, layer, out_dtype):
    s, n = a.shape
    g, _, kg, _ = wr.shape
    tm, tn, tk = _pick(s, TM_PREFS), _pick(kg, (1408, 512, 256, 128)), _pick(n, FULL_K_PREFS)
    per = kg // tn
    return matmul(name, "nt", a, wr, grid=(g * per, s // tm, n // tk), tk_steps=n // tk,
                  a_spec=pl.BlockSpec((tm, tk), lambda j, i, kk: (i, kk)),
                  b_spec=pl.BlockSpec((None, None, tn, tk), lambda j, i, kk: (j // per, layer, j % per, kk)),
                  o_spec=pl.BlockSpec((tm, tn), lambda j, i, kk: (i, j)),
                  out_shape=jax.ShapeDtypeStruct((s, g * kg), out_dtype), acc_shape=(tm, tn))


def mm_wgrad_cols(name, a, dy, ng):
    s, k = a.shape
    n = dy.shape[1]
    g = n // ng
    tm = _pick(k, TM_PREFS)
    tk = _pick(s, TM_PREFS)
    tn = _pick(ng, TW_PREFS)
    per = ng // tn
    return matmul(name, "tn", a, dy, grid=(k // tm, g * per, s // tk), tk_steps=s // tk,
                  a_spec=pl.BlockSpec((tk, tm), lambda i, j, kk: (kk, i)),
                  b_spec=pl.BlockSpec((tk, tn), lambda i, j, kk: (kk, j)),
                  o_spec=pl.BlockSpec((None, tm, tn), lambda i, j, kk: (j // per, i, j % per)),
                  out_shape=jax.ShapeDtypeStruct((g, k, ng), BF16), acc_shape=(tm, tn))


def mm_plain(name, mode, a, b, out_dtype):
    if mode == "nn":
        (m, k), n = a.shape, b.shape[1]
    elif mode == "nt":
        (m, k), n = a.shape, b.shape[0]
    else:
        (k, m), n = a.shape, b.shape[1]
    tm = _pick(m, (1408,) + TM_PREFS)
    tn = _pick(n, TM_PREFS)
    tk = _pick(k, TM_PREFS)
    a_spec = (pl.BlockSpec((tk, tm), lambda i, j, kk: (kk, i)) if mode == "tn"
              else pl.BlockSpec((tm, tk), lambda i, j, kk: (i, kk)))
    b_spec = (pl.BlockSpec((tn, tk), lambda i, j, kk: (j, kk)) if mode == "nt"
              else pl.BlockSpec((tk, tn), lambda i, j, kk: (kk, j)))
    return matmul(name, mode, a, b, grid=(m // tm, n // tn, k // tk), tk_steps=k // tk, a_spec=a_spec, b_spec=b_spec,
                  o_spec=pl.BlockSpec((tm, tn), lambda i, j, kk: (i, j)),
                  out_shape=jax.ShapeDtypeStruct((m, n), out_dtype), acc_shape=(tm, tn))


def _silu_parts(gt):
    sg = jax.nn.sigmoid(gt)
    return gt * sg, sg


def ffn_up(name, h, wgate, wup, layer):
    s, d = h.shape
    g, _, _, ng = wgate.shape
    assert d <= FULL_K_PREFS[0]
    tm = _pick(s, (256, 128))
    tn = _pick(ng, TW_PREFS)
    per = ng // tn

    def body(h_ref, wg_ref, wu_ref, gt_ref, up_ref, act_ref):
        hh = h_ref[...]
        gt = _dot(hh, wg_ref[...], "nn")
        up = _dot(hh, wu_ref[...], "nn")
        gt_ref[...] = gt.astype(BF16)
        up_ref[...] = up.astype(BF16)
        act_ref[...] = (_silu_parts(gt)[0] * up).astype(BF16)

    wspec = pl.BlockSpec((None, None, d, tn), lambda j, i: (j // per, layer, 0, j % per))
    ospec = pl.BlockSpec((tm, tn), lambda j, i: (i, j))
    osh = jax.ShapeDtypeStruct((s, g * ng), BF16)
    return pl.pallas_call(body, name=name, grid=(g * per, s // tm),
                          in_specs=[pl.BlockSpec((tm, d), lambda j, i: (i, 0)), wspec, wspec],
                          out_specs=[ospec, ospec, ospec], out_shape=[osh, osh, osh],
                          compiler_params=_params("parallel", "parallel"))(h, wgate, wup)


def ffn_down_bwd(name, df, wdown, layer, gt, up):
    s, d = df.shape
    g, _, fg, _ = wdown.shape
    f = g * fg
    assert d <= FULL_K_PREFS[0]
    tm = _pick(s, (256, 128))
    tn = _pick(fg, TW_PREFS)
    per = fg // tn

    def body(df_ref, wd_ref, gt_ref, up_ref, dgt_ref, dup_ref):
        da = _dot(df_ref[...], wd_ref[...], "nt")
        gt = gt_ref[...].astype(F32)
        up = up_ref[...].astype(F32)
        silu, sg = _silu_parts(gt)
        dgt_ref[...] = (da * up * (sg * (1.0 + gt * (1.0 - sg)))).astype(BF16)
        dup_ref[...] = (da * silu).astype(BF16)

    ospec = pl.BlockSpec((tm, tn), lambda j, i: (i, j))
    osh = jax.ShapeDtypeStruct((s, f), BF16)
    return pl.pallas_call(body, name=name, grid=(f // tn, s // tm),
                          in_specs=[pl.BlockSpec((tm, d), lambda j, i: (i, 0)),
                                    pl.BlockSpec((None, None, tn, d), lambda j, i: (j // per, layer, j % per, 0)),
                                    ospec, ospec],
                          out_specs=[ospec, ospec], out_shape=[osh, osh],
                          compiler_params=_params("parallel", "parallel"))(df, wdown, gt, up)


def ffn_in_bwd(name, dgt, dup, wgate, wup, layer):
    s, f = dgt.shape
    g, _, d, ng = wgate.shape
    tm = _pick(s, (512, 256, 128))
    tn = _pick(d, TM_PREFS)
    tk = _pick(ng, TW_PREFS)
    per = ng // tk
    nk = g * per

    def body(dg_ref, du_ref, wg_ref, wu_ref, o_ref, acc):
        kk = pl.program_id(2)

        @pl.when(kk == 0)
        def _():
            acc[...] = jnp.zeros_like(acc)

        acc[...] += _dot(dg_ref[...], wg_ref[...], "nt") + _dot(du_ref[...], wu_ref[...], "nt")

        @pl.when(kk == nk - 1)
        def _():
            o_ref[...] = acc[...]

    aspec = pl.BlockSpec((tm, tk), lambda i, j, kk: (i, kk))
    wspec = pl.BlockSpec((None, None, tn, tk), lambda i, j, kk: (kk // per, layer, j, kk % per))
    return pl.pallas_call(body, name=name, grid=(s // tm, d // tn, nk), in_specs=[aspec, aspec, wspec, wspec],
                          out_specs=pl.BlockSpec((tm, tn), lambda i, j, kk: (i, j)),
                          out_shape=jax.ShapeDtypeStruct((s, d), F32), scratch_shapes=[pltpu.VMEM((tm, tn), F32)],
                          compiler_params=_params("parallel", "parallel", "arbitrary"))(dgt, dup, wgate, wup)


def _halo_specs(tm, s, cols, cb, halo=HALO):
    r = tm // halo
    last = s // halo - 1
    return [pl.BlockSpec((halo, cols), lambda i: (jnp.maximum(i * r - 1, 0), cb)),
            pl.BlockSpec((tm, cols), lambda i: (i, cb)),
            pl.BlockSpec((halo, cols), lambda i: (jnp.minimum((i + 1) * r, last), cb))]


def _edge_masked(prev_ref, next_ref, i, nt):
    return jnp.where(i > 0, prev_ref[...], 0.0), jnp.where(i < nt - 1, next_ref[...], 0.0)


def _glu(a, gate):
    return a * jax.nn.sigmoid(gate)


def _conv_post(conv, lg, lb, gm):
    mu = jnp.mean(conv, axis=-1, keepdims=True)
    xc = conv - mu
    rs = lax.rsqrt(jnp.mean(xc * xc, axis=-1, keepdims=True) + EPS)
    xhat = xc * rs
    yl = xhat * lg + lb
    ya, sg = _silu_parts(yl)
    return xhat, rs, yl, sg, ya


def _fill_glu(hg, ap, am, an, gp, gm_, gn, i, nt, tm):
    a0, a2 = _edge_masked(ap, an, i, nt)
    g0, g2 = _edge_masked(gp, gn, i, nt)
    hg[pl.ds(0, HALO), :] = _glu(a0, g0)
    hg[pl.ds(HALO, tm), :] = _glu(am[...], gm_[...])
    hg[pl.ds(HALO + tm, HALO), :] = _glu(a2, g2)


def conv_fwd(name, proj, cw, cb, lg, lb, gmix):
    s = proj.shape[0]
    tm = _pick(s, (256, 128))
    nt = s // tm

    def body(ap, am, an, gp, gm_, gn, cw_ref, cb_ref, lg_ref, lb_ref, gx_ref, o_ref, hg):
        i = pl.program_id(0)
        _fill_glu(hg, ap, am, an, gp, gm_, gn, i, nt, tm)
        acc = jnp.zeros((tm, C_A), F32) + cb_ref[...]
        for t in range(CONV_WIDTH):
            acc = acc + cw_ref[pl.ds(t, 1), :] * hg[pl.ds(HALO - CONV_HALF + t, tm), :]
        ya = _conv_post(acc, lg_ref[...], lb_ref[...], None)[4]
        o_ref[...] = (ya * _rms_scale(ya) * gx_ref[...]).astype(BF16)

    vec = _const_spec((1, C_A))
    return pl.pallas_call(body, name=name, grid=(nt,),
                          in_specs=_halo_specs(tm, s, C_A, 0) + _halo_specs(tm, s, C_A, 1)
                          + [_const_spec((32, C_A)), vec, vec, vec, vec],
                          out_specs=_row_spec(tm, C_A), out_shape=jax.ShapeDtypeStruct((s, C_A), BF16),
                          scratch_shapes=[pltpu.VMEM((tm + 2 * HALO, C_A), F32)],
                          compiler_params=_params("parallel"))(proj, proj, proj, proj, proj, proj, cw, cb, lg, lb, gmix)


def conv_bwd(name, proj, dy, cw, cb, lg, lb, gmix):
    s = proj.shape[0]
    tm = _pick(s, (256, 128))
    nt = s // tm
    te = tm + HALO
    off = HALO // 2

    def body(ap, am, an, gp, gm_, gn, dp, dm, dn, cw_ref, cb_ref, lg_ref, lb_ref, gx_ref,
             dag_ref, dcw_ref, dcb_ref, dlg_ref, dlb_ref, dgx_ref, hg, dc, dyx):
        i = pl.program_id(0)
        _fill_glu(hg, ap, am, an, gp, gm_, gn, i, nt, tm)
        d0, d2 = _edge_masked(dp, dn, i, nt)
        dyx[pl.ds(0, HALO), :] = d0
        dyx[pl.ds(HALO, tm), :] = dm[...]
        dyx[pl.ds(HALO + tm, HALO), :] = d2
        lg, lb, gx = lg_ref[...], lb_ref[...], gx_ref[...]
        conv = jnp.zeros((te, C_A), F32) + cb_ref[...]
        for t in range(CONV_WIDTH):
            conv = conv + cw_ref[pl.ds(t, 1), :] * hg[pl.ds(1 + t, te), :]
        xhat, rs, yl, sg, ya = _conv_post(conv, lg, lb, gx)
        dout = dyx[pl.ds(off, te), :]
        dya, dgx_rows = _rms_bwd(dout, ya, gx)
        dyl = dya * (sg * (1.0 + yl * (1.0 - sg)))
        dxh = dyl * lg
        dconv = rs * (dxh - jnp.mean(dxh, axis=-1, keepdims=True)
                      - xhat * jnp.mean(dxh * xhat, axis=-1, keepdims=True))
        dc[...] = dconv

        @pl.when(i == 0)
        def _():
            for r in (dcw_ref, dcb_ref, dlg_ref, dlb_ref, dgx_ref):
                r[...] = jnp.zeros_like(r)

        dcb_ref[...] += _sum8(dconv[off:off + tm])
        dlg_ref[...] += _sum8((dyl * xhat)[off:off + tm])
        dlb_ref[...] += _sum8(dyl[off:off + tm])
        dgx_ref[...] += _sum8(dgx_rows[off:off + tm])
        dcm = dconv[off:off + tm]
        dhg = jnp.zeros((tm, C_A), F32)
        for t in range(CONV_WIDTH):
            dhg = dhg + cw_ref[pl.ds(t, 1), :] * dc[pl.ds(HALO - 1 - t, tm), :]
            dcw_ref[pl.ds(8 * t, 8), :] += _sum8(dcm * hg[pl.ds(HALO - CONV_HALF + t, tm), :])
        a, gate = am[...], gm_[...]
        sgate = jax.nn.sigmoid(gate)
        dag_ref[:, pl.ds(0, C_A)] = (dhg * sgate).astype(BF16)
        dag_ref[:, pl.ds(C_A, C_A)] = (dhg * a * sgate * (1.0 - sgate)).astype(BF16)

    vec = _const_spec((1, C_A))
    acc8 = _const_spec((8, C_A))
    sh8 = jax.ShapeDtypeStruct((8, C_A), F32)
    return pl.pallas_call(body, name=name, grid=(nt,),
                          in_specs=_halo_specs(tm, s, C_A, 0) + _halo_specs(tm, s, C_A, 1) + _halo_specs(tm, s, C_A, 0)
                          + [_const_spec((32, C_A)), vec, vec, vec, vec],
                          out_specs=[_row_spec(tm, 2 * C_A), _const_spec((32 * 8, C_A)), acc8, acc8, acc8, acc8],
                          out_shape=[jax.ShapeDtypeStruct((s, 2 * C_A), BF16), jax.ShapeDtypeStruct((32 * 8, C_A), F32),
                                     sh8, sh8, sh8, sh8],
                          scratch_shapes=[pltpu.VMEM((tm + 2 * HALO, C_A), F32), pltpu.VMEM((te, C_A), F32),
                                          pltpu.VMEM((tm + 2 * HALO, C_A), F32)],
                          compiler_params=_params("arbitrary"))(proj, proj, proj, proj, proj, proj, dy, dy, dy,
                                                                cw, cb, lg, lb, gmix)


def _pool_counts(pos, win, s):
    lo = jnp.maximum(pos - win // 2, 0)
    hi = jnp.minimum(pos + win - win // 2, s)
    return jnp.maximum(hi - lo, 1).astype(F32)


def _pooled(uext, base, rows, pos, s):
    outs = []
    for gi, win in enumerate(POOL_WINDOWS):
        lanes = pl.ds(gi * C_G, C_G)
        acc = jnp.zeros((rows, C_G), F32)
        for o in range(-(win // 2), win - win // 2):
            acc = acc + uext[pl.ds(base + o, rows), lanes]
        outs.append(acc / _pool_counts(pos, win, s) - uext[pl.ds(base, rows), lanes])
    return outs


def pool_fwd(name, proj, pw, scale, gmix):
    s = proj.shape[0]
    tm = _pick(s, (256, 128))
    nt = s // tm
    ucol = (proj.shape[1] - C_C) // C_C

    def body(up, um, un, pw_ref, sc_ref, gx_ref, o_ref, uext):
        i = pl.program_id(0)
        u0, u2 = _edge_masked(up, un, i, nt)
        uext[pl.ds(0, HALO), :] = u0
        uext[pl.ds(HALO, tm), :] = um[...]
        uext[pl.ds(HALO + tm, HALO), :] = u2
        pos = i * tm + lax.broadcasted_iota(jnp.int32, (tm, 1), 0)
        pooled = _pooled(uext, HALO, tm, pos, s)
        mixed = jnp.concatenate([_dot(pooled[g].astype(BF16), pw_ref[g], "nn") for g in range(4)], axis=1)
        yc = mixed * sc_ref[...]
        o_ref[...] = (yc * _rms_scale(yc) * gx_ref[...]).astype(BF16)

    vec = _const_spec((1, C_C))
    return pl.pallas_call(body, name=name, grid=(nt,),
                          in_specs=_halo_specs(tm, s, C_C, ucol) + [_const_spec((4, C_G, C_G)), vec, vec],
                          out_specs=_row_spec(tm, C_C), out_shape=jax.ShapeDtypeStruct((s, C_C), BF16),
                          scratch_shapes=[pltpu.VMEM((tm + 2 * HALO, C_C), F32)],
                          compiler_params=_params("parallel"))(proj, proj, proj, pw, scale, gmix)


def pool_bwd(name, proj, dy, pw, scale, gmix):
    s = proj.shape[0]
    tm = _pick(s, (256, 128))
    nt = s // tm
    te = tm + HALO
    off = HALO // 2
    ucol = (proj.shape[1] - C_C) // C_C
    dcol = (dy.shape[1] - C_C) // C_C

    def body(up, um, un, dp, dm, dn, pw_ref, sc_ref, gx_ref, du_ref, dpw_ref, dsc_ref, dgx_ref, uext, dyx, qs, dps):
        i = pl.program_id(0)
        u0, u2 = _edge_masked(up, un, i, nt)
        uext[pl.ds(0, HALO), :] = u0
        uext[pl.ds(HALO, tm), :] = um[...]
        uext[pl.ds(HALO + tm, HALO), :] = u2
        d0, d2 = _edge_masked(dp, dn, i, nt)
        dyx[pl.ds(0, HALO), :] = d0
        dyx[pl.ds(HALO, tm), :] = dm[...]
        dyx[pl.ds(HALO + tm, HALO), :] = d2
        pos = i * tm - off + lax.broadcasted_iota(jnp.int32, (te, 1), 0)
        pooled = _pooled(uext, off, te, pos, s)
        mixed = jnp.concatenate([_dot(pooled[g].astype(BF16), pw_ref[g], "nn") for g in range(4)], axis=1)
        sc = sc_ref[...]
        yc = mixed * sc
        dyc, dgx_rows = _rms_bwd(dyx[pl.ds(off, te), :], yc, gx_ref[...])
        dmixed = dyc * sc

        @pl.when(i == 0)
        def _():
            for r in (dpw_ref, dsc_ref, dgx_ref):
                r[...] = jnp.zeros_like(r)

        dsc_ref[...] += _sum8((dyc * mixed)[off:off + tm])
        dgx_ref[...] += _sum8(dgx_rows[off:off + tm])
        for gi, win in enumerate(POOL_WINDOWS):
            lanes = pl.ds(gi * C_G, C_G)
            dmg = dmixed[:, gi * C_G:(gi + 1) * C_G].astype(BF16)
            dpw_ref[gi] += _dot(pooled[gi][off:off + tm].astype(BF16), dmg[off:off + tm], "tn")
            dpl = _dot(dmg, pw_ref[gi], "nt")
            dps[:, lanes] = dpl
            qs[:, lanes] = dpl / _pool_counts(pos, win, s)
        for gi, win in enumerate(POOL_WINDOWS):
            lanes = pl.ds(gi * C_G, C_G)
            acc = jnp.zeros((tm, C_G), F32) - dps[pl.ds(off, tm), lanes]
            for o in range(-(win // 2) + 1, win // 2 + 1):
                acc = acc + qs[pl.ds(off + o, tm), lanes]
            du_ref[:, lanes] = acc.astype(BF16)

    vec = _const_spec((1, C_C))
    acc8 = _const_spec((8, C_C))
    sh8 = jax.ShapeDtypeStruct((8, C_C), F32)
    return pl.pallas_call(body, name=name, grid=(nt,),
                          in_specs=_halo_specs(tm, s, C_C, ucol) + _halo_specs(tm, s, C_C, dcol)
                          + [_const_spec((4, C_G, C_G)), vec, vec],
                          out_specs=[_row_spec(tm, C_C), _const_spec((4, C_G, C_G)), acc8, acc8],
                          out_shape=[jax.ShapeDtypeStruct((s, C_C), BF16), jax.ShapeDtypeStruct((4, C_G, C_G), F32),
                                     sh8, sh8],
                          scratch_shapes=[pltpu.VMEM((tm + 2 * HALO, C_C), F32), pltpu.VMEM((tm + 2 * HALO, C_C), F32),
                                          pltpu.VMEM((te, C_C), F32), pltpu.VMEM((te, C_C), F32)],
                          compiler_params=_params("arbitrary"))(proj, proj, proj, dy, dy, dy, pw, scale, gmix)


def rope_tables(s):
    pos = jnp.arange(s, dtype=F32)
    inv = ROPE_THETA ** (-jnp.arange(0, ROT_DIM, 2, dtype=F32) / ROT_DIM)
    ang = pos[:, None] * inv[None, :]
    cos, sin = jnp.cos(ang), jnp.sin(ang)
    half = ROT_DIM // 2
    rest = HEAD_DIM - ROT_DIM
    c = jnp.concatenate([cos, cos, jnp.ones((s, rest), F32)], axis=1)
    sa = jnp.concatenate([-sin, jnp.zeros((s, HEAD_DIM - half), F32)], axis=1)
    sb = jnp.concatenate([jnp.zeros((s, half), F32), sin, jnp.zeros((s, rest), F32)], axis=1)
    return tuple(jnp.concatenate([t, t], axis=1) for t in (c, sa, sb))


def rope_fwd(name, proj, tabs):
    s = proj.shape[0]
    tm = _pick(s, (256, 128))
    scale = HEAD_DIM ** -0.5

    def body(q_ref, k_ref, v_ref, c_ref, sa_ref, sb_ref, qo, ko, vo):
        c, sa, sb = c_ref[...], sa_ref[...], sb_ref[...]
        for j in range(C_B // LANES):
            lanes = pl.ds(j * LANES, LANES)
            for src, dst, mul in ((q_ref, qo, scale), (k_ref, ko, 1.0)):
                t = src[:, lanes]
                r = t * c + pltpu.roll(t, LANES - ROT_DIM // 2, 1) * sa + pltpu.roll(t, ROT_DIM // 2, 1) * sb
                dst[:, lanes] = (r * mul).astype(BF16)
        vo[...] = v_ref[...].astype(BF16)

    tab = _row_spec(tm, LANES)
    osh = jax.ShapeDtypeStruct((s, C_B), BF16)
    return pl.pallas_call(body, name=name, grid=(s // tm,),
                          in_specs=[_row_spec(tm, C_B, 1), _row_spec(tm, C_B, 2), _row_spec(tm, C_B, 3), tab, tab, tab],
                          out_specs=[_row_spec(tm, C_B)] * 3, out_shape=[osh, osh, osh],
                          compiler_params=_params("parallel"))(proj, proj, proj, *tabs)


def rope_bwd(name, dqs, dks, dvs, tabs):
    s = dqs[0].shape[0]
    tm = _pick(s, (256, 128))
    scale = HEAD_DIM ** -0.5

    def body(q1, q2, q3, k1, k2, k3, v1, v2, v3, c_ref, sa_ref, sb_ref, o_ref):
        c, sa, sb = c_ref[...], sa_ref[...], sb_ref[...]
        for j in range(C_B // LANES):
            lanes = pl.ds(j * LANES, LANES)
            for which, (r1, r2, r3), mul in ((0, (q1, q2, q3), scale), (1, (k1, k2, k3), 1.0)):
                d = r1[:, lanes].astype(F32) + r2[:, lanes].astype(F32) + r3[:, lanes].astype(F32)
                t = d * c + pltpu.roll(d * sa, ROT_DIM // 2, 1) + pltpu.roll(d * sb, LANES - ROT_DIM // 2, 1)
                o_ref[:, pl.ds(which * C_B + j * LANES, LANES)] = (t * mul).astype(BF16)
        o_ref[:, pl.ds(2 * C_B, C_B)] = (v1[...].astype(F32) + v2[...].astype(F32) + v3[...].astype(F32)).astype(BF16)

    tab = _row_spec(tm, LANES)
    return pl.pallas_call(body, name=name, grid=(s // tm,), in_specs=[_row_spec(tm, C_B)] * 9 + [tab, tab, tab],
                          out_specs=_row_spec(tm, 3 * C_B), out_shape=jax.ShapeDtypeStruct((s, 3 * C_B), BF16),
                          compiler_params=_params("parallel"))(*dqs, *dks, *dvs, *tabs)


def to_dilated(t, d):
    if d == 1:
        return t
    s, c = t.shape
    return t.reshape(s // d, d, c).transpose(1, 0, 2).reshape(s, c)


def from_dilated(t, d):
    if d == 1:
        return t
    s, c = t.shape
    return t.reshape(d, s // d, c).transpose(1, 0, 2).reshape(s, c)


def _window_specs(tq, s, cols):
    r = tq // ATT_HALF
    last = s // ATT_HALF - 1
    return [pl.BlockSpec((ATT_HALF, cols), lambda i: (jnp.maximum(i * r - 1, 0), 0)),
            pl.BlockSpec((tq, cols), lambda i: (i, 0)),
            pl.BlockSpec((ATT_HALF, cols), lambda i: (jnp.minimum((i + 1) * r, last), 0))]


def _fill_window(win, prev_ref, main_ref, next_ref, tq):
    win[pl.ds(0, ATT_HALF), :] = prev_ref[...]
    win[pl.ds(ATT_HALF, tq), :] = main_ref[...]
    win[pl.ds(ATT_HALF + tq, ATT_HALF), :] = next_ref[...]


def _band_valid(tile, window, j0, seg, tile_is_rows):
    shape = (2 * tile, window) if tile_is_rows else (2 * window, tile)
    ri = lax.broadcasted_iota(jnp.int32, shape, 0)
    ci = lax.broadcasted_iota(jnp.int32, shape, 1)
    per_head = tile if tile_is_rows else window
    ri = jnp.where(ri >= per_head, ri - per_head, ri)
    ti, wi = (ri, ci) if tile_is_rows else (ci, ri)
    jw = j0 - ATT_HALF + wi
    return (jnp.abs(wi - ATT_HALF - ti) <= ATT_HALF) & (jw >= 0) & (jw < seg)


def _first_head():
    return lax.broadcasted_iota(jnp.int32, (1, LANES), 1) < HEAD_DIM


def _stack_heads(t, first):
    z = jnp.zeros_like(t)
    return jnp.concatenate([jnp.where(first, t, z), jnp.where(first, z, t)], axis=0)


def _unstack_heads(t2, first, rows):
    return jnp.where(first, t2[:rows], t2[rows:])


def _head_columns(ref, pair):
    return jnp.concatenate([ref[:, pl.ds(pair * LANES, 1)], ref[:, pl.ds(pair * LANES + HEAD_DIM, 1)]], axis=0)


def band_attn_fwd(name, q, k, v, seg):
    s, c = q.shape
    tq = _pick(seg, (256, 128))
    per_seg = seg // tq

    wrows = tq + 2 * ATT_HALF

    def body(q_ref, kp, km, kn, vp, vm, vn, o_ref, lse_ref, kw, vw):
        j0 = (pl.program_id(0) % per_seg) * tq
        _fill_window(kw, kp, km, kn, tq)
        _fill_window(vw, vp, vm, vn, tq)
        valid = _band_valid(tq, wrows, j0, seg, True)
        first = _first_head()
        for pair in range(c // LANES):
            lanes = pl.ds(pair * LANES, LANES)
            sc = jnp.where(valid, _dot(_stack_heads(q_ref[:, lanes], first), kw[:, lanes], "nt"), NEG)
            m = jnp.max(sc, axis=-1, keepdims=True)
            p = jnp.exp(sc - m)
            l = jnp.sum(p, axis=-1, keepdims=True)
            o = _dot(p.astype(BF16), vw[:, lanes], "nn") / l
            o_ref[:, lanes] = _unstack_heads(o, first, tq).astype(BF16)
            lse_ref[:, lanes] = _unstack_heads(m + jnp.log(l), first, tq)

    win = _window_specs(tq, s, c)
    tile = _row_spec(tq, c)
    return pl.pallas_call(body, name=name, grid=(s // tq,), in_specs=[tile] + win + win, out_specs=[tile, tile],
                          out_shape=[jax.ShapeDtypeStruct((s, c), BF16), jax.ShapeDtypeStruct((s, c), F32)],
                          scratch_shapes=[pltpu.VMEM((wrows, c), BF16)] * 2,
                          compiler_params=_params("parallel"))(q, k, k, k, v, v, v)


def band_attn_dq(name, q, k, v, do, lse, delta, seg):
    s, c = q.shape
    tq = _pick(seg, (256, 128))
    per_seg = seg // tq

    wrows = tq + 2 * ATT_HALF

    def body(q_ref, kp, km, kn, vp, vm, vn, do_ref, lse_ref, dl_ref, dq_ref, kw, vw):
        j0 = (pl.program_id(0) % per_seg) * tq
        _fill_window(kw, kp, km, kn, tq)
        _fill_window(vw, vp, vm, vn, tq)
        valid = _band_valid(tq, wrows, j0, seg, True)
        first = _first_head()
        for pair in range(c // LANES):
            lanes = pl.ds(pair * LANES, LANES)
            sc = _dot(_stack_heads(q_ref[:, lanes], first), kw[:, lanes], "nt")
            p = jnp.where(valid, jnp.exp(sc - _head_columns(lse_ref, pair)), 0.0)
            dp = _dot(_stack_heads(do_ref[:, lanes], first), vw[:, lanes], "nt")
            ds = p * (dp - _head_columns(dl_ref, pair))
            dq = _dot(ds.astype(BF16), kw[:, lanes], "nn")
            dq_ref[:, lanes] = _unstack_heads(dq, first, tq).astype(BF16)

    win = _window_specs(tq, s, c)
    tile = _row_spec(tq, c)
    return pl.pallas_call(body, name=name, grid=(s // tq,), in_specs=[tile] + win + win + [tile, tile, tile],
                          out_specs=tile, out_shape=jax.ShapeDtypeStruct((s, c), BF16),
                          scratch_shapes=[pltpu.VMEM((wrows, c), BF16)] * 2,
                          compiler_params=_params("parallel"))(q, k, k, k, v, v, v, do, lse, delta)


def band_attn_dkv(name, q, k, v, do, lse, delta, seg):
    s, c = q.shape
    tk = _pick(seg, (256, 128))
    per_seg = seg // tk

    wrows = tk + 2 * ATT_HALF

    def body(k_ref, v_ref, qp, qm, qn, dop, dom, don, lp, lm, ln, dlp, dlm, dln, dk_ref, dv_ref, qw, dow, lsew, dlw):
        j0 = (pl.program_id(0) % per_seg) * tk
        _fill_window(qw, qp, qm, qn, tk)
        _fill_window(dow, dop, dom, don, tk)
        _fill_window(lsew, lp, lm, ln, tk)
        _fill_window(dlw, dlp, dlm, dln, tk)
        valid = _band_valid(tk, wrows, j0, seg, False)
        first = _first_head()
        for pair in range(c // LANES):
            lanes = pl.ds(pair * LANES, LANES)
            qq = _stack_heads(qw[:, lanes], first)
            dd = _stack_heads(dow[:, lanes], first)
            sc = _dot(qq, k_ref[:, lanes], "nt")
            p = jnp.where(valid, jnp.exp(sc - _head_columns(lsew, pair)), 0.0)
            dv_ref[:, lanes] = _dot(p.astype(BF16), dd, "tn").astype(BF16)
            dp = _dot(dd, v_ref[:, lanes], "nt")
            ds = p * (dp - _head_columns(dlw, pair))
            dk_ref[:, lanes] = _dot(ds.astype(BF16), qq, "tn").astype(BF16)

    win = _window_specs(tk, s, c)
    tile = _row_spec(tk, c)
    osh = jax.ShapeDtypeStruct((s, c), BF16)
    return pl.pallas_call(body, name=name, grid=(s // tk,), in_specs=[tile, tile] + win * 4,
                          out_specs=[tile, tile], out_shape=[osh, osh],
                          scratch_shapes=[pltpu.VMEM((wrows, c), BF16)] * 2 + [pltpu.VMEM((wrows, c), F32)] * 2,
                          compiler_params=_params("parallel"))(k, v, q, q, q, do, do, do, lse, lse, lse,
                                                               delta, delta, delta)


def attn_combine_fwd(name, os_, lses, gmix):
    s, c = os_[0].shape
    tm = _pick(s, (256, 128))

    def body(o1, o2, o3, l1, l2, l3, gx_ref, yb_ref, lse_ref, y_ref):
        la, lb, lc = l1[...], l2[...], l3[...]
        m = jnp.maximum(jnp.maximum(la, lb), lc)
        ea, eb, ec = jnp.exp(la - m), jnp.exp(lb - m), jnp.exp(lc - m)
        den = ea + eb + ec
        yb = (ea * o1[...].astype(F32) + eb * o2[...].astype(F32) + ec * o3[...].astype(F32)) / den
        yb_ref[...] = yb
        lse_ref[...] = m + jnp.log(den)
        y_ref[...] = (yb * _rms_scale(yb) * gx_ref[...]).astype(BF16)

    row = _row_spec(tm, c)
    return pl.pallas_call(body, name=name, grid=(s // tm,), in_specs=[row] * 6 + [_const_spec((1, c))],
                          out_specs=[row, row, row],
                          out_shape=[jax.ShapeDtypeStruct((s, c), F32), jax.ShapeDtypeStruct((s, c), F32),
                                     jax.ShapeDtypeStruct((s, c), BF16)],
                          compiler_params=_params("parallel"))(*os_, *lses, gmix)


def attn_combine_bwd(name, dy, yb, gmix):
    s, c = yb.shape
    tm = _pick(s, (256, 128))
    half = c // 2

    def body(d1, d2, yb_ref, gx_ref, dyb_ref, dl_ref, dgx_ref):
        i = pl.program_id(0)
        d = jnp.concatenate([d1[...], d2[...]], axis=1)
        yb_ = yb_ref[...]
        dyb, dgx_rows = _rms_bwd(d, yb_, gx_ref[...])
        dyb_ref[...] = dyb.astype(BF16)

        @pl.when(i == 0)
        def _():
            dgx_ref[...] = jnp.zeros_like(dgx_ref)

        dgx_ref[...] += _sum8(dgx_rows)
        ri = lax.broadcasted_iota(jnp.int32, (LANES, LANES), 0) // HEAD_DIM
        ci = lax.broadcasted_iota(jnp.int32, (LANES, LANES), 1) // HEAD_DIM
        same_head = jnp.where(ri == ci, 1.0, 0.0).astype(BF16)
        prod = dyb * yb_
        for j in range(c // LANES):
            pj = prod[:, j * LANES:(j + 1) * LANES]
            hi = pj.astype(BF16)
            lo = (pj - hi.astype(F32)).astype(BF16)
            dl_ref[:, pl.ds(j * LANES, LANES)] = _dot(hi, same_head, "nn") + _dot(lo, same_head, "nn")

    row = _row_spec(tm, c)
    return pl.pallas_call(body, name=name, grid=(s // tm,),
                          in_specs=[_row_spec(tm, half, 1), _row_spec(tm, half, 2), row, _const_spec((1, c))],
                          out_specs=[row, row, _const_spec((8, c))],
                          out_shape=[jax.ShapeDtypeStruct((s, c), BF16), jax.ShapeDtypeStruct((s, c), F32),
                                     jax.ShapeDtypeStruct((8, c), F32)],
                          compiler_params=_params("arbitrary"))(dy, dy, yb, gmix)


def _ew_rows(rows, cols, n_bufs):
    budget = 24 * 1024 * 1024 // (8 * n_bufs * cols)
    return _pick(rows, tuple(t for t in (2048, 1024, 512, 256, 128, 64, 32, 16, 8) if t <= max(budget, 8)))


def elementwise(name, fn, ins, out_dtypes):
    shape = ins[0].shape
    cols = shape[-1]
    rows = math.prod(shape[:-1])
    tr = _ew_rows(rows, cols, len(ins) + len(out_dtypes))
    n_in = len(ins)

    def body(*refs):
        outs = fn(*[r[...] for r in refs[:n_in]])
        for o_ref, o in zip(refs[n_in:], outs):
            o_ref[...] = o.astype(o_ref.dtype)

    spec = _row_spec(tr, cols)
    res = pl.pallas_call(body, name=name, grid=(rows // tr,), in_specs=[spec] * n_in, out_specs=[spec] * len(out_dtypes),
                         out_shape=[jax.ShapeDtypeStruct((rows, cols), dt) for dt in out_dtypes],
                         compiler_params=_params("parallel"))(*[t.reshape(rows, cols) for t in ins])
    return [r.reshape(shape) for r in res]


def _adamw(g, w, m, v):
    m = ADAM_B1 * m + (1.0 - ADAM_B1) * g
    v = ADAM_B2 * v + (1.0 - ADAM_B2) * (g * g)
    m_hat = m / (1.0 - ADAM_B1 ** ADAM_STEP)
    v_hat = v / (1.0 - ADAM_B2 ** ADAM_STEP)
    delta = -ADAM_LR * (m_hat / (jnp.sqrt(v_hat) + ADAM_EPS) + ADAM_WD * w)
    return delta, m, v


def cast_into_gathered(w, place):
    lyr, r, c = w.shape
    tr = _ew_rows(r, c, 2)

    def body(p_ref, w_ref, o_ref):
        o_ref[...] = w_ref[...].astype(BF16)

    grid_spec = pltpu.PrefetchScalarGridSpec(
        num_scalar_prefetch=1, grid=(lyr, r // tr),
        in_specs=[pl.BlockSpec((None, tr, c), lambda l, i, p: (l, i, 0))],
        out_specs=pl.BlockSpec((None, None, tr, c), lambda l, i, p: (p[0], l, i, 0)))
    return pl.pallas_call(body, name="cast_into_gathered", grid_spec=grid_spec,
                          out_shape=jax.ShapeDtypeStruct((N_GROUPS, lyr, r, c), BF16),
                          compiler_params=_params("parallel", "parallel"))(place, w)


def sum_core_halves(name, grad, theirs, place):
    g, r, c = grad.shape
    h = r // 2
    tr = _ew_rows(h, c, 3)
    nh = h // tr

    def body(p_ref, a_ref, b_ref, o_ref):
        o_ref[...] = (a_ref[...].astype(F32) + b_ref[...].astype(F32)).astype(BF16)

    blk = pl.BlockSpec((None, tr, c), lambda gi, i, p: (gi, i, 0))
    grid_spec = pltpu.PrefetchScalarGridSpec(
        num_scalar_prefetch=1, grid=(g, nh),
        in_specs=[pl.BlockSpec((None, tr, c), lambda gi, i, p: (gi, p[1] * nh + i, 0)), blk], out_specs=blk)
    return pl.pallas_call(body, name=name, grid_spec=grid_spec, out_shape=jax.ShapeDtypeStruct((g, h, c), BF16),
                          compiler_params=_params("parallel", "parallel"))(place, grad, theirs)


def sum_chip_partials(name, parts, recv, place, layer, n_layers, into):
    _, h, c = parts.shape
    tr = _ew_rows(h, c, 5)
    nh = h // tr

    def body(p_ref, o_ref, r0, r1, r2, *rest):
        rest[-1][...] = ((o_ref[...].astype(F32) + r0[...].astype(F32)) + r1[...].astype(F32)) + r2[...].astype(F32)

    in_specs = [pl.BlockSpec((None, tr, c), lambda i, p: (p[0], i, 0))]
    in_specs += [pl.BlockSpec((None, tr, c), lambda i, p, j=j: (j, i, 0)) for j in range(3)]
    args = [place, parts, recv, recv, recv]
    aliases = {}
    if into is not None:
        in_specs.append(ANY)
        args.append(into)
        aliases = {5: 0}
    grid_spec = pltpu.PrefetchScalarGridSpec(
        num_scalar_prefetch=1, grid=(nh,), in_specs=in_specs,
        out_specs=pl.BlockSpec((None, tr, c), lambda i, p: (layer, p[1] * nh + i, 0)))
    return pl.pallas_call(body, name=name, grid_spec=grid_spec, input_output_aliases=aliases,
                          out_shape=jax.ShapeDtypeStruct((n_layers, 2 * h, c), F32),
                          compiler_params=_params("parallel"))(*args)


def small_update(name, gall, w, m, v):
    _, rows, cols = gall.shape
    tr = _ew_rows(rows, cols, 16)

    def body(g_ref, w_ref, m_ref, v_ref, go, do, mo, vo):
        g = g_ref[0]
        for dev in range(1, 8):
            g = g + g_ref[dev]
        delta, mn, vn = _adamw(g, w_ref[...], m_ref[...], v_ref[...])
        go[...] = g
        do[...] = delta
        mo[...] = mn
        vo[...] = vn

    spec = _row_spec(tr, cols)
    sh = jax.ShapeDtypeStruct((rows, cols), F32)
    return pl.pallas_call(body, name=name, grid=(rows // tr,),
                          in_specs=[pl.BlockSpec((8, tr, cols), lambda i: (0, i, 0)), spec, spec, spec],
                          out_specs=[spec] * 4, out_shape=[sh] * 4, compiler_params=_params("parallel"))(gall, w, m, v)


def _place():
    x, y, c = lax.axis_index("x"), lax.axis_index("y"), lax.axis_index("c")
    chips = [(1 - x, y), (x, 1 - y), (1 - x, 1 - y)]
    return x, y, c, chips


def _remote(src, dst, send_sems, recv_sems, k, to):
    return pltpu.make_async_remote_copy(src_ref=src, dst_ref=dst, send_sem=send_sems.at[k], recv_sem=recv_sems.at[k],
                                        device_id=to, device_id_type=MESH)


ANY = pl.BlockSpec(memory_space=pl.ANY)


def allgather_small(name, block):
    m_per, n = block.shape

    def body(x_ref, out_ref, send_sems, recv_sems, local_sem):
        x, y, c, chips = _place()
        me, sibling = (x, y, c), (x, y, 1 - c)

        def rows(px, py, pc):
            return out_ref.at[pl.ds((4 * px + 2 * py + pc) * m_per, m_per), :]

        def copy(k, blk, to, src=None):
            return _remote(rows(*blk) if src is None else src, rows(*blk), send_sems, recv_sems, k, to)

        mine = pltpu.make_async_copy(x_ref, rows(*me), local_sem)
        mine.start()
        first = [copy(0, me, sibling, src=x_ref)]
        first += [copy(1 + j, me, (*chip, c), src=x_ref) for j, chip in enumerate(chips)]
        for cp in first:
            cp.start()
        passed = [copy(4 + j, (*chip, c), sibling) for j, chip in enumerate(chips)]
        for j, chip in enumerate(chips):
            copy(1 + j, (*chip, c), me).wait_recv()
            passed[j].start()
        copy(0, sibling, me).wait_recv()
        for j, chip in enumerate(chips):
            copy(4 + j, (*chip, 1 - c), me).wait_recv()
        for cp in first + passed:
            cp.wait_send()
        mine.wait()

    return pl.pallas_call(body, name=name, out_shape=jax.ShapeDtypeStruct((8 * m_per, n), block.dtype),
                          in_specs=[pl.BlockSpec(memory_space=pltpu.VMEM)], out_specs=pl.BlockSpec(memory_space=pltpu.VMEM),
                          scratch_shapes=[pltpu.SemaphoreType.DMA((7,)), pltpu.SemaphoreType.DMA((7,)),
                                          pltpu.SemaphoreType.DMA],
                          compiler_params=pltpu.CompilerParams(vmem_limit_bytes=VMEM_LIMIT))(block)


def gather_weights(bufs):
    n = len(bufs)
    halves = [t.shape[2] // 2 for t in bufs]

    def body(*refs):
        outs = refs[n:2 * n]
        send_sems, recv_sems = refs[2 * n:]
        x, y, c, chips = _place()
        g0 = 2 * x + y
        sibling = (x, y, 1 - c)
        groups = [2 * cx + cy for cx, cy in chips]

        def half(a, cc):
            return pl.ds(cc * halves[a], halves[a])

        def ici(a, j):
            mine = outs[a].at[g0, :, half(a, c), :]
            return _remote(mine, mine, send_sems, recv_sems, 3 * a + j, (*chips[j], c))

        def ici_arrival(a, j):
            landed = outs[a].at[groups[j], :, half(a, c), :]
            return _remote(landed, landed, send_sems, recv_sems, 3 * a + j, (*chips[j], c))

        def passed(a, j, cc):
            landed = outs[a].at[groups[j], :, half(a, cc), :]
            return _remote(landed, landed, send_sems, recv_sems, 3 * n + 3 * a + j, sibling)

        for a in range(n):
            for j in range(3):
                ici(a, j).start()
        for a in range(n):
            for j in range(3):
                ici_arrival(a, j).wait_recv()
                passed(a, j, c).start()
        for a in range(n):
            for j in range(3):
                passed(a, j, 1 - c).wait_recv()
        for a in range(n):
            for j in range(3):
                ici(a, j).wait_send()
                passed(a, j, c).wait_send()

    return pl.pallas_call(body, name="gather_weights",
                          out_shape=[jax.ShapeDtypeStruct(t.shape, t.dtype) for t in bufs],
                          in_specs=[ANY] * n, out_specs=[ANY] * n, input_output_aliases={a: a for a in range(n)},
                          scratch_shapes=[pltpu.SemaphoreType.DMA((6 * n,)), pltpu.SemaphoreType.DMA((6 * n,))])(*bufs)


def exchange_core_halves(name, grads):
    n = len(grads)
    halves = [t.shape[1] // 2 for t in grads]

    def body(*refs):
        ins, theirs = refs[:n], refs[n:2 * n]
        send_sems, recv_sems = refs[2 * n:]
        x, y, c, _ = _place()
        sibling = (x, y, 1 - c)

        def give(a, g):
            return _remote(ins[a].at[g, pl.ds((1 - c) * halves[a], halves[a]), :], theirs[a].at[g], send_sems,
                           recv_sems, N_GROUPS * a + g, sibling)

        for a in range(n):
            for g in range(N_GROUPS):
                give(a, g).start()
        for a in range(n):
            for g in range(N_GROUPS):
                give(a, g).wait_recv()
        for a in range(n):
            for g in range(N_GROUPS):
                give(a, g).wait_send()

    shapes = [jax.ShapeDtypeStruct((t.shape[0], h, t.shape[2]), t.dtype) for t, h in zip(grads, halves)]
    k = N_GROUPS * n
    return pl.pallas_call(body, name=name, out_shape=shapes, in_specs=[ANY] * n, out_specs=[ANY] * n,
                          scratch_shapes=[pltpu.SemaphoreType.DMA((k,)), pltpu.SemaphoreType.DMA((k,))])(*grads)


def exchange_chip_partials(name, parts):
    n = len(parts)

    def body(*refs):
        ins, recv = refs[:n], refs[n:2 * n]
        send_sems, recv_sems = refs[2 * n:]
        x, y, c, chips = _place()

        def give(a, j):
            return _remote(ins[a].at[2 * chips[j][0] + chips[j][1]], recv[a].at[j], send_sems, recv_sems, 3 * a + j,
                           (*chips[j], c))

        for a in range(n):
            for j in range(3):
                give(a, j).start()
        for a in range(n):
            for j in range(3):
                give(a, j).wait_recv()
        for a in range(n):
            for j in range(3):
                give(a, j).wait_send()

    recv_sh = [jax.ShapeDtypeStruct((3,) + t.shape[1:], t.dtype) for t in parts]
    return pl.pallas_call(body, name=name, out_shape=recv_sh, in_specs=[ANY] * n, out_specs=[ANY] * n,
                          scratch_shapes=[pltpu.SemaphoreType.DMA((3 * n,)), pltpu.SemaphoreType.DMA((3 * n,))])(*parts)


def share_reduced_halves(bufs):
    n = len(bufs)
    n_layers = bufs[0].shape[0]
    halves = [t.shape[1] // 2 for t in bufs]

    def body(*refs):
        outs = refs[n:2 * n]
        send_sems, recv_sems = refs[2 * n:]
        x, y, c, _ = _place()
        sibling = (x, y, 1 - c)

        def give(l, a, cc):
            part = outs[a].at[l, pl.ds(cc * halves[a], halves[a]), :]
            return _remote(part, part, send_sems, recv_sems, l * n + a, sibling)

        for l in range(n_layers):
            for a in range(n):
                give(l, a, c).start()
        for l in range(n_layers):
            for a in range(n):
                give(l, a, 1 - c).wait_recv()
        for l in range(n_layers):
            for a in range(n):
                give(l, a, c).wait_send()

    k = n_layers * n
    return pl.pallas_call(body, name="share_reduced_halves", out_shape=[jax.ShapeDtypeStruct(t.shape, t.dtype) for t in bufs],
                          in_specs=[ANY] * n, out_specs=[ANY] * n, input_output_aliases={a: a for a in range(n)},
                          scratch_shapes=[pltpu.SemaphoreType.DMA((k,)), pltpu.SemaphoreType.DMA((k,))])(*bufs)


def reduce_layer(grads, place, layer, n_layers, into):
    theirs = exchange_core_halves("exchange_core_halves", grads)
    parts = [sum_core_halves("sum_core_halves", g, t, place) for g, t in zip(grads, theirs)]
    recv = exchange_chip_partials("exchange_chip_partials", parts)
    return [sum_chip_partials("sum_chip_partials", p, r, place, layer, n_layers, None if into is None else into[a])
            for a, (p, r) in enumerate(zip(parts, recv))]


def _pack(parts):
    flat = jnp.concatenate([p.reshape(-1) for p in parts])
    pad = (-flat.shape[0]) % (8 * LANES)
    if pad:
        flat = jnp.concatenate([flat, jnp.zeros((pad,), flat.dtype)])
    return flat.reshape(-1, LANES)


def _unpack(buf, shapes):
    flat = buf.reshape(-1)
    out, at = [], 0
    for sh in shapes:
        size = math.prod(sh)
        out.append(flat[at:at + size].reshape(sh))
        at += size
    return out


def kernel(x, w_in, conv_w, conv_b, conv_ln_g, conv_ln_b, pool_w, pool_scale, g_mix, w_out, g_pre_mix, g_post_mix, g_pre_ffn, g_post_ffn, w_gate, w_up, w_down, loss_target, m_w_in, m_conv_w, m_conv_b, m_conv_ln_g, m_conv_ln_b, m_pool_w, m_pool_scale, m_g_mix, m_w_out, m_g_pre_mix, m_g_post_mix, m_g_pre_ffn, m_g_post_ffn, m_w_gate, m_w_up, m_w_down, v_w_in, v_conv_w, v_conv_b, v_conv_ln_g, v_conv_ln_b, v_pool_w, v_pool_scale, v_g_mix, v_w_out, v_g_pre_mix, v_g_post_mix, v_g_pre_ffn, v_g_post_ffn, v_w_gate, v_w_up, v_w_down):
    xs, target = x[0], loss_target[0]
    s, d = xs.shape
    n_layers = w_in.shape[0]
    my_group = 2 * lax.axis_index("x") + lax.axis_index("y")

    big = [w_in, w_out, w_gate, w_up, w_down]
    place = jnp.stack([my_group, lax.axis_index("c")]).astype(jnp.int32)
    win_g, wout_g, wgate_g, wup_g, wdown_g = gather_weights([cast_into_gathered(w, place) for w in big])
    cw_pad = jnp.pad(conv_w, ((0, 0), (0, 32 - CONV_WIDTH), (0, 0)))
    cw_all = allgather_small("allgather_conv_w", cw_pad.reshape(n_layers * 32, LANES))
    cw_full = cw_all.reshape(N_GROUPS, 2, n_layers, 32, LANES)[:, 0].transpose(1, 2, 0, 3).reshape(n_layers, 32, C_A)
    pw_bf16 = pool_w.astype(BF16)
    tabs = rope_tables(s)
    gmix_a, gmix_b, gmix_c = g_mix[:, :C_A], g_mix[:, C_A:C_A + C_B], g_mix[:, C_A + C_B:]
    row = lambda t: t.reshape(1, -1)

    saved = []
    (h,) = norm_fwd("norm_first", xs, None, None, g_pre_mix[0])
    xin = xs
    for l in range(n_layers):
        proj = mm_act_wcols("proj", h, win_g, l, F32)
        ya = conv_fwd("conv_fwd", proj, cw_full[l], row(conv_b[l]), row(conv_ln_g[l]), row(conv_ln_b[l]), row(gmix_a[l]))
        qr, kr, vr = rope_fwd("rope_fwd", proj, tabs)
        qkv_d, os_, lses = [], [], []
        for dil in DILATIONS:
            qd, kd, vd = (to_dilated(t, dil) for t in (qr, kr, vr))
            o, lse = band_attn_fwd(f"band_attn_fwd_d{dil}", qd, kd, vd, s // dil)
            qkv_d.append((qd, kd, vd))
            os_.append(from_dilated(o, dil))
            lses.append(from_dilated(lse, dil))
        yb, lse_joint, ybn = attn_combine_fwd("attn_combine_fwd", os_, lses, row(gmix_b[l]))
        yc = pool_fwd("pool_fwd", proj, pw_bf16[l], row(pool_scale[l]), row(gmix_c[l]))
        y = jnp.concatenate([ya, ybn, yc], axis=1)
        z = mm_act_wrows("mix_out", y, wout_g, l, F32)
        x1, h2 = norm_fwd("norm_mid", xin, z, g_post_mix[l], g_pre_ffn[l])
        gt, up, act = ffn_up("ffn_up", h2, wgate_g, wup_g, l)
        f = mm_act_wrows("ffn_down", act, wdown_g, l, F32)
        if l + 1 < n_layers:
            x2, h_next = norm_fwd("norm_next", x1, f, g_post_ffn[l], g_pre_mix[l + 1])
        else:
            (x2,), h_next = norm_fwd("norm_last", x1, f, g_post_ffn[l], None), None
        saved.append(dict(xin=xin, h=h, proj=proj, qkv_d=qkv_d, yb=yb, lse=lse_joint, y=y, z=z, x1=x1, h2=h2,
                          gt=gt, up=up, act=act, f=f))
        xin, h = x2, h_next

    dx, sq = loss_head(xin, target)
    loss = lax.psum(0.5 * jnp.sum(sq) / d, ("x", "y", "c"))

    small = {k: [None] * n_layers for k in ("cw", "cb", "lg", "lb", "pw", "ps", "gmix", "gpre_mix", "gpost_mix",
                                            "gpre_ffn", "gpost_ffn")}
    reduced = None
    df, dg = norm_bwd("norm_bwd_top", dx, None, None, None, saved[-1]["f"], g_post_ffn[n_layers - 1])
    small["gpost_ffn"][n_layers - 1] = dg
    for l in reversed(range(n_layers)):
        sv = saved[l]
        dgt, dup = ffn_down_bwd("ffn_down_bwd", df, wdown_g, l, sv["gt"], sv["up"])
        d_wdown = mm_plain("wgrad_rows_ffn", "tn", sv["act"], df, BF16)
        dh2 = ffn_in_bwd("ffn_in_bwd", dgt, dup, wgate_g, wup_g, l)
        d_wgate = mm_wgrad_cols("wgrad_cols_ffn", sv["h2"], dgt, wgate_g.shape[3])
        d_wup = mm_wgrad_cols("wgrad_cols_ffn", sv["h2"], dup, wup_g.shape[3])
        dx1, small["gpre_ffn"][l], dz, small["gpost_mix"][l] = norm_bwd(
            "norm_bwd_mid", dx, dh2, sv["x1"], g_pre_ffn[l], sv["z"], g_post_mix[l])
        dy = mm_act_wrows_t("mix_out_bwd", dz, wout_g, l, F32)
        d_wout = mm_plain("wgrad_rows_mix", "tn", sv["y"], dz, BF16)
        proj = sv["proj"]
        dag, dcw, small["cb"][l], small["lg"][l], small["lb"][l], dgx_a = conv_bwd(
            "conv_bwd", proj, dy, cw_full[l], row(conv_b[l]), row(conv_ln_g[l]), row(conv_ln_b[l]), row(gmix_a[l]))
        small["cw"][l] = dcw.reshape(32, 8, C_A).sum(axis=1)
        dyb, delta, dgx_b = attn_combine_bwd("attn_combine_bwd", dy, sv["yb"], row(gmix_b[l]))
        dqs, dks, dvs = [], [], []
        for dil, (qd, kd, vd) in zip(DILATIONS, sv["qkv_d"]):
            do_d, lse_d, delta_d = (to_dilated(t, dil) for t in (dyb, sv["lse"], delta))
            dq = band_attn_dq(f"band_attn_dq_d{dil}", qd, kd, vd, do_d, lse_d, delta_d, s // dil)
            dk, dv = band_attn_dkv(f"band_attn_dkv_d{dil}", qd, kd, vd, do_d, lse_d, delta_d, s // dil)
            dqs.append(from_dilated(dq, dil))
            dks.append(from_dilated(dk, dil))
            dvs.append(from_dilated(dv, dil))
        dqkv = rope_bwd("rope_bwd", dqs, dks, dvs, tabs)
        du, small["pw"][l], small["ps"][l], dgx_c = pool_bwd("pool_bwd", proj, dy, pw_bf16[l], row(pool_scale[l]),
                                                              row(gmix_c[l]))
        small["gmix"][l] = jnp.concatenate([dgx_a, dgx_b, dgx_c], axis=1)
        dproj = jnp.concatenate([dag, dqkv, du], axis=1)
        dh1 = mm_act_wcols_t("proj_bwd", dproj, win_g, l, F32)
        d_win = mm_wgrad_cols("wgrad_cols_proj", sv["h"], dproj, win_g.shape[3])
        if l > 0:
            dx, small["gpre_mix"][l], df, small["gpost_ffn"][l - 1] = norm_bwd(
                "norm_bwd_next", dx1, dh1, sv["xin"], g_pre_mix[l], saved[l - 1]["f"], g_post_ffn[l - 1])
        else:
            dx, small["gpre_mix"][l] = norm_bwd("norm_bwd_first", dx1, dh1, sv["xin"], g_pre_mix[l], None, None)
        d_wout = d_wout.reshape(N_GROUPS, -1, d_wout.shape[1])
        d_wdown = d_wdown.reshape(N_GROUPS, -1, d_wdown.shape[1])
        reduced = reduce_layer([d_win, d_wout, d_wgate, d_wup, d_wdown], place, l, n_layers, reduced)

    big_grads = share_reduced_halves(reduced)
    big_m = [m_w_in, m_w_out, m_w_gate, m_w_up, m_w_down]
    big_v = [v_w_in, v_w_out, v_w_gate, v_w_up, v_w_down]
    big_upd = [elementwise("adamw", _adamw, [g, w, m, v], [F32, F32, F32])
               for g, w, m, v in zip(big_grads, big, big_m, big_v)]

    def stack8(k):
        return jnp.stack([t.sum(axis=0) if t.shape[0] == 8 and t.ndim == 2 else t for t in small[k]])

    order = ("cw", "cb", "lg", "lb", "pw", "ps", "gmix", "gpre_mix", "gpost_mix", "gpre_ffn", "gpost_ffn")
    partial = [jnp.stack(small["cw"])] + [stack8(k) for k in order[1:]]
    rep_w = [conv_b, conv_ln_g, conv_ln_b, pool_w, pool_scale, g_mix, g_pre_mix, g_post_mix, g_pre_ffn, g_post_ffn]
    rep_m = [m_conv_b, m_conv_ln_g, m_conv_ln_b, m_pool_w, m_pool_scale, m_g_mix, m_g_pre_mix, m_g_post_mix,
             m_g_pre_ffn, m_g_post_ffn]
    rep_v = [v_conv_b, v_conv_ln_g, v_conv_ln_b, v_pool_w, v_pool_scale, v_g_mix, v_g_pre_mix, v_g_post_mix,
             v_g_pre_ffn, v_g_post_ffn]
    packed = _pack(partial)
    rows = packed.shape[0]
    gall = allgather_small("allgather_small_grads", packed).reshape(8, rows, LANES)
    cw_hole = jnp.zeros((n_layers, 32, C_A), F32)
    rep_out = small_update("small_update", gall, _pack([cw_hole] + rep_w), _pack([cw_hole] + rep_m),
                           _pack([cw_hole] + rep_v))
    shapes = [cw_hole.shape] + [t.shape for t in rep_w]
    rep_g, rep_d, rep_mn, rep_vn = ([t for t in _unpack(buf, shapes)[1:]] for buf in rep_out)
    cw_rows = n_layers * 32
    gall_cw = gall[:, :cw_rows * C_A // LANES].reshape(8, cw_rows, C_A // LANES, LANES)
    gall_cw = lax.dynamic_index_in_dim(gall_cw, my_group, axis=2, keepdims=False)
    pad_cw = lambda t: jnp.pad(t, ((0, 0), (0, 32 - CONV_WIDTH), (0, 0))).reshape(cw_rows, LANES)
    cw_out = small_update("conv_w_update", gall_cw, pad_cw(conv_w), pad_cw(m_conv_w), pad_cw(v_conv_w))
    cw_g, cw_d, cw_mn, cw_vn = (t.reshape(n_layers, 32, LANES)[:, :CONV_WIDTH] for t in cw_out)

    def assemble(bigs, cw, reps):
        return [bigs[0], cw] + list(reps[:6]) + [bigs[1]] + list(reps[6:]) + list(bigs[2:])

    grads = assemble(big_grads, cw_g, rep_g)
    deltas = assemble([u[0] for u in big_upd], cw_d, rep_d)
    new_m = assemble([u[1] for u in big_upd], cw_mn, rep_mn)
    new_v = assemble([u[2] for u in big_upd], cw_vn, rep_vn)
    return (loss, dx[None], *grads, *deltas, *new_m, *new_v)
```

```python
import functools
import math

import jax
import jax.numpy as jnp
from jax import lax
from jax.experimental import pallas as pl
from jax.experimental.pallas import tpu as pltpu

F32 = jnp.float32
BF16 = jnp.bfloat16
MESH = pl.DeviceIdType.MESH

EPS = 1e-6
NEG = -1e30
C_A = 512
C_B = 1024
C_C = 512
HEAD_DIM = 64
POOL_WINDOWS = (2, 4, 8, 16)
C_G = 128
CONV_WIDTH = 31
CONV_HALF = 15
DILATIONS = (1, 4, 16)
ATT_HALF = 64
ROT_DIM = 16
ROPE_THETA = 500000.0
ADAM_LR, ADAM_B1, ADAM_B2, ADAM_EPS, ADAM_WD, ADAM_STEP = 0.001, 0.9, 0.999, 1e-08, 0.01, 10
N_GROUPS = 4
HALO = 32
LANES = 128
VMEM_LIMIT = 48 * 1024 * 1024


def _params(*sem):
    return pltpu.CompilerParams(dimension_semantics=sem, vmem_limit_bytes=VMEM_LIMIT)


def _pick(n, prefs):
    for p in prefs:
        if p <= n and n % p == 0:
            return p
    return n


def _rms_scale(t):
    return lax.rsqrt(jnp.mean(t * t, axis=-1, keepdims=True) + EPS)


def _rms_bwd(d, x, g):
    r = _rms_scale(x)
    u = d * g
    dx = r * u - x * (r * r * r) * jnp.mean(u * x, axis=-1, keepdims=True)
    return dx, d * x * r


def _sum8(t):
    rows, cols = t.shape
    return jnp.sum(t.reshape(rows // 8, 8, cols), axis=0)


def _dot(a, b, mode):
    dn = {"nn": (((1,), (0,)), ((), ())), "nt": (((1,), (1,)), ((), ())), "tn": (((0,), (0,)), ((), ()))}[mode]
    return lax.dot_general(a, b, dn, preferred_element_type=F32)


def _row_spec(tm, cols, col_block=0):
    return pl.BlockSpec((tm, cols), lambda i, cb=col_block: (i, cb))


def _const_spec(shape):
    return pl.BlockSpec(shape, lambda i: tuple(0 for _ in shape))


def norm_fwd(name, x, z, g_post, g_next):
    s, d = x.shape
    tm = _pick(s, (512, 256, 128))
    has_z, has_h = z is not None, g_next is not None

    def body(*refs):
        refs = list(refs)
        x_ref = refs.pop(0)
        xn = x_ref[...]
        if has_z:
            z_ref, gp_ref = refs.pop(0), refs.pop(0)
            zz = z_ref[...]
            xn = xn + zz * _rms_scale(zz) * gp_ref[...]
        if has_h:
            gn_ref = refs.pop(0)
        if has_z:
            refs.pop(0)[...] = xn
        if has_h:
            refs.pop(0)[...] = (xn * _rms_scale(xn) * gn_ref[...]).astype(BF16)

    ins, specs, outs, ospecs = [x], [_row_spec(tm, d)], [], []
    if has_z:
        ins += [z, g_post.reshape(1, d)]
        specs += [_row_spec(tm, d), _const_spec((1, d))]
        outs.append(jax.ShapeDtypeStruct((s, d), F32))
        ospecs.append(_row_spec(tm, d))
    if has_h:
        ins.append(g_next.reshape(1, d))
        specs.append(_const_spec((1, d)))
        outs.append(jax.ShapeDtypeStruct((s, d), BF16))
        ospecs.append(_row_spec(tm, d))
    res = pl.pallas_call(body, name=name, grid=(s // tm,), in_specs=specs, out_specs=ospecs, out_shape=outs,
                         compiler_params=_params("parallel"))(*ins)
    return list(res)


def norm_bwd(name, dres, dh, xin, g_pre, zin, g_post):
    s, d = dres.shape
    tm = _pick(s, (256, 128))
    has_pre, has_post = dh is not None, zin is not None

    def body(*refs):
        refs = list(refs)
        i = pl.program_id(0)
        dx = refs.pop(0)[...]
        if has_pre:
            dh_ref, x_ref, g_ref = refs.pop(0), refs.pop(0), refs.pop(0)
            ddx, dg = _rms_bwd(dh_ref[...].astype(F32), x_ref[...], g_ref[...])
            dx = dx + ddx
        if has_post:
            z_ref, gp_ref = refs.pop(0), refs.pop(0)
            dz, dgp = _rms_bwd(dx, z_ref[...], gp_ref[...])
        if has_pre:
            refs.pop(0)[...] = dx
            dg_ref = refs.pop(0)

            @pl.when(i == 0)
            def _():
                dg_ref[...] = jnp.zeros_like(dg_ref)

            dg_ref[...] += _sum8(dg)
        if has_post:
            refs.pop(0)[...] = dz.astype(BF16)
            dgp_ref = refs.pop(0)

            @pl.when(i == 0)
            def _():
                dgp_ref[...] = jnp.zeros_like(dgp_ref)

            dgp_ref[...] += _sum8(dgp)

    ins, specs, outs, ospecs = [dres], [_row_spec(tm, d)], [], []
    if has_pre:
        ins += [dh, xin, g_pre.reshape(1, d)]
        specs += [_row_spec(tm, d), _row_spec(tm, d), _const_spec((1, d))]
        outs += [jax.ShapeDtypeStruct((s, d), F32), jax.ShapeDtypeStruct((8, d), F32)]
        ospecs += [_row_spec(tm, d), _const_spec((8, d))]
    if has_post:
        ins += [zin, g_post.reshape(1, d)]
        specs += [_row_spec(tm, d), _const_spec((1, d))]
        outs += [jax.ShapeDtypeStruct((s, d), BF16), jax.ShapeDtypeStruct((8, d), F32)]
        ospecs += [_row_spec(tm, d), _const_spec((8, d))]
    res = pl.pallas_call(body, name=name, grid=(s // tm,), in_specs=specs, out_specs=ospecs, out_shape=outs,
                         compiler_params=_params("arbitrary"))(*ins)
    return list(res)


def loss_head(x, target):
    s, d = x.shape
    tm = _pick(s, (512, 256, 128))

    def body(x_ref, t_ref, dx_ref, sq_ref):
        i = pl.program_id(0)
        e = x_ref[...] - t_ref[...]
        dx_ref[...] = e * (1.0 / d)

        @pl.when(i == 0)
        def _():
            sq_ref[...] = jnp.zeros_like(sq_ref)

        sq_ref[...] += _sum8(e * e)

    return pl.pallas_call(body, name="loss_head", grid=(s // tm,),
                          in_specs=[_row_spec(tm, d), _row_spec(tm, d)],
                          out_specs=[_row_spec(tm, d), _const_spec((8, d))],
                          out_shape=[jax.ShapeDtypeStruct((s, d), F32), jax.ShapeDtypeStruct((8, d), F32)],
                          compiler_params=_params("arbitrary"))(x, target)


def matmul(name, mode, a, b, *, grid, tk_steps, a_spec, b_spec, o_spec, out_shape, acc_shape, b_flat=None):
    nk = tk_steps

    def rhs(b_ref):
        return b_ref[...] if b_flat is None else b_ref[...].reshape(b_flat)

    def body(a_ref, b_ref, o_ref, *scratch):
        if nk == 1:
            o_ref[...] = _dot(a_ref[...], rhs(b_ref), mode).astype(o_ref.dtype)
            return
        acc = scratch[0]
        kk = pl.program_id(2)

        @pl.when(kk == 0)
        def _():
            acc[...] = jnp.zeros_like(acc)

        acc[...] += _dot(a_ref[...], rhs(b_ref), mode)

        @pl.when(kk == nk - 1)
        def _():
            o_ref[...] = acc[...].astype(o_ref.dtype)

    return pl.pallas_call(body, name=name, grid=grid, in_specs=[a_spec, b_spec], out_specs=o_spec, out_shape=out_shape,
                          scratch_shapes=[] if nk == 1 else [pltpu.VMEM(acc_shape, F32)],
                          compiler_params=_params("parallel", "parallel", "arbitrary"))(a, b)


TM_PREFS = (1024, 512, 256, 128)
TW_PREFS = (1408, 1152, 1024, 512, 384, 256, 128)
TK_PREFS = (512, 384, 256, 128)
FULL_K_PREFS = (2048, 1024) + TK_PREFS


def mm_act_wcols(name, a, wg, layer, out_dtype):
    s, k = a.shape
    g, _, _, ng = wg.shape
    tm, tk, tn = _pick(s, TM_PREFS), _pick(k, FULL_K_PREFS), _pick(ng, TW_PREFS)
    per = ng // tn
    return matmul(name, "nn", a, wg, grid=(g * per, s // tm, k // tk), tk_steps=k // tk,
                  a_spec=pl.BlockSpec((tm, tk), lambda j, i, kk: (i, kk)),
                  b_spec=pl.BlockSpec((None, None, tk, tn), lambda j, i, kk: (j // per, layer, kk, j % per)),
                  o_spec=pl.BlockSpec((tm, tn), lambda j, i, kk: (i, j)),
                  out_shape=jax.ShapeDtypeStruct((s, g * ng), out_dtype), acc_shape=(tm, tn))


def mm_act_wcols_t(name, a, wg, layer, out_dtype):
    s, n = a.shape
    g, _, k, ng = wg.shape
    tm, tn, tk = _pick(s, TM_PREFS), _pick(k, TM_PREFS), _pick(ng, (1152, 384, 128))
    per = ng // tk
    return matmul(name, "nt", a, wg, grid=(s // tm, k // tn, g * per), tk_steps=g * per,
                  a_spec=pl.BlockSpec((tm, tk), lambda i, j, kk: (i, kk)),
                  b_spec=pl.BlockSpec((None, None, tn, tk), lambda i, j, kk: (kk // per, layer, j, kk % per)),
                  o_spec=pl.BlockSpec((tm, tn), lambda i, j, kk: (i, j)),
                  out_shape=jax.ShapeDtypeStruct((s, k), out_dtype), acc_shape=(tm, tn))


def mm_act_wrows(name, a, wr, layer, out_dtype):
    s, k = a.shape
    g, _, kg, n = wr.shape
    tm, tn, tk = _pick(s, TM_PREFS), _pick(n, TM_PREFS), _pick(kg, (1408, 512, 256, 128))
    if k <= FULL_K_PREFS[0]:
        return matmul(name, "nn", a, wr, grid=(n // tn, s // tm, 1), tk_steps=1,
                      a_spec=pl.BlockSpec((tm, k), lambda j, i, kk: (i, 0)),
                      b_spec=pl.BlockSpec((g, None, kg, tn), lambda j, i, kk: (0, layer, 0, j)),
                      o_spec=pl.BlockSpec((tm, tn), lambda j, i, kk: (i, j)),
                      out_shape=jax.ShapeDtypeStruct((s, n), out_dtype), acc_shape=(tm, tn), b_flat=(k, tn))
    per = kg // tk
    return matmul(name, "nn", a, wr, grid=(s // tm, n // tn, g * per), tk_steps=g * per,
                  a_spec=pl.BlockSpec((tm, tk), lambda i, j, kk: (i, kk)),
                  b_spec=pl.BlockSpec((None, None, tk, tn), lambda i, j, kk: (kk // per, layer, kk % per, j)),
                  o_spec=pl.BlockSpec((tm, tn), lambda i, j, kk: (i, j)),
                  out_shape=jax.ShapeDtypeStruct((s, n), out_dtype), acc_shape=(tm, tn))


def mm_act_wrows_t(name, a, wr, layer, out_dtype):
    s, n = a.shape
    g, _, kg, _ = wr.shape
    tm, tn, tk = _pick(s, TM_PREFS), _pick(kg, (1408, 512, 256, 128)), _pick(n, FULL_K_PREFS)
    per = kg // tn
    return matmul(name, "nt", a, wr, grid=(g * per, s // tm, n // tk), tk_steps=n // tk,
                  a_spec=pl.BlockSpec((tm, tk), lambda j, i, kk: (i, kk)),
                  b_spec=pl.BlockSpec((None, None, tn, tk), lambda j, i, kk: (j // per, layer, j % per, kk)),
                  o_spec=pl.BlockSpec((tm, tn), lambda j, i, kk: (i, j)),
                  out_shape=jax.ShapeDtypeStruct((s, g * kg), out_dtype), acc_shape=(tm, tn))


def mm_wgrad_cols(name, a, dy, ng):
    s, k = a.shape
    n = dy.shape[1]
    g = n // ng
    tm = _pick(k, TM_PREFS)
    tk = _pick(s, TM_PREFS)
    tn = _pick(ng, TW_PREFS)
    per = ng // tn
    return matmul(name, "tn", a, dy, grid=(k // tm, g * per, s // tk), tk_steps=s // tk,
                  a_spec=pl.BlockSpec((tk, tm), lambda i, j, kk: (kk, i)),
                  b_spec=pl.BlockSpec((tk, tn), lambda i, j, kk: (kk, j)),
                  o_spec=pl.BlockSpec((None, tm, tn), lambda i, j, kk: (j // per, i, j % per)),
                  out_shape=jax.ShapeDtypeStruct((g, k, ng), BF16), acc_shape=(tm, tn))


def mm_plain(name, mode, a, b, out_dtype):
    if mode == "nn":
        (m, k), n = a.shape, b.shape[1]
    elif mode == "nt":
        (m, k), n = a.shape, b.shape[0]
    else:
        (k, m), n = a.shape, b.shape[1]
    tm = _pick(m, (1408,) + TM_PREFS)
    tn = _pick(n, TM_PREFS)
    tk = _pick(k, TM_PREFS)
    a_spec = (pl.BlockSpec((tk, tm), lambda i, j, kk: (kk, i)) if mode == "tn"
              else pl.BlockSpec((tm, tk), lambda i, j, kk: (i, kk)))
    b_spec = (pl.BlockSpec((tn, tk), lambda i, j, kk: (j, kk)) if mode == "nt"
              else pl.BlockSpec((tk, tn), lambda i, j, kk: (kk, j)))
    return matmul(name, mode, a, b, grid=(m // tm, n // tn, k // tk), tk_steps=k // tk, a_spec=a_spec, b_spec=b_spec,
                  o_spec=pl.BlockSpec((tm, tn), lambda i, j, kk: (i, j)),
                  out_shape=jax.ShapeDtypeStruct((m, n), out_dtype), acc_shape=(tm, tn))


def _silu_parts(gt):
    sg = jax.nn.sigmoid(gt)
    return gt * sg, sg


def ffn_up(name, h, wgate, wup, layer, gather=()):
    s, d = h.shape
    g, _, _, ng = wgate.shape
    assert d <= FULL_K_PREFS[0]
    tm = _pick(s, (256, 128))
    tn = _pick(ng, TW_PREFS)
    per = ng // tn

    n = len(gather)
    grid = (g * per, s // tm)

    def body(h_ref, wg_ref, wu_ref, *rest):
        gt_ref, up_ref, act_ref = rest[n:n + 3]
        if n:
            start, finish = _gather_plan(rest[n + 3:2 * n + 3], *rest[2 * n + 3:])
            pl.when((pl.program_id(0) == 0) & (pl.program_id(1) == 0))(start)
        hh = h_ref[...]
        gt = _dot(hh, wg_ref[...], "nn")
        up = _dot(hh, wu_ref[...], "nn")
        gt_ref[...] = gt.astype(BF16)
        up_ref[...] = up.astype(BF16)
        act_ref[...] = (_silu_parts(gt)[0] * up).astype(BF16)
        if n:
            pl.when((pl.program_id(0) == grid[0] - 1) & (pl.program_id(1) == grid[1] - 1))(finish)

    wspec = pl.BlockSpec((None, None, d, tn), lambda j, i: (j // per, layer, 0, j % per))
    ospec = pl.BlockSpec((tm, tn), lambda j, i: (i, j))
    osh = jax.ShapeDtypeStruct((s, g * ng), BF16)
    res = pl.pallas_call(body, name=name, grid=grid,
                         in_specs=[pl.BlockSpec((tm, d), lambda j, i: (i, 0)), wspec, wspec] + [ANY] * n,
                         out_specs=[ospec, ospec, ospec] + [ANY] * n,
                         out_shape=[osh, osh, osh] + [jax.ShapeDtypeStruct(t.shape, t.dtype) for t in gather],
                         input_output_aliases={3 + a: 3 + a for a in range(n)},
                         scratch_shapes=_gather_sems(n) if n else [],
                         compiler_params=_params(*(["arbitrary"] * 2 if n else ["parallel"] * 2)))(h, wgate, wup, *gather)
    return res[:3], list(res[3:])


def ffn_down_bwd(name, df, wdown, layer, gt, up):
    s, d = df.shape
    g, _, fg, _ = wdown.shape
    f = g * fg
    assert d <= FULL_K_PREFS[0]
    tm = _pick(s, (256, 128))
    tn = _pick(fg, TW_PREFS)
    per = fg // tn

    def body(df_ref, wd_ref, gt_ref, up_ref, dgt_ref, dup_ref):
        da = _dot(df_ref[...], wd_ref[...], "nt")
        gt = gt_ref[...].astype(F32)
        up = up_ref[...].astype(F32)
        silu, sg = _silu_parts(gt)
        dgt_ref[...] = (da * up * (sg * (1.0 + gt * (1.0 - sg)))).astype(BF16)
        dup_ref[...] = (da * silu).astype(BF16)

    ospec = pl.BlockSpec((tm, tn), lambda j, i: (i, j))
    osh = jax.ShapeDtypeStruct((s, f), BF16)
    return pl.pallas_call(body, name=name, grid=(f // tn, s // tm),
                          in_specs=[pl.BlockSpec((tm, d), lambda j, i: (i, 0)),
                                    pl.BlockSpec((None, None, tn, d), lambda j, i: (j // per, layer, j % per, 0)),
                                    ospec, ospec],
                          out_specs=[ospec, ospec], out_shape=[osh, osh],
                          compiler_params=_params("parallel", "parallel"))(df, wdown, gt, up)


def ffn_in_bwd(name, dgt, dup, wgate, wup, layer, exchange=()):
    s, f = dgt.shape
    g, _, d, ng = wgate.shape
    tm = _pick(s, (512, 256, 128))
    tn = _pick(d, TM_PREFS)
    tk = _pick(ng, TW_PREFS)
    per = ng // tk
    nk = g * per

    n = len(exchange)
    grid = (s // tm, d // tn, nk)

    def body(dg_ref, du_ref, wg_ref, wu_ref, *rest):
        o_ref, acc = rest[n], rest[2 * n + 1]
        kk = pl.program_id(2)
        if n:
            start, finish = _chip_exchange_plan(rest[:n], rest[n + 1:2 * n + 1], *rest[2 * n + 2:])
            pl.when((pl.program_id(0) == 0) & (pl.program_id(1) == 0) & (kk == 0))(start)

        @pl.when(kk == 0)
        def _():
            acc[...] = jnp.zeros_like(acc)

        acc[...] += _dot(dg_ref[...], wg_ref[...], "nt") + _dot(du_ref[...], wu_ref[...], "nt")

        @pl.when(kk == nk - 1)
        def _():
            o_ref[...] = acc[...]

        if n:
            pl.when((pl.program_id(0) == grid[0] - 1) & (pl.program_id(1) == grid[1] - 1) & (kk == nk - 1))(finish)

    aspec = pl.BlockSpec((tm, tk), lambda i, j, kk: (i, kk))
    wspec = pl.BlockSpec((None, None, tn, tk), lambda i, j, kk: (kk // per, layer, j, kk % per))
    res = pl.pallas_call(body, name=name, grid=grid, in_specs=[aspec, aspec, wspec, wspec] + [ANY] * n,
                         out_specs=[pl.BlockSpec((tm, tn), lambda i, j, kk: (i, j))] + [ANY] * n,
                         out_shape=[jax.ShapeDtypeStruct((s, d), F32)] + _chip_exchange_shapes(exchange),
                         scratch_shapes=[pltpu.VMEM((tm, tn), F32)] + (_chip_exchange_sems(n) if n else []),
                         compiler_params=_params(*(["arbitrary"] * 3 if n else ["parallel", "parallel", "arbitrary"])))(
                             dgt, dup, wgate, wup, *exchange)
    return res[0], list(res[1:])


def _halo_specs(tm, s, cols, cb, halo=HALO):
    r = tm // halo
    last = s // halo - 1
    return [pl.BlockSpec((halo, cols), lambda i: (jnp.maximum(i * r - 1, 0), cb)),
            pl.BlockSpec((tm, cols), lambda i: (i, cb)),
            pl.BlockSpec((halo, cols), lambda i: (jnp.minimum((i + 1) * r, last), cb))]


def _edge_masked(prev_ref, next_ref, i, nt):
    return jnp.where(i > 0, prev_ref[...], 0.0), jnp.where(i < nt - 1, next_ref[...], 0.0)


def _glu(a, gate):
    return a * jax.nn.sigmoid(gate)


def _conv_post(conv, lg, lb, gm):
    mu = jnp.mean(conv, axis=-1, keepdims=True)
    xc = conv - mu
    rs = lax.rsqrt(jnp.mean(xc * xc, axis=-1, keepdims=True) + EPS)
    xhat = xc * rs
    yl = xhat * lg + lb
    ya, sg = _silu_parts(yl)
    return xhat, rs, yl, sg, ya


def _fill_glu(hg, ap, am, an, gp, gm_, gn, i, nt, tm):
    a0, a2 = _edge_masked(ap, an, i, nt)
    g0, g2 = _edge_masked(gp, gn, i, nt)
    hg[pl.ds(0, HALO), :] = _glu(a0, g0)
    hg[pl.ds(HALO, tm), :] = _glu(am[...], gm_[...])
    hg[pl.ds(HALO + tm, HALO), :] = _glu(a2, g2)


def conv_fwd(name, proj, cw, cb, lg, lb, gmix):
    s = proj.shape[0]
    tm = _pick(s, (256, 128))
    nt = s // tm

    def body(ap, am, an, gp, gm_, gn, cw_ref, cb_ref, lg_ref, lb_ref, gx_ref, o_ref, hg):
        i = pl.program_id(0)
        _fill_glu(hg, ap, am, an, gp, gm_, gn, i, nt, tm)
        acc = jnp.zeros((tm, C_A), F32) + cb_ref[...]
        for t in range(CONV_WIDTH):
            acc = acc + cw_ref[pl.ds(t, 1), :] * hg[pl.ds(HALO - CONV_HALF + t, tm), :]
        ya = _conv_post(acc, lg_ref[...], lb_ref[...], None)[4]
        o_ref[...] = (ya * _rms_scale(ya) * gx_ref[...]).astype(BF16)

    vec = _const_spec((1, C_A))
    return pl.pallas_call(body, name=name, grid=(nt,),
                          in_specs=_halo_specs(tm, s, C_A, 0) + _halo_specs(tm, s, C_A, 1)
                          + [_const_spec((32, C_A)), vec, vec, vec, vec],
                          out_specs=_row_spec(tm, C_A), out_shape=jax.ShapeDtypeStruct((s, C_A), BF16),
                          scratch_shapes=[pltpu.VMEM((tm + 2 * HALO, C_A), F32)],
                          compiler_params=_params("parallel"))(proj, proj, proj, proj, proj, proj, cw, cb, lg, lb, gmix)


def conv_bwd(name, proj, dy, cw, cb, lg, lb, gmix):
    s = proj.shape[0]
    tm = _pick(s, (256, 128))
    nt = s // tm
    te = tm + HALO
    off = HALO // 2

    def body(ap, am, an, gp, gm_, gn, dp, dm, dn, cw_ref, cb_ref, lg_ref, lb_ref, gx_ref,
             dag_ref, dcw_ref, dcb_ref, dlg_ref, dlb_ref, dgx_ref, hg, dc, dyx):
        i = pl.program_id(0)
        _fill_glu(hg, ap, am, an, gp, gm_, gn, i, nt, tm)
        d0, d2 = _edge_masked(dp, dn, i, nt)
        dyx[pl.ds(0, HALO), :] = d0
        dyx[pl.ds(HALO, tm), :] = dm[...]
        dyx[pl.ds(HALO + tm, HALO), :] = d2
        lg, lb, gx = lg_ref[...], lb_ref[...], gx_ref[...]
        conv = jnp.zeros((te, C_A), F32) + cb_ref[...]
        for t in range(CONV_WIDTH):
            conv = conv + cw_ref[pl.ds(t, 1), :] * hg[pl.ds(1 + t, te), :]
        xhat, rs, yl, sg, ya = _conv_post(conv, lg, lb, gx)
        dout = dyx[pl.ds(off, te), :]
        dya, dgx_rows = _rms_bwd(dout, ya, gx)
        dyl = dya * (sg * (1.0 + yl * (1.0 - sg)))
        dxh = dyl * lg
        dconv = rs * (dxh - jnp.mean(dxh, axis=-1, keepdims=True)
                      - xhat * jnp.mean(dxh * xhat, axis=-1, keepdims=True))
        dc[...] = dconv

        @pl.when(i == 0)
        def _():
            for r in (dcw_ref, dcb_ref, dlg_ref, dlb_ref, dgx_ref):
                r[...] = jnp.zeros_like(r)

        dcb_ref[...] += _sum8(dconv[off:off + tm])
        dlg_ref[...] += _sum8((dyl * xhat)[off:off + tm])
        dlb_ref[...] += _sum8(dyl[off:off + tm])
        dgx_ref[...] += _sum8(dgx_rows[off:off + tm])
        dcm = dconv[off:off + tm]
        dhg = jnp.zeros((tm, C_A), F32)
        for t in range(CONV_WIDTH):
            dhg = dhg + cw_ref[pl.ds(t, 1), :] * dc[pl.ds(HALO - 1 - t, tm), :]
            dcw_ref[pl.ds(8 * t, 8), :] += _sum8(dcm * hg[pl.ds(HALO - CONV_HALF + t, tm), :])
        a, gate = am[...], gm_[...]
        sgate = jax.nn.sigmoid(gate)
        dag_ref[:, pl.ds(0, C_A)] = (dhg * sgate).astype(BF16)
        dag_ref[:, pl.ds(C_A, C_A)] = (dhg * a * sgate * (1.0 - sgate)).astype(BF16)

    vec = _const_spec((1, C_A))
    acc8 = _const_spec((8, C_A))
    sh8 = jax.ShapeDtypeStruct((8, C_A), F32)
    return pl.pallas_call(body, name=name, grid=(nt,),
                          in_specs=_halo_specs(tm, s, C_A, 0) + _halo_specs(tm, s, C_A, 1) + _halo_specs(tm, s, C_A, 0)
                          + [_const_spec((32, C_A)), vec, vec, vec, vec],
                          out_specs=[_row_spec(tm, 2 * C_A), _const_spec((32 * 8, C_A)), acc8, acc8, acc8, acc8],
                          out_shape=[jax.ShapeDtypeStruct((s, 2 * C_A), BF16), jax.ShapeDtypeStruct((32 * 8, C_A), F32),
                                     sh8, sh8, sh8, sh8],
                          scratch_shapes=[pltpu.VMEM((tm + 2 * HALO, C_A), F32), pltpu.VMEM((te, C_A), F32),
                                          pltpu.VMEM((tm + 2 * HALO, C_A), F32)],
                          compiler_params=_params("arbitrary"))(proj, proj, proj, proj, proj, proj, dy, dy, dy,
                                                                cw, cb, lg, lb, gmix)


def _pool_counts(pos, win, s):
    lo = jnp.maximum(pos - win // 2, 0)
    hi = jnp.minimum(pos + win - win // 2, s)
    return jnp.maximum(hi - lo, 1).astype(F32)


def _pooled(uext, base, rows, pos, s):
    outs = []
    for gi, win in enumerate(POOL_WINDOWS):
        lanes = pl.ds(gi * C_G, C_G)
        acc = jnp.zeros((rows, C_G), F32)
        for o in range(-(win // 2), win - win // 2):
            acc = acc + uext[pl.ds(base + o, rows), lanes]
        outs.append(acc / _pool_counts(pos, win, s) - uext[pl.ds(base, rows), lanes])
    return outs


def pool_fwd(name, proj, pw, scale, gmix):
    s = proj.shape[0]
    tm = _pick(s, (256, 128))
    nt = s // tm
    ucol = (proj.shape[1] - C_C) // C_C

    def body(up, um, un, pw_ref, sc_ref, gx_ref, o_ref, uext):
        i = pl.program_id(0)
        u0, u2 = _edge_masked(up, un, i, nt)
        uext[pl.ds(0, HALO), :] = u0
        uext[pl.ds(HALO, tm), :] = um[...]
        uext[pl.ds(HALO + tm, HALO), :] = u2
        pos = i * tm + lax.broadcasted_iota(jnp.int32, (tm, 1), 0)
        pooled = _pooled(uext, HALO, tm, pos, s)
        mixed = jnp.concatenate([_dot(pooled[g].astype(BF16), pw_ref[g], "nn") for g in range(4)], axis=1)
        yc = mixed * sc_ref[...]
        o_ref[...] = (yc * _rms_scale(yc) * gx_ref[...]).astype(BF16)

    vec = _const_spec((1, C_C))
    return pl.pallas_call(body, name=name, grid=(nt,),
                          in_specs=_halo_specs(tm, s, C_C, ucol) + [_const_spec((4, C_G, C_G)), vec, vec],
                          out_specs=_row_spec(tm, C_C), out_shape=jax.ShapeDtypeStruct((s, C_C), BF16),
                          scratch_shapes=[pltpu.VMEM((tm + 2 * HALO, C_C), F32)],
                          compiler_params=_params("parallel"))(proj, proj, proj, pw, scale, gmix)


def pool_bwd(name, proj, dy, pw, scale, gmix):
    s = proj.shape[0]
    tm = _pick(s, (256, 128))
    nt = s // tm
    te = tm + HALO
    off = HALO // 2
    ucol = (proj.shape[1] - C_C) // C_C
    dcol = (dy.shape[1] - C_C) // C_C

    def body(up, um, un, dp, dm, dn, pw_ref, sc_ref, gx_ref, du_ref, dpw_ref, dsc_ref, dgx_ref, uext, dyx, qs, dps):
        i = pl.program_id(0)
        u0, u2 = _edge_masked(up, un, i, nt)
        uext[pl.ds(0, HALO), :] = u0
        uext[pl.ds(HALO, tm), :] = um[...]
        uext[pl.ds(HALO + tm, HALO), :] = u2
        d0, d2 = _edge_masked(dp, dn, i, nt)
        dyx[pl.ds(0, HALO), :] = d0
        dyx[pl.ds(HALO, tm), :] = dm[...]
        dyx[pl.ds(HALO + tm, HALO), :] = d2
        pos = i * tm - off + lax.broadcasted_iota(jnp.int32, (te, 1), 0)
        pooled = _pooled(uext, off, te, pos, s)
        mixed = jnp.concatenate([_dot(pooled[g].astype(BF16), pw_ref[g], "nn") for g in range(4)], axis=1)
        sc = sc_ref[...]
        yc = mixed * sc
        dyc, dgx_rows = _rms_bwd(dyx[pl.ds(off, te), :], yc, gx_ref[...])
        dmixed = dyc * sc

        @pl.when(i == 0)
        def _():
            for r in (dpw_ref, dsc_ref, dgx_ref):
                r[...] = jnp.zeros_like(r)

        dsc_ref[...] += _sum8((dyc * mixed)[off:off + tm])
        dgx_ref[...] += _sum8(dgx_rows[off:off + tm])
        for gi, win in enumerate(POOL_WINDOWS):
            lanes = pl.ds(gi * C_G, C_G)
            dmg = dmixed[:, gi * C_G:(gi + 1) * C_G].astype(BF16)
            dpw_ref[gi] += _dot(pooled[gi][off:off + tm].astype(BF16), dmg[off:off + tm], "tn")
            dpl = _dot(dmg, pw_ref[gi], "nt")
            dps[:, lanes] = dpl
            qs[:, lanes] = dpl / _pool_counts(pos, win, s)
        for gi, win in enumerate(POOL_WINDOWS):
            lanes = pl.ds(gi * C_G, C_G)
            acc = jnp.zeros((tm, C_G), F32) - dps[pl.ds(off, tm), lanes]
            for o in range(-(win // 2) + 1, win // 2 + 1):
                acc = acc + qs[pl.ds(off + o, tm), lanes]
            du_ref[:, lanes] = acc.astype(BF16)

    vec = _const_spec((1, C_C))
    acc8 = _const_spec((8, C_C))
    sh8 = jax.ShapeDtypeStruct((8, C_C), F32)
    return pl.pallas_call(body, name=name, grid=(nt,),
                          in_specs=_halo_specs(tm, s, C_C, ucol) + _halo_specs(tm, s, C_C, dcol)
                          + [_const_spec((4, C_G, C_G)), vec, vec],
                          out_specs=[_row_spec(tm, C_C), _const_spec((4, C_G, C_G)), acc8, acc8],
                          out_shape=[jax.ShapeDtypeStruct((s, C_C), BF16), jax.ShapeDtypeStruct((4, C_G, C_G), F32),
                                     sh8, sh8],
                          scratch_shapes=[pltpu.VMEM((tm + 2 * HALO, C_C), F32), pltpu.VMEM((tm + 2 * HALO, C_C), F32),
                                          pltpu.VMEM((te, C_C), F32), pltpu.VMEM((te, C_C), F32)],
                          compiler_params=_params("arbitrary"))(proj, proj, proj, dy, dy, dy, pw, scale, gmix)


def rope_tables(s):
    pos = jnp.arange(s, dtype=F32)
    inv = ROPE_THETA ** (-jnp.arange(0, ROT_DIM, 2, dtype=F32) / ROT_DIM)
    ang = pos[:, None] * inv[None, :]
    cos, sin = jnp.cos(ang), jnp.sin(ang)
    half = ROT_DIM // 2
    rest = HEAD_DIM - ROT_DIM
    c = jnp.concatenate([cos, cos, jnp.ones((s, rest), F32)], axis=1)
    sa = jnp.concatenate([-sin, jnp.zeros((s, HEAD_DIM - half), F32)], axis=1)
    sb = jnp.concatenate([jnp.zeros((s, half), F32), sin, jnp.zeros((s, rest), F32)], axis=1)
    return tuple(jnp.concatenate([t, t], axis=1) for t in (c, sa, sb))


def rope_fwd(name, proj, tabs):
    s = proj.shape[0]
    tm = _pick(s, (256, 128))
    scale = HEAD_DIM ** -0.5

    def body(q_ref, k_ref, v_ref, c_ref, sa_ref, sb_ref, qo, ko, vo):
        c, sa, sb = c_ref[...], sa_ref[...], sb_ref[...]
        for j in range(C_B // LANES):
            lanes = pl.ds(j * LANES, LANES)
            for src, dst, mul in ((q_ref, qo, scale), (k_ref, ko, 1.0)):
                t = src[:, lanes]
                r = t * c + pltpu.roll(t, LANES - ROT_DIM // 2, 1) * sa + pltpu.roll(t, ROT_DIM // 2, 1) * sb
                dst[:, lanes] = (r * mul).astype(BF16)
        vo[...] = v_ref[...].astype(BF16)

    tab = _row_spec(tm, LANES)
    osh = jax.ShapeDtypeStruct((s, C_B), BF16)
    return pl.pallas_call(body, name=name, grid=(s // tm,),
                          in_specs=[_row_spec(tm, C_B, 1), _row_spec(tm, C_B, 2), _row_spec(tm, C_B, 3), tab, tab, tab],
                          out_specs=[_row_spec(tm, C_B)] * 3, out_shape=[osh, osh, osh],
                          compiler_params=_params("parallel"))(proj, proj, proj, *tabs)


def rope_bwd(name, dqs, dks, dvs, tabs):
    s = dqs[0].shape[0]
    tm = _pick(s, (256, 128))
    scale = HEAD_DIM ** -0.5

    def body(q1, q2, q3, k1, k2, k3, v1, v2, v3, c_ref, sa_ref, sb_ref, o_ref):
        c, sa, sb = c_ref[...], sa_ref[...], sb_ref[...]
        for j in range(C_B // LANES):
            lanes = pl.ds(j * LANES, LANES)
            for which, (r1, r2, r3), mul in ((0, (q1, q2, q3), scale), (1, (k1, k2, k3), 1.0)):
                d = r1[:, lanes].astype(F32) + r2[:, lanes].astype(F32) + r3[:, lanes].astype(F32)
                t = d * c + pltpu.roll(d * sa, ROT_DIM // 2, 1) + pltpu.roll(d * sb, LANES - ROT_DIM // 2, 1)
                o_ref[:, pl.ds(which * C_B + j * LANES, LANES)] = (t * mul).astype(BF16)
        o_ref[:, pl.ds(2 * C_B, C_B)] = (v1[...].astype(F32) + v2[...].astype(F32) + v3[...].astype(F32)).astype(BF16)

    tab = _row_spec(tm, LANES)
    return pl.pallas_call(body, name=name, grid=(s // tm,), in_specs=[_row_spec(tm, C_B)] * 9 + [tab, tab, tab],
                          out_specs=_row_spec(tm, 3 * C_B), out_shape=jax.ShapeDtypeStruct((s, 3 * C_B), BF16),
                          compiler_params=_params("parallel"))(*dqs, *dks, *dvs, *tabs)


def to_dilated(t, d):
    if d == 1:
        return t
    s, c = t.shape
    return t.reshape(s // d, d, c).transpose(1, 0, 2).reshape(s, c)


def from_dilated(t, d):
    if d == 1:
        return t
    s, c = t.shape
    return t.reshape(d, s // d, c).transpose(1, 0, 2).reshape(s, c)


def _window_specs(tq, s, cols):
    r = tq // ATT_HALF
    last = s // ATT_HALF - 1
    return [pl.BlockSpec((ATT_HALF, cols), lambda i: (jnp.maximum(i * r - 1, 0), 0)),
            pl.BlockSpec((tq, cols), lambda i: (i, 0)),
            pl.BlockSpec((ATT_HALF, cols), lambda i: (jnp.minimum((i + 1) * r, last), 0))]


def _fill_window(win, prev_ref, main_ref, next_ref, tq):
    win[pl.ds(0, ATT_HALF), :] = prev_ref[...]
    win[pl.ds(ATT_HALF, tq), :] = main_ref[...]
    win[pl.ds(ATT_HALF + tq, ATT_HALF), :] = next_ref[...]


def _band_valid(tile, window, j0, seg, tile_is_rows):
    shape = (2 * tile, window) if tile_is_rows else (2 * window, tile)
    ri = lax.broadcasted_iota(jnp.int32, shape, 0)
    ci = lax.broadcasted_iota(jnp.int32, shape, 1)
    per_head = tile if tile_is_rows else window
    ri = jnp.where(ri >= per_head, ri - per_head, ri)
    ti, wi = (ri, ci) if tile_is_rows else (ci, ri)
    jw = j0 - ATT_HALF + wi
    return (jnp.abs(wi - ATT_HALF - ti) <= ATT_HALF) & (jw >= 0) & (jw < seg)


def _first_head():
    return lax.broadcasted_iota(jnp.int32, (1, LANES), 1) < HEAD_DIM


def _stack_heads(t, first):
    z = jnp.zeros_like(t)
    return jnp.concatenate([jnp.where(first, t, z), jnp.where(first, z, t)], axis=0)


def _unstack_heads(t2, first, rows):
    return jnp.where(first, t2[:rows], t2[rows:])


def _head_columns(ref, pair):
    return jnp.concatenate([ref[:, pl.ds(pair * LANES, 1)], ref[:, pl.ds(pair * LANES + HEAD_DIM, 1)]], axis=0)


def band_attn_fwd(name, q, k, v, seg):
    s, c = q.shape
    tq = _pick(seg, (256, 128))
    per_seg = seg // tq

    wrows = tq + 2 * ATT_HALF

    def body(q_ref, kp, km, kn, vp, vm, vn, o_ref, lse_ref, kw, vw):
        j0 = (pl.program_id(0) % per_seg) * tq
        _fill_window(kw, kp, km, kn, tq)
        _fill_window(vw, vp, vm, vn, tq)
        valid = _band_valid(tq, wrows, j0, seg, True)
        first = _first_head()
        for pair in range(c // LANES):
            lanes = pl.ds(pair * LANES, LANES)
            sc = jnp.where(valid, _dot(_stack_heads(q_ref[:, lanes], first), kw[:, lanes], "nt"), NEG)
            m = jnp.max(sc, axis=-1, keepdims=True)
            p = jnp.exp(sc - m)
            l = jnp.sum(p, axis=-1, keepdims=True)
            o = _dot(p.astype(BF16), vw[:, lanes], "nn") / l
            o_ref[:, lanes] = _unstack_heads(o, first, tq).astype(BF16)
            lse_ref[:, lanes] = _unstack_heads(m + jnp.log(l), first, tq)

    win = _window_specs(tq, s, c)
    tile = _row_spec(tq, c)
    return pl.pallas_call(body, name=name, grid=(s // tq,), in_specs=[tile] + win + win, out_specs=[tile, tile],
                          out_shape=[jax.ShapeDtypeStruct((s, c), BF16), jax.ShapeDtypeStruct((s, c), F32)],
                          scratch_shapes=[pltpu.VMEM((wrows, c), BF16)] * 2,
                          compiler_params=_params("parallel"))(q, k, k, k, v, v, v)


def band_attn_dq(name, q, k, v, do, lse, delta, seg):
    s, c = q.shape
    tq = _pick(seg, (256, 128))
    per_seg = seg // tq

    wrows = tq + 2 * ATT_HALF

    def body(q_ref, kp, km, kn, vp, vm, vn, do_ref, lse_ref, dl_ref, dq_ref, kw, vw):
        j0 = (pl.program_id(0) % per_seg) * tq
        _fill_window(kw, kp, km, kn, tq)
        _fill_window(vw, vp, vm, vn, tq)
        valid = _band_valid(tq, wrows, j0, seg, True)
        first = _first_head()
        for pair in range(c // LANES):
            lanes = pl.ds(pair * LANES, LANES)
            sc = _dot(_stack_heads(q_ref[:, lanes], first), kw[:, lanes], "nt")
            p = jnp.where(valid, jnp.exp(sc - _head_columns(lse_ref, pair)), 0.0)
            dp = _dot(_stack_heads(do_ref[:, lanes], first), vw[:, lanes], "nt")
            ds = p * (dp - _head_columns(dl_ref, pair))
            dq = _dot(ds.astype(BF16), kw[:, lanes], "nn")
            dq_ref[:, lanes] = _unstack_heads(dq, first, tq).astype(BF16)

    win = _window_specs(tq, s, c)
    tile = _row_spec(tq, c)
    return pl.pallas_call(body, name=name, grid=(s // tq,), in_specs=[tile] + win + win + [tile, tile, tile],
                          out_specs=tile, out_shape=jax.ShapeDtypeStruct((s, c), BF16),
                          scratch_shapes=[pltpu.VMEM((wrows, c), BF16)] * 2,
                          compiler_params=_params("parallel"))(q, k, k, k, v, v, v, do, lse, delta)


def band_attn_dkv(name, q, k, v, do, lse, delta, seg):
    s, c = q.shape
    tk = _pick(seg, (256, 128))
    per_seg = seg // tk

    wrows = tk + 2 * ATT_HALF

    def body(k_ref, v_ref, qp, qm, qn, dop, dom, don, lp, lm, ln, dlp, dlm, dln, dk_ref, dv_ref, qw, dow, lsew, dlw):
        j0 = (pl.program_id(0) % per_seg) * tk
        _fill_window(qw, qp, qm, qn, tk)
        _fill_window(dow, dop, dom, don, tk)
        _fill_window(lsew, lp, lm, ln, tk)
        _fill_window(dlw, dlp, dlm, dln, tk)
        valid = _band_valid(tk, wrows, j0, seg, False)
        first = _first_head()
        for pair in range(c // LANES):
            lanes = pl.ds(pair * LANES, LANES)
            qq = _stack_heads(qw[:, lanes], first)
            dd = _stack_heads(dow[:, lanes], first)
            sc = _dot(qq, k_ref[:, lanes], "nt")
            p = jnp.where(valid, jnp.exp(sc - _head_columns(lsew, pair)), 0.0)
            dv_ref[:, lanes] = _dot(p.astype(BF16), dd, "tn").astype(BF16)
            dp = _dot(dd, v_ref[:, lanes], "nt")
            ds = p * (dp - _head_columns(dlw, pair))
            dk_ref[:, lanes] = _dot(ds.astype(BF16), qq, "tn").astype(BF16)

    win = _window_specs(tk, s, c)
    tile = _row_spec(tk, c)
    osh = jax.ShapeDtypeStruct((s, c), BF16)
    return pl.pallas_call(body, name=name, grid=(s // tk,), in_specs=[tile, tile] + win * 4,
                          out_specs=[tile, tile], out_shape=[osh, osh],
                          scratch_shapes=[pltpu.VMEM((wrows, c), BF16)] * 2 + [pltpu.VMEM((wrows, c), F32)] * 2,
                          compiler_params=_params("parallel"))(k, v, q, q, q, do, do, do, lse, lse, lse,
                                                               delta, delta, delta)


def attn_combine_fwd(name, os_, lses, gmix):
    s, c = os_[0].shape
    tm = _pick(s, (256, 128))

    def body(o1, o2, o3, l1, l2, l3, gx_ref, yb_ref, lse_ref, y_ref):
        la, lb, lc = l1[...], l2[...], l3[...]
        m = jnp.maximum(jnp.maximum(la, lb), lc)
        ea, eb, ec = jnp.exp(la - m), jnp.exp(lb - m), jnp.exp(lc - m)
        den = ea + eb + ec
        yb = (ea * o1[...].astype(F32) + eb * o2[...].astype(F32) + ec * o3[...].astype(F32)) / den
        yb_ref[...] = yb
        lse_ref[...] = m + jnp.log(den)
        y_ref[...] = (yb * _rms_scale(yb) * gx_ref[...]).astype(BF16)

    row = _row_spec(tm, c)
    return pl.pallas_call(body, name=name, grid=(s // tm,), in_specs=[row] * 6 + [_const_spec((1, c))],
                          out_specs=[row, row, row],
                          out_shape=[jax.ShapeDtypeStruct((s, c), F32), jax.ShapeDtypeStruct((s, c), F32),
                                     jax.ShapeDtypeStruct((s, c), BF16)],
                          compiler_params=_params("parallel"))(*os_, *lses, gmix)


def attn_combine_bwd(name, dy, yb, gmix):
    s, c = yb.shape
    tm = _pick(s, (256, 128))
    half = c // 2

    def body(d1, d2, yb_ref, gx_ref, dyb_ref, dl_ref, dgx_ref):
        i = pl.program_id(0)
        d = jnp.concatenate([d1[...], d2[...]], axis=1)
        yb_ = yb_ref[...]
        dyb, dgx_rows = _rms_bwd(d, yb_, gx_ref[...])
        dyb_ref[...] = dyb.astype(BF16)

        @pl.when(i == 0)
        def _():
            dgx_ref[...] = jnp.zeros_like(dgx_ref)

        dgx_ref[...] += _sum8(dgx_rows)
        ri = lax.broadcasted_iota(jnp.int32, (LANES, LANES), 0) // HEAD_DIM
        ci = lax.broadcasted_iota(jnp.int32, (LANES, LANES), 1) // HEAD_DIM
        same_head = jnp.where(ri == ci, 1.0, 0.0).astype(BF16)
        prod = dyb * yb_
        for j in range(c // LANES):
            pj = prod[:, j * LANES:(j + 1) * LANES]
            hi = pj.astype(BF16)
            lo = (pj - hi.astype(F32)).astype(BF16)
            dl_ref[:, pl.ds(j * LANES, LANES)] = _dot(hi, same_head, "nn") + _dot(lo, same_head, "nn")

    row = _row_spec(tm, c)
    return pl.pallas_call(body, name=name, grid=(s // tm,),
                          in_specs=[_row_spec(tm, half, 1), _row_spec(tm, half, 2), row, _const_spec((1, c))],
                          out_specs=[row, row, _const_spec((8, c))],
                          out_shape=[jax.ShapeDtypeStruct((s, c), BF16), jax.ShapeDtypeStruct((s, c), F32),
                                     jax.ShapeDtypeStruct((8, c), F32)],
                          compiler_params=_params("arbitrary"))(dy, dy, yb, gmix)


def _ew_rows(rows, cols, n_bufs):
    budget = 24 * 1024 * 1024 // (8 * n_bufs * cols)
    return _pick(rows, tuple(t for t in (2048, 1024, 512, 256, 128, 64, 32, 16, 8) if t <= max(budget, 8)))


def elementwise(name, fn, ins, out_dtypes):
    shape = ins[0].shape
    cols = shape[-1]
    rows = math.prod(shape[:-1])
    tr = _ew_rows(rows, cols, len(ins) + len(out_dtypes))
    n_in = len(ins)

    def body(*refs):
        outs = fn(*[r[...] for r in refs[:n_in]])
        for o_ref, o in zip(refs[n_in:], outs):
            o_ref[...] = o.astype(o_ref.dtype)

    spec = _row_spec(tr, cols)
    res = pl.pallas_call(body, name=name, grid=(rows // tr,), in_specs=[spec] * n_in, out_specs=[spec] * len(out_dtypes),
                         out_shape=[jax.ShapeDtypeStruct((rows, cols), dt) for dt in out_dtypes],
                         compiler_params=_params("parallel"))(*[t.reshape(rows, cols) for t in ins])
    return [r.reshape(shape) for r in res]


def _adamw(g, w, m, v):
    m = ADAM_B1 * m + (1.0 - ADAM_B1) * g
    v = ADAM_B2 * v + (1.0 - ADAM_B2) * (g * g)
    m_hat = m / (1.0 - ADAM_B1 ** ADAM_STEP)
    v_hat = v / (1.0 - ADAM_B2 ** ADAM_STEP)
    delta = -ADAM_LR * (m_hat / (jnp.sqrt(v_hat) + ADAM_EPS) + ADAM_WD * w)
    return delta, m, v


def cast_into_gathered(w, place, layer):
    _, r, c = w.shape
    tr = _ew_rows(r, c, 2)

    def body(p_ref, w_ref, o_ref):
        o_ref[...] = w_ref[...].astype(BF16)

    grid_spec = pltpu.PrefetchScalarGridSpec(
        num_scalar_prefetch=1, grid=(r // tr,),
        in_specs=[pl.BlockSpec((None, tr, c), lambda i, p: (layer, i, 0))],
        out_specs=pl.BlockSpec((None, None, tr, c), lambda i, p: (p[0], 0, i, 0)))
    return pl.pallas_call(body, name="cast_into_gathered", grid_spec=grid_spec,
                          out_shape=jax.ShapeDtypeStruct((N_GROUPS, 1, r, c), BF16),
                          compiler_params=_params("parallel"))(place, w)


def sum_core_halves(name, grad, theirs, place):
    g, r, c = grad.shape
    h = r // 2
    tr = _ew_rows(h, c, 3)
    nh = h // tr

    def body(p_ref, a_ref, b_ref, o_ref):
        o_ref[...] = (a_ref[...].astype(F32) + b_ref[...].astype(F32)).astype(BF16)

    blk = pl.BlockSpec((None, tr, c), lambda gi, i, p: (gi, i, 0))
    grid_spec = pltpu.PrefetchScalarGridSpec(
        num_scalar_prefetch=1, grid=(g, nh),
        in_specs=[pl.BlockSpec((None, tr, c), lambda gi, i, p: (gi, p[1] * nh + i, 0)), blk], out_specs=blk)
    return pl.pallas_call(body, name=name, grid_spec=grid_spec, out_shape=jax.ShapeDtypeStruct((g, h, c), BF16),
                          compiler_params=_params("parallel", "parallel"))(place, grad, theirs)


def sum_chip_partials(name, parts, recv, place, layer, n_layers, into):
    _, h, c = parts.shape
    tr = _ew_rows(h, c, 5)
    nh = h // tr

    def body(p_ref, o_ref, r0, r1, r2, *rest):
        rest[-1][...] = ((o_ref[...].astype(F32) + r0[...].astype(F32)) + r1[...].astype(F32)) + r2[...].astype(F32)

    in_specs = [pl.BlockSpec((None, tr, c), lambda i, p: (p[0], i, 0))]
    in_specs += [pl.BlockSpec((None, tr, c), lambda i, p, j=j: (j, i, 0)) for j in range(3)]
    args = [place, parts, recv, recv, recv]
    aliases = {}
    if into is not None:
        in_specs.append(ANY)
        args.append(into)
        aliases = {5: 0}
    grid_spec = pltpu.PrefetchScalarGridSpec(
        num_scalar_prefetch=1, grid=(nh,), in_specs=in_specs,
        out_specs=pl.BlockSpec((None, tr, c), lambda i, p: (layer, p[1] * nh + i, 0)))
    return pl.pallas_call(body, name=name, grid_spec=grid_spec, input_output_aliases=aliases,
                          out_shape=jax.ShapeDtypeStruct((n_layers, 2 * h, c), F32),
                          compiler_params=_params("parallel"))(*args)


def small_update(name, gall, w, m, v):
    _, rows, cols = gall.shape
    tr = _ew_rows(rows, cols, 16)

    def body(g_ref, w_ref, m_ref, v_ref, go, do, mo, vo):
        g = g_ref[0]
        for dev in range(1, 8):
            g = g + g_ref[dev]
        delta, mn, vn = _adamw(g, w_ref[...], m_ref[...], v_ref[...])
        go[...] = g
        do[...] = delta
        mo[...] = mn
        vo[...] = vn

    spec = _row_spec(tr, cols)
    sh = jax.ShapeDtypeStruct((rows, cols), F32)
    return pl.pallas_call(body, name=name, grid=(rows // tr,),
                          in_specs=[pl.BlockSpec((8, tr, cols), lambda i: (0, i, 0)), spec, spec, spec],
                          out_specs=[spec] * 4, out_shape=[sh] * 4, compiler_params=_params("parallel"))(gall, w, m, v)


def _place():
    x, y, c = lax.axis_index("x"), lax.axis_index("y"), lax.axis_index("c")
    chips = [(1 - x, y), (x, 1 - y), (1 - x, 1 - y)]
    return x, y, c, chips


def _remote(src, dst, send_sems, recv_sems, k, to):
    return pltpu.make_async_remote_copy(src_ref=src, dst_ref=dst, send_sem=send_sems.at[k], recv_sem=recv_sems.at[k],
                                        device_id=to, device_id_type=MESH)


ANY = pl.BlockSpec(memory_space=pl.ANY)


def allgather_small(name, block):
    m_per, n = block.shape

    def body(x_ref, out_ref, send_sems, recv_sems, local_sem):
        x, y, c, chips = _place()
        me, sibling = (x, y, c), (x, y, 1 - c)

        def rows(px, py, pc):
            return out_ref.at[pl.ds((4 * px + 2 * py + pc) * m_per, m_per), :]

        def copy(k, blk, to, src=None):
            return _remote(rows(*blk) if src is None else src, rows(*blk), send_sems, recv_sems, k, to)

        mine = pltpu.make_async_copy(x_ref, rows(*me), local_sem)
        mine.start()
        first = [copy(0, me, sibling, src=x_ref)]
        first += [copy(1 + j, me, (*chip, c), src=x_ref) for j, chip in enumerate(chips)]
        for cp in first:
            cp.start()
        passed = [copy(4 + j, (*chip, c), sibling) for j, chip in enumerate(chips)]
        for j, chip in enumerate(chips):
            copy(1 + j, (*chip, c), me).wait_recv()
            passed[j].start()
        copy(0, sibling, me).wait_recv()
        for j, chip in enumerate(chips):
            copy(4 + j, (*chip, 1 - c), me).wait_recv()
        for cp in first + passed:
            cp.wait_send()
        mine.wait()

    return pl.pallas_call(body, name=name, out_shape=jax.ShapeDtypeStruct((8 * m_per, n), block.dtype),
                          in_specs=[pl.BlockSpec(memory_space=pltpu.VMEM)], out_specs=pl.BlockSpec(memory_space=pltpu.VMEM),
                          scratch_shapes=[pltpu.SemaphoreType.DMA((7,)), pltpu.SemaphoreType.DMA((7,)),
                                          pltpu.SemaphoreType.DMA],
                          compiler_params=pltpu.CompilerParams(vmem_limit_bytes=VMEM_LIMIT))(block)


def gather_weights(bufs):
    n = len(bufs)

    def body(*refs):
        start, finish = _gather_plan(refs[n:2 * n], *refs[2 * n:])
        start()
        finish()

    return pl.pallas_call(body, name="gather_weights",
                          out_shape=[jax.ShapeDtypeStruct(t.shape, t.dtype) for t in bufs],
                          in_specs=[ANY] * n, out_specs=[ANY] * n, input_output_aliases={a: a for a in range(n)},
                          scratch_shapes=_gather_sems(n))(*bufs)


def _gather_sems(n):
    return [pltpu.SemaphoreType.DMA((6 * n,)), pltpu.SemaphoreType.DMA((6 * n,))]


def _gather_plan(outs, send_sems, recv_sems):
    n = len(outs)
    x, y, c, chips = _place()
    g0 = 2 * x + y
    sibling = (x, y, 1 - c)
    groups = [2 * cx + cy for cx, cy in chips]

    def part(a, g, cc):
        h = outs[a].shape[2] // 2
        return outs[a].at[g, :, pl.ds(cc * h, h), :]

    def ici(a, j):
        return _remote(part(a, g0, c), part(a, g0, c), send_sems, recv_sems, 3 * a + j, (*chips[j], c))

    def ici_arrival(a, j):
        return _remote(part(a, groups[j], c), part(a, groups[j], c), send_sems, recv_sems, 3 * a + j, (*chips[j], c))

    def passed(a, j, cc):
        return _remote(part(a, groups[j], cc), part(a, groups[j], cc), send_sems, recv_sems, 3 * n + 3 * a + j, sibling)

    def start():
        for a in range(n):
            for j in range(3):
                ici(a, j).start()

    def finish():
        for a in range(n):
            for j in range(3):
                ici_arrival(a, j).wait_recv()
                passed(a, j, c).start()
        for a in range(n):
            for j in range(3):
                passed(a, j, 1 - c).wait_recv()
        for a in range(n):
            for j in range(3):
                ici(a, j).wait_send()
                passed(a, j, c).wait_send()

    return start, finish


def exchange_core_halves(name, grads):
    n = len(grads)
    halves = [t.shape[1] // 2 for t in grads]

    def body(*refs):
        ins, theirs = refs[:n], refs[n:2 * n]
        send_sems, recv_sems = refs[2 * n:]
        x, y, c, _ = _place()
        sibling = (x, y, 1 - c)

        def give(a, g):
            return _remote(ins[a].at[g, pl.ds((1 - c) * halves[a], halves[a]), :], theirs[a].at[g], send_sems,
                           recv_sems, N_GROUPS * a + g, sibling)

        for a in range(n):
            for g in range(N_GROUPS):
                give(a, g).start()
        for a in range(n):
            for g in range(N_GROUPS):
                give(a, g).wait_recv()
        for a in range(n):
            for g in range(N_GROUPS):
                give(a, g).wait_send()

    shapes = [jax.ShapeDtypeStruct((t.shape[0], h, t.shape[2]), t.dtype) for t, h in zip(grads, halves)]
    k = N_GROUPS * n
    return pl.pallas_call(body, name=name, out_shape=shapes, in_specs=[ANY] * n, out_specs=[ANY] * n,
                          scratch_shapes=[pltpu.SemaphoreType.DMA((k,)), pltpu.SemaphoreType.DMA((k,))])(*grads)


def exchange_chip_partials(name, parts):
    n = len(parts)

    def body(*refs):
        start, finish = _chip_exchange_plan(refs[:n], refs[n:2 * n], *refs[2 * n:])
        start()
        finish()

    return pl.pallas_call(body, name=name, out_shape=_chip_exchange_shapes(parts), in_specs=[ANY] * n,
                          out_specs=[ANY] * n, scratch_shapes=_chip_exchange_sems(n))(*parts)


def _chip_exchange_shapes(parts):
    return [jax.ShapeDtypeStruct((3,) + t.shape[1:], t.dtype) for t in parts]


def _chip_exchange_sems(n):
    return [pltpu.SemaphoreType.DMA((3 * n,)), pltpu.SemaphoreType.DMA((3 * n,))]


def _chip_exchange_plan(ins, recv, send_sems, recv_sems):
    n = len(ins)
    x, y, c, chips = _place()

    def give(a, j):
        return _remote(ins[a].at[2 * chips[j][0] + chips[j][1]], recv[a].at[j], send_sems, recv_sems, 3 * a + j,
                       (*chips[j], c))

    def start():
        for a in range(n):
            for j in range(3):
                give(a, j).start()

    def finish():
        for a in range(n):
            for j in range(3):
                give(a, j).wait_recv()
        for a in range(n):
            for j in range(3):
                give(a, j).wait_send()

    return start, finish


def share_reduced_halves(bufs):
    n = len(bufs)
    n_layers = bufs[0].shape[0]
    halves = [t.shape[1] // 2 for t in bufs]

    def body(*refs):
        outs = refs[n:2 * n]
        send_sems, recv_sems = refs[2 * n:]
        x, y, c, _ = _place()
        sibling = (x, y, 1 - c)

        def give(l, a, cc):
            part = outs[a].at[l, pl.ds(cc * halves[a], halves[a]), :]
            return _remote(part, part, send_sems, recv_sems, l * n + a, sibling)

        for l in range(n_layers):
            for a in range(n):
                give(l, a, c).start()
        for l in range(n_layers):
            for a in range(n):
                give(l, a, 1 - c).wait_recv()
        for l in range(n_layers):
            for a in range(n):
                give(l, a, c).wait_send()

    k = n_layers * n
    return pl.pallas_call(body, name="share_reduced_halves", out_shape=[jax.ShapeDtypeStruct(t.shape, t.dtype) for t in bufs],
                          in_specs=[ANY] * n, out_specs=[ANY] * n, input_output_aliases={a: a for a in range(n)},
                          scratch_shapes=[pltpu.SemaphoreType.DMA((k,)), pltpu.SemaphoreType.DMA((k,))])(*bufs)


def reduce_in_chip(grads, place):
    theirs = exchange_core_halves("exchange_core_halves", grads)
    return [sum_core_halves("sum_core_halves", g, t, place) for g, t in zip(grads, theirs)]


def reduce_over_chips(parts, recv, place, layer, n_layers, into):
    return [sum_chip_partials("sum_chip_partials", p, r, place, layer, n_layers, None if into is None else into[a])
            for a, (p, r) in enumerate(zip(parts, recv))]


def _pack(parts):
    flat = jnp.concatenate([p.reshape(-1) for p in parts])
    pad = (-flat.shape[0]) % (8 * LANES)
    if pad:
        flat = jnp.concatenate([flat, jnp.zeros((pad,), flat.dtype)])
    return flat.reshape(-1, LANES)


def _unpack(buf, shapes):
    flat = buf.reshape(-1)
    out, at = [], 0
    for sh in shapes:
        size = math.prod(sh)
        out.append(flat[at:at + size].reshape(sh))
        at += size
    return out


def kernel(x, w_in, conv_w, conv_b, conv_ln_g, conv_ln_b, pool_w, pool_scale, g_mix, w_out, g_pre_mix, g_post_mix, g_pre_ffn, g_post_ffn, w_gate, w_up, w_down, loss_target, m_w_in, m_conv_w, m_conv_b, m_conv_ln_g, m_conv_ln_b, m_pool_w, m_pool_scale, m_g_mix, m_w_out, m_g_pre_mix, m_g_post_mix, m_g_pre_ffn, m_g_post_ffn, m_w_gate, m_w_up, m_w_down, v_w_in, v_conv_w, v_conv_b, v_conv_ln_g, v_conv_ln_b, v_pool_w, v_pool_scale, v_g_mix, v_w_out, v_g_pre_mix, v_g_post_mix, v_g_pre_ffn, v_g_post_ffn, v_w_gate, v_w_up, v_w_down):
    xs, target = x[0], loss_target[0]
    s, d = xs.shape
    n_layers = w_in.shape[0]
    my_group = 2 * lax.axis_index("x") + lax.axis_index("y")

    big = [w_in, w_out, w_gate, w_up, w_down]
    place = jnp.stack([my_group, lax.axis_index("c")]).astype(jnp.int32)
    gathered = [[cast_into_gathered(w, place, l) for w in big] for l in range(n_layers)]
    gathered[0] = gather_weights(gathered[0])
    cw_pad = jnp.pad(conv_w, ((0, 0), (0, 32 - CONV_WIDTH), (0, 0)))
    cw_all = allgather_small("allgather_conv_w", cw_pad.reshape(n_layers * 32, LANES))
    cw_full = cw_all.reshape(N_GROUPS, 2, n_layers, 32, LANES)[:, 0].transpose(1, 2, 0, 3).reshape(n_layers, 32, C_A)
    pw_bf16 = pool_w.astype(BF16)
    tabs = rope_tables(s)
    gmix_a, gmix_b, gmix_c = g_mix[:, :C_A], g_mix[:, C_A:C_A + C_B], g_mix[:, C_A + C_B:]
    row = lambda t: t.reshape(1, -1)

    saved = []
    (h,) = norm_fwd("norm_first", xs, None, None, g_pre_mix[0])
    xin = xs
    for l in range(n_layers):
        win_g, wout_g, wgate_g, wup_g, wdown_g = gathered[l]
        proj = mm_act_wcols("proj", h, win_g, 0, F32)
        ya = conv_fwd("conv_fwd", proj, cw_full[l], row(conv_b[l]), row(conv_ln_g[l]), row(conv_ln_b[l]), row(gmix_a[l]))
        qr, kr, vr = rope_fwd("rope_fwd", proj, tabs)
        qkv_d, os_, lses = [], [], []
        for dil in DILATIONS:
            qd, kd, vd = (to_dilated(t, dil) for t in (qr, kr, vr))
            o, lse = band_attn_fwd(f"band_attn_fwd_d{dil}", qd, kd, vd, s // dil)
            qkv_d.append((qd, kd, vd))
            os_.append(from_dilated(o, dil))
            lses.append(from_dilated(lse, dil))
        yb, lse_joint, ybn = attn_combine_fwd("attn_combine_fwd", os_, lses, row(gmix_b[l]))
        yc = pool_fwd("pool_fwd", proj, pw_bf16[l], row(pool_scale[l]), row(gmix_c[l]))
        y = jnp.concatenate([ya, ybn, yc], axis=1)
        z = mm_act_wrows("mix_out", y, wout_g, 0, F32)
        x1, h2 = norm_fwd("norm_mid", xin, z, g_post_mix[l], g_pre_ffn[l])
        if l + 1 < n_layers:
            (gt, up, act), gathered[l + 1] = ffn_up("ffn_up_gather", h2, wgate_g, wup_g, 0, gathered[l + 1])
        else:
            (gt, up, act), _ = ffn_up("ffn_up", h2, wgate_g, wup_g, 0)
        f = mm_act_wrows("ffn_down", act, wdown_g, 0, F32)
        if l + 1 < n_layers:
            x2, h_next = norm_fwd("norm_next", x1, f, g_post_ffn[l], g_pre_mix[l + 1])
        else:
            (x2,), h_next = norm_fwd("norm_last", x1, f, g_post_ffn[l], None), None
        saved.append(dict(xin=xin, h=h, proj=proj, qkv_d=qkv_d, yb=yb, lse=lse_joint, y=y, z=z, x1=x1, h2=h2,
                          gt=gt, up=up, act=act, f=f))
        xin, h = x2, h_next

    dx, sq = loss_head(xin, target)
    loss = lax.psum(0.5 * jnp.sum(sq) / d, ("x", "y", "c"))

    small = {k: [None] * n_layers for k in ("cw", "cb", "lg", "lb", "pw", "ps", "gmix", "gpre_mix", "gpost_mix",
                                            "gpre_ffn", "gpost_ffn")}
    reduced = None
    df, dg = norm_bwd("norm_bwd_top", dx, None, None, None, saved[-1]["f"], g_post_ffn[n_layers - 1])
    small["gpost_ffn"][n_layers - 1] = dg
    pending = None
    for l in reversed(range(n_layers)):
        sv = saved[l]
        win_g, wout_g, wgate_g, wup_g, wdown_g = gathered[l]
        dgt, dup = ffn_down_bwd("ffn_down_bwd", df, wdown_g, 0, sv["gt"], sv["up"])
        d_wdown = mm_plain("wgrad_rows_ffn", "tn", sv["act"], df, BF16)
        if pending is None:
            dh2, _ = ffn_in_bwd("ffn_in_bwd", dgt, dup, wgate_g, wup_g, 0)
        else:
            dh2, recv = ffn_in_bwd("ffn_in_bwd_exchange", dgt, dup, wgate_g, wup_g, 0, pending[1])
            reduced = reduce_over_chips(pending[1], recv, place, pending[0], n_layers, reduced)
        d_wgate = mm_wgrad_cols("wgrad_cols_ffn", sv["h2"], dgt, wgate_g.shape[3])
        d_wup = mm_wgrad_cols("wgrad_cols_ffn", sv["h2"], dup, wup_g.shape[3])
        dx1, small["gpre_ffn"][l], dz, small["gpost_mix"][l] = norm_bwd(
            "norm_bwd_mid", dx, dh2, sv["x1"], g_pre_ffn[l], sv["z"], g_post_mix[l])
        dy = mm_act_wrows_t("mix_out_bwd", dz, wout_g, 0, F32)
        d_wout = mm_plain("wgrad_rows_mix", "tn", sv["y"], dz, BF16)
        proj = sv["proj"]
        dag, dcw, small["cb"][l], small["lg"][l], small["lb"][l], dgx_a = conv_bwd(
            "conv_bwd", proj, dy, cw_full[l], row(conv_b[l]), row(conv_ln_g[l]), row(conv_ln_b[l]), row(gmix_a[l]))
        small["cw"][l] = dcw.reshape(32, 8, C_A).sum(axis=1)
        dyb, delta, dgx_b = attn_combine_bwd("attn_combine_bwd", dy, sv["yb"], row(gmix_b[l]))
        dqs, dks, dvs = [], [], []
        for dil, (qd, kd, vd) in zip(DILATIONS, sv["qkv_d"]):
            do_d, lse_d, delta_d = (to_dilated(t, dil) for t in (dyb, sv["lse"], delta))
            dq = band_attn_dq(f"band_attn_dq_d{dil}", qd, kd, vd, do_d, lse_d, delta_d, s // dil)
            dk, dv = band_attn_dkv(f"band_attn_dkv_d{dil}", qd, kd, vd, do_d, lse_d, delta_d, s // dil)
            dqs.append(from_dilated(dq, dil))
            dks.append(from_dilated(dk, dil))
            dvs.append(from_dilated(dv, dil))
        dqkv = rope_bwd("rope_bwd", dqs, dks, dvs, tabs)
        du, small["pw"][l], small["ps"][l], dgx_c = pool_bwd("pool_bwd", proj, dy, pw_bf16[l], row(pool_scale[l]),
                                                              row(gmix_c[l]))
        small["gmix"][l] = jnp.concatenate([dgx_a, dgx_b, dgx_c], axis=1)
        dproj = jnp.concatenate([dag, dqkv, du], axis=1)
        dh1 = mm_act_wcols_t("proj_bwd", dproj, win_g, 0, F32)
        d_win = mm_wgrad_cols("wgrad_cols_proj", sv["h"], dproj, win_g.shape[3])
        if l > 0:
            dx, small["gpre_mix"][l], df, small["gpost_ffn"][l - 1] = norm_bwd(
                "norm_bwd_next", dx1, dh1, sv["xin"], g_pre_mix[l], saved[l - 1]["f"], g_post_ffn[l - 1])
        else:
            dx, small["gpre_mix"][l] = norm_bwd("norm_bwd_first", dx1, dh1, sv["xin"], g_pre_mix[l], None, None)
        d_wout = d_wout.reshape(N_GROUPS, -1, d_wout.shape[1])
        d_wdown = d_wdown.reshape(N_GROUPS, -1, d_wdown.shape[1])
        pending = (l, reduce_in_chip([d_win, d_wout, d_wgate, d_wup, d_wdown], place))
    recv = exchange_chip_partials("exchange_chip_partials", pending[1])
    reduced = reduce_over_chips(pending[1], recv, place, pending[0], n_layers, reduced)

    big_grads = share_reduced_halves(reduced)
    big_m = [m_w_in, m_w_out, m_w_gate, m_w_up, m_w_down]
    big_v = [v_w_in, v_w_out, v_w_gate, v_w_up, v_w_down]
    big_upd = [elementwise("adamw", _adamw, [g, w, m, v], [F32, F32, F32])
               for g, w, m, v in zip(big_grads, big, big_m, big_v)]

    def stack8(k):
        return jnp.stack([t.sum(axis=0) if t.shape[0] == 8 and t.ndim == 2 else t for t in small[k]])

    order = ("cw", "cb", "lg", "lb", "pw", "ps", "gmix", "gpre_mix", "gpost_mix", "gpre_ffn", "gpost_ffn")
    partial = [jnp.stack(small["cw"])] + [stack8(k) for k in order[1:]]
    rep_w = [conv_b, conv_ln_g, conv_ln_b, pool_w, pool_scale, g_mix, g_pre_mix, g_post_mix, g_pre_ffn, g_post_ffn]
    rep_m = [m_conv_b, m_conv_ln_g, m_conv_ln_b, m_pool_w, m_pool_scale, m_g_mix, m_g_pre_mix, m_g_post_mix,
             m_g_pre_ffn, m_g_post_ffn]
    rep_v = [v_conv_b, v_conv_ln_g, v_conv_ln_b, v_pool_w, v_pool_scale, v_g_mix, v_g_pre_mix, v_g_post_mix,
             v_g_pre_ffn, v_g_post_ffn]
    packed = _pack(partial)
    rows = packed.shape[0]
    gall = allgather_small("allgather_small_grads", packed).reshape(8, rows, LANES)
    cw_hole = jnp.zeros((n_layers, 32, C_A), F32)
    rep_out = small_update("small_update", gall, _pack([cw_hole] + rep_w), _pack([cw_hole] + rep_m),
                           _pack([cw_hole] + rep_v))
    shapes = [cw_hole.shape] + [t.shape for t in rep_w]
    rep_g, rep_d, rep_mn, rep_vn = ([t for t in _unpack(buf, shapes)[1:]] for buf in rep_out)
    cw_rows = n_layers * 32
    gall_cw = gall[:, :cw_rows * C_A // LANES].reshape(8, cw_rows, C_A // LANES, LANES)
    gall_cw = lax.dynamic_index_in_dim(gall_cw, my_group, axis=2, keepdims=False)
    pad_cw = lambda t: jnp.pad(t, ((0, 0), (0, 32 - CONV_WIDTH), (0, 0))).reshape(cw_rows, LANES)
    cw_out = small_update("conv_w_update", gall_cw, pad_cw(conv_w), pad_cw(m_conv_w), pad_cw(v_conv_w))
    cw_g, cw_d, cw_mn, cw_vn = (t.reshape(n_layers, 32, LANES)[:, :CONV_WIDTH] for t in cw_out)

    def assemble(bigs, cw, reps):
        return [bigs[0], cw] + list(reps[:6]) + [bigs[1]] + list(reps[6:]) + list(bigs[2:])

    grads = assemble(big_grads, cw_g, rep_g)
    deltas = assemble([u[0] for u in big_upd], cw_d, rep_d)
    new_m = assemble([u[1] for u in big_upd], cw_mn, rep_mn)
    new_v = assemble([u[2] for u in big_upd], cw_vn, rep_vn)
    return (loss, dx[None], *grads, *deltas, *new_m, *new_v)
```

```python
import functools
import math

import jax
import jax.numpy as jnp
from jax import lax
from jax.experimental import pallas as pl
from jax.experimental.pallas import tpu as pltpu

F32 = jnp.float32
BF16 = jnp.bfloat16
MESH = pl.DeviceIdType.MESH

EPS = 1e-6
NEG = -1e30
C_A = 512
C_B = 1024
C_C = 512
HEAD_DIM = 64
POOL_WINDOWS = (2, 4, 8, 16)
C_G = 128
CONV_WIDTH = 31
CONV_HALF = 15
DILATIONS = (1, 4, 16)
ATT_HALF = 64
ROT_DIM = 16
ROPE_THETA = 500000.0
ADAM_LR, ADAM_B1, ADAM_B2, ADAM_EPS, ADAM_WD, ADAM_STEP = 0.001, 0.9, 0.999, 1e-08, 0.01, 10
N_GROUPS = 4
HALO = 32
LANES = 128
VMEM_LIMIT = 48 * 1024 * 1024


def _params(*sem):
    return pltpu.CompilerParams(dimension_semantics=sem, vmem_limit_bytes=VMEM_LIMIT)


def _pick(n, prefs):
    for p in prefs:
        if p <= n and n % p == 0:
            return p
    return n


def _rms_scale(t):
    return lax.rsqrt(jnp.mean(t * t, axis=-1, keepdims=True) + EPS)


def _rms_bwd(d, x, g):
    r = _rms_scale(x)
    u = d * g
    dx = r * u - x * (r * r * r) * jnp.mean(u * x, axis=-1, keepdims=True)
    return dx, d * x * r


def _sum8(t):
    rows, cols = t.shape
    return jnp.sum(t.reshape(rows // 8, 8, cols), axis=0)


def _dot(a, b, mode):
    dn = {"nn": (((1,), (0,)), ((), ())), "nt": (((1,), (1,)), ((), ())), "tn": (((0,), (0,)), ((), ()))}[mode]
    return lax.dot_general(a, b, dn, preferred_element_type=F32)


def _row_spec(tm, cols, col_block=0):
    return pl.BlockSpec((tm, cols), lambda i, cb=col_block: (i, cb))


def _const_spec(shape):
    return pl.BlockSpec(shape, lambda i: tuple(0 for _ in shape))


def norm_fwd(name, x, z, g_post, g_next):
    s, d = x.shape
    tm = _pick(s, (512, 256, 128))
    has_z, has_h = z is not None, g_next is not None

    def body(*refs):
        refs = list(refs)
        x_ref = refs.pop(0)
        xn = x_ref[...]
        if has_z:
            z_ref, gp_ref = refs.pop(0), refs.pop(0)
            zz = z_ref[...]
            xn = xn + zz * _rms_scale(zz) * gp_ref[...]
        if has_h:
            gn_ref = refs.pop(0)
        if has_z:
            refs.pop(0)[...] = xn
        if has_h:
            refs.pop(0)[...] = (xn * _rms_scale(xn) * gn_ref[...]).astype(BF16)

    ins, specs, outs, ospecs = [x], [_row_spec(tm, d)], [], []
    if has_z:
        ins += [z, g_post.reshape(1, d)]
        specs += [_row_spec(tm, d), _const_spec((1, d))]
        outs.append(jax.ShapeDtypeStruct((s, d), F32))
        ospecs.append(_row_spec(tm, d))
    if has_h:
        ins.append(g_next.reshape(1, d))
        specs.append(_const_spec((1, d)))
        outs.append(jax.ShapeDtypeStruct((s, d), BF16))
        ospecs.append(_row_spec(tm, d))
    res = pl.pallas_call(body, name=name, grid=(s // tm,), in_specs=specs, out_specs=ospecs, out_shape=outs,
                         compiler_params=_params("parallel"))(*ins)
    return list(res)


def norm_bwd(name, dres, dh, xin, g_pre, zin, g_post):
    s, d = dres.shape
    tm = _pick(s, (256, 128))
    has_pre, has_post = dh is not None, zin is not None

    def body(*refs):
        refs = list(refs)
        i = pl.program_id(0)
        dx = refs.pop(0)[...]
        if has_pre:
            dh_ref, x_ref, g_ref = refs.pop(0), refs.pop(0), refs.pop(0)
            ddx, dg = _rms_bwd(dh_ref[...].astype(F32), x_ref[...], g_ref[...])
            dx = dx + ddx
        if has_post:
            z_ref, gp_ref = refs.pop(0), refs.pop(0)
            dz, dgp = _rms_bwd(dx, z_ref[...], gp_ref[...])
        if has_pre:
            refs.pop(0)[...] = dx
            dg_ref = refs.pop(0)

            @pl.when(i == 0)
            def _():
                dg_ref[...] = jnp.zeros_like(dg_ref)

            dg_ref[...] += _sum8(dg)
        if has_post:
            refs.pop(0)[...] = dz.astype(BF16)
            dgp_ref = refs.pop(0)

            @pl.when(i == 0)
            def _():
                dgp_ref[...] = jnp.zeros_like(dgp_ref)

            dgp_ref[...] += _sum8(dgp)

    ins, specs, outs, ospecs = [dres], [_row_spec(tm, d)], [], []
    if has_pre:
        ins += [dh, xin, g_pre.reshape(1, d)]
        specs += [_row_spec(tm, d), _row_spec(tm, d), _const_spec((1, d))]
        outs += [jax.ShapeDtypeStruct((s, d), F32), jax.ShapeDtypeStruct((8, d), F32)]
        ospecs += [_row_spec(tm, d), _const_spec((8, d))]
    if has_post:
        ins += [zin, g_post.reshape(1, d)]
        specs += [_row_spec(tm, d), _const_spec((1, d))]
        outs += [jax.ShapeDtypeStruct((s, d), BF16), jax.ShapeDtypeStruct((8, d), F32)]
        ospecs += [_row_spec(tm, d), _const_spec((8, d))]
    res = pl.pallas_call(body, name=name, grid=(s // tm,), in_specs=specs, out_specs=ospecs, out_shape=outs,
                         compiler_params=_params("arbitrary"))(*ins)
    return list(res)


def loss_head(x, target):
    s, d = x.shape
    tm = _pick(s, (512, 256, 128))

    def body(x_ref, t_ref, dx_ref, sq_ref):
        i = pl.program_id(0)
        e = x_ref[...] - t_ref[...]
        dx_ref[...] = e * (1.0 / d)

        @pl.when(i == 0)
        def _():
            sq_ref[...] = jnp.zeros_like(sq_ref)

        sq_ref[...] += _sum8(e * e)

    return pl.pallas_call(body, name="loss_head", grid=(s // tm,),
                          in_specs=[_row_spec(tm, d), _row_spec(tm, d)],
                          out_specs=[_row_spec(tm, d), _const_spec((8, d))],
                          out_shape=[jax.ShapeDtypeStruct((s, d), F32), jax.ShapeDtypeStruct((8, d), F32)],
                          compiler_params=_params("arbitrary"))(x, target)


def matmul(name, mode, a, b, *, grid, tk_steps, a_spec, b_spec, o_spec, out_shape, acc_shape, b_flat=None):
    nk = tk_steps

    def rhs(b_ref):
        return b_ref[...] if b_flat is None else b_ref[...].reshape(b_flat)

    def body(a_ref, b_ref, o_ref, *scratch):
        if nk == 1:
            o_ref[...] = _dot(a_ref[...], rhs(b_ref), mode).astype(o_ref.dtype)
            return
        acc = scratch[0]
        kk = pl.program_id(2)

        @pl.when(kk == 0)
        def _():
            acc[...] = jnp.zeros_like(acc)

        acc[...] += _dot(a_ref[...], rhs(b_ref), mode)

        @pl.when(kk == nk - 1)
        def _():
            o_ref[...] = acc[...].astype(o_ref.dtype)

    return pl.pallas_call(body, name=name, grid=grid, in_specs=[a_spec, b_spec], out_specs=o_spec, out_shape=out_shape,
                          scratch_shapes=[] if nk == 1 else [pltpu.VMEM(acc_shape, F32)],
                          compiler_params=_params("parallel", "parallel", "arbitrary"))(a, b)


TM_PREFS = (1024, 512, 256, 128)
TW_PREFS = (1408, 1152, 1024, 512, 384, 256, 128)
TK_PREFS = (512, 384, 256, 128)
FULL_K_PREFS = (2048, 1024) + TK_PREFS


def mm_act_wcols(name, a, wg, layer, out_dtype):
    s, k = a.shape
    g, _, _, ng = wg.shape
    tm, tk, tn = _pick(s, TM_PREFS), _pick(k, FULL_K_PREFS), _pick(ng, TW_PREFS)
    per = ng // tn
    return matmul(name, "nn", a, wg, grid=(g * per, s // tm, k // tk), tk_steps=k // tk,
                  a_spec=pl.BlockSpec((tm, tk), lambda j, i, kk: (i, kk)),
                  b_spec=pl.BlockSpec((None, None, tk, tn), lambda j, i, kk: (j // per, layer, kk, j % per)),
                  o_spec=pl.BlockSpec((tm, tn), lambda j, i, kk: (i, j)),
                  out_shape=jax.ShapeDtypeStruct((s, g * ng), out_dtype), acc_shape=(tm, tn))


def mm_act_wcols_t(name, a, wg, layer, out_dtype):
    s, n = a.shape
    g, _, k, ng = wg.shape
    tm, tn, tk = _pick(s, TM_PREFS), _pick(k, TM_PREFS), _pick(ng, (1152, 384, 128))
    per = ng // tk
    return matmul(name, "nt", a, wg, grid=(s // tm, k // tn, g * per), tk_steps=g * per,
                  a_spec=pl.BlockSpec((tm, tk), lambda i, j, kk: (i, kk)),
                  b_spec=pl.BlockSpec((None, None, tn, tk), lambda i, j, kk: (kk // per, layer, j, kk % per)),
                  o_spec=pl.BlockSpec((tm, tn), lambda i, j, kk: (i, j)),
                  out_shape=jax.ShapeDtypeStruct((s, k), out_dtype), acc_shape=(tm, tn))


def mm_act_wrows(name, a, wr, layer, out_dtype):
    s, k = a.shape
    g, _, kg, n = wr.shape
    tm, tn, tk = _pick(s, TM_PREFS), _pick(n, TM_PREFS), _pick(kg, (1408, 512, 256, 128))
    if k <= FULL_K_PREFS[0]:
        return matmul(name, "nn", a, wr, grid=(n // tn, s // tm, 1), tk_steps=1,
                      a_spec=pl.BlockSpec((tm, k), lambda j, i, kk: (i, 0)),
                      b_spec=pl.BlockSpec((g, None, kg, tn), lambda j, i, kk: (0, layer, 0, j)),
                      o_spec=pl.BlockSpec((tm, tn), lambda j, i, kk: (i, j)),
                      out_shape=jax.ShapeDtypeStruct((s, n), out_dtype), acc_shape=(tm, tn), b_flat=(k, tn))
    per = kg // tk
    return matmul(name, "nn", a, wr, grid=(s // tm, n // tn, g * per), tk_steps=g * per,
                  a_spec=pl.BlockSpec((tm, tk), lambda i, j, kk: (i, kk)),
                  b_spec=pl.BlockSpec((None, None, tk, tn), lambda i, j, kk: (kk // per, layer, kk % per, j)),
                  o_spec=pl.BlockSpec((tm, tn), lambda i, j, kk: (i, j)),
                  out_shape=jax.ShapeDtypeStruct((s, n), out_dtype), acc_shape=(tm, tn))


def mm_act_wrows_t(name, a, wr, layer, out_dtype):
    s, n = a.shape
    g, _, kg, _ = wr.shape
    tm, tn, tk = _pick(s, TM_PREFS), _pick(kg, (1408, 512, 256, 128)), _pick(n, FULL_K_PREFS)
    per = kg // tn
    return matmul(name, "nt", a, wr, grid=(g * per, s // tm, n // tk), tk_steps=n // tk,
                  a_spec=pl.BlockSpec((tm, tk), lambda j, i, kk: (i, kk)),
                  b_spec=pl.BlockSpec((None, None, tn, tk), lambda j, i, kk: (j // per, layer, j % per, kk)),
                  o_spec=pl.BlockSpec((tm, tn), lambda j, i, kk: (i, j)),
                  out_shape=jax.ShapeDtypeStruct((s, g * kg), out_dtype), acc_shape=(tm, tn))


def mm_wgrad_cols(name, a, dy, ng):
    s, k = a.shape
    n = dy.shape[1]
    g = n // ng
    tm = _pick(k, TM_PREFS)
    tk = _pick(s, TM_PREFS)
    tn = _pick(ng, TW_PREFS)
    per = ng // tn
    return matmul(name, "tn", a, dy, grid=(k // tm, g * per, s // tk), tk_steps=s // tk,
                  a_spec=pl.BlockSpec((tk, tm), lambda i, j, kk: (kk, i)),
                  b_spec=pl.BlockSpec((tk, tn), lambda i, j, kk: (kk, j)),
                  o_spec=pl.BlockSpec((None, tm, tn), lambda i, j, kk: (j // per, i, j % per)),
                  out_shape=jax.ShapeDtypeStruct((g, k, ng), BF16), acc_shape=(tm, tn))


def mm_plain(name, mode, a, b, out_dtype):
    if mode == "nn":
        (m, k), n = a.shape, b.shape[1]
    elif mode == "nt":
        (m, k), n = a.shape, b.shape[0]
    else:
        (k, m), n = a.shape, b.shape[1]
    tm = _pick(m, (1408,) + TM_PREFS)
    tn = _pick(n, TM_PREFS)
    tk = _pick(k, TM_PREFS)
    a_spec = (pl.BlockSpec((tk, tm), lambda i, j, kk: (kk, i)) if mode == "tn"
              else pl.BlockSpec((tm, tk), lambda i, j, kk: (i, kk)))
    b_spec = (pl.BlockSpec((tn, tk), lambda i, j, kk: (j, kk)) if mode == "nt"
              else pl.BlockSpec((tk, tn), lambda i, j, kk: (kk, j)))
    return matmul(name, mode, a, b, grid=(m // tm, n // tn, k // tk), tk_steps=k // tk, a_spec=a_spec, b_spec=b_spec,
                  o_spec=pl.BlockSpec((tm, tn), lambda i, j, kk: (i, j)),
                  out_shape=jax.ShapeDtypeStruct((m, n), out_dtype), acc_shape=(tm, tn))


def _silu_parts(gt):
    sg = jax.nn.sigmoid(gt)
    return gt * sg, sg


def ffn_up(name, h, wgate, wup, layer, gather=()):
    s, d = h.shape
    g, _, _, ng = wgate.shape
    assert d <= FULL_K_PREFS[0]
    tm = _pick(s, (256, 128))
    tn = _pick(ng, TW_PREFS)
    per = ng // tn

    n = len(gather)
    grid = (g * per, s // tm)

    def body(h_ref, wg_ref, wu_ref, *rest):
        gt_ref, up_ref, act_ref = rest[n:n + 3]
        if n:
            start, finish = _gather_plan(rest[n + 3:2 * n + 3], *rest[2 * n + 3:])
            pl.when((pl.program_id(0) == 0) & (pl.program_id(1) == 0))(start)
        hh = h_ref[...]
        gt = _dot(hh, wg_ref[...], "nn")
        up = _dot(hh, wu_ref[...], "nn")
        gt_ref[...] = gt.astype(BF16)
        up_ref[...] = up.astype(BF16)
        act_ref[...] = (_silu_parts(gt)[0] * up).astype(BF16)
        if n:
            pl.when((pl.program_id(0) == grid[0] - 1) & (pl.program_id(1) == grid[1] - 1))(finish)

    wspec = pl.BlockSpec((None, None, d, tn), lambda j, i: (j // per, layer, 0, j % per))
    ospec = pl.BlockSpec((tm, tn), lambda j, i: (i, j))
    osh = jax.ShapeDtypeStruct((s, g * ng), BF16)
    res = pl.pallas_call(body, name=name, grid=grid,
                         in_specs=[pl.BlockSpec((tm, d), lambda j, i: (i, 0)), wspec, wspec] + [ANY] * n,
                         out_specs=[ospec, ospec, ospec] + [ANY] * n,
                         out_shape=[osh, osh, osh] + [jax.ShapeDtypeStruct(t.shape, t.dtype) for t in gather],
                         input_output_aliases={3 + a: 3 + a for a in range(n)},
                         scratch_shapes=_gather_sems(n) if n else [],
                         compiler_params=_params(*(["arbitrary"] * 2 if n else ["parallel"] * 2)))(h, wgate, wup, *gather)
    return res[:3], list(res[3:])


def ffn_down_bwd(name, df, wdown, layer, gt, up):
    s, d = df.shape
    g, _, fg, _ = wdown.shape
    f = g * fg
    assert d <= FULL_K_PREFS[0]
    tm = _pick(s, (256, 128))
    tn = _pick(fg, TW_PREFS)
    per = fg // tn

    def body(df_ref, wd_ref, gt_ref, up_ref, dgt_ref, dup_ref):
        da = _dot(df_ref[...], wd_ref[...], "nt")
        gt = gt_ref[...].astype(F32)
        up = up_ref[...].astype(F32)
        silu, sg = _silu_parts(gt)
        dgt_ref[...] = (da * up * (sg * (1.0 + gt * (1.0 - sg)))).astype(BF16)
        dup_ref[...] = (da * silu).astype(BF16)

    ospec = pl.BlockSpec((tm, tn), lambda j, i: (i, j))
    osh = jax.ShapeDtypeStruct((s, f), BF16)
    return pl.pallas_call(body, name=name, grid=(f // tn, s // tm),
                          in_specs=[pl.BlockSpec((tm, d), lambda j, i: (i, 0)),
                                    pl.BlockSpec((None, None, tn, d), lambda j, i: (j // per, layer, j % per, 0)),
                                    ospec, ospec],
                          out_specs=[ospec, ospec], out_shape=[osh, osh],
                          compiler_params=_params("parallel", "parallel"))(df, wdown, gt, up)


def ffn_in_bwd(name, dgt, dup, wgate, wup, layer, exchange=()):
    s, f = dgt.shape
    g, _, d, ng = wgate.shape
    tm = _pick(s, (512, 256, 128))
    tn = _pick(d, TM_PREFS)
    tk = _pick(ng, TW_PREFS)
    per = ng // tk
    nk = g * per

    n = len(exchange)
    grid = (s // tm, d // tn, nk)

    def body(dg_ref, du_ref, wg_ref, wu_ref, *rest):
        o_ref, acc = rest[n], rest[2 * n + 1]
        kk = pl.program_id(2)
        if n:
            start, finish = _chip_exchange_plan(rest[:n], rest[n + 1:2 * n + 1], *rest[2 * n + 2:])
            pl.when((pl.program_id(0) == 0) & (pl.program_id(1) == 0) & (kk == 0))(start)

        @pl.when(kk == 0)
        def _():
            acc[...] = jnp.zeros_like(acc)

        acc[...] += _dot(dg_ref[...], wg_ref[...], "nt") + _dot(du_ref[...], wu_ref[...], "nt")

        @pl.when(kk == nk - 1)
        def _():
            o_ref[...] = acc[...]

        if n:
            pl.when((pl.program_id(0) == grid[0] - 1) & (pl.program_id(1) == grid[1] - 1) & (kk == nk - 1))(finish)

    aspec = pl.BlockSpec((tm, tk), lambda i, j, kk: (i, kk))
    wspec = pl.BlockSpec((None, None, tn, tk), lambda i, j, kk: (kk // per, layer, j, kk % per))
    res = pl.pallas_call(body, name=name, grid=grid, in_specs=[aspec, aspec, wspec, wspec] + [ANY] * n,
                         out_specs=[pl.BlockSpec((tm, tn), lambda i, j, kk: (i, j))] + [ANY] * n,
                         out_shape=[jax.ShapeDtypeStruct((s, d), F32)] + _chip_exchange_shapes(exchange),
                         scratch_shapes=[pltpu.VMEM((tm, tn), F32)] + (_chip_exchange_sems(n) if n else []),
                         compiler_params=_params(*(["arbitrary"] * 3 if n else ["parallel", "parallel", "arbitrary"])))(
                             dgt, dup, wgate, wup, *exchange)
    return res[0], list(res[1:])


def _halo_specs(tm, s, cols, cb, halo=HALO):
    r = tm // halo
    last = s // halo - 1
    return [pl.BlockSpec((halo, cols), lambda i: (jnp.maximum(i * r - 1, 0), cb)),
            pl.BlockSpec((tm, cols), lambda i: (i, cb)),
            pl.BlockSpec((halo, cols), lambda i: (jnp.minimum((i + 1) * r, last), cb))]


def _edge_masked(prev_ref, next_ref, i, nt):
    return jnp.where(i > 0, prev_ref[...], 0.0), jnp.where(i < nt - 1, next_ref[...], 0.0)


def _glu(a, gate):
    return a * jax.nn.sigmoid(gate)


def _conv_post(conv, lg, lb, gm):
    mu = jnp.mean(conv, axis=-1, keepdims=True)
    xc = conv - mu
    rs = lax.rsqrt(jnp.mean(xc * xc, axis=-1, keepdims=True) + EPS)
    xhat = xc * rs
    yl = xhat * lg + lb
    ya, sg = _silu_parts(yl)
    return xhat, rs, yl, sg, ya


def _fill_glu(hg, ap, am, an, gp, gm_, gn, i, nt, tm):
    a0, a2 = _edge_masked(ap, an, i, nt)
    g0, g2 = _edge_masked(gp, gn, i, nt)
    hg[pl.ds(0, HALO), :] = _glu(a0, g0)
    hg[pl.ds(HALO, tm), :] = _glu(am[...], gm_[...])
    hg[pl.ds(HALO + tm, HALO), :] = _glu(a2, g2)


def conv_fwd(name, proj, cw, cb, lg, lb, gmix):
    s = proj.shape[0]
    tm = _pick(s, (256, 128))
    nt = s // tm

    def body(ap, am, an, gp, gm_, gn, cw_ref, cb_ref, lg_ref, lb_ref, gx_ref, o_ref, hg):
        i = pl.program_id(0)
        _fill_glu(hg, ap, am, an, gp, gm_, gn, i, nt, tm)
        acc = jnp.zeros((tm, C_A), F32) + cb_ref[...]
        for t in range(CONV_WIDTH):
            acc = acc + cw_ref[pl.ds(t, 1), :] * hg[pl.ds(HALO - CONV_HALF + t, tm), :]
        ya = _conv_post(acc, lg_ref[...], lb_ref[...], None)[4]
        o_ref[...] = (ya * _rms_scale(ya) * gx_ref[...]).astype(BF16)

    vec = _const_spec((1, C_A))
    return pl.pallas_call(body, name=name, grid=(nt,),
                          in_specs=_halo_specs(tm, s, C_A, 0) + _halo_specs(tm, s, C_A, 1)
                          + [_const_spec((32, C_A)), vec, vec, vec, vec],
                          out_specs=_row_spec(tm, C_A), out_shape=jax.ShapeDtypeStruct((s, C_A), BF16),
                          scratch_shapes=[pltpu.VMEM((tm + 2 * HALO, C_A), F32)],
                          compiler_params=_params("parallel"))(proj, proj, proj, proj, proj, proj, cw, cb, lg, lb, gmix)


def conv_bwd(name, proj, dy, cw, cb, lg, lb, gmix):
    s = proj.shape[0]
    tm = _pick(s, (256, 128))
    nt = s // tm
    te = tm + HALO
    off = HALO // 2

    def body(ap, am, an, gp, gm_, gn, dp, dm, dn, cw_ref, cb_ref, lg_ref, lb_ref, gx_ref,
             dag_ref, dcw_ref, dcb_ref, dlg_ref, dlb_ref, dgx_ref, hg, dc, dyx):
        i = pl.program_id(0)
        _fill_glu(hg, ap, am, an, gp, gm_, gn, i, nt, tm)
        d0, d2 = _edge_masked(dp, dn, i, nt)
        dyx[pl.ds(0, HALO), :] = d0
        dyx[pl.ds(HALO, tm), :] = dm[...]
        dyx[pl.ds(HALO + tm, HALO), :] = d2
        lg, lb, gx = lg_ref[...], lb_ref[...], gx_ref[...]
        conv = jnp.zeros((te, C_A), F32) + cb_ref[...]
        for t in range(CONV_WIDTH):
            conv = conv + cw_ref[pl.ds(t, 1), :] * hg[pl.ds(1 + t, te), :]
        xhat, rs, yl, sg, ya = _conv_post(conv, lg, lb, gx)
        dout = dyx[pl.ds(off, te), :]
        dya, dgx_rows = _rms_bwd(dout, ya, gx)
        dyl = dya * (sg * (1.0 + yl * (1.0 - sg)))
        dxh = dyl * lg
        dconv = rs * (dxh - jnp.mean(dxh, axis=-1, keepdims=True)
                      - xhat * jnp.mean(dxh * xhat, axis=-1, keepdims=True))
        dc[...] = dconv

        @pl.when(i == 0)
        def _():
            for r in (dcw_ref, dcb_ref, dlg_ref, dlb_ref, dgx_ref):
                r[...] = jnp.zeros_like(r)

        dcb_ref[...] += _sum8(dconv[off:off + tm])
        dlg_ref[...] += _sum8((dyl * xhat)[off:off + tm])
        dlb_ref[...] += _sum8(dyl[off:off + tm])
        dgx_ref[...] += _sum8(dgx_rows[off:off + tm])
        dcm = dconv[off:off + tm]
        dhg = jnp.zeros((tm, C_A), F32)
        for t in range(CONV_WIDTH):
            dhg = dhg + cw_ref[pl.ds(t, 1), :] * dc[pl.ds(HALO - 1 - t, tm), :]
            dcw_ref[pl.ds(8 * t, 8), :] += _sum8(dcm * hg[pl.ds(HALO - CONV_HALF + t, tm), :])
        a, gate = am[...], gm_[...]
        sgate = jax.nn.sigmoid(gate)
        dag_ref[:, pl.ds(0, C_A)] = (dhg * sgate).astype(BF16)
        dag_ref[:, pl.ds(C_A, C_A)] = (dhg * a * sgate * (1.0 - sgate)).astype(BF16)

    vec = _const_spec((1, C_A))
    acc8 = _const_spec((8, C_A))
    sh8 = jax.ShapeDtypeStruct((8, C_A), F32)
    return pl.pallas_call(body, name=name, grid=(nt,),
                          in_specs=_halo_specs(tm, s, C_A, 0) + _halo_specs(tm, s, C_A, 1) + _halo_specs(tm, s, C_A, 0)
                          + [_const_spec((32, C_A)), vec, vec, vec, vec],
                          out_specs=[_row_spec(tm, 2 * C_A), _const_spec((32 * 8, C_A)), acc8, acc8, acc8, acc8],
                          out_shape=[jax.ShapeDtypeStruct((s, 2 * C_A), BF16), jax.ShapeDtypeStruct((32 * 8, C_A), F32),
                                     sh8, sh8, sh8, sh8],
                          scratch_shapes=[pltpu.VMEM((tm + 2 * HALO, C_A), F32), pltpu.VMEM((te, C_A), F32),
                                          pltpu.VMEM((tm + 2 * HALO, C_A), F32)],
                          compiler_params=_params("arbitrary"))(proj, proj, proj, proj, proj, proj, dy, dy, dy,
                                                                cw, cb, lg, lb, gmix)


def _pool_counts(pos, win, s):
    lo = jnp.maximum(pos - win // 2, 0)
    hi = jnp.minimum(pos + win - win // 2, s)
    return jnp.maximum(hi - lo, 1).astype(F32)


def _pooled(uext, base, rows, pos, s):
    outs = []
    for gi, win in enumerate(POOL_WINDOWS):
        lanes = pl.ds(gi * C_G, C_G)
        acc = jnp.zeros((rows, C_G), F32)
        for o in range(-(win // 2), win - win // 2):
            acc = acc + uext[pl.ds(base + o, rows), lanes]
        outs.append(acc / _pool_counts(pos, win, s) - uext[pl.ds(base, rows), lanes])
    return outs


def pool_fwd(name, proj, pw, scale, gmix):
    s = proj.shape[0]
    tm = _pick(s, (256, 128))
    nt = s // tm
    ucol = (proj.shape[1] - C_C) // C_C

    def body(up, um, un, pw_ref, sc_ref, gx_ref, o_ref, uext):
        i = pl.program_id(0)
        u0, u2 = _edge_masked(up, un, i, nt)
        uext[pl.ds(0, HALO), :] = u0
        uext[pl.ds(HALO, tm), :] = um[...]
        uext[pl.ds(HALO + tm, HALO), :] = u2
        pos = i * tm + lax.broadcasted_iota(jnp.int32, (tm, 1), 0)
        pooled = _pooled(uext, HALO, tm, pos, s)
        mixed = jnp.concatenate([_dot(pooled[g].astype(BF16), pw_ref[g], "nn") for g in range(4)], axis=1)
        yc = mixed * sc_ref[...]
        o_ref[...] = (yc * _rms_scale(yc) * gx_ref[...]).astype(BF16)

    vec = _const_spec((1, C_C))
    return pl.pallas_call(body, name=name, grid=(nt,),
                          in_specs=_halo_specs(tm, s, C_C, ucol) + [_const_spec((4, C_G, C_G)), vec, vec],
                          out_specs=_row_spec(tm, C_C), out_shape=jax.ShapeDtypeStruct((s, C_C), BF16),
                          scratch_shapes=[pltpu.VMEM((tm + 2 * HALO, C_C), F32)],
                          compiler_params=_params("parallel"))(proj, proj, proj, pw, scale, gmix)


def pool_bwd(name, proj, dy, pw, scale, gmix):
    s = proj.shape[0]
    tm = _pick(s, (256, 128))
    nt = s // tm
    te = tm + HALO
    off = HALO // 2
    ucol = (proj.shape[1] - C_C) // C_C
    dcol = (dy.shape[1] - C_C) // C_C

    def body(up, um, un, dp, dm, dn, pw_ref, sc_ref, gx_ref, du_ref, dpw_ref, dsc_ref, dgx_ref, uext, dyx, qs, dps):
        i = pl.program_id(0)
        u0, u2 = _edge_masked(up, un, i, nt)
        uext[pl.ds(0, HALO), :] = u0
        uext[pl.ds(HALO, tm), :] = um[...]
        uext[pl.ds(HALO + tm, HALO), :] = u2
        d0, d2 = _edge_masked(dp, dn, i, nt)
        dyx[pl.ds(0, HALO), :] = d0
        dyx[pl.ds(HALO, tm), :] = dm[...]
        dyx[pl.ds(HALO + tm, HALO), :] = d2
        pos = i * tm - off + lax.broadcasted_iota(jnp.int32, (te, 1), 0)
        pooled = _pooled(uext, off, te, pos, s)
        mixed = jnp.concatenate([_dot(pooled[g].astype(BF16), pw_ref[g], "nn") for g in range(4)], axis=1)
        sc = sc_ref[...]
        yc = mixed * sc
        dyc, dgx_rows = _rms_bwd(dyx[pl.ds(off, te), :], yc, gx_ref[...])
        dmixed = dyc * sc

        @pl.when(i == 0)
        def _():
            for r in (dpw_ref, dsc_ref, dgx_ref):
                r[...] = jnp.zeros_like(r)

        dsc_ref[...] += _sum8((dyc * mixed)[off:off + tm])
        dgx_ref[...] += _sum8(dgx_rows[off:off + tm])
        for gi, win in enumerate(POOL_WINDOWS):
            lanes = pl.ds(gi * C_G, C_G)
            dmg = dmixed[:, gi * C_G:(gi + 1) * C_G].astype(BF16)
            dpw_ref[gi] += _dot(pooled[gi][off:off + tm].astype(BF16), dmg[off:off + tm], "tn")
            dpl = _dot(dmg, pw_ref[gi], "nt")
            dps[:, lanes] = dpl
            qs[:, lanes] = dpl / _pool_counts(pos, win, s)
        for gi, win in enumerate(POOL_WINDOWS):
            lanes = pl.ds(gi * C_G, C_G)
            acc = jnp.zeros((tm, C_G), F32) - dps[pl.ds(off, tm), lanes]
            for o in range(-(win // 2) + 1, win // 2 + 1):
                acc = acc + qs[pl.ds(off + o, tm), lanes]
            du_ref[:, lanes] = acc.astype(BF16)

    vec = _const_spec((1, C_C))
    acc8 = _const_spec((8, C_C))
    sh8 = jax.ShapeDtypeStruct((8, C_C), F32)
    return pl.pallas_call(body, name=name, grid=(nt,),
                          in_specs=_halo_specs(tm, s, C_C, ucol) + _halo_specs(tm, s, C_C, dcol)
                          + [_const_spec((4, C_G, C_G)), vec, vec],
                          out_specs=[_row_spec(tm, C_C), _const_spec((4, C_G, C_G)), acc8, acc8],
                          out_shape=[jax.ShapeDtypeStruct((s, C_C), BF16), jax.ShapeDtypeStruct((4, C_G, C_G), F32),
                                     sh8, sh8],
                          scratch_shapes=[pltpu.VMEM((tm + 2 * HALO, C_C), F32), pltpu.VMEM((tm + 2 * HALO, C_C), F32),
                                          pltpu.VMEM((te, C_C), F32), pltpu.VMEM((te, C_C), F32)],
                          compiler_params=_params("arbitrary"))(proj, proj, proj, dy, dy, dy, pw, scale, gmix)


def rope_tables(s):
    pos = jnp.arange(s, dtype=F32)
    inv = ROPE_THETA ** (-jnp.arange(0, ROT_DIM, 2, dtype=F32) / ROT_DIM)
    ang = pos[:, None] * inv[None, :]
    cos, sin = jnp.cos(ang), jnp.sin(ang)
    half = ROT_DIM // 2
    rest = HEAD_DIM - ROT_DIM
    c = jnp.concatenate([cos, cos, jnp.ones((s, rest), F32)], axis=1)
    sa = jnp.concatenate([-sin, jnp.zeros((s, HEAD_DIM - half), F32)], axis=1)
    sb = jnp.concatenate([jnp.zeros((s, half), F32), sin, jnp.zeros((s, rest), F32)], axis=1)
    return tuple(jnp.concatenate([t, t], axis=1) for t in (c, sa, sb))


DIL_TILE = 256
N_CHUNKS = C_B // LANES


def _dilated_shape(s, d, dtype):
    return jax.ShapeDtypeStruct((s, C_B) if d == 1 else (d, s // d, C_B), dtype)


def _dilated_spec(tm, d):
    return _row_spec(tm, C_B) if d == 1 else pl.BlockSpec((d, tm // d, C_B), lambda i: (0, i, 0))


def _as_dilated(t, d):
    return t if d == 1 else t.reshape(d, t.shape[0] // d, t.shape[1])


def _dil_scratch(tm):
    return pltpu.VMEM((N_CHUNKS, tm, LANES), F32)


def _store_dilated(o_ref, scr, d, tm):
    for r in range(d):
        for j in range(N_CHUNKS):
            o_ref[r, :, pl.ds(j * LANES, LANES)] = scr[j, pl.ds(r, tm // d, stride=d), :].astype(o_ref.dtype)


def _load_dilated(in_ref, scr, d, tm):
    for r in range(d):
        for j in range(N_CHUNKS):
            scr[j, pl.ds(r, tm // d, stride=d), :] = in_ref[r, :, pl.ds(j * LANES, LANES)].astype(F32)


def rope_fwd(name, proj, tabs):
    s = proj.shape[0]
    tm = DIL_TILE
    scale = HEAD_DIM ** -0.5

    def body(q_ref, k_ref, v_ref, c_ref, sa_ref, sb_ref, *rest):
        outs, scr = rest[:9], rest[9:]
        c, sa, sb = c_ref[...], sa_ref[...], sb_ref[...]
        for j in range(N_CHUNKS):
            lanes = pl.ds(j * LANES, LANES)
            for which, (src, mul) in enumerate(((q_ref, scale), (k_ref, 1.0))):
                t = src[:, lanes]
                r = t * c + pltpu.roll(t, LANES - ROT_DIM // 2, 1) * sa + pltpu.roll(t, ROT_DIM // 2, 1) * sb
                scr[which][j] = r * mul
            scr[2][j] = v_ref[:, lanes]
            for which in range(3):
                outs[which][:, lanes] = scr[which][j].astype(BF16)
        for pi, d in enumerate(DILATIONS[1:]):
            for which in range(3):
                _store_dilated(outs[3 * (pi + 1) + which], scr[which], d, tm)

    tab = _row_spec(tm, LANES)
    res = pl.pallas_call(body, name=name, grid=(s // tm,),
                         in_specs=[_row_spec(tm, C_B, 1), _row_spec(tm, C_B, 2), _row_spec(tm, C_B, 3), tab, tab, tab],
                         out_specs=[_dilated_spec(tm, d) for d in DILATIONS for _ in range(3)],
                         out_shape=[_dilated_shape(s, d, BF16) for d in DILATIONS for _ in range(3)],
                         scratch_shapes=[_dil_scratch(tm)] * 3,
                         compiler_params=_params("parallel"))(proj, proj, proj, *tabs)
    return [tuple(t.reshape(s, C_B) for t in res[3 * pi:3 * pi + 3]) for pi in range(len(DILATIONS))]


def rope_bwd(name, dqs, dks, dvs, tabs):
    s = dqs[0].shape[0]
    tm = DIL_TILE
    scale = HEAD_DIM ** -0.5
    n_dil = len(DILATIONS)

    def body(*refs):
        ins = [refs[n_dil * which:n_dil * (which + 1)] for which in range(3)]
        c_ref, sa_ref, sb_ref, o_ref = refs[3 * n_dil:3 * n_dil + 4]
        scr = refs[3 * n_dil + 4:]
        for which in range(3):
            for pi, d in enumerate(DILATIONS[1:]):
                _load_dilated(ins[which][pi + 1], scr[which * (n_dil - 1) + pi], d, tm)
        c, sa, sb = c_ref[...], sa_ref[...], sb_ref[...]
        for j in range(N_CHUNKS):
            lanes = pl.ds(j * LANES, LANES)
            for which, mul in ((0, scale), (1, 1.0), (2, None)):
                d = ins[which][0][:, lanes].astype(F32)
                for pi in range(n_dil - 1):
                    d = d + scr[which * (n_dil - 1) + pi][j]
                if mul is not None:
                    d = (d * c + pltpu.roll(d * sa, ROT_DIM // 2, 1) + pltpu.roll(d * sb, LANES - ROT_DIM // 2, 1)) * mul
                o_ref[:, pl.ds(which * C_B + j * LANES, LANES)] = d.astype(BF16)

    tab = _row_spec(tm, LANES)
    args = [_as_dilated(t, d) for ts in (dqs, dks, dvs) for t, d in zip(ts, DILATIONS)]
    return pl.pallas_call(body, name=name, grid=(s // tm,),
                          in_specs=[_dilated_spec(tm, d) for _ in range(3) for d in DILATIONS] + [tab, tab, tab],
                          out_specs=_row_spec(tm, 3 * C_B), out_shape=jax.ShapeDtypeStruct((s, 3 * C_B), BF16),
                          scratch_shapes=[_dil_scratch(tm)] * (3 * (n_dil - 1)),
                          compiler_params=_params("parallel"))(*args, *tabs)


def _window_specs(tq, s, cols):
    r = tq // ATT_HALF
    last = s // ATT_HALF - 1
    return [pl.BlockSpec((ATT_HALF, cols), lambda i: (jnp.maximum(i * r - 1, 0), 0)),
            pl.BlockSpec((tq, cols), lambda i: (i, 0)),
            pl.BlockSpec((ATT_HALF, cols), lambda i: (jnp.minimum((i + 1) * r, last), 0))]


def _fill_window(win, prev_ref, main_ref, next_ref, tq):
    win[pl.ds(0, ATT_HALF), :] = prev_ref[...]
    win[pl.ds(ATT_HALF, tq), :] = main_ref[...]
    win[pl.ds(ATT_HALF + tq, ATT_HALF), :] = next_ref[...]


def _band_valid(tile, window, j0, seg, tile_is_rows):
    shape = (2 * tile, window) if tile_is_rows else (2 * window, tile)
    ri = lax.broadcasted_iota(jnp.int32, shape, 0)
    ci = lax.broadcasted_iota(jnp.int32, shape, 1)
    per_head = tile if tile_is_rows else window
    ri = jnp.where(ri >= per_head, ri - per_head, ri)
    ti, wi = (ri, ci) if tile_is_rows else (ci, ri)
    jw = j0 - ATT_HALF + wi
    return (jnp.abs(wi - ATT_HALF - ti) <= ATT_HALF) & (jw >= 0) & (jw < seg)


def _first_head():
    return lax.broadcasted_iota(jnp.int32, (1, LANES), 1) < HEAD_DIM


def _stack_heads(t, first):
    z = jnp.zeros_like(t)
    return jnp.concatenate([jnp.where(first, t, z), jnp.where(first, z, t)], axis=0)


def _unstack_heads(t2, first, rows):
    return jnp.where(first, t2[:rows], t2[rows:])


def _head_columns(ref, pair):
    return jnp.concatenate([ref[:, pl.ds(pair * LANES, 1)], ref[:, pl.ds(pair * LANES + HEAD_DIM, 1)]], axis=0)


def band_attn_fwd(name, q, k, v, seg):
    s, c = q.shape
    tq = _pick(seg, (256, 128))
    per_seg = seg // tq

    wrows = tq + 2 * ATT_HALF

    def body(q_ref, kp, km, kn, vp, vm, vn, o_ref, lse_ref, kw, vw):
        j0 = (pl.program_id(0) % per_seg) * tq
        _fill_window(kw, kp, km, kn, tq)
        _fill_window(vw, vp, vm, vn, tq)
        valid = _band_valid(tq, wrows, j0, seg, True)
        first = _first_head()
        for pair in range(c // LANES):
            lanes = pl.ds(pair * LANES, LANES)
            sc = jnp.where(valid, _dot(_stack_heads(q_ref[:, lanes], first), kw[:, lanes], "nt"), NEG)
            m = jnp.max(sc, axis=-1, keepdims=True)
            p = jnp.exp(sc - m)
            l = jnp.sum(p, axis=-1, keepdims=True)
            o = _dot(p.astype(BF16), vw[:, lanes], "nn") / l
            o_ref[:, lanes] = _unstack_heads(o, first, tq).astype(BF16)
            lse_ref[:, lanes] = _unstack_heads(m + jnp.log(l), first, tq)

    win = _window_specs(tq, s, c)
    tile = _row_spec(tq, c)
    return pl.pallas_call(body, name=name, grid=(s // tq,), in_specs=[tile] + win + win, out_specs=[tile, tile],
                          out_shape=[jax.ShapeDtypeStruct((s, c), BF16), jax.ShapeDtypeStruct((s, c), F32)],
                          scratch_shapes=[pltpu.VMEM((wrows, c), BF16)] * 2,
                          compiler_params=_params("parallel"))(q, k, k, k, v, v, v)


def band_attn_dq(name, q, k, v, do, lse, delta, seg):
    s, c = q.shape
    tq = _pick(seg, (256, 128))
    per_seg = seg // tq

    wrows = tq + 2 * ATT_HALF

    def body(q_ref, kp, km, kn, vp, vm, vn, do_ref, lse_ref, dl_ref, dq_ref, kw, vw):
        j0 = (pl.program_id(0) % per_seg) * tq
        _fill_window(kw, kp, km, kn, tq)
        _fill_window(vw, vp, vm, vn, tq)
        valid = _band_valid(tq, wrows, j0, seg, True)
        first = _first_head()
        for pair in range(c // LANES):
            lanes = pl.ds(pair * LANES, LANES)
            sc = _dot(_stack_heads(q_ref[:, lanes], first), kw[:, lanes], "nt")
            p = jnp.where(valid, jnp.exp(sc - _head_columns(lse_ref, pair)), 0.0)
            dp = _dot(_stack_heads(do_ref[:, lanes], first), vw[:, lanes], "nt")
            ds = p * (dp - _head_columns(dl_ref, pair))
            dq = _dot(ds.astype(BF16), kw[:, lanes], "nn")
            dq_ref[:, lanes] = _unstack_heads(dq, first, tq).astype(BF16)

    win = _window_specs(tq, s, c)
    tile = _row_spec(tq, c)
    return pl.pallas_call(body, name=name, grid=(s // tq,), in_specs=[tile] + win + win + [tile, tile, tile],
                          out_specs=tile, out_shape=jax.ShapeDtypeStruct((s, c), BF16),
                          scratch_shapes=[pltpu.VMEM((wrows, c), BF16)] * 2,
                          compiler_params=_params("parallel"))(q, k, k, k, v, v, v, do, lse, delta)


def band_attn_dkv(name, q, k, v, do, lse, delta, seg):
    s, c = q.shape
    tk = _pick(seg, (256, 128))
    per_seg = seg // tk

    wrows = tk + 2 * ATT_HALF

    def body(k_ref, v_ref, qp, qm, qn, dop, dom, don, lp, lm, ln, dlp, dlm, dln, dk_ref, dv_ref, qw, dow, lsew, dlw):
        j0 = (pl.program_id(0) % per_seg) * tk
        _fill_window(qw, qp, qm, qn, tk)
        _fill_window(dow, dop, dom, don, tk)
        _fill_window(lsew, lp, lm, ln, tk)
        _fill_window(dlw, dlp, dlm, dln, tk)
        valid = _band_valid(tk, wrows, j0, seg, False)
        first = _first_head()
        for pair in range(c // LANES):
            lanes = pl.ds(pair * LANES, LANES)
            qq = _stack_heads(qw[:, lanes], first)
            dd = _stack_heads(dow[:, lanes], first)
            sc = _dot(qq, k_ref[:, lanes], "nt")
            p = jnp.where(valid, jnp.exp(sc - _head_columns(lsew, pair)), 0.0)
            dv_ref[:, lanes] = _dot(p.astype(BF16), dd, "tn").astype(BF16)
            dp = _dot(dd, v_ref[:, lanes], "nt")
            ds = p * (dp - _head_columns(dlw, pair))
            dk_ref[:, lanes] = _dot(ds.astype(BF16), qq, "tn").astype(BF16)

    win = _window_specs(tk, s, c)
    tile = _row_spec(tk, c)
    osh = jax.ShapeDtypeStruct((s, c), BF16)
    return pl.pallas_call(body, name=name, grid=(s // tk,), in_specs=[tile, tile] + win * 4,
                          out_specs=[tile, tile], out_shape=[osh, osh],
                          scratch_shapes=[pltpu.VMEM((wrows, c), BF16)] * 2 + [pltpu.VMEM((wrows, c), F32)] * 2,
                          compiler_params=_params("parallel"))(k, v, q, q, q, do, do, do, lse, lse, lse,
                                                               delta, delta, delta)


def attn_combine_fwd(name, os_, lses, gmix):
    s, c = os_[0].shape
    tm = DIL_TILE
    n_dil = len(DILATIONS)

    def body(*refs):
        o_refs, l_refs, gx_ref = refs[:n_dil], refs[n_dil:2 * n_dil], refs[2 * n_dil]
        yb_ref, y_ref = refs[2 * n_dil + 1:2 * n_dil + 3]
        lse_refs = refs[2 * n_dil + 3:3 * n_dil + 3]
        scr = refs[3 * n_dil + 3:]
        scr_o, scr_l, scr_lse = scr[:n_dil - 1], scr[n_dil - 1:2 * (n_dil - 1)], scr[-1]
        for pi, d in enumerate(DILATIONS[1:]):
            _load_dilated(o_refs[pi + 1], scr_o[pi], d, tm)
            _load_dilated(l_refs[pi + 1], scr_l[pi], d, tm)
        sumsq = jnp.zeros((tm, 1), F32)
        for j in range(N_CHUNKS):
            lanes = pl.ds(j * LANES, LANES)
            ls = [l_refs[0][:, lanes]] + [t[j] for t in scr_l]
            vals = [o_refs[0][:, lanes].astype(F32)] + [t[j] for t in scr_o]
            m = functools.reduce(jnp.maximum, ls)
            es = [jnp.exp(l - m) for l in ls]
            den = functools.reduce(lambda a, b: a + b, es)
            yb = functools.reduce(lambda a, b: a + b, [e * v for e, v in zip(es, vals)]) / den
            yb_ref[:, lanes] = yb
            lse = m + jnp.log(den)
            lse_refs[0][:, lanes] = lse
            scr_lse[j] = lse
            sumsq = sumsq + jnp.sum(yb * yb, axis=-1, keepdims=True)
        r = lax.rsqrt(sumsq / c + EPS)
        y_ref[...] = (yb_ref[...] * r * gx_ref[...]).astype(BF16)
        for pi, d in enumerate(DILATIONS[1:]):
            _store_dilated(lse_refs[pi + 1], scr_lse, d, tm)

    row = _row_spec(tm, c)
    dil = [_dilated_spec(tm, d) for d in DILATIONS]
    res = pl.pallas_call(body, name=name, grid=(s // tm,), in_specs=dil + dil + [_const_spec((1, c))],
                         out_specs=[row, row] + dil,
                         out_shape=[jax.ShapeDtypeStruct((s, c), F32), jax.ShapeDtypeStruct((s, c), BF16)]
                         + [_dilated_shape(s, d, F32) for d in DILATIONS],
                         scratch_shapes=[_dil_scratch(tm)] * (2 * (n_dil - 1) + 1),
                         compiler_params=_params("parallel"))(
                             *[_as_dilated(t, d) for t, d in zip(os_, DILATIONS)],
                             *[_as_dilated(t, d) for t, d in zip(lses, DILATIONS)], gmix)
    return res[0], res[1], [t.reshape(s, c) for t in res[2:]]


def attn_combine_bwd(name, dy, yb, gmix):
    s, c = yb.shape
    tm = DIL_TILE
    half = c // 2
    n_dil = len(DILATIONS)

    def body(d1, d2, yb_ref, gx_ref, *rest):
        dyb_refs, dl_refs, dgx_ref = rest[:n_dil], rest[n_dil:2 * n_dil], rest[2 * n_dil]
        scr_dyb, scr_dl = rest[2 * n_dil + 1:]
        i = pl.program_id(0)
        d = jnp.concatenate([d1[...], d2[...]], axis=1)
        yb_ = yb_ref[...]
        dyb, dgx_rows = _rms_bwd(d, yb_, gx_ref[...])
        dyb_refs[0][...] = dyb.astype(BF16)

        @pl.when(i == 0)
        def _():
            dgx_ref[...] = jnp.zeros_like(dgx_ref)

        dgx_ref[...] += _sum8(dgx_rows)
        ri = lax.broadcasted_iota(jnp.int32, (LANES, LANES), 0) // HEAD_DIM
        ci = lax.broadcasted_iota(jnp.int32, (LANES, LANES), 1) // HEAD_DIM
        same_head = jnp.where(ri == ci, 1.0, 0.0).astype(BF16)
        prod = dyb * yb_
        for j in range(N_CHUNKS):
            pj = prod[:, j * LANES:(j + 1) * LANES]
            hi = pj.astype(BF16)
            lo = (pj - hi.astype(F32)).astype(BF16)
            delta = _dot(hi, same_head, "nn") + _dot(lo, same_head, "nn")
            dl_refs[0][:, pl.ds(j * LANES, LANES)] = delta
            scr_dl[j] = delta
            scr_dyb[j] = dyb[:, j * LANES:(j + 1) * LANES]
        for pi, dil in enumerate(DILATIONS[1:]):
            _store_dilated(dyb_refs[pi + 1], scr_dyb, dil, tm)
            _store_dilated(dl_refs[pi + 1], scr_dl, dil, tm)

    row = _row_spec(tm, c)
    dil_specs = [_dilated_spec(tm, d) for d in DILATIONS]
    res = pl.pallas_call(body, name=name, grid=(s // tm,),
                         in_specs=[_row_spec(tm, half, 1), _row_spec(tm, half, 2), row, _const_spec((1, c))],
                         out_specs=dil_specs + dil_specs + [_const_spec((8, c))],
                         out_shape=[_dilated_shape(s, d, BF16) for d in DILATIONS]
                         + [_dilated_shape(s, d, F32) for d in DILATIONS] + [jax.ShapeDtypeStruct((8, c), F32)],
                         scratch_shapes=[_dil_scratch(tm)] * 2,
                         compiler_params=_params("arbitrary"))(dy, dy, yb, gmix)
    return ([t.reshape(s, c) for t in res[:n_dil]], [t.reshape(s, c) for t in res[n_dil:2 * n_dil]], res[2 * n_dil])


def _ew_rows(rows, cols, n_bufs):
    budget = 24 * 1024 * 1024 // (8 * n_bufs * cols)
    return _pick(rows, tuple(t for t in (2048, 1024, 512, 256, 128, 64, 32, 16, 8) if t <= max(budget, 8)))


def elementwise(name, fn, ins, out_dtypes):
    shape = ins[0].shape
    cols = shape[-1]
    rows = math.prod(shape[:-1])
    tr = _ew_rows(rows, cols, len(ins) + len(out_dtypes))
    n_in = len(ins)

    def body(*refs):
        outs = fn(*[r[...] for r in refs[:n_in]])
        for o_ref, o in zip(refs[n_in:], outs):
            o_ref[...] = o.astype(o_ref.dtype)

    spec = _row_spec(tr, cols)
    res = pl.pallas_call(body, name=name, grid=(rows // tr,), in_specs=[spec] * n_in, out_specs=[spec] * len(out_dtypes),
                         out_shape=[jax.ShapeDtypeStruct((rows, cols), dt) for dt in out_dtypes],
                         compiler_params=_params("parallel"))(*[t.reshape(rows, cols) for t in ins])
    return [r.reshape(shape) for r in res]


def _adamw(g, w, m, v):
    m = ADAM_B1 * m + (1.0 - ADAM_B1) * g
    v = ADAM_B2 * v + (1.0 - ADAM_B2) * (g * g)
    m_hat = m / (1.0 - ADAM_B1 ** ADAM_STEP)
    v_hat = v / (1.0 - ADAM_B2 ** ADAM_STEP)
    delta = -ADAM_LR * (m_hat / (jnp.sqrt(v_hat) + ADAM_EPS) + ADAM_WD * w)
    return delta, m, v


def cast_into_gathered(w, place, layer):
    _, r, c = w.shape
    tr = _ew_rows(r, c, 2)

    def body(p_ref, w_ref, o_ref):
        o_ref[...] = w_ref[...].astype(BF16)

    grid_spec = pltpu.PrefetchScalarGridSpec(
        num_scalar_prefetch=1, grid=(r // tr,),
        in_specs=[pl.BlockSpec((None, tr, c), lambda i, p: (layer, i, 0))],
        out_specs=pl.BlockSpec((None, None, tr, c), lambda i, p: (p[0], 0, i, 0)))
    return pl.pallas_call(body, name="cast_into_gathered", grid_spec=grid_spec,
                          out_shape=jax.ShapeDtypeStruct((N_GROUPS, 1, r, c), BF16),
                          compiler_params=_params("parallel"))(place, w)


def sum_core_halves(name, grad, theirs, place):
    g, r, c = grad.shape
    h = r // 2
    tr = _ew_rows(h, c, 3)
    nh = h // tr

    def body(p_ref, a_ref, b_ref, o_ref):
        o_ref[...] = (a_ref[...].astype(F32) + b_ref[...].astype(F32)).astype(BF16)

    blk = pl.BlockSpec((None, tr, c), lambda gi, i, p: (gi, i, 0))
    grid_spec = pltpu.PrefetchScalarGridSpec(
        num_scalar_prefetch=1, grid=(g, nh),
        in_specs=[pl.BlockSpec((None, tr, c), lambda gi, i, p: (gi, p[1] * nh + i, 0)), blk], out_specs=blk)
    return pl.pallas_call(body, name=name, grid_spec=grid_spec, out_shape=jax.ShapeDtypeStruct((g, h, c), BF16),
                          compiler_params=_params("parallel", "parallel"))(place, grad, theirs)


def sum_chip_partials(name, parts, recv, place, layer, n_layers, into):
    _, h, c = parts.shape
    tr = _ew_rows(h, c, 5)
    nh = h // tr

    def body(p_ref, o_ref, r0, r1, r2, *rest):
        rest[-1][...] = ((o_ref[...].astype(F32) + r0[...].astype(F32)) + r1[...].astype(F32)) + r2[...].astype(F32)

    in_specs = [pl.BlockSpec((None, tr, c), lambda i, p: (p[0], i, 0))]
    in_specs += [pl.BlockSpec((None, tr, c), lambda i, p, j=j: (j, i, 0)) for j in range(3)]
    args = [place, parts, recv, recv, recv]
    aliases = {}
    if into is not None:
        in_specs.append(ANY)
        args.append(into)
        aliases = {5: 0}
    grid_spec = pltpu.PrefetchScalarGridSpec(
        num_scalar_prefetch=1, grid=(nh,), in_specs=in_specs,
        out_specs=pl.BlockSpec((None, tr, c), lambda i, p: (layer, p[1] * nh + i, 0)))
    return pl.pallas_call(body, name=name, grid_spec=grid_spec, input_output_aliases=aliases,
                          out_shape=jax.ShapeDtypeStruct((n_layers, 2 * h, c), F32),
                          compiler_params=_params("parallel"))(*args)


def small_update(name, gall, w, m, v):
    _, rows, cols = gall.shape
    tr = _ew_rows(rows, cols, 16)

    def body(g_ref, w_ref, m_ref, v_ref, go, do, mo, vo):
        g = g_ref[0]
        for dev in range(1, 8):
            g = g + g_ref[dev]
        delta, mn, vn = _adamw(g, w_ref[...], m_ref[...], v_ref[...])
        go[...] = g
        do[...] = delta
        mo[...] = mn
        vo[...] = vn

    spec = _row_spec(tr, cols)
    sh = jax.ShapeDtypeStruct((rows, cols), F32)
    return pl.pallas_call(body, name=name, grid=(rows // tr,),
                          in_specs=[pl.BlockSpec((8, tr, cols), lambda i: (0, i, 0)), spec, spec, spec],
                          out_specs=[spec] * 4, out_shape=[sh] * 4, compiler_params=_params("parallel"))(gall, w, m, v)


def _place():
    x, y, c = lax.axis_index("x"), lax.axis_index("y"), lax.axis_index("c")
    chips = [(1 - x, y), (x, 1 - y), (1 - x, 1 - y)]
    return x, y, c, chips


def _remote(src, dst, send_sems, recv_sems, k, to):
    return pltpu.make_async_remote_copy(src_ref=src, dst_ref=dst, send_sem=send_sems.at[k], recv_sem=recv_sems.at[k],
                                        device_id=to, device_id_type=MESH)


ANY = pl.BlockSpec(memory_space=pl.ANY)


def allgather_small(name, block):
    m_per, n = block.shape

    def body(x_ref, out_ref, send_sems, recv_sems, local_sem):
        x, y, c, chips = _place()
        me, sibling = (x, y, c), (x, y, 1 - c)

        def rows(px, py, pc):
            return out_ref.at[pl.ds((4 * px + 2 * py + pc) * m_per, m_per), :]

        def copy(k, blk, to, src=None):
            return _remote(rows(*blk) if src is None else src, rows(*blk), send_sems, recv_sems, k, to)

        mine = pltpu.make_async_copy(x_ref, rows(*me), local_sem)
        mine.start()
        first = [copy(0, me, sibling, src=x_ref)]
        first += [copy(1 + j, me, (*chip, c), src=x_ref) for j, chip in enumerate(chips)]
        for cp in first:
            cp.start()
        passed = [copy(4 + j, (*chip, c), sibling) for j, chip in enumerate(chips)]
        for j, chip in enumerate(chips):
            copy(1 + j, (*chip, c), me).wait_recv()
            passed[j].start()
        copy(0, sibling, me).wait_recv()
        for j, chip in enumerate(chips):
            copy(4 + j, (*chip, 1 - c), me).wait_recv()
        for cp in first + passed:
            cp.wait_send()
        mine.wait()

    return pl.pallas_call(body, name=name, out_shape=jax.ShapeDtypeStruct((8 * m_per, n), block.dtype),
                          in_specs=[pl.BlockSpec(memory_space=pltpu.VMEM)], out_specs=pl.BlockSpec(memory_space=pltpu.VMEM),
                          scratch_shapes=[pltpu.SemaphoreType.DMA((7,)), pltpu.SemaphoreType.DMA((7,)),
                                          pltpu.SemaphoreType.DMA],
                          compiler_params=pltpu.CompilerParams(vmem_limit_bytes=VMEM_LIMIT))(block)


def gather_weights(bufs):
    n = len(bufs)

    def body(*refs):
        start, finish = _gather_plan(refs[n:2 * n], *refs[2 * n:])
        start()
        finish()

    return pl.pallas_call(body, name="gather_weights",
                          out_shape=[jax.ShapeDtypeStruct(t.shape, t.dtype) for t in bufs],
                          in_specs=[ANY] * n, out_specs=[ANY] * n, input_output_aliases={a: a for a in range(n)},
                          scratch_shapes=_gather_sems(n))(*bufs)


def _gather_sems(n):
    return [pltpu.SemaphoreType.DMA((6 * n,)), pltpu.SemaphoreType.DMA((6 * n,))]


def _gather_plan(outs, send_sems, recv_sems):
    n = len(outs)
    x, y, c, chips = _place()
    g0 = 2 * x + y
    sibling = (x, y, 1 - c)
    groups = [2 * cx + cy for cx, cy in chips]

    def part(a, g, cc):
        h = outs[a].shape[2] // 2
        return outs[a].at[g, :, pl.ds(cc * h, h), :]

    def ici(a, j):
        return _remote(part(a, g0, c), part(a, g0, c), send_sems, recv_sems, 3 * a + j, (*chips[j], c))

    def ici_arrival(a, j):
        return _remote(part(a, groups[j], c), part(a, groups[j], c), send_sems, recv_sems, 3 * a + j, (*chips[j], c))

    def passed(a, j, cc):
        return _remote(part(a, groups[j], cc), part(a, groups[j], cc), send_sems, recv_sems, 3 * n + 3 * a + j, sibling)

    def start():
        for a in range(n):
            for j in range(3):
                ici(a, j).start()

    def finish():
        for a in range(n):
            for j in range(3):
                ici_arrival(a, j).wait_recv()
                passed(a, j, c).start()
        for a in range(n):
            for j in range(3):
                passed(a, j, 1 - c).wait_recv()
        for a in range(n):
            for j in range(3):
                ici(a, j).wait_send()
                passed(a, j, c).wait_send()

    return start, finish


def exchange_core_halves(name, grads):
    n = len(grads)
    halves = [t.shape[1] // 2 for t in grads]

    def body(*refs):
        ins, theirs = refs[:n], refs[n:2 * n]
        send_sems, recv_sems = refs[2 * n:]
        x, y, c, _ = _place()
        sibling = (x, y, 1 - c)

        def give(a, g):
            return _remote(ins[a].at[g, pl.ds((1 - c) * halves[a], halves[a]), :], theirs[a].at[g], send_sems,
                           recv_sems, N_GROUPS * a + g, sibling)

        for a in range(n):
            for g in range(N_GROUPS):
                give(a, g).start()
        for a in range(n):
            for g in range(N_GROUPS):
                give(a, g).wait_recv()
        for a in range(n):
            for g in range(N_GROUPS):
                give(a, g).wait_send()

    shapes = [jax.ShapeDtypeStruct((t.shape[0], h, t.shape[2]), t.dtype) for t, h in zip(grads, halves)]
    k = N_GROUPS * n
    return pl.pallas_call(body, name=name, out_shape=shapes, in_specs=[ANY] * n, out_specs=[ANY] * n,
                          scratch_shapes=[pltpu.SemaphoreType.DMA((k,)), pltpu.SemaphoreType.DMA((k,))])(*grads)


def exchange_chip_partials(name, parts):
    n = len(parts)

    def body(*refs):
        start, finish = _chip_exchange_plan(refs[:n], refs[n:2 * n], *refs[2 * n:])
        start()
        finish()

    return pl.pallas_call(body, name=name, out_shape=_chip_exchange_shapes(parts), in_specs=[ANY] * n,
                          out_specs=[ANY] * n, scratch_shapes=_chip_exchange_sems(n))(*parts)


def _chip_exchange_shapes(parts):
    return [jax.ShapeDtypeStruct((3,) + t.shape[1:], t.dtype) for t in parts]


def _chip_exchange_sems(n):
    return [pltpu.SemaphoreType.DMA((3 * n,)), pltpu.SemaphoreType.DMA((3 * n,))]


def _chip_exchange_plan(ins, recv, send_sems, recv_sems):
    n = len(ins)
    x, y, c, chips = _place()

    def give(a, j):
        return _remote(ins[a].at[2 * chips[j][0] + chips[j][1]], recv[a].at[j], send_sems, recv_sems, 3 * a + j,
                       (*chips[j], c))

    def start():
        for a in range(n):
            for j in range(3):
                give(a, j).start()

    def finish():
        for a in range(n):
            for j in range(3):
                give(a, j).wait_recv()
        for a in range(n):
            for j in range(3):
                give(a, j).wait_send()

    return start, finish


def share_reduced_halves(bufs):
    n = len(bufs)
    n_layers = bufs[0].shape[0]
    halves = [t.shape[1] // 2 for t in bufs]

    def body(*refs):
        outs = refs[n:2 * n]
        send_sems, recv_sems = refs[2 * n:]
        x, y, c, _ = _place()
        sibling = (x, y, 1 - c)

        def give(l, a, cc):
            part = outs[a].at[l, pl.ds(cc * halves[a], halves[a]), :]
            return _remote(part, part, send_sems, recv_sems, l * n + a, sibling)

        for l in range(n_layers):
            for a in range(n):
                give(l, a, c).start()
        for l in range(n_layers):
            for a in range(n):
                give(l, a, 1 - c).wait_recv()
        for l in range(n_layers):
            for a in range(n):
                give(l, a, c).wait_send()

    k = n_layers * n
    return pl.pallas_call(body, name="share_reduced_halves", out_shape=[jax.ShapeDtypeStruct(t.shape, t.dtype) for t in bufs],
                          in_specs=[ANY] * n, out_specs=[ANY] * n, input_output_aliases={a: a for a in range(n)},
                          scratch_shapes=[pltpu.SemaphoreType.DMA((k,)), pltpu.SemaphoreType.DMA((k,))])(*bufs)


def reduce_in_chip(grads, place):
    theirs = exchange_core_halves("exchange_core_halves", grads)
    return [sum_core_halves("sum_core_halves", g, t, place) for g, t in zip(grads, theirs)]


def reduce_over_chips(parts, recv, place, layer, n_layers, into):
    return [sum_chip_partials("sum_chip_partials", p, r, place, layer, n_layers, None if into is None else into[a])
            for a, (p, r) in enumerate(zip(parts, recv))]


def _pack(parts):
    flat = jnp.concatenate([p.reshape(-1) for p in parts])
    pad = (-flat.shape[0]) % (8 * LANES)
    if pad:
        flat = jnp.concatenate([flat, jnp.zeros((pad,), flat.dtype)])
    return flat.reshape(-1, LANES)


def _unpack(buf, shapes):
    flat = buf.reshape(-1)
    out, at = [], 0
    for sh in shapes:
        size = math.prod(sh)
        out.append(flat[at:at + size].reshape(sh))
        at += size
    return out


def kernel(x, w_in, conv_w, conv_b, conv_ln_g, conv_ln_b, pool_w, pool_scale, g_mix, w_out, g_pre_mix, g_post_mix, g_pre_ffn, g_post_ffn, w_gate, w_up, w_down, loss_target, m_w_in, m_conv_w, m_conv_b, m_conv_ln_g, m_conv_ln_b, m_pool_w, m_pool_scale, m_g_mix, m_w_out, m_g_pre_mix, m_g_post_mix, m_g_pre_ffn, m_g_post_ffn, m_w_gate, m_w_up, m_w_down, v_w_in, v_conv_w, v_conv_b, v_conv_ln_g, v_conv_ln_b, v_pool_w, v_pool_scale, v_g_mix, v_w_out, v_g_pre_mix, v_g_post_mix, v_g_pre_ffn, v_g_post_ffn, v_w_gate, v_w_up, v_w_down):
    xs, target = x[0], loss_target[0]
    s, d = xs.shape
    n_layers = w_in.shape[0]
    my_group = 2 * lax.axis_index("x") + lax.axis_index("y")

    big = [w_in, w_out, w_gate, w_up, w_down]
    place = jnp.stack([my_group, lax.axis_index("c")]).astype(jnp.int32)
    gathered = [[cast_into_gathered(w, place, l) for w in big] for l in range(n_layers)]
    gathered[0] = gather_weights(gathered[0])
    cw_pad = jnp.pad(conv_w, ((0, 0), (0, 32 - CONV_WIDTH), (0, 0)))
    cw_all = allgather_small("allgather_conv_w", cw_pad.reshape(n_layers * 32, LANES))
    cw_full = cw_all.reshape(N_GROUPS, 2, n_layers, 32, LANES)[:, 0].transpose(1, 2, 0, 3).reshape(n_layers, 32, C_A)
    pw_bf16 = pool_w.astype(BF16)
    tabs = rope_tables(s)
    gmix_a, gmix_b, gmix_c = g_mix[:, :C_A], g_mix[:, C_A:C_A + C_B], g_mix[:, C_A + C_B:]
    row = lambda t: t.reshape(1, -1)

    saved = []
    (h,) = norm_fwd("norm_first", xs, None, None, g_pre_mix[0])
    xin = xs
    for l in range(n_layers):
        win_g, wout_g, wgate_g, wup_g, wdown_g = gathered[l]
        proj = mm_act_wcols("proj", h, win_g, 0, F32)
        ya = conv_fwd("conv_fwd", proj, cw_full[l], row(conv_b[l]), row(conv_ln_g[l]), row(conv_ln_b[l]), row(gmix_a[l]))
        qkv_d = rope_fwd("rope_fwd", proj, tabs)
        os_, lses = [], []
        for dil, (qd, kd, vd) in zip(DILATIONS, qkv_d):
            o, lse = band_attn_fwd(f"band_attn_fwd_d{dil}", qd, kd, vd, s // dil)
            os_.append(o)
            lses.append(lse)
        yb, ybn, lse_joint = attn_combine_fwd("attn_combine_fwd", os_, lses, row(gmix_b[l]))
        yc = pool_fwd("pool_fwd", proj, pw_bf16[l], row(pool_scale[l]), row(gmix_c[l]))
        y = jnp.concatenate([ya, ybn, yc], axis=1)
        z = mm_act_wrows("mix_out", y, wout_g, 0, F32)
        x1, h2 = norm_fwd("norm_mid", xin, z, g_post_mix[l], g_pre_ffn[l])
        if l + 1 < n_layers:
            (gt, up, act), gathered[l + 1] = ffn_up("ffn_up_gather", h2, wgate_g, wup_g, 0, gathered[l + 1])
        else:
            (gt, up, act), _ = ffn_up("ffn_up", h2, wgate_g, wup_g, 0)
        f = mm_act_wrows("ffn_down", act, wdown_g, 0, F32)
        if l + 1 < n_layers:
            x2, h_next = norm_fwd("norm_next", x1, f, g_post_ffn[l], g_pre_mix[l + 1])
        else:
            (x2,), h_next = norm_fwd("norm_last", x1, f, g_post_ffn[l], None), None
        saved.append(dict(xin=xin, h=h, proj=proj, qkv_d=qkv_d, yb=yb, lse=lse_joint, y=y, z=z, x1=x1, h2=h2,
                          gt=gt, up=up, act=act, f=f))
        xin, h = x2, h_next

    dx, sq = loss_head(xin, target)
    loss = lax.psum(0.5 * jnp.sum(sq) / d, ("x", "y", "c"))

    small = {k: [None] * n_layers for k in ("cw", "cb", "lg", "lb", "pw", "ps", "gmix", "gpre_mix", "gpost_mix",
                                            "gpre_ffn", "gpost_ffn")}
    reduced = None
    df, dg = norm_bwd("norm_bwd_top", dx, None, None, None, saved[-1]["f"], g_post_ffn[n_layers - 1])
    small["gpost_ffn"][n_layers - 1] = dg
    pending = None
    for l in reversed(range(n_layers)):
        sv = saved[l]
        win_g, wout_g, wgate_g, wup_g, wdown_g = gathered[l]
        dgt, dup = ffn_down_bwd("ffn_down_bwd", df, wdown_g, 0, sv["gt"], sv["up"])
        d_wdown = mm_plain("wgrad_rows_ffn", "tn", sv["act"], df, BF16)
        if pending is None:
            dh2, _ = ffn_in_bwd("ffn_in_bwd", dgt, dup, wgate_g, wup_g, 0)
        else:
            dh2, recv = ffn_in_bwd("ffn_in_bwd_exchange", dgt, dup, wgate_g, wup_g, 0, pending[1])
            reduced = reduce_over_chips(pending[1], recv, place, pending[0], n_layers, reduced)
        d_wgate = mm_wgrad_cols("wgrad_cols_ffn", sv["h2"], dgt, wgate_g.shape[3])
        d_wup = mm_wgrad_cols("wgrad_cols_ffn", sv["h2"], dup, wup_g.shape[3])
        dx1, small["gpre_ffn"][l], dz, small["gpost_mix"][l] = norm_bwd(
            "norm_bwd_mid", dx, dh2, sv["x1"], g_pre_ffn[l], sv["z"], g_post_mix[l])
        dy = mm_act_wrows_t("mix_out_bwd", dz, wout_g, 0, F32)
        d_wout = mm_plain("wgrad_rows_mix", "tn", sv["y"], dz, BF16)
        proj = sv["proj"]
        dag, dcw, small["cb"][l], small["lg"][l], small["lb"][l], dgx_a = conv_bwd(
            "conv_bwd", proj, dy, cw_full[l], row(conv_b[l]), row(conv_ln_g[l]), row(conv_ln_b[l]), row(gmix_a[l]))
        small["cw"][l] = dcw.reshape(32, 8, C_A).sum(axis=1)
        dybs, deltas, dgx_b = attn_combine_bwd("attn_combine_bwd", dy, sv["yb"], row(gmix_b[l]))
        dqs, dks, dvs = [], [], []
        for dil, (qd, kd, vd), do_d, lse_d, delta_d in zip(DILATIONS, sv["qkv_d"], dybs, sv["lse"], deltas):
            dqs.append(band_attn_dq(f"band_attn_dq_d{dil}", qd, kd, vd, do_d, lse_d, delta_d, s // dil))
            dk, dv = band_attn_dkv(f"band_attn_dkv_d{dil}", qd, kd, vd, do_d, lse_d, delta_d, s // dil)
            dks.append(dk)
            dvs.append(dv)
        dqkv = rope_bwd("rope_bwd", dqs, dks, dvs, tabs)
        du, small["pw"][l], small["ps"][l], dgx_c = pool_bwd("pool_bwd", proj, dy, pw_bf16[l], row(pool_scale[l]),
                                                              row(gmix_c[l]))
        small["gmix"][l] = jnp.concatenate([dgx_a, dgx_b, dgx_c], axis=1)
        dproj = jnp.concatenate([dag, dqkv, du], axis=1)
        dh1 = mm_act_wcols_t("proj_bwd", dproj, win_g, 0, F32)
        d_win = mm_wgrad_cols("wgrad_cols_proj", sv["h"], dproj, win_g.shape[3])
        if l > 0:
            dx, small["gpre_mix"][l], df, small["gpost_ffn"][l - 1] = norm_bwd(
                "norm_bwd_next", dx1, dh1, sv["xin"], g_pre_mix[l], saved[l - 1]["f"], g_post_ffn[l - 1])
        else:
            dx, small["gpre_mix"][l] = norm_bwd("norm_bwd_first", dx1, dh1, sv["xin"], g_pre_mix[l], None, None)
        d_wout = d_wout.reshape(N_GROUPS, -1, d_wout.shape[1])
        d_wdown = d_wdown.reshape(N_GROUPS, -1, d_wdown.shape[1])
        pending = (l, reduce_in_chip([d_win, d_wout, d_wgate, d_wup, d_wdown], place))
    recv = exchange_chip_partials("exchange_chip_partials", pending[1])
    reduced = reduce_over_chips(pending[1], recv, place, pending[0], n_layers, reduced)

    big_grads = share_reduced_halves(reduced)
    big_m = [m_w_in, m_w_out, m_w_gate, m_w_up, m_w_down]
    big_v = [v_w_in, v_w_out, v_w_gate, v_w_up, v_w_down]
    big_upd = [elementwise("adamw", _adamw, [g, w, m, v], [F32, F32, F32])
               for g, w, m, v in zip(big_grads, big, big_m, big_v)]

    def stack8(k):
        return jnp.stack([t.sum(axis=0) if t.shape[0] == 8 and t.ndim == 2 else t for t in small[k]])

    order = ("cw", "cb", "lg", "lb", "pw", "ps", "gmix", "gpre_mix", "gpost_mix", "gpre_ffn", "gpost_ffn")
    partial = [jnp.stack(small["cw"])] + [stack8(k) for k in order[1:]]
    rep_w = [conv_b, conv_ln_g, conv_ln_b, pool_w, pool_scale, g_mix, g_pre_mix, g_post_mix, g_pre_ffn, g_post_ffn]
    rep_m = [m_conv_b, m_conv_ln_g, m_conv_ln_b, m_pool_w, m_pool_scale, m_g_mix, m_g_pre_mix, m_g_post_mix,
             m_g_pre_ffn, m_g_post_ffn]
    rep_v = [v_conv_b, v_conv_ln_g, v_conv_ln_b, v_pool_w, v_pool_scale, v_g_mix, v_g_pre_mix, v_g_post_mix,
             v_g_pre_ffn, v_g_post_ffn]
    packed = _pack(partial)
    rows = packed.shape[0]
    gall = allgather_small("allgather_small_grads", packed).reshape(8, rows, LANES)
    cw_hole = jnp.zeros((n_layers, 32, C_A), F32)
    rep_out = small_update("small_update", gall, _pack([cw_hole] + rep_w), _pack([cw_hole] + rep_m),
                           _pack([cw_hole] + rep_v))
    shapes = [cw_hole.shape] + [t.shape for t in rep_w]
    rep_g, rep_d, rep_mn, rep_vn = ([t for t in _unpack(buf, shapes)[1:]] for buf in rep_out)
    cw_rows = n_layers * 32
    gall_cw = gall[:, :cw_rows * C_A // LANES].reshape(8, cw_rows, C_A // LANES, LANES)
    gall_cw = lax.dynamic_index_in_dim(gall_cw, my_group, axis=2, keepdims=False)
    pad_cw = lambda t: jnp.pad(t, ((0, 0), (0, 32 - CONV_WIDTH), (0, 0))).reshape(cw_rows, LANES)
    cw_out = small_update("conv_w_update", gall_cw, pad_cw(conv_w), pad_cw(m_conv_w), pad_cw(v_conv_w))
    cw_g, cw_d, cw_mn, cw_vn = (t.reshape(n_layers, 32, LANES)[:, :CONV_WIDTH] for t in cw_out)

    def assemble(bigs, cw, reps):
        return [bigs[0], cw] + list(reps[:6]) + [bigs[1]] + list(reps[6:]) + list(bigs[2:])

    grads = assemble(big_grads, cw_g, rep_g)
    deltas = assemble([u[0] for u in big_upd], cw_d, rep_d)
    new_m = assemble([u[1] for u in big_upd], cw_mn, rep_mn)
    new_v = assemble([u[2] for u in big_upd], cw_vn, rep_vn)
    return (loss, dx[None], *grads, *deltas, *new_m, *new_v)
```

```python
import functools
import math

import jax
import jax.numpy as jnp
from jax import lax
from jax.experimental import pallas as pl
from jax.experimental.pallas import tpu as pltpu

F32 = jnp.float32
BF16 = jnp.bfloat16
MESH = pl.DeviceIdType.MESH

EPS = 1e-6
NEG = -1e30
C_A = 512
C_B = 1024
C_C = 512
HEAD_DIM = 64
POOL_WINDOWS = (2, 4, 8, 16)
C_G = 128
CONV_WIDTH = 31
CONV_HALF = 15
DILATIONS = (1, 4, 16)
ATT_HALF = 64
ROT_DIM = 16
ROPE_THETA = 500000.0
ADAM_LR, ADAM_B1, ADAM_B2, ADAM_EPS, ADAM_WD, ADAM_STEP = 0.001, 0.9, 0.999, 1e-08, 0.01, 10
N_GROUPS = 4
HALO = 32
LANES = 128
VMEM_LIMIT = 48 * 1024 * 1024


def _params(*sem):
    return pltpu.CompilerParams(dimension_semantics=sem, vmem_limit_bytes=VMEM_LIMIT)


def _pick(n, prefs):
    for p in prefs:
        if p <= n and n % p == 0:
            return p
    return n


def _rms_scale(t):
    return lax.rsqrt(jnp.mean(t * t, axis=-1, keepdims=True) + EPS)


def _rms_bwd(d, x, g):
    r = _rms_scale(x)
    u = d * g
    dx = r * u - x * (r * r * r) * jnp.mean(u * x, axis=-1, keepdims=True)
    return dx, d * x * r


def _sum8(t):
    rows, cols = t.shape
    return jnp.sum(t.reshape(rows // 8, 8, cols), axis=0)


def _dot(a, b, mode):
    dn = {"nn": (((1,), (0,)), ((), ())), "nt": (((1,), (1,)), ((), ())), "tn": (((0,), (0,)), ((), ()))}[mode]
    return lax.dot_general(a, b, dn, preferred_element_type=F32)


def _row_spec(tm, cols, col_block=0):
    return pl.BlockSpec((tm, cols), lambda i, cb=col_block: (i, cb))


def _const_spec(shape):
    return pl.BlockSpec(shape, lambda i: tuple(0 for _ in shape))


def norm_fwd(name, x, z, g_post, g_next):
    s, d = x.shape
    tm = _pick(s, (512, 256, 128))
    has_z, has_h = z is not None, g_next is not None

    def body(*refs):
        refs = list(refs)
        x_ref = refs.pop(0)
        xn = x_ref[...]
        if has_z:
            z_ref, gp_ref = refs.pop(0), refs.pop(0)
            zz = z_ref[...]
            xn = xn + zz * _rms_scale(zz) * gp_ref[...]
        if has_h:
            gn_ref = refs.pop(0)
        if has_z:
            refs.pop(0)[...] = xn
        if has_h:
            refs.pop(0)[...] = (xn * _rms_scale(xn) * gn_ref[...]).astype(BF16)

    ins, specs, outs, ospecs = [x], [_row_spec(tm, d)], [], []
    if has_z:
        ins += [z, g_post.reshape(1, d)]
        specs += [_row_spec(tm, d), _const_spec((1, d))]
        outs.append(jax.ShapeDtypeStruct((s, d), F32))
        ospecs.append(_row_spec(tm, d))
    if has_h:
        ins.append(g_next.reshape(1, d))
        specs.append(_const_spec((1, d)))
        outs.append(jax.ShapeDtypeStruct((s, d), BF16))
        ospecs.append(_row_spec(tm, d))
    res = pl.pallas_call(body, name=name, grid=(s // tm,), in_specs=specs, out_specs=ospecs, out_shape=outs,
                         compiler_params=_params("parallel"))(*ins)
    return list(res)


def norm_bwd(name, dres, dh, xin, g_pre, zin, g_post):
    s, d = dres.shape
    tm = _pick(s, (256, 128))
    has_pre, has_post = dh is not None, zin is not None

    def body(*refs):
        refs = list(refs)
        i = pl.program_id(0)
        dx = refs.pop(0)[...]
        if has_pre:
            dh_ref, x_ref, g_ref = refs.pop(0), refs.pop(0), refs.pop(0)
            ddx, dg = _rms_bwd(dh_ref[...].astype(F32), x_ref[...], g_ref[...])
            dx = dx + ddx
        if has_post:
            z_ref, gp_ref = refs.pop(0), refs.pop(0)
            dz, dgp = _rms_bwd(dx, z_ref[...], gp_ref[...])
        if has_pre:
            refs.pop(0)[...] = dx
            dg_ref = refs.pop(0)

            @pl.when(i == 0)
            def _():
                dg_ref[...] = jnp.zeros_like(dg_ref)

            dg_ref[...] += _sum8(dg)
        if has_post:
            refs.pop(0)[...] = dz.astype(BF16)
            dgp_ref = refs.pop(0)

            @pl.when(i == 0)
            def _():
                dgp_ref[...] = jnp.zeros_like(dgp_ref)

            dgp_ref[...] += _sum8(dgp)

    ins, specs, outs, ospecs = [dres], [_row_spec(tm, d)], [], []
    if has_pre:
        ins += [dh, xin, g_pre.reshape(1, d)]
        specs += [_row_spec(tm, d), _row_spec(tm, d), _const_spec((1, d))]
        outs += [jax.ShapeDtypeStruct((s, d), F32), jax.ShapeDtypeStruct((8, d), F32)]
        ospecs += [_row_spec(tm, d), _const_spec((8, d))]
    if has_post:
        ins += [zin, g_post.reshape(1, d)]
        specs += [_row_spec(tm, d), _const_spec((1, d))]
        outs += [jax.ShapeDtypeStruct((s, d), BF16), jax.ShapeDtypeStruct((8, d), F32)]
        ospecs += [_row_spec(tm, d), _const_spec((8, d))]
    res = pl.pallas_call(body, name=name, grid=(s // tm,), in_specs=specs, out_specs=ospecs, out_shape=outs,
                         compiler_params=_params("arbitrary"))(*ins)
    return list(res)


def loss_head(x, target):
    s, d = x.shape
    tm = _pick(s, (512, 256, 128))

    def body(x_ref, t_ref, dx_ref, sq_ref):
        i = pl.program_id(0)
        e = x_ref[...] - t_ref[...]
        dx_ref[...] = e * (1.0 / d)

        @pl.when(i == 0)
        def _():
            sq_ref[...] = jnp.zeros_like(sq_ref)

        sq_ref[...] += _sum8(e * e)

    return pl.pallas_call(body, name="loss_head", grid=(s // tm,),
                          in_specs=[_row_spec(tm, d), _row_spec(tm, d)],
                          out_specs=[_row_spec(tm, d), _const_spec((8, d))],
                          out_shape=[jax.ShapeDtypeStruct((s, d), F32), jax.ShapeDtypeStruct((8, d), F32)],
                          compiler_params=_params("arbitrary"))(x, target)


def matmul(name, mode, a, b, *, grid, tk_steps, a_spec, b_spec, o_spec, out_shape, acc_shape, b_flat=None):
    nk = tk_steps

    def rhs(b_ref):
        return b_ref[...] if b_flat is None else b_ref[...].reshape(b_flat)

    def body(a_ref, b_ref, o_ref, *scratch):
        if nk == 1:
            o_ref[...] = _dot(a_ref[...], rhs(b_ref), mode).astype(o_ref.dtype)
            return
        acc = scratch[0]
        kk = pl.program_id(2)

        @pl.when(kk == 0)
        def _():
            acc[...] = jnp.zeros_like(acc)

        acc[...] += _dot(a_ref[...], rhs(b_ref), mode)

        @pl.when(kk == nk - 1)
        def _():
            o_ref[...] = acc[...].astype(o_ref.dtype)

    return pl.pallas_call(body, name=name, grid=grid, in_specs=[a_spec, b_spec], out_specs=o_spec, out_shape=out_shape,
                          scratch_shapes=[] if nk == 1 else [pltpu.VMEM(acc_shape, F32)],
                          compiler_params=_params("parallel", "parallel", "arbitrary"))(a, b)


TM_PREFS = (1024, 512, 256, 128)
TW_PREFS = (1408, 1152, 1024, 512, 384, 256, 128)
TK_PREFS = (512, 384, 256, 128)
FULL_K_PREFS = (2048, 1024) + TK_PREFS


def mm_act_wcols(name, a, wg, layer, out_dtype):
    s, k = a.shape
    g, _, _, ng = wg.shape
    tm, tk, tn = _pick(s, TM_PREFS), _pick(k, FULL_K_PREFS), _pick(ng, TW_PREFS)
    per = ng // tn
    return matmul(name, "nn", a, wg, grid=(g * per, s // tm, k // tk), tk_steps=k // tk,
                  a_spec=pl.BlockSpec((tm, tk), lambda j, i, kk: (i, kk)),
                  b_spec=pl.BlockSpec((None, None, tk, tn), lambda j, i, kk: (j // per, layer, kk, j % per)),
                  o_spec=pl.BlockSpec((tm, tn), lambda j, i, kk: (i, j)),
                  out_shape=jax.ShapeDtypeStruct((s, g * ng), out_dtype), acc_shape=(tm, tn))


def mm_act_wcols_t(name, a, wg, layer, out_dtype):
    s, n = a.shape
    g, _, k, ng = wg.shape
    tm, tn, tk = _pick(s, TM_PREFS), _pick(k, TM_PREFS), _pick(ng, (1152, 384, 128))
    per = ng // tk
    return matmul(name, "nt", a, wg, grid=(s // tm, k // tn, g * per), tk_steps=g * per,
                  a_spec=pl.BlockSpec((tm, tk), lambda i, j, kk: (i, kk)),
                  b_spec=pl.BlockSpec((None, None, tn, tk), lambda i, j, kk: (kk // per, layer, j, kk % per)),
                  o_spec=pl.BlockSpec((tm, tn), lambda i, j, kk: (i, j)),
                  out_shape=jax.ShapeDtypeStruct((s, k), out_dtype), acc_shape=(tm, tn))


def mm_act_wrows(name, a, wr, layer, out_dtype):
    s, k = a.shape
    g, _, kg, n = wr.shape
    tm, tn, tk = _pick(s, TM_PREFS), _pick(n, TM_PREFS), _pick(kg, (1408, 512, 256, 128))
    if k <= FULL_K_PREFS[0]:
        return matmul(name, "nn", a, wr, grid=(n // tn, s // tm, 1), tk_steps=1,
                      a_spec=pl.BlockSpec((tm, k), lambda j, i, kk: (i, 0)),
                      b_spec=pl.BlockSpec((g, None, kg, tn), lambda j, i, kk: (0, layer, 0, j)),
                      o_spec=pl.BlockSpec((tm, tn), lambda j, i, kk: (i, j)),
                      out_shape=jax.ShapeDtypeStruct((s, n), out_dtype), acc_shape=(tm, tn), b_flat=(k, tn))
    per = kg // tk
    return matmul(name, "nn", a, wr, grid=(s // tm, n // tn, g * per), tk_steps=g * per,
                  a_spec=pl.BlockSpec((tm, tk), lambda i, j, kk: (i, kk)),
                  b_spec=pl.BlockSpec((None, None, tk, tn), lambda i, j, kk: (kk // per, layer, kk % per, j)),
                  o_spec=pl.BlockSpec((tm, tn), lambda i, j, kk: (i, j)),
                  out_shape=jax.ShapeDtypeStruct((s, n), out_dtype), acc_shape=(tm, tn))


def mm_act_wrows_t(name, a, wr, layer, out_dtype):
    s, n = a.shape
    g, _, kg, _ = wr.shape
    tm, tn, tk = _pick(s, TM_PREFS), _pick(kg, (1408, 512, 256, 128)), _pick(n, FULL_K_PREFS)
    per = kg // tn
    return matmul(name, "nt", a, wr, grid=(g * per, s // tm, n // tk), tk_steps=n // tk,
                  a_spec=pl.BlockSpec((tm, tk), lambda j, i, kk: (i, kk)),
                  b_spec=pl.BlockSpec((None, None, tn, tk), lambda j, i, kk: (j // per, layer, j % per, kk)),
                  o_spec=pl.BlockSpec((tm, tn), lambda j, i, kk: (i, j)),
                  out_shape=jax.ShapeDtypeStruct((s, g * kg), out_dtype), acc_shape=(tm, tn))


def mm_wgrad_cols(name, a, dy, ng):
    s, k = a.shape
    n = dy.shape[1]
    g = n // ng
    tm = _pick(k, TM_PREFS)
    tk = _pick(s, TM_PREFS)
    tn = _pick(ng, TW_PREFS)
    per = ng // tn
    return matmul(name, "tn", a, dy, grid=(k // tm, g * per, s // tk), tk_steps=s // tk,
                  a_spec=pl.BlockSpec((tk, tm), lambda i, j, kk: (kk, i)),
                  b_spec=pl.BlockSpec((tk, tn), lambda i, j, kk: (kk, j)),
                  o_spec=pl.BlockSpec((None, tm, tn), lambda i, j, kk: (j // per, i, j % per)),
                  out_shape=jax.ShapeDtypeStruct((g, k, ng), BF16), acc_shape=(tm, tn))


def mm_plain(name, mode, a, b, out_dtype):
    if mode == "nn":
        (m, k), n = a.shape, b.shape[1]
    elif mode == "nt":
        (m, k), n = a.shape, b.shape[0]
    else:
        (k, m), n = a.shape, b.shape[1]
    tm = _pick(m, (1408,) + TM_PREFS)
    tn = _pick(n, TM_PREFS)
    tk = _pick(k, TM_PREFS)
    a_spec = (pl.BlockSpec((tk, tm), lambda i, j, kk: (kk, i)) if mode == "tn"
              else pl.BlockSpec((tm, tk), lambda i, j, kk: (i, kk)))
    b_spec = (pl.BlockSpec((tn, tk), lambda i, j, kk: (j, kk)) if mode == "nt"
              else pl.BlockSpec((tk, tn), lambda i, j, kk: (kk, j)))
    return matmul(name, mode, a, b, grid=(m // tm, n // tn, k // tk), tk_steps=k // tk, a_spec=a_spec, b_spec=b_spec,
                  o_spec=pl.BlockSpec((tm, tn), lambda i, j, kk: (i, j)),
                  out_shape=jax.ShapeDtypeStruct((m, n), out_dtype), acc_shape=(tm, tn))


def _silu_parts(gt):
    sg = jax.nn.sigmoid(gt)
    return gt * sg, sg


def ffn_up(name, h, wgate, wup, layer, gather=()):
    s, d = h.shape
    g, _, _, ng = wgate.shape
    assert d <= FULL_K_PREFS[0]
    tm = _pick(s, (256, 128))
    tn = _pick(ng, TW_PREFS)
    per = ng // tn

    n = len(gather)
    grid = (g * per, s // tm)

    def body(h_ref, wg_ref, wu_ref, *rest):
        gt_ref, up_ref, act_ref = rest[n:n + 3]
        if n:
            start, finish = _gather_plan(rest[n + 3:2 * n + 3], *rest[2 * n + 3:])
            pl.when((pl.program_id(0) == 0) & (pl.program_id(1) == 0))(start)
        hh = h_ref[...]
        gt = _dot(hh, wg_ref[...], "nn")
        up = _dot(hh, wu_ref[...], "nn")
        gt_ref[...] = gt.astype(BF16)
        up_ref[...] = up.astype(BF16)
        act_ref[...] = (_silu_parts(gt)[0] * up).astype(BF16)
        if n:
            pl.when((pl.program_id(0) == grid[0] - 1) & (pl.program_id(1) == grid[1] - 1))(finish)

    wspec = pl.BlockSpec((None, None, d, tn), lambda j, i: (j // per, layer, 0, j % per))
    ospec = pl.BlockSpec((tm, tn), lambda j, i: (i, j))
    osh = jax.ShapeDtypeStruct((s, g * ng), BF16)
    res = pl.pallas_call(body, name=name, grid=grid,
                         in_specs=[pl.BlockSpec((tm, d), lambda j, i: (i, 0)), wspec, wspec] + [ANY] * n,
                         out_specs=[ospec, ospec, ospec] + [ANY] * n,
                         out_shape=[osh, osh, osh] + [jax.ShapeDtypeStruct(t.shape, t.dtype) for t in gather],
                         input_output_aliases={3 + a: 3 + a for a in range(n)},
                         scratch_shapes=_gather_sems(n) if n else [],
                         compiler_params=_params(*(["arbitrary"] * 2 if n else ["parallel"] * 2)))(h, wgate, wup, *gather)
    return res[:3], list(res[3:])


def ffn_down_bwd(name, df, wdown, layer, gt, up):
    s, d = df.shape
    g, _, fg, _ = wdown.shape
    f = g * fg
    assert d <= FULL_K_PREFS[0]
    tm = _pick(s, (256, 128))
    tn = _pick(fg, TW_PREFS)
    per = fg // tn

    def body(df_ref, wd_ref, gt_ref, up_ref, dgt_ref, dup_ref):
        da = _dot(df_ref[...], wd_ref[...], "nt")
        gt = gt_ref[...].astype(F32)
        up = up_ref[...].astype(F32)
        silu, sg = _silu_parts(gt)
        dgt_ref[...] = (da * up * (sg * (1.0 + gt * (1.0 - sg)))).astype(BF16)
        dup_ref[...] = (da * silu).astype(BF16)

    ospec = pl.BlockSpec((tm, tn), lambda j, i: (i, j))
    osh = jax.ShapeDtypeStruct((s, f), BF16)
    return pl.pallas_call(body, name=name, grid=(f // tn, s // tm),
                          in_specs=[pl.BlockSpec((tm, d), lambda j, i: (i, 0)),
                                    pl.BlockSpec((None, None, tn, d), lambda j, i: (j // per, layer, j % per, 0)),
                                    ospec, ospec],
                          out_specs=[ospec, ospec], out_shape=[osh, osh],
                          compiler_params=_params("parallel", "parallel"))(df, wdown, gt, up)


def ffn_in_bwd(name, dgt, dup, wgate, wup, layer, exchange=()):
    s, f = dgt.shape
    g, _, d, ng = wgate.shape
    tm = _pick(s, (512, 256, 128))
    tn = _pick(d, TM_PREFS)
    tk = _pick(ng, TW_PREFS)
    per = ng // tk
    nk = g * per

    n = len(exchange)
    grid = (s // tm, d // tn, nk)

    def body(dg_ref, du_ref, wg_ref, wu_ref, *rest):
        o_ref, acc = rest[n], rest[2 * n + 1]
        kk = pl.program_id(2)
        if n:
            start, finish = _chip_exchange_plan(rest[:n], rest[n + 1:2 * n + 1], *rest[2 * n + 2:])
            pl.when((pl.program_id(0) == 0) & (pl.program_id(1) == 0) & (kk == 0))(start)

        @pl.when(kk == 0)
        def _():
            acc[...] = jnp.zeros_like(acc)

        acc[...] += _dot(dg_ref[...], wg_ref[...], "nt") + _dot(du_ref[...], wu_ref[...], "nt")

        @pl.when(kk == nk - 1)
        def _():
            o_ref[...] = acc[...]

        if n:
            pl.when((pl.program_id(0) == grid[0] - 1) & (pl.program_id(1) == grid[1] - 1) & (kk == nk - 1))(finish)

    aspec = pl.BlockSpec((tm, tk), lambda i, j, kk: (i, kk))
    wspec = pl.BlockSpec((None, None, tn, tk), lambda i, j, kk: (kk // per, layer, j, kk % per))
    res = pl.pallas_call(body, name=name, grid=grid, in_specs=[aspec, aspec, wspec, wspec] + [ANY] * n,
                         out_specs=[pl.BlockSpec((tm, tn), lambda i, j, kk: (i, j))] + [ANY] * n,
                         out_shape=[jax.ShapeDtypeStruct((s, d), F32)] + _chip_exchange_shapes(exchange),
                         scratch_shapes=[pltpu.VMEM((tm, tn), F32)] + (_chip_exchange_sems(n) if n else []),
                         compiler_params=_params(*(["arbitrary"] * 3 if n else ["parallel", "parallel", "arbitrary"])))(
                             dgt, dup, wgate, wup, *exchange)
    return res[0], list(res[1:])


def _halo_specs(tm, s, cols, cb, halo=HALO):
    r = tm // halo
    last = s // halo - 1
    return [pl.BlockSpec((halo, cols), lambda i: (jnp.maximum(i * r - 1, 0), cb)),
            pl.BlockSpec((tm, cols), lambda i: (i, cb)),
            pl.BlockSpec((halo, cols), lambda i: (jnp.minimum((i + 1) * r, last), cb))]


def _edge_masked(prev_ref, next_ref, i, nt):
    return jnp.where(i > 0, prev_ref[...], 0.0), jnp.where(i < nt - 1, next_ref[...], 0.0)


def _glu(a, gate):
    return a * jax.nn.sigmoid(gate)


def _conv_post(conv, lg, lb, gm):
    mu = jnp.mean(conv, axis=-1, keepdims=True)
    xc = conv - mu
    rs = lax.rsqrt(jnp.mean(xc * xc, axis=-1, keepdims=True) + EPS)
    xhat = xc * rs
    yl = xhat * lg + lb
    ya, sg = _silu_parts(yl)
    return xhat, rs, yl, sg, ya


def _fill_glu(hg, ap, am, an, gp, gm_, gn, i, nt, tm):
    a0, a2 = _edge_masked(ap, an, i, nt)
    g0, g2 = _edge_masked(gp, gn, i, nt)
    hg[pl.ds(0, HALO), :] = _glu(a0, g0)
    hg[pl.ds(HALO, tm), :] = _glu(am[...], gm_[...])
    hg[pl.ds(HALO + tm, HALO), :] = _glu(a2, g2)


def conv_fwd(name, proj, cw, cb, lg, lb, gmix):
    s = proj.shape[0]
    tm = _pick(s, (256, 128))
    nt = s // tm

    def body(ap, am, an, gp, gm_, gn, cw_ref, cb_ref, lg_ref, lb_ref, gx_ref, o_ref, conv_ref, hg):
        i = pl.program_id(0)
        _fill_glu(hg, ap, am, an, gp, gm_, gn, i, nt, tm)
        acc = jnp.zeros((tm, C_A), F32) + cb_ref[...]
        for t in range(CONV_WIDTH):
            acc = acc + cw_ref[pl.ds(t, 1), :] * hg[pl.ds(HALO - CONV_HALF + t, tm), :]
        conv_ref[...] = acc
        ya = _conv_post(acc, lg_ref[...], lb_ref[...], None)[4]
        o_ref[...] = (ya * _rms_scale(ya) * gx_ref[...]).astype(BF16)

    vec = _const_spec((1, C_A))
    return pl.pallas_call(body, name=name, grid=(nt,),
                          in_specs=_halo_specs(tm, s, C_A, 0) + _halo_specs(tm, s, C_A, 1)
                          + [_const_spec((32, C_A)), vec, vec, vec, vec],
                          out_specs=[_row_spec(tm, C_A), _row_spec(tm, C_A)],
                          out_shape=[jax.ShapeDtypeStruct((s, C_A), BF16), jax.ShapeDtypeStruct((s, C_A), F32)],
                          scratch_shapes=[pltpu.VMEM((tm + 2 * HALO, C_A), F32)],
                          compiler_params=_params("parallel"))(proj, proj, proj, proj, proj, proj, cw, cb, lg, lb, gmix)


def conv_bwd(name, proj, conv_out, dy, cw, lg, lb, gmix):
    s = proj.shape[0]
    tm = _pick(s, (256, 128))
    nt = s // tm
    te = tm + HALO
    off = HALO // 2

    def extended(prev_ref, main_ref, next_ref, i):
        p0, p2 = _edge_masked(prev_ref, next_ref, i, nt)
        return jnp.concatenate([p0[off:], main_ref[...], p2[:off]], axis=0)

    def body(ap, am, an, gp, gm_, gn, cp, cm, cn, dp, dm, dn, cw_ref, lg_ref, lb_ref, gx_ref,
             dag_ref, dcw_ref, dcb_ref, dlg_ref, dlb_ref, dgx_ref, hg, dc):
        i = pl.program_id(0)
        _fill_glu(hg, ap, am, an, gp, gm_, gn, i, nt, tm)
        lg, lb, gx = lg_ref[...], lb_ref[...], gx_ref[...]
        xhat, rs, yl, sg, ya = _conv_post(extended(cp, cm, cn, i), lg, lb, gx)
        dout = extended(dp, dm, dn, i)
        dya, dgx_rows = _rms_bwd(dout, ya, gx)
        dyl = dya * (sg * (1.0 + yl * (1.0 - sg)))
        dxh = dyl * lg
        dconv = rs * (dxh - jnp.mean(dxh, axis=-1, keepdims=True)
                      - xhat * jnp.mean(dxh * xhat, axis=-1, keepdims=True))
        dc[...] = dconv

        @pl.when(i == 0)
        def _():
            for r in (dcw_ref, dcb_ref, dlg_ref, dlb_ref, dgx_ref):
                r[...] = jnp.zeros_like(r)

        dcb_ref[...] += _sum8(dconv[off:off + tm])
        dlg_ref[...] += _sum8((dyl * xhat)[off:off + tm])
        dlb_ref[...] += _sum8(dyl[off:off + tm])
        dgx_ref[...] += _sum8(dgx_rows[off:off + tm])
        dcm = dconv[off:off + tm]
        dhg = jnp.zeros((tm, C_A), F32)
        for t in range(CONV_WIDTH):
            dhg = dhg + cw_ref[pl.ds(t, 1), :] * dc[pl.ds(HALO - 1 - t, tm), :]
            dcw_ref[pl.ds(8 * t, 8), :] += _sum8(dcm * hg[pl.ds(HALO - CONV_HALF + t, tm), :])
        a, gate = am[...], gm_[...]
        sgate = jax.nn.sigmoid(gate)
        dag_ref[:, pl.ds(0, C_A)] = (dhg * sgate).astype(BF16)
        dag_ref[:, pl.ds(C_A, C_A)] = (dhg * a * sgate * (1.0 - sgate)).astype(BF16)

    vec = _const_spec((1, C_A))
    acc8 = _const_spec((8, C_A))
    sh8 = jax.ShapeDtypeStruct((8, C_A), F32)
    halo = functools.partial(_halo_specs, tm, s, C_A)
    return pl.pallas_call(body, name=name, grid=(nt,),
                          in_specs=halo(0) + halo(1) + halo(0) + halo(0) + [_const_spec((32, C_A)), vec, vec, vec],
                          out_specs=[_row_spec(tm, 2 * C_A), _const_spec((32 * 8, C_A)), acc8, acc8, acc8, acc8],
                          out_shape=[jax.ShapeDtypeStruct((s, 2 * C_A), BF16), jax.ShapeDtypeStruct((32 * 8, C_A), F32),
                                     sh8, sh8, sh8, sh8],
                          scratch_shapes=[pltpu.VMEM((tm + 2 * HALO, C_A), F32), pltpu.VMEM((te, C_A), F32)],
                          compiler_params=_params("arbitrary"))(proj, proj, proj, proj, proj, proj,
                                                                conv_out, conv_out, conv_out, dy, dy, dy, cw, lg, lb, gmix)


def _pool_counts(pos, win, s):
    lo = jnp.maximum(pos - win // 2, 0)
    hi = jnp.minimum(pos + win - win // 2, s)
    return jnp.maximum(hi - lo, 1).astype(F32)


def _pooled(uext, base, rows, pos, s):
    outs = []
    for gi, win in enumerate(POOL_WINDOWS):
        lanes = pl.ds(gi * C_G, C_G)
        acc = jnp.zeros((rows, C_G), F32)
        for o in range(-(win // 2), win - win // 2):
            acc = acc + uext[pl.ds(base + o, rows), lanes]
        outs.append(acc / _pool_counts(pos, win, s) - uext[pl.ds(base, rows), lanes])
    return outs


def pool_fwd(name, proj, pw, scale, gmix):
    s = proj.shape[0]
    tm = _pick(s, (256, 128))
    nt = s // tm
    ucol = (proj.shape[1] - C_C) // C_C

    def body(up, um, un, pw_ref, sc_ref, gx_ref, o_ref, uext):
        i = pl.program_id(0)
        u0, u2 = _edge_masked(up, un, i, nt)
        uext[pl.ds(0, HALO), :] = u0
        uext[pl.ds(HALO, tm), :] = um[...]
        uext[pl.ds(HALO + tm, HALO), :] = u2
        pos = i * tm + lax.broadcasted_iota(jnp.int32, (tm, 1), 0)
        pooled = _pooled(uext, HALO, tm, pos, s)
        mixed = jnp.concatenate([_dot(pooled[g].astype(BF16), pw_ref[g], "nn") for g in range(4)], axis=1)
        yc = mixed * sc_ref[...]
        o_ref[...] = (yc * _rms_scale(yc) * gx_ref[...]).astype(BF16)

    vec = _const_spec((1, C_C))
    return pl.pallas_call(body, name=name, grid=(nt,),
                          in_specs=_halo_specs(tm, s, C_C, ucol) + [_const_spec((4, C_G, C_G)), vec, vec],
                          out_specs=_row_spec(tm, C_C), out_shape=jax.ShapeDtypeStruct((s, C_C), BF16),
                          scratch_shapes=[pltpu.VMEM((tm + 2 * HALO, C_C), F32)],
                          compiler_params=_params("parallel"))(proj, proj, proj, pw, scale, gmix)


def pool_bwd(name, proj, dy, pw, scale, gmix):
    s = proj.shape[0]
    tm = _pick(s, (256, 128))
    nt = s // tm
    te = tm + HALO
    off = HALO // 2
    ucol = (proj.shape[1] - C_C) // C_C
    dcol = (dy.shape[1] - C_C) // C_C

    def body(up, um, un, dp, dm, dn, pw_ref, sc_ref, gx_ref, du_ref, dpw_ref, dsc_ref, dgx_ref, uext, dyx, qs, dps):
        i = pl.program_id(0)
        u0, u2 = _edge_masked(up, un, i, nt)
        uext[pl.ds(0, HALO), :] = u0
        uext[pl.ds(HALO, tm), :] = um[...]
        uext[pl.ds(HALO + tm, HALO), :] = u2
        d0, d2 = _edge_masked(dp, dn, i, nt)
        dyx[pl.ds(0, HALO), :] = d0
        dyx[pl.ds(HALO, tm), :] = dm[...]
        dyx[pl.ds(HALO + tm, HALO), :] = d2
        pos = i * tm - off + lax.broadcasted_iota(jnp.int32, (te, 1), 0)
        pooled = _pooled(uext, off, te, pos, s)
        mixed = jnp.concatenate([_dot(pooled[g].astype(BF16), pw_ref[g], "nn") for g in range(4)], axis=1)
        sc = sc_ref[...]
        yc = mixed * sc
        dyc, dgx_rows = _rms_bwd(dyx[pl.ds(off, te), :], yc, gx_ref[...])
        dmixed = dyc * sc

        @pl.when(i == 0)
        def _():
            for r in (dpw_ref, dsc_ref, dgx_ref):
                r[...] = jnp.zeros_like(r)

        dsc_ref[...] += _sum8((dyc * mixed)[off:off + tm])
        dgx_ref[...] += _sum8(dgx_rows[off:off + tm])
        for gi, win in enumerate(POOL_WINDOWS):
            lanes = pl.ds(gi * C_G, C_G)
            dmg = dmixed[:, gi * C_G:(gi + 1) * C_G].astype(BF16)
            dpw_ref[gi] += _dot(pooled[gi][off:off + tm].astype(BF16), dmg[off:off + tm], "tn")
            dpl = _dot(dmg, pw_ref[gi], "nt")
            dps[:, lanes] = dpl
            qs[:, lanes] = dpl / _pool_counts(pos, win, s)
        for gi, win in enumerate(POOL_WINDOWS):
            lanes = pl.ds(gi * C_G, C_G)
            acc = jnp.zeros((tm, C_G), F32) - dps[pl.ds(off, tm), lanes]
            for o in range(-(win // 2) + 1, win // 2 + 1):
                acc = acc + qs[pl.ds(off + o, tm), lanes]
            du_ref[:, lanes] = acc.astype(BF16)

    vec = _const_spec((1, C_C))
    acc8 = _const_spec((8, C_C))
    sh8 = jax.ShapeDtypeStruct((8, C_C), F32)
    return pl.pallas_call(body, name=name, grid=(nt,),
                          in_specs=_halo_specs(tm, s, C_C, ucol) + _halo_specs(tm, s, C_C, dcol)
                          + [_const_spec((4, C_G, C_G)), vec, vec],
                          out_specs=[_row_spec(tm, C_C), _const_spec((4, C_G, C_G)), acc8, acc8],
                          out_shape=[jax.ShapeDtypeStruct((s, C_C), BF16), jax.ShapeDtypeStruct((4, C_G, C_G), F32),
                                     sh8, sh8],
                          scratch_shapes=[pltpu.VMEM((tm + 2 * HALO, C_C), F32), pltpu.VMEM((tm + 2 * HALO, C_C), F32),
                                          pltpu.VMEM((te, C_C), F32), pltpu.VMEM((te, C_C), F32)],
                          compiler_params=_params("arbitrary"))(proj, proj, proj, dy, dy, dy, pw, scale, gmix)


def rope_tables(s):
    pos = jnp.arange(s, dtype=F32)
    inv = ROPE_THETA ** (-jnp.arange(0, ROT_DIM, 2, dtype=F32) / ROT_DIM)
    ang = pos[:, None] * inv[None, :]
    cos, sin = jnp.cos(ang), jnp.sin(ang)
    half = ROT_DIM // 2
    rest = HEAD_DIM - ROT_DIM
    c = jnp.concatenate([cos, cos, jnp.ones((s, rest), F32)], axis=1)
    sa = jnp.concatenate([-sin, jnp.zeros((s, HEAD_DIM - half), F32)], axis=1)
    sb = jnp.concatenate([jnp.zeros((s, half), F32), sin, jnp.zeros((s, rest), F32)], axis=1)
    return tuple(jnp.concatenate([t, t], axis=1) for t in (c, sa, sb))


DIL_TILE = 256
N_CHUNKS = C_B // LANES


def _dilated_shape(s, d, dtype):
    return jax.ShapeDtypeStruct((s, C_B) if d == 1 else (d, s // d, C_B), dtype)


def _dilated_spec(tm, d):
    return _row_spec(tm, C_B) if d == 1 else pl.BlockSpec((d, tm // d, C_B), lambda i: (0, i, 0))


def _as_dilated(t, d):
    return t if d == 1 else t.reshape(d, t.shape[0] // d, t.shape[1])


def _dil_scratch(tm):
    return pltpu.VMEM((N_CHUNKS, tm, LANES), F32)


def _store_dilated(o_ref, scr, d, tm):
    for r in range(d):
        for j in range(N_CHUNKS):
            o_ref[r, :, pl.ds(j * LANES, LANES)] = scr[j, pl.ds(r, tm // d, stride=d), :].astype(o_ref.dtype)


def _load_dilated(in_ref, scr, d, tm):
    for r in range(d):
        for j in range(N_CHUNKS):
            scr[j, pl.ds(r, tm // d, stride=d), :] = in_ref[r, :, pl.ds(j * LANES, LANES)].astype(F32)


def rope_fwd(name, proj, tabs):
    s = proj.shape[0]
    tm = DIL_TILE
    scale = HEAD_DIM ** -0.5

    def body(q_ref, k_ref, v_ref, c_ref, sa_ref, sb_ref, *rest):
        outs, scr = rest[:9], rest[9:]
        c, sa, sb = c_ref[...], sa_ref[...], sb_ref[...]
        for j in range(N_CHUNKS):
            lanes = pl.ds(j * LANES, LANES)
            for which, (src, mul) in enumerate(((q_ref, scale), (k_ref, 1.0))):
                t = src[:, lanes]
                r = t * c + pltpu.roll(t, LANES - ROT_DIM // 2, 1) * sa + pltpu.roll(t, ROT_DIM // 2, 1) * sb
                scr[which][j] = r * mul
            scr[2][j] = v_ref[:, lanes]
            for which in range(3):
                outs[which][:, lanes] = scr[which][j].astype(BF16)
        for pi, d in enumerate(DILATIONS[1:]):
            for which in range(3):
                _store_dilated(outs[3 * (pi + 1) + which], scr[which], d, tm)

    tab = _row_spec(tm, LANES)
    res = pl.pallas_call(body, name=name, grid=(s // tm,),
                         in_specs=[_row_spec(tm, C_B, 1), _row_spec(tm, C_B, 2), _row_spec(tm, C_B, 3), tab, tab, tab],
                         out_specs=[_dilated_spec(tm, d) for d in DILATIONS for _ in range(3)],
                         out_shape=[_dilated_shape(s, d, BF16) for d in DILATIONS for _ in range(3)],
                         scratch_shapes=[_dil_scratch(tm)] * 3,
                         compiler_params=_params("parallel"))(proj, proj, proj, *tabs)
    return [tuple(t.reshape(s, C_B) for t in res[3 * pi:3 * pi + 3]) for pi in range(len(DILATIONS))]


def rope_bwd(name, dqs, dks, dvs, tabs):
    s = dqs[0].shape[0]
    tm = DIL_TILE
    scale = HEAD_DIM ** -0.5
    n_dil = len(DILATIONS)

    def body(*refs):
        ins = [refs[n_dil * which:n_dil * (which + 1)] for which in range(3)]
        c_ref, sa_ref, sb_ref, o_ref = refs[3 * n_dil:3 * n_dil + 4]
        scr = refs[3 * n_dil + 4:]
        for which in range(3):
            for pi, d in enumerate(DILATIONS[1:]):
                _load_dilated(ins[which][pi + 1], scr[which * (n_dil - 1) + pi], d, tm)
        c, sa, sb = c_ref[...], sa_ref[...], sb_ref[...]
        for j in range(N_CHUNKS):
            lanes = pl.ds(j * LANES, LANES)
            for which, mul in ((0, scale), (1, 1.0), (2, None)):
                d = ins[which][0][:, lanes].astype(F32)
                for pi in range(n_dil - 1):
                    d = d + scr[which * (n_dil - 1) + pi][j]
                if mul is not None:
                    d = (d * c + pltpu.roll(d * sa, ROT_DIM // 2, 1) + pltpu.roll(d * sb, LANES - ROT_DIM // 2, 1)) * mul
                o_ref[:, pl.ds(which * C_B + j * LANES, LANES)] = d.astype(BF16)

    tab = _row_spec(tm, LANES)
    args = [_as_dilated(t, d) for ts in (dqs, dks, dvs) for t, d in zip(ts, DILATIONS)]
    return pl.pallas_call(body, name=name, grid=(s // tm,),
                          in_specs=[_dilated_spec(tm, d) for _ in range(3) for d in DILATIONS] + [tab, tab, tab],
                          out_specs=_row_spec(tm, 3 * C_B), out_shape=jax.ShapeDtypeStruct((s, 3 * C_B), BF16),
                          scratch_shapes=[_dil_scratch(tm)] * (3 * (n_dil - 1)),
                          compiler_params=_params("parallel"))(*args, *tabs)


def _window_specs(tq, s, cols):
    r = tq // ATT_HALF
    last = s // ATT_HALF - 1
    return [pl.BlockSpec((ATT_HALF, cols), lambda i: (jnp.maximum(i * r - 1, 0), 0)),
            pl.BlockSpec((tq, cols), lambda i: (i, 0)),
            pl.BlockSpec((ATT_HALF, cols), lambda i: (jnp.minimum((i + 1) * r, last), 0))]


def _fill_window(win, prev_ref, main_ref, next_ref, tq):
    win[pl.ds(0, ATT_HALF), :] = prev_ref[...]
    win[pl.ds(ATT_HALF, tq), :] = main_ref[...]
    win[pl.ds(ATT_HALF + tq, ATT_HALF), :] = next_ref[...]


def _band_valid(tile, window, j0, seg, tile_is_rows):
    shape = (2 * tile, window) if tile_is_rows else (2 * window, tile)
    ri = lax.broadcasted_iota(jnp.int32, shape, 0)
    ci = lax.broadcasted_iota(jnp.int32, shape, 1)
    per_head = tile if tile_is_rows else window
    ri = jnp.where(ri >= per_head, ri - per_head, ri)
    ti, wi = (ri, ci) if tile_is_rows else (ci, ri)
    jw = j0 - ATT_HALF + wi
    return (jnp.abs(wi - ATT_HALF - ti) <= ATT_HALF) & (jw >= 0) & (jw < seg)


def _first_head():
    return lax.broadcasted_iota(jnp.int32, (1, LANES), 1) < HEAD_DIM


def _stack_heads(t, first):
    z = jnp.zeros_like(t)
    return jnp.concatenate([jnp.where(first, t, z), jnp.where(first, z, t)], axis=0)


def _unstack_heads(t2, first, rows):
    return jnp.where(first, t2[:rows], t2[rows:])


def _head_columns(ref, pair, rows):
    return jnp.concatenate([ref[rows, pl.ds(pair * LANES, 1)], ref[rows, pl.ds(pair * LANES + HEAD_DIM, 1)]], axis=0)


def _sub_tiles(tile):
    sub = min(tile, 2 * ATT_HALF)
    return sub, tile // sub, sub + 2 * ATT_HALF


def band_attn_fwd(name, q, k, v, seg):
    s, c = q.shape
    tq = _pick(seg, (256, 128))
    per_seg = seg // tq

    wrows = tq + 2 * ATT_HALF

    sub, n_sub, wsub = _sub_tiles(tq)

    def body(q_ref, kp, km, kn, vp, vm, vn, o_ref, lse_ref, kw, vw):
        j0 = (pl.program_id(0) % per_seg) * tq
        _fill_window(kw, kp, km, kn, tq)
        _fill_window(vw, vp, vm, vn, tq)
        first = _first_head()
        for h in range(n_sub):
            rows, wnd = pl.ds(h * sub, sub), pl.ds(h * sub, wsub)
            valid = _band_valid(sub, wsub, j0 + h * sub, seg, True)
            for pair in range(c // LANES):
                lanes = pl.ds(pair * LANES, LANES)
                sc = jnp.where(valid, _dot(_stack_heads(q_ref[rows, lanes], first), kw[wnd, lanes], "nt"), NEG)
                m = jnp.max(sc, axis=-1, keepdims=True)
                p = jnp.exp(sc - m)
                l = jnp.sum(p, axis=-1, keepdims=True)
                o = _dot(p.astype(BF16), vw[wnd, lanes], "nn") / l
                o_ref[rows, lanes] = _unstack_heads(o, first, sub).astype(BF16)
                lse_ref[rows, lanes] = _unstack_heads(m + jnp.log(l), first, sub)

    win = _window_specs(tq, s, c)
    tile = _row_spec(tq, c)
    return pl.pallas_call(body, name=name, grid=(s // tq,), in_specs=[tile] + win + win, out_specs=[tile, tile],
                          out_shape=[jax.ShapeDtypeStruct((s, c), BF16), jax.ShapeDtypeStruct((s, c), F32)],
                          scratch_shapes=[pltpu.VMEM((wrows, c), BF16)] * 2,
                          compiler_params=_params("parallel"))(q, k, k, k, v, v, v)


def band_attn_dq(name, q, k, v, do, lse, delta, seg):
    s, c = q.shape
    tq = _pick(seg, (256, 128))
    per_seg = seg // tq

    wrows = tq + 2 * ATT_HALF

    sub, n_sub, wsub = _sub_tiles(tq)

    def body(q_ref, kp, km, kn, vp, vm, vn, do_ref, lse_ref, dl_ref, dq_ref, kw, vw):
        j0 = (pl.program_id(0) % per_seg) * tq
        _fill_window(kw, kp, km, kn, tq)
        _fill_window(vw, vp, vm, vn, tq)
        first = _first_head()
        for h in range(n_sub):
            rows, wnd = pl.ds(h * sub, sub), pl.ds(h * sub, wsub)
            valid = _band_valid(sub, wsub, j0 + h * sub, seg, True)
            for pair in range(c // LANES):
                lanes = pl.ds(pair * LANES, LANES)
                sc = _dot(_stack_heads(q_ref[rows, lanes], first), kw[wnd, lanes], "nt")
                p = jnp.where(valid, jnp.exp(sc - _head_columns(lse_ref, pair, rows)), 0.0)
                dp = _dot(_stack_heads(do_ref[rows, lanes], first), vw[wnd, lanes], "nt")
                ds = p * (dp - _head_columns(dl_ref, pair, rows))
                dq = _dot(ds.astype(BF16), kw[wnd, lanes], "nn")
                dq_ref[rows, lanes] = _unstack_heads(dq, first, sub).astype(BF16)

    win = _window_specs(tq, s, c)
    tile = _row_spec(tq, c)
    return pl.pallas_call(body, name=name, grid=(s // tq,), in_specs=[tile] + win + win + [tile, tile, tile],
                          out_specs=tile, out_shape=jax.ShapeDtypeStruct((s, c), BF16),
                          scratch_shapes=[pltpu.VMEM((wrows, c), BF16)] * 2,
                          compiler_params=_params("parallel"))(q, k, k, k, v, v, v, do, lse, delta)


def band_attn_dkv(name, q, k, v, do, lse, delta, seg):
    s, c = q.shape
    tk = _pick(seg, (256, 128))
    per_seg = seg // tk

    wrows = tk + 2 * ATT_HALF
    sub, n_sub, wsub = tk, 1, wrows

    def body(k_ref, v_ref, qp, qm, qn, dop, dom, don, lp, lm, ln, dlp, dlm, dln, dk_ref, dv_ref, qw, dow, lsew, dlw):
        j0 = (pl.program_id(0) % per_seg) * tk
        _fill_window(qw, qp, qm, qn, tk)
        _fill_window(dow, dop, dom, don, tk)
        _fill_window(lsew, lp, lm, ln, tk)
        _fill_window(dlw, dlp, dlm, dln, tk)
        first = _first_head()
        for h in range(n_sub):
            rows, wnd = pl.ds(h * sub, sub), pl.ds(h * sub, wsub)
            valid = _band_valid(sub, wsub, j0 + h * sub, seg, False)
            for pair in range(c // LANES):
                lanes = pl.ds(pair * LANES, LANES)
                qq = _stack_heads(qw[wnd, lanes], first)
                dd = _stack_heads(dow[wnd, lanes], first)
                sc = _dot(qq, k_ref[rows, lanes], "nt")
                p = jnp.where(valid, jnp.exp(sc - _head_columns(lsew, pair, wnd)), 0.0)
                dv_ref[rows, lanes] = _dot(p.astype(BF16), dd, "tn").astype(BF16)
                dp = _dot(dd, v_ref[rows, lanes], "nt")
                ds = p * (dp - _head_columns(dlw, pair, wnd))
                dk_ref[rows, lanes] = _dot(ds.astype(BF16), qq, "tn").astype(BF16)

    win = _window_specs(tk, s, c)
    tile = _row_spec(tk, c)
    osh = jax.ShapeDtypeStruct((s, c), BF16)
    return pl.pallas_call(body, name=name, grid=(s // tk,), in_specs=[tile, tile] + win * 4,
                          out_specs=[tile, tile], out_shape=[osh, osh],
                          scratch_shapes=[pltpu.VMEM((wrows, c), BF16)] * 2 + [pltpu.VMEM((wrows, c), F32)] * 2,
                          compiler_params=_params("parallel"))(k, v, q, q, q, do, do, do, lse, lse, lse,
                                                               delta, delta, delta)


def attn_combine_fwd(name, os_, lses, gmix):
    s, c = os_[0].shape
    tm = DIL_TILE
    n_dil = len(DILATIONS)

    def body(*refs):
        o_refs, l_refs, gx_ref = refs[:n_dil], refs[n_dil:2 * n_dil], refs[2 * n_dil]
        yb_ref, y_ref = refs[2 * n_dil + 1:2 * n_dil + 3]
        lse_refs = refs[2 * n_dil + 3:3 * n_dil + 3]
        scr = refs[3 * n_dil + 3:]
        scr_o, scr_l, scr_lse = scr[:n_dil - 1], scr[n_dil - 1:2 * (n_dil - 1)], scr[-1]
        for pi, d in enumerate(DILATIONS[1:]):
            _load_dilated(o_refs[pi + 1], scr_o[pi], d, tm)
            _load_dilated(l_refs[pi + 1], scr_l[pi], d, tm)
        sumsq = jnp.zeros((tm, 1), F32)
        for j in range(N_CHUNKS):
            lanes = pl.ds(j * LANES, LANES)
            ls = [l_refs[0][:, lanes]] + [t[j] for t in scr_l]
            vals = [o_refs[0][:, lanes].astype(F32)] + [t[j] for t in scr_o]
            m = functools.reduce(jnp.maximum, ls)
            es = [jnp.exp(l - m) for l in ls]
            den = functools.reduce(lambda a, b: a + b, es)
            yb = functools.reduce(lambda a, b: a + b, [e * v for e, v in zip(es, vals)]) / den
            yb_ref[:, lanes] = yb
            lse = m + jnp.log(den)
            lse_refs[0][:, lanes] = lse
            scr_lse[j] = lse
            sumsq = sumsq + jnp.sum(yb * yb, axis=-1, keepdims=True)
        r = lax.rsqrt(sumsq / c + EPS)
        y_ref[...] = (yb_ref[...] * r * gx_ref[...]).astype(BF16)
        for pi, d in enumerate(DILATIONS[1:]):
            _store_dilated(lse_refs[pi + 1], scr_lse, d, tm)

    row = _row_spec(tm, c)
    dil = [_dilated_spec(tm, d) for d in DILATIONS]
    res = pl.pallas_call(body, name=name, grid=(s // tm,), in_specs=dil + dil + [_const_spec((1, c))],
                         out_specs=[row, row] + dil,
                         out_shape=[jax.ShapeDtypeStruct((s, c), F32), jax.ShapeDtypeStruct((s, c), BF16)]
                         + [_dilated_shape(s, d, F32) for d in DILATIONS],
                         scratch_shapes=[_dil_scratch(tm)] * (2 * (n_dil - 1) + 1),
                         compiler_params=_params("parallel"))(
                             *[_as_dilated(t, d) for t, d in zip(os_, DILATIONS)],
                             *[_as_dilated(t, d) for t, d in zip(lses, DILATIONS)], gmix)
    return res[0], res[1], [t.reshape(s, c) for t in res[2:]]


def attn_combine_bwd(name, dy, yb, gmix):
    s, c = yb.shape
    tm = DIL_TILE
    half = c // 2
    n_dil = len(DILATIONS)

    def body(d1, d2, yb_ref, gx_ref, *rest):
        dyb_refs, dl_refs, dgx_ref = rest[:n_dil], rest[n_dil:2 * n_dil], rest[2 * n_dil]
        scr_dyb, scr_dl = rest[2 * n_dil + 1:]
        i = pl.program_id(0)
        d = jnp.concatenate([d1[...], d2[...]], axis=1)
        yb_ = yb_ref[...]
        dyb, dgx_rows = _rms_bwd(d, yb_, gx_ref[...])
        dyb_refs[0][...] = dyb.astype(BF16)

        @pl.when(i == 0)
        def _():
            dgx_ref[...] = jnp.zeros_like(dgx_ref)

        dgx_ref[...] += _sum8(dgx_rows)
        ri = lax.broadcasted_iota(jnp.int32, (LANES, LANES), 0) // HEAD_DIM
        ci = lax.broadcasted_iota(jnp.int32, (LANES, LANES), 1) // HEAD_DIM
        same_head = jnp.where(ri == ci, 1.0, 0.0).astype(BF16)
        prod = dyb * yb_
        for j in range(N_CHUNKS):
            pj = prod[:, j * LANES:(j + 1) * LANES]
            hi = pj.astype(BF16)
            lo = (pj - hi.astype(F32)).astype(BF16)
            delta = _dot(hi, same_head, "nn") + _dot(lo, same_head, "nn")
            dl_refs[0][:, pl.ds(j * LANES, LANES)] = delta
            scr_dl[j] = delta
            scr_dyb[j] = dyb[:, j * LANES:(j + 1) * LANES]
        for pi, dil in enumerate(DILATIONS[1:]):
            _store_dilated(dyb_refs[pi + 1], scr_dyb, dil, tm)
            _store_dilated(dl_refs[pi + 1], scr_dl, dil, tm)

    row = _row_spec(tm, c)
    dil_specs = [_dilated_spec(tm, d) for d in DILATIONS]
    res = pl.pallas_call(body, name=name, grid=(s // tm,),
                         in_specs=[_row_spec(tm, half, 1), _row_spec(tm, half, 2), row, _const_spec((1, c))],
                         out_specs=dil_specs + dil_specs + [_const_spec((8, c))],
                         out_shape=[_dilated_shape(s, d, BF16) for d in DILATIONS]
                         + [_dilated_shape(s, d, F32) for d in DILATIONS] + [jax.ShapeDtypeStruct((8, c), F32)],
                         scratch_shapes=[_dil_scratch(tm)] * 2,
                         compiler_params=_params("arbitrary"))(dy, dy, yb, gmix)
    return ([t.reshape(s, c) for t in res[:n_dil]], [t.reshape(s, c) for t in res[n_dil:2 * n_dil]], res[2 * n_dil])


def _ew_rows(rows, cols, n_bufs):
    budget = 24 * 1024 * 1024 // (8 * n_bufs * cols)
    return _pick(rows, tuple(t for t in (2048, 1024, 512, 256, 128, 64, 32, 16, 8) if t <= max(budget, 8)))


def elementwise(name, fn, ins, out_dtypes):
    shape = ins[0].shape
    cols = shape[-1]
    rows = math.prod(shape[:-1])
    tr = _ew_rows(rows, cols, len(ins) + len(out_dtypes))
    n_in = len(ins)

    def body(*refs):
        outs = fn(*[r[...] for r in refs[:n_in]])
        for o_ref, o in zip(refs[n_in:], outs):
            o_ref[...] = o.astype(o_ref.dtype)

    spec = _row_spec(tr, cols)
    res = pl.pallas_call(body, name=name, grid=(rows // tr,), in_specs=[spec] * n_in, out_specs=[spec] * len(out_dtypes),
                         out_shape=[jax.ShapeDtypeStruct((rows, cols), dt) for dt in out_dtypes],
                         compiler_params=_params("parallel"))(*[t.reshape(rows, cols) for t in ins])
    return [r.reshape(shape) for r in res]


def _adamw(g, w, m, v):
    m = ADAM_B1 * m + (1.0 - ADAM_B1) * g
    v = ADAM_B2 * v + (1.0 - ADAM_B2) * (g * g)
    m_hat = m / (1.0 - ADAM_B1 ** ADAM_STEP)
    v_hat = v / (1.0 - ADAM_B2 ** ADAM_STEP)
    delta = -ADAM_LR * (m_hat / (jnp.sqrt(v_hat) + ADAM_EPS) + ADAM_WD * w)
    return delta, m, v


def cast_into_gathered(w, place, layer):
    _, r, c = w.shape
    tr = _ew_rows(r, c, 2)

    def body(p_ref, w_ref, o_ref):
        o_ref[...] = w_ref[...].astype(BF16)

    grid_spec = pltpu.PrefetchScalarGridSpec(
        num_scalar_prefetch=1, grid=(r // tr,),
        in_specs=[pl.BlockSpec((None, tr, c), lambda i, p: (layer, i, 0))],
        out_specs=pl.BlockSpec((None, None, tr, c), lambda i, p: (p[0], 0, i, 0)))
    return pl.pallas_call(body, name="cast_into_gathered", grid_spec=grid_spec,
                          out_shape=jax.ShapeDtypeStruct((N_GROUPS, 1, r, c), BF16),
                          compiler_params=_params("parallel"))(place, w)


def sum_core_halves(name, grad, theirs, place):
    g, r, c = grad.shape
    h = r // 2
    tr = _ew_rows(h, c, 3)
    nh = h // tr

    def body(p_ref, a_ref, b_ref, o_ref):
        o_ref[...] = (a_ref[...].astype(F32) + b_ref[...].astype(F32)).astype(BF16)

    blk = pl.BlockSpec((None, tr, c), lambda gi, i, p: (gi, i, 0))
    grid_spec = pltpu.PrefetchScalarGridSpec(
        num_scalar_prefetch=1, grid=(g, nh),
        in_specs=[pl.BlockSpec((None, tr, c), lambda gi, i, p: (gi, p[1] * nh + i, 0)), blk], out_specs=blk)
    return pl.pallas_call(body, name=name, grid_spec=grid_spec, out_shape=jax.ShapeDtypeStruct((g, h, c), BF16),
                          compiler_params=_params("parallel", "parallel"))(place, grad, theirs)


def sum_chip_partials(name, parts, recv, place, layer, n_layers, into):
    _, h, c = parts.shape
    tr = _ew_rows(h, c, 5)
    nh = h // tr

    def body(p_ref, o_ref, r0, r1, r2, *rest):
        rest[-1][...] = ((o_ref[...].astype(F32) + r0[...].astype(F32)) + r1[...].astype(F32)) + r2[...].astype(F32)

    in_specs = [pl.BlockSpec((None, tr, c), lambda i, p: (p[0], i, 0))]
    in_specs += [pl.BlockSpec((None, tr, c), lambda i, p, j=j: (j, i, 0)) for j in range(3)]
    args = [place, parts, recv, recv, recv]
    aliases = {}
    if into is not None:
        in_specs.append(ANY)
        args.append(into)
        aliases = {5: 0}
    grid_spec = pltpu.PrefetchScalarGridSpec(
        num_scalar_prefetch=1, grid=(nh,), in_specs=in_specs,
        out_specs=pl.BlockSpec((None, tr, c), lambda i, p: (layer, p[1] * nh + i, 0)))
    return pl.pallas_call(body, name=name, grid_spec=grid_spec, input_output_aliases=aliases,
                          out_shape=jax.ShapeDtypeStruct((n_layers, 2 * h, c), F32),
                          compiler_params=_params("parallel"))(*args)


def small_update(name, gall, w, m, v):
    _, rows, cols = gall.shape
    tr = _ew_rows(rows, cols, 16)

    def body(g_ref, w_ref, m_ref, v_ref, go, do, mo, vo):
        g = g_ref[0]
        for dev in range(1, 8):
            g = g + g_ref[dev]
        delta, mn, vn = _adamw(g, w_ref[...], m_ref[...], v_ref[...])
        go[...] = g
        do[...] = delta
        mo[...] = mn
        vo[...] = vn

    spec = _row_spec(tr, cols)
    sh = jax.ShapeDtypeStruct((rows, cols), F32)
    return pl.pallas_call(body, name=name, grid=(rows // tr,),
                          in_specs=[pl.BlockSpec((8, tr, cols), lambda i: (0, i, 0)), spec, spec, spec],
                          out_specs=[spec] * 4, out_shape=[sh] * 4, compiler_params=_params("parallel"))(gall, w, m, v)


def _place():
    x, y, c = lax.axis_index("x"), lax.axis_index("y"), lax.axis_index("c")
    chips = [(1 - x, y), (x, 1 - y), (1 - x, 1 - y)]
    return x, y, c, chips


def _remote(src, dst, send_sems, recv_sems, k, to):
    return pltpu.make_async_remote_copy(src_ref=src, dst_ref=dst, send_sem=send_sems.at[k], recv_sem=recv_sems.at[k],
                                        device_id=to, device_id_type=MESH)


ANY = pl.BlockSpec(memory_space=pl.ANY)


def allgather_small(name, block):
    m_per, n = block.shape

    def body(x_ref, out_ref, send_sems, recv_sems, local_sem):
        x, y, c, chips = _place()
        me, sibling = (x, y, c), (x, y, 1 - c)

        def rows(px, py, pc):
            return out_ref.at[pl.ds((4 * px + 2 * py + pc) * m_per, m_per), :]

        def copy(k, blk, to, src=None):
            return _remote(rows(*blk) if src is None else src, rows(*blk), send_sems, recv_sems, k, to)

        mine = pltpu.make_async_copy(x_ref, rows(*me), local_sem)
        mine.start()
        first = [copy(0, me, sibling, src=x_ref)]
        first += [copy(1 + j, me, (*chip, c), src=x_ref) for j, chip in enumerate(chips)]
        for cp in first:
            cp.start()
        passed = [copy(4 + j, (*chip, c), sibling) for j, chip in enumerate(chips)]
        for j, chip in enumerate(chips):
            copy(1 + j, (*chip, c), me).wait_recv()
            passed[j].start()
        copy(0, sibling, me).wait_recv()
        for j, chip in enumerate(chips):
            copy(4 + j, (*chip, 1 - c), me).wait_recv()
        for cp in first + passed:
            cp.wait_send()
        mine.wait()

    return pl.pallas_call(body, name=name, out_shape=jax.ShapeDtypeStruct((8 * m_per, n), block.dtype),
                          in_specs=[pl.BlockSpec(memory_space=pltpu.VMEM)], out_specs=pl.BlockSpec(memory_space=pltpu.VMEM),
                          scratch_shapes=[pltpu.SemaphoreType.DMA((7,)), pltpu.SemaphoreType.DMA((7,)),
                                          pltpu.SemaphoreType.DMA],
                          compiler_params=pltpu.CompilerParams(vmem_limit_bytes=VMEM_LIMIT))(block)


def gather_weights(bufs):
    n = len(bufs)

    def body(*refs):
        start, finish = _gather_plan(refs[n:2 * n], *refs[2 * n:])
        start()
        finish()

    return pl.pallas_call(body, name="gather_weights",
                          out_shape=[jax.ShapeDtypeStruct(t.shape, t.dtype) for t in bufs],
                          in_specs=[ANY] * n, out_specs=[ANY] * n, input_output_aliases={a: a for a in range(n)},
                          scratch_shapes=_gather_sems(n))(*bufs)


def _gather_sems(n):
    return [pltpu.SemaphoreType.DMA((6 * n,)), pltpu.SemaphoreType.DMA((6 * n,))]


def _gather_plan(outs, send_sems, recv_sems):
    n = len(outs)
    x, y, c, chips = _place()
    g0 = 2 * x + y
    sibling = (x, y, 1 - c)
    groups = [2 * cx + cy for cx, cy in chips]

    def part(a, g, cc):
        h = outs[a].shape[2] // 2
        return outs[a].at[g, :, pl.ds(cc * h, h), :]

    def ici(a, j):
        return _remote(part(a, g0, c), part(a, g0, c), send_sems, recv_sems, 3 * a + j, (*chips[j], c))

    def ici_arrival(a, j):
        return _remote(part(a, groups[j], c), part(a, groups[j], c), send_sems, recv_sems, 3 * a + j, (*chips[j], c))

    def passed(a, j, cc):
        return _remote(part(a, groups[j], cc), part(a, groups[j], cc), send_sems, recv_sems, 3 * n + 3 * a + j, sibling)

    def start():
        for a in range(n):
            for j in range(3):
                ici(a, j).start()

    def finish():
        for a in range(n):
            for j in range(3):
                ici_arrival(a, j).wait_recv()
                passed(a, j, c).start()
        for a in range(n):
            for j in range(3):
                passed(a, j, 1 - c).wait_recv()
        for a in range(n):
            for j in range(3):
                ici(a, j).wait_send()
                passed(a, j, c).wait_send()

    return start, finish


def exchange_core_halves(name, grads):
    n = len(grads)
    halves = [t.shape[1] // 2 for t in grads]

    def body(*refs):
        ins, theirs = refs[:n], refs[n:2 * n]
        send_sems, recv_sems = refs[2 * n:]
        x, y, c, _ = _place()
        sibling = (x, y, 1 - c)

        def give(a, g):
            return _remote(ins[a].at[g, pl.ds((1 - c) * halves[a], halves[a]), :], theirs[a].at[g], send_sems,
                           recv_sems, N_GROUPS * a + g, sibling)

        for a in range(n):
            for g in range(N_GROUPS):
                give(a, g).start()
        for a in range(n):
            for g in range(N_GROUPS):
                give(a, g).wait_recv()
        for a in range(n):
            for g in range(N_GROUPS):
                give(a, g).wait_send()

    shapes = [jax.ShapeDtypeStruct((t.shape[0], h, t.shape[2]), t.dtype) for t, h in zip(grads, halves)]
    k = N_GROUPS * n
    return pl.pallas_call(body, name=name, out_shape=shapes, in_specs=[ANY] * n, out_specs=[ANY] * n,
                          scratch_shapes=[pltpu.SemaphoreType.DMA((k,)), pltpu.SemaphoreType.DMA((k,))])(*grads)


def exchange_chip_partials(name, parts):
    n = len(parts)

    def body(*refs):
        start, finish = _chip_exchange_plan(refs[:n], refs[n:2 * n], *refs[2 * n:])
        start()
        finish()

    return pl.pallas_call(body, name=name, out_shape=_chip_exchange_shapes(parts), in_specs=[ANY] * n,
                          out_specs=[ANY] * n, scratch_shapes=_chip_exchange_sems(n))(*parts)


def _chip_exchange_shapes(parts):
    return [jax.ShapeDtypeStruct((3,) + t.shape[1:], t.dtype) for t in parts]


def _chip_exchange_sems(n):
    return [pltpu.SemaphoreType.DMA((3 * n,)), pltpu.SemaphoreType.DMA((3 * n,))]


def _chip_exchange_plan(ins, recv, send_sems, recv_sems):
    n = len(ins)
    x, y, c, chips = _place()

    def give(a, j):
        return _remote(ins[a].at[2 * chips[j][0] + chips[j][1]], recv[a].at[j], send_sems, recv_sems, 3 * a + j,
                       (*chips[j], c))

    def start():
        for a in range(n):
            for j in range(3):
                give(a, j).start()

    def finish():
        for a in range(n):
            for j in range(3):
                give(a, j).wait_recv()
        for a in range(n):
            for j in range(3):
                give(a, j).wait_send()

    return start, finish


def share_reduced_halves(bufs):
    n = len(bufs)
    n_layers = bufs[0].shape[0]
    halves = [t.shape[1] // 2 for t in bufs]

    def body(*refs):
        outs = refs[n:2 * n]
        send_sems, recv_sems = refs[2 * n:]
        x, y, c, _ = _place()
        sibling = (x, y, 1 - c)

        def give(l, a, cc):
            part = outs[a].at[l, pl.ds(cc * halves[a], halves[a]), :]
            return _remote(part, part, send_sems, recv_sems, l * n + a, sibling)

        for l in range(n_layers):
            for a in range(n):
                give(l, a, c).start()
        for l in range(n_layers):
            for a in range(n):
                give(l, a, 1 - c).wait_recv()
        for l in range(n_layers):
            for a in range(n):
                give(l, a, c).wait_send()

    k = n_layers * n
    return pl.pallas_call(body, name="share_reduced_halves", out_shape=[jax.ShapeDtypeStruct(t.shape, t.dtype) for t in bufs],
                          in_specs=[ANY] * n, out_specs=[ANY] * n, input_output_aliases={a: a for a in range(n)},
                          scratch_shapes=[pltpu.SemaphoreType.DMA((k,)), pltpu.SemaphoreType.DMA((k,))])(*bufs)


def reduce_in_chip(grads, place):
    theirs = exchange_core_halves("exchange_core_halves", grads)
    return [sum_core_halves("sum_core_halves", g, t, place) for g, t in zip(grads, theirs)]


def reduce_over_chips(parts, recv, place, layer, n_layers, into):
    return [sum_chip_partials("sum_chip_partials", p, r, place, layer, n_layers, None if into is None else into[a])
            for a, (p, r) in enumerate(zip(parts, recv))]


def _pack(parts):
    flat = jnp.concatenate([p.reshape(-1) for p in parts])
    pad = (-flat.shape[0]) % (8 * LANES)
    if pad:
        flat = jnp.concatenate([flat, jnp.zeros((pad,), flat.dtype)])
    return flat.reshape(-1, LANES)


def _unpack(buf, shapes):
    flat = buf.reshape(-1)
    out, at = [], 0
    for sh in shapes:
        size = math.prod(sh)
        out.append(flat[at:at + size].reshape(sh))
        at += size
    return out


def kernel(x, w_in, conv_w, conv_b, conv_ln_g, conv_ln_b, pool_w, pool_scale, g_mix, w_out, g_pre_mix, g_post_mix, g_pre_ffn, g_post_ffn, w_gate, w_up, w_down, loss_target, m_w_in, m_conv_w, m_conv_b, m_conv_ln_g, m_conv_ln_b, m_pool_w, m_pool_scale, m_g_mix, m_w_out, m_g_pre_mix, m_g_post_mix, m_g_pre_ffn, m_g_post_ffn, m_w_gate, m_w_up, m_w_down, v_w_in, v_conv_w, v_conv_b, v_conv_ln_g, v_conv_ln_b, v_pool_w, v_pool_scale, v_g_mix, v_w_out, v_g_pre_mix, v_g_post_mix, v_g_pre_ffn, v_g_post_ffn, v_w_gate, v_w_up, v_w_down):
    xs, target = x[0], loss_target[0]
    s, d = xs.shape
    n_layers = w_in.shape[0]
    my_group = 2 * lax.axis_index("x") + lax.axis_index("y")

    big = [w_in, w_out, w_gate, w_up, w_down]
    place = jnp.stack([my_group, lax.axis_index("c")]).astype(jnp.int32)
    gathered = [[cast_into_gathered(w, place, l) for w in big] for l in range(n_layers)]
    gathered[0] = gather_weights(gathered[0])
    cw_pad = jnp.pad(conv_w, ((0, 0), (0, 32 - CONV_WIDTH), (0, 0)))
    cw_all = allgather_small("allgather_conv_w", cw_pad.reshape(n_layers * 32, LANES))
    cw_full = cw_all.reshape(N_GROUPS, 2, n_layers, 32, LANES)[:, 0].transpose(1, 2, 0, 3).reshape(n_layers, 32, C_A)
    pw_bf16 = pool_w.astype(BF16)
    tabs = rope_tables(s)
    gmix_a, gmix_b, gmix_c = g_mix[:, :C_A], g_mix[:, C_A:C_A + C_B], g_mix[:, C_A + C_B:]
    row = lambda t: t.reshape(1, -1)

    saved = []
    (h,) = norm_fwd("norm_first", xs, None, None, g_pre_mix[0])
    xin = xs
    for l in range(n_layers):
        win_g, wout_g, wgate_g, wup_g, wdown_g = gathered[l]
        proj = mm_act_wcols("proj", h, win_g, 0, F32)
        ya, conv_out = conv_fwd("conv_fwd", proj, cw_full[l], row(conv_b[l]), row(conv_ln_g[l]), row(conv_ln_b[l]),
                                row(gmix_a[l]))
        qkv_d = rope_fwd("rope_fwd", proj, tabs)
        os_, lses = [], []
        for dil, (qd, kd, vd) in zip(DILATIONS, qkv_d):
            o, lse = band_attn_fwd(f"band_attn_fwd_d{dil}", qd, kd, vd, s // dil)
            os_.append(o)
            lses.append(lse)
        yb, ybn, lse_joint = attn_combine_fwd("attn_combine_fwd", os_, lses, row(gmix_b[l]))
        yc = pool_fwd("pool_fwd", proj, pw_bf16[l], row(pool_scale[l]), row(gmix_c[l]))
        y = jnp.concatenate([ya, ybn, yc], axis=1)
        z = mm_act_wrows("mix_out", y, wout_g, 0, F32)
        x1, h2 = norm_fwd("norm_mid", xin, z, g_post_mix[l], g_pre_ffn[l])
        if l + 1 < n_layers:
            (gt, up, act), gathered[l + 1] = ffn_up("ffn_up_gather", h2, wgate_g, wup_g, 0, gathered[l + 1])
        else:
            (gt, up, act), _ = ffn_up("ffn_up", h2, wgate_g, wup_g, 0)
        f = mm_act_wrows("ffn_down", act, wdown_g, 0, F32)
        if l + 1 < n_layers:
            x2, h_next = norm_fwd("norm_next", x1, f, g_post_ffn[l], g_pre_mix[l + 1])
        else:
            (x2,), h_next = norm_fwd("norm_last", x1, f, g_post_ffn[l], None), None
        saved.append(dict(xin=xin, h=h, proj=proj, conv=conv_out, qkv_d=qkv_d, yb=yb, lse=lse_joint, y=y, z=z, x1=x1, h2=h2,
                          gt=gt, up=up, act=act, f=f))
        xin, h = x2, h_next

    dx, sq = loss_head(xin, target)
    loss = lax.psum(0.5 * jnp.sum(sq) / d, ("x", "y", "c"))

    small = {k: [None] * n_layers for k in ("cw", "cb", "lg", "lb", "pw", "ps", "gmix", "gpre_mix", "gpost_mix",
                                            "gpre_ffn", "gpost_ffn")}
    reduced = None
    df, dg = norm_bwd("norm_bwd_top", dx, None, None, None, saved[-1]["f"], g_post_ffn[n_layers - 1])
    small["gpost_ffn"][n_layers - 1] = dg
    pending = None
    for l in reversed(range(n_layers)):
        sv = saved[l]
        win_g, wout_g, wgate_g, wup_g, wdown_g = gathered[l]
        dgt, dup = ffn_down_bwd("ffn_down_bwd", df, wdown_g, 0, sv["gt"], sv["up"])
        d_wdown = mm_plain("wgrad_rows_ffn", "tn", sv["act"], df, BF16)
        if pending is None:
            dh2, _ = ffn_in_bwd("ffn_in_bwd", dgt, dup, wgate_g, wup_g, 0)
        else:
            dh2, recv = ffn_in_bwd("ffn_in_bwd_exchange", dgt, dup, wgate_g, wup_g, 0, pending[1])
            reduced = reduce_over_chips(pending[1], recv, place, pending[0], n_layers, reduced)
        d_wgate = mm_wgrad_cols("wgrad_cols_ffn", sv["h2"], dgt, wgate_g.shape[3])
        d_wup = mm_wgrad_cols("wgrad_cols_ffn", sv["h2"], dup, wup_g.shape[3])
        dx1, small["gpre_ffn"][l], dz, small["gpost_mix"][l] = norm_bwd(
            "norm_bwd_mid", dx, dh2, sv["x1"], g_pre_ffn[l], sv["z"], g_post_mix[l])
        dy = mm_act_wrows_t("mix_out_bwd", dz, wout_g, 0, F32)
        d_wout = mm_plain("wgrad_rows_mix", "tn", sv["y"], dz, BF16)
        proj = sv["proj"]
        dag, dcw, small["cb"][l], small["lg"][l], small["lb"][l], dgx_a = conv_bwd(
            "conv_bwd", proj, sv["conv"], dy, cw_full[l], row(conv_ln_g[l]), row(conv_ln_b[l]), row(gmix_a[l]))
        small["cw"][l] = dcw.reshape(32, 8, C_A).sum(axis=1)
        dybs, deltas, dgx_b = attn_combine_bwd("attn_combine_bwd", dy, sv["yb"], row(gmix_b[l]))
        dqs, dks, dvs = [], [], []
        for dil, (qd, kd, vd), do_d, lse_d, delta_d in zip(DILATIONS, sv["qkv_d"], dybs, sv["lse"], deltas):
            dqs.append(band_attn_dq(f"band_attn_dq_d{dil}", qd, kd, vd, do_d, lse_d, delta_d, s // dil))
            dk, dv = band_attn_dkv(f"band_attn_dkv_d{dil}", qd, kd, vd, do_d, lse_d, delta_d, s // dil)
            dks.append(dk)
            dvs.append(dv)
        dqkv = rope_bwd("rope_bwd", dqs, dks, dvs, tabs)
        du, small["pw"][l], small["ps"][l], dgx_c = pool_bwd("pool_bwd", proj, dy, pw_bf16[l], row(pool_scale[l]),
                                                              row(gmix_c[l]))
        small["gmix"][l] = jnp.concatenate([dgx_a, dgx_b, dgx_c], axis=1)
        dproj = jnp.concatenate([dag, dqkv, du], axis=1)
        dh1 = mm_act_wcols_t("proj_bwd", dproj, win_g, 0, F32)
        d_win = mm_wgrad_cols("wgrad_cols_proj", sv["h"], dproj, win_g.shape[3])
        if l > 0:
            dx, small["gpre_mix"][l], df, small["gpost_ffn"][l - 1] = norm_bwd(
                "norm_bwd_next", dx1, dh1, sv["xin"], g_pre_mix[l], saved[l - 1]["f"], g_post_ffn[l - 1])
        else:
            dx, small["gpre_mix"][l] = norm_bwd("norm_bwd_first", dx1, dh1, sv["xin"], g_pre_mix[l], None, None)
        d_wout = d_wout.reshape(N_GROUPS, -1, d_wout.shape[1])
        d_wdown = d_wdown.reshape(N_GROUPS, -1, d_wdown.shape[1])
        pending = (l, reduce_in_chip([d_win, d_wout, d_wgate, d_wup, d_wdown], place))
    recv = exchange_chip_partials("exchange_chip_partials", pending[1])
    reduced = reduce_over_chips(pending[1], recv, place, pending[0], n_layers, reduced)

    big_grads = share_reduced_halves(reduced)
    big_m = [m_w_in, m_w_out, m_w_gate, m_w_up, m_w_down]
    big_v = [v_w_in, v_w_out, v_w_gate, v_w_up, v_w_down]
    big_upd = [elementwise("adamw", _adamw, [g, w, m, v], [F32, F32, F32])
               for g, w, m, v in zip(big_grads, big, big_m, big_v)]

    def stack8(k):
        return jnp.stack([t.sum(axis=0) if t.shape[0] == 8 and t.ndim == 2 else t for t in small[k]])

    order = ("cw", "cb", "lg", "lb", "pw", "ps", "gmix", "gpre_mix", "gpost_mix", "gpre_ffn", "gpost_ffn")
    partial = [jnp.stack(small["cw"])] + [stack8(k) for k in order[1:]]
    rep_w = [conv_b, conv_ln_g, conv_ln_b, pool_w, pool_scale, g_mix, g_pre_mix, g_post_mix, g_pre_ffn, g_post_ffn]
    rep_m = [m_conv_b, m_conv_ln_g, m_conv_ln_b, m_pool_w, m_pool_scale, m_g_mix, m_g_pre_mix, m_g_post_mix,
             m_g_pre_ffn, m_g_post_ffn]
    rep_v = [v_conv_b, v_conv_ln_g, v_conv_ln_b, v_pool_w, v_pool_scale, v_g_mix, v_g_pre_mix, v_g_post_mix,
             v_g_pre_ffn, v_g_post_ffn]
    packed = _pack(partial)
    rows = packed.shape[0]
    gall = allgather_small("allgather_small_grads", packed).reshape(8, rows, LANES)
    cw_hole = jnp.zeros((n_layers, 32, C_A), F32)
    rep_out = small_update("small_update", gall, _pack([cw_hole] + rep_w), _pack([cw_hole] + rep_m),
                           _pack([cw_hole] + rep_v))
    shapes = [cw_hole.shape] + [t.shape for t in rep_w]
    rep_g, rep_d, rep_mn, rep_vn = ([t for t in _unpack(buf, shapes)[1:]] for buf in rep_out)
    cw_rows = n_layers * 32
    gall_cw = gall[:, :cw_rows * C_A // LANES].reshape(8, cw_rows, C_A // LANES, LANES)
    gall_cw = lax.dynamic_index_in_dim(gall_cw, my_group, axis=2, keepdims=False)
    pad_cw = lambda t: jnp.pad(t, ((0, 0), (0, 32 - CONV_WIDTH), (0, 0))).reshape(cw_rows, LANES)
    cw_out = small_update("conv_w_update", gall_cw, pad_cw(conv_w), pad_cw(m_conv_w), pad_cw(v_conv_w))
    cw_g, cw_d, cw_mn, cw_vn = (t.reshape(n_layers, 32, LANES)[:, :CONV_WIDTH] for t in cw_out)

    def assemble(bigs, cw, reps):
        return [bigs[0], cw] + list(reps[:6]) + [bigs[1]] + list(reps[6:]) + list(bigs[2:])

    grads = assemble(big_grads, cw_g, rep_g)
    deltas = assemble([u[0] for u in big_upd], cw_d, rep_d)
    new_m = assemble([u[1] for u in big_upd], cw_mn, rep_mn)
    new_v = assemble([u[2] for u in big_upd], cw_vn, rep_vn)
    return (loss, dx[None], *grads, *deltas, *new_m, *new_v)
```

```python
import functools
import math

import jax
import jax.numpy as jnp
from jax import lax
from jax.experimental import pallas as pl
from jax.experimental.pallas import tpu as pltpu

F32 = jnp.float32
BF16 = jnp.bfloat16
MESH = pl.DeviceIdType.MESH

EPS = 1e-6
NEG = -1e30
C_A = 512
C_B = 1024
C_C = 512
HEAD_DIM = 64
POOL_WINDOWS = (2, 4, 8, 16)
C_G = 128
CONV_WIDTH = 31
CONV_HALF = 15
DILATIONS = (1, 4, 16)
ATT_HALF = 64
ROT_DIM = 16
ROPE_THETA = 500000.0
ADAM_LR, ADAM_B1, ADAM_B2, ADAM_EPS, ADAM_WD, ADAM_STEP = 0.001, 0.9, 0.999, 1e-08, 0.01, 10
N_GROUPS = 4
HALO = 32
LANES = 128
VMEM_LIMIT = 48 * 1024 * 1024


def _params(*sem):
    return pltpu.CompilerParams(dimension_semantics=sem, vmem_limit_bytes=VMEM_LIMIT)


def _pick(n, prefs):
    for p in prefs:
        if p <= n and n % p == 0:
            return p
    return n


def _rms_scale(t):
    return lax.rsqrt(jnp.mean(t * t, axis=-1, keepdims=True) + EPS)


def _rms_bwd(d, x, g):
    r = _rms_scale(x)
    u = d * g
    dx = r * u - x * (r * r * r) * jnp.mean(u * x, axis=-1, keepdims=True)
    return dx, d * x * r


def _sum8(t):
    rows, cols = t.shape
    return jnp.sum(t.reshape(rows // 8, 8, cols), axis=0)


def _dot(a, b, mode):
    dn = {"nn": (((1,), (0,)), ((), ())), "nt": (((1,), (1,)), ((), ())), "tn": (((0,), (0,)), ((), ()))}[mode]
    return lax.dot_general(a, b, dn, preferred_element_type=F32)


def _row_spec(tm, cols, col_block=0):
    return pl.BlockSpec((tm, cols), lambda i, cb=col_block: (i, cb))


def _const_spec(shape):
    return pl.BlockSpec(shape, lambda i: tuple(0 for _ in shape))


def norm_fwd(name, x, z, g_post, g_next):
    s, d = x.shape
    tm = _pick(s, (512, 256, 128))
    has_z, has_h = z is not None, g_next is not None

    def body(*refs):
        refs = list(refs)
        x_ref = refs.pop(0)
        xn = x_ref[...]
        if has_z:
            z_ref, gp_ref = refs.pop(0), refs.pop(0)
            zz = z_ref[...]
            xn = xn + zz * _rms_scale(zz) * gp_ref[...]
        if has_h:
            gn_ref = refs.pop(0)
        if has_z:
            refs.pop(0)[...] = xn
        if has_h:
            refs.pop(0)[...] = (xn * _rms_scale(xn) * gn_ref[...]).astype(BF16)

    ins, specs, outs, ospecs = [x], [_row_spec(tm, d)], [], []
    if has_z:
        ins += [z, g_post.reshape(1, d)]
        specs += [_row_spec(tm, d), _const_spec((1, d))]
        outs.append(jax.ShapeDtypeStruct((s, d), F32))
        ospecs.append(_row_spec(tm, d))
    if has_h:
        ins.append(g_next.reshape(1, d))
        specs.append(_const_spec((1, d)))
        outs.append(jax.ShapeDtypeStruct((s, d), BF16))
        ospecs.append(_row_spec(tm, d))
    res = pl.pallas_call(body, name=name, grid=(s // tm,), in_specs=specs, out_specs=ospecs, out_shape=outs,
                         compiler_params=_params("parallel"))(*ins)
    return list(res)


def norm_bwd(name, dres, dh, xin, g_pre, zin, g_post):
    s, d = dres.shape
    tm = _pick(s, (256, 128))
    has_pre, has_post = dh is not None, zin is not None

    def body(*refs):
        refs = list(refs)
        i = pl.program_id(0)
        dx = refs.pop(0)[...]
        if has_pre:
            dh_ref, x_ref, g_ref = refs.pop(0), refs.pop(0), refs.pop(0)
            ddx, dg = _rms_bwd(dh_ref[...].astype(F32), x_ref[...], g_ref[...])
            dx = dx + ddx
        if has_post:
            z_ref, gp_ref = refs.pop(0), refs.pop(0)
            dz, dgp = _rms_bwd(dx, z_ref[...], gp_ref[...])
        if has_pre:
            refs.pop(0)[...] = dx
            dg_ref = refs.pop(0)

            @pl.when(i == 0)
            def _():
                dg_ref[...] = jnp.zeros_like(dg_ref)

            dg_ref[...] += _sum8(dg)
        if has_post:
            refs.pop(0)[...] = dz.astype(BF16)
            dgp_ref = refs.pop(0)

            @pl.when(i == 0)
            def _():
                dgp_ref[...] = jnp.zeros_like(dgp_ref)

            dgp_ref[...] += _sum8(dgp)

    ins, specs, outs, ospecs = [dres], [_row_spec(tm, d)], [], []
    if has_pre:
        ins += [dh, xin, g_pre.reshape(1, d)]
        specs += [_row_spec(tm, d), _row_spec(tm, d), _const_spec((1, d))]
        outs += [jax.ShapeDtypeStruct((s, d), F32), jax.ShapeDtypeStruct((8, d), F32)]
        ospecs += [_row_spec(tm, d), _const_spec((8, d))]
    if has_post:
        ins += [zin, g_post.reshape(1, d)]
        specs += [_row_spec(tm, d), _const_spec((1, d))]
        outs += [jax.ShapeDtypeStruct((s, d), BF16), jax.ShapeDtypeStruct((8, d), F32)]
        ospecs += [_row_spec(tm, d), _const_spec((8, d))]
    res = pl.pallas_call(body, name=name, grid=(s // tm,), in_specs=specs, out_specs=ospecs, out_shape=outs,
                         compiler_params=_params("arbitrary"))(*ins)
    return list(res)


def loss_head(x, target):
    s, d = x.shape
    tm = _pick(s, (512, 256, 128))

    def body(x_ref, t_ref, dx_ref, sq_ref):
        i = pl.program_id(0)
        e = x_ref[...] - t_ref[...]
        dx_ref[...] = e * (1.0 / d)

        @pl.when(i == 0)
        def _():
            sq_ref[...] = jnp.zeros_like(sq_ref)

        sq_ref[...] += _sum8(e * e)

    return pl.pallas_call(body, name="loss_head", grid=(s // tm,),
                          in_specs=[_row_spec(tm, d), _row_spec(tm, d)],
                          out_specs=[_row_spec(tm, d), _const_spec((8, d))],
                          out_shape=[jax.ShapeDtypeStruct((s, d), F32), jax.ShapeDtypeStruct((8, d), F32)],
                          compiler_params=_params("arbitrary"))(x, target)


def matmul(name, mode, a, b, *, grid, tk_steps, a_spec, b_spec, o_spec, out_shape, acc_shape, b_flat=None):
    nk = tk_steps

    def rhs(b_ref):
        return b_ref[...] if b_flat is None else b_ref[...].reshape(b_flat)

    def body(a_ref, b_ref, o_ref, *scratch):
        if nk == 1:
            o_ref[...] = _dot(a_ref[...], rhs(b_ref), mode).astype(o_ref.dtype)
            return
        acc = scratch[0]
        kk = pl.program_id(2)

        @pl.when(kk == 0)
        def _():
            acc[...] = jnp.zeros_like(acc)

        acc[...] += _dot(a_ref[...], rhs(b_ref), mode)

        @pl.when(kk == nk - 1)
        def _():
            o_ref[...] = acc[...].astype(o_ref.dtype)

    return pl.pallas_call(body, name=name, grid=grid, in_specs=[a_spec, b_spec], out_specs=o_spec, out_shape=out_shape,
                          scratch_shapes=[] if nk == 1 else [pltpu.VMEM(acc_shape, F32)],
                          compiler_params=_params("parallel", "parallel", "arbitrary"))(a, b)


TM_PREFS = (1024, 512, 256, 128)
TW_PREFS = (1408, 1152, 1024, 512, 384, 256, 128)
TK_PREFS = (512, 384, 256, 128)
FULL_K_PREFS = (2048, 1024) + TK_PREFS


def mm_act_wcols(name, a, wg, layer, out_dtype):
    s, k = a.shape
    g, _, _, ng = wg.shape
    tm, tk, tn = _pick(s, TM_PREFS), _pick(k, FULL_K_PREFS), _pick(ng, TW_PREFS)
    per = ng // tn
    return matmul(name, "nn", a, wg, grid=(g * per, s // tm, k // tk), tk_steps=k // tk,
                  a_spec=pl.BlockSpec((tm, tk), lambda j, i, kk: (i, kk)),
                  b_spec=pl.BlockSpec((None, None, tk, tn), lambda j, i, kk: (j // per, layer, kk, j % per)),
                  o_spec=pl.BlockSpec((tm, tn), lambda j, i, kk: (i, j)),
                  out_shape=jax.ShapeDtypeStruct((s, g * ng), out_dtype), acc_shape=(tm, tn))


def mm_act_wcols_t(name, a, wg, layer, out_dtype):
    s, n = a.shape
    g, _, k, ng = wg.shape
    tm, tn, tk = _pick(s, TM_PREFS), _pick(k, TM_PREFS), _pick(ng, (1152, 384, 128))
    per = ng // tk
    return matmul(name, "nt", a, wg, grid=(s // tm, k // tn, g * per), tk_steps=g * per,
                  a_spec=pl.BlockSpec((tm, tk), lambda i, j, kk: (i, kk)),
                  b_spec=pl.BlockSpec((None, None, tn, tk), lambda i, j, kk: (kk // per, layer, j, kk % per)),
                  o_spec=pl.BlockSpec((tm, tn), lambda i, j, kk: (i, j)),
                  out_shape=jax.ShapeDtypeStruct((s, k), out_dtype), acc_shape=(tm, tn))


def mm_act_wrows(name, a, wr, layer, out_dtype):
    s, k = a.shape
    g, _, kg, n = wr.shape
    tm, tn, tk = _pick(s, TM_PREFS), _pick(n, TM_PREFS), _pick(kg, (1408, 512, 256, 128))
    if k <= FULL_K_PREFS[0]:
        return matmul(name, "nn", a, wr, grid=(n // tn, s // tm, 1), tk_steps=1,
                      a_spec=pl.BlockSpec((tm, k), lambda j, i, kk: (i, 0)),
                      b_spec=pl.BlockSpec((g, None, kg, tn), lambda j, i, kk: (0, layer, 0, j)),
                      o_spec=pl.BlockSpec((tm, tn), lambda j, i, kk: (i, j)),
                      out_shape=jax.ShapeDtypeStruct((s, n), out_dtype), acc_shape=(tm, tn), b_flat=(k, tn))
    per = kg // tk
    return matmul(name, "nn", a, wr, grid=(s // tm, n // tn, g * per), tk_steps=g * per,
                  a_spec=pl.BlockSpec((tm, tk), lambda i, j, kk: (i, kk)),
                  b_spec=pl.BlockSpec((None, None, tk, tn), lambda i, j, kk: (kk // per, layer, kk % per, j)),
                  o_spec=pl.BlockSpec((tm, tn), lambda i, j, kk: (i, j)),
                  out_shape=jax.ShapeDtypeStruct((s, n), out_dtype), acc_shape=(tm, tn))


def mm_act_wrows_t(name, a, wr, layer, out_dtype):
    s, n = a.shape
    g, _, kg, _ = wr.shape
    tm, tn, tk = _pick(s, TM_PREFS), _pick(kg, (1408, 512, 256, 128)), _pick(n, FULL_K_PREFS)
    per = kg // tn
    return matmul(name, "nt", a, wr, grid=(g * per, s // tm, n // tk), tk_steps=n // tk,
                  a_spec=pl.BlockSpec((tm, tk), lambda j, i, kk: (i, kk)),
                  b_spec=pl.BlockSpec((None, None, tn, tk), lambda j, i, kk: (j // per, layer, j % per, kk)),
                  o_spec=pl.BlockSpec((tm, tn), lambda j, i, kk: (i, j)),
                  out_shape=jax.ShapeDtypeStruct((s, g * kg), out_dtype), acc_shape=(tm, tn))


def mm_wgrad_cols(name, a, dy, ng):
    s, k = a.shape
    n = dy.shape[1]
    g = n // ng
    tm = _pick(k, TM_PREFS)
    tk = _pick(s, TM_PREFS)
    tn = _pick(ng, TW_PREFS)
    per = ng // tn
    return matmul(name, "tn", a, dy, grid=(k // tm, g * per, s // tk), tk_steps=s // tk,
                  a_spec=pl.BlockSpec((tk, tm), lambda i, j, kk: (kk, i)),
                  b_spec=pl.BlockSpec((tk, tn), lambda i, j, kk: (kk, j)),
                  o_spec=pl.BlockSpec((None, tm, tn), lambda i, j, kk: (j // per, i, j % per)),
                  out_shape=jax.ShapeDtypeStruct((g, k, ng), BF16), acc_shape=(tm, tn))


def mm_plain(name, mode, a, b, out_dtype):
    if mode == "nn":
        (m, k), n = a.shape, b.shape[1]
    elif mode == "nt":
        (m, k), n = a.shape, b.shape[0]
    else:
        (k, m), n = a.shape, b.shape[1]
    tm = _pick(m, (1408,) + TM_PREFS)
    tn = _pick(n, TM_PREFS)
    tk = _pick(k, TM_PREFS)
    a_spec = (pl.BlockSpec((tk, tm), lambda i, j, kk: (kk, i)) if mode == "tn"
              else pl.BlockSpec((tm, tk), lambda i, j, kk: (i, kk)))
    b_spec = (pl.BlockSpec((tn, tk), lambda i, j, kk: (j, kk)) if mode == "nt"
              else pl.BlockSpec((tk, tn), lambda i, j, kk: (kk, j)))
    return matmul(name, mode, a, b, grid=(m // tm, n // tn, k // tk), tk_steps=k // tk, a_spec=a_spec, b_spec=b_spec,
                  o_spec=pl.BlockSpec((tm, tn), lambda i, j, kk: (i, j)),
                  out_shape=jax.ShapeDtypeStruct((m, n), out_dtype), acc_shape=(tm, tn))


def _silu_parts(gt):
    sg = jax.nn.sigmoid(gt)
    return gt * sg, sg


def ffn_up(name, h, wgate, wup, layer, gather=()):
    s, d = h.shape
    g, _, _, ng = wgate.shape
    assert d <= FULL_K_PREFS[0]
    tm = _pick(s, (256, 128))
    tn = _pick(ng, TW_PREFS)
    per = ng // tn

    n = len(gather)
    grid = (g * per, s // tm)

    def body(h_ref, wg_ref, wu_ref, *rest):
        gt_ref, up_ref, act_ref = rest[n:n + 3]
        if n:
            start, finish = _gather_plan(rest[n + 3:2 * n + 3], *rest[2 * n + 3:])
            pl.when((pl.program_id(0) == 0) & (pl.program_id(1) == 0))(start)
        hh = h_ref[...]
        gt = _dot(hh, wg_ref[...], "nn")
        up = _dot(hh, wu_ref[...], "nn")
        gt_ref[...] = gt.astype(BF16)
        up_ref[...] = up.astype(BF16)
        act_ref[...] = (_silu_parts(gt)[0] * up).astype(BF16)
        if n:
            pl.when((pl.program_id(0) == grid[0] - 1) & (pl.program_id(1) == grid[1] - 1))(finish)

    wspec = pl.BlockSpec((None, None, d, tn), lambda j, i: (j // per, layer, 0, j % per))
    ospec = pl.BlockSpec((tm, tn), lambda j, i: (i, j))
    osh = jax.ShapeDtypeStruct((s, g * ng), BF16)
    res = pl.pallas_call(body, name=name, grid=grid,
                         in_specs=[pl.BlockSpec((tm, d), lambda j, i: (i, 0)), wspec, wspec] + [ANY] * n,
                         out_specs=[ospec, ospec, ospec] + [ANY] * n,
                         out_shape=[osh, osh, osh] + [jax.ShapeDtypeStruct(t.shape, t.dtype) for t in gather],
                         input_output_aliases={3 + a: 3 + a for a in range(n)},
                         scratch_shapes=_gather_sems(n) if n else [],
                         compiler_params=_params(*(["arbitrary"] * 2 if n else ["parallel"] * 2)))(h, wgate, wup, *gather)
    return res[:3], list(res[3:])


def ffn_down_bwd(name, df, wdown, layer, gt, up):
    s, d = df.shape
    g, _, fg, _ = wdown.shape
    f = g * fg
    assert d <= FULL_K_PREFS[0]
    tm = _pick(s, (256, 128))
    tn = _pick(fg, TW_PREFS)
    per = fg // tn

    def body(df_ref, wd_ref, gt_ref, up_ref, dgt_ref, dup_ref):
        da = _dot(df_ref[...], wd_ref[...], "nt")
        gt = gt_ref[...].astype(F32)
        up = up_ref[...].astype(F32)
        silu, sg = _silu_parts(gt)
        dgt_ref[...] = (da * up * (sg * (1.0 + gt * (1.0 - sg)))).astype(BF16)
        dup_ref[...] = (da * silu).astype(BF16)

    ospec = pl.BlockSpec((tm, tn), lambda j, i: (i, j))
    osh = jax.ShapeDtypeStruct((s, f), BF16)
    return pl.pallas_call(body, name=name, grid=(f // tn, s // tm),
                          in_specs=[pl.BlockSpec((tm, d), lambda j, i: (i, 0)),
                                    pl.BlockSpec((None, None, tn, d), lambda j, i: (j // per, layer, j % per, 0)),
                                    ospec, ospec],
                          out_specs=[ospec, ospec], out_shape=[osh, osh],
                          compiler_params=_params("parallel", "parallel"))(df, wdown, gt, up)


def ffn_in_bwd(name, dgt, dup, wgate, wup, layer, exchange=()):
    s, f = dgt.shape
    g, _, d, ng = wgate.shape
    tm = _pick(s, (512, 256, 128))
    tn = _pick(d, TM_PREFS)
    tk = _pick(ng, TW_PREFS)
    per = ng // tk
    nk = g * per

    n = len(exchange)
    grid = (s // tm, d // tn, nk)

    def body(dg_ref, du_ref, wg_ref, wu_ref, *rest):
        o_ref, acc = rest[n], rest[2 * n + 1]
        kk = pl.program_id(2)
        if n:
            start, finish = _chip_exchange_plan(rest[:n], rest[n + 1:2 * n + 1], *rest[2 * n + 2:])
            pl.when((pl.program_id(0) == 0) & (pl.program_id(1) == 0) & (kk == 0))(start)

        @pl.when(kk == 0)
        def _():
            acc[...] = jnp.zeros_like(acc)

        acc[...] += _dot(dg_ref[...], wg_ref[...], "nt") + _dot(du_ref[...], wu_ref[...], "nt")

        @pl.when(kk == nk - 1)
        def _():
            o_ref[...] = acc[...]

        if n:
            pl.when((pl.program_id(0) == grid[0] - 1) & (pl.program_id(1) == grid[1] - 1) & (kk == nk - 1))(finish)

    aspec = pl.BlockSpec((tm, tk), lambda i, j, kk: (i, kk))
    wspec = pl.BlockSpec((None, None, tn, tk), lambda i, j, kk: (kk // per, layer, j, kk % per))
    res = pl.pallas_call(body, name=name, grid=grid, in_specs=[aspec, aspec, wspec, wspec] + [ANY] * n,
                         out_specs=[pl.BlockSpec((tm, tn), lambda i, j, kk: (i, j))] + [ANY] * n,
                         out_shape=[jax.ShapeDtypeStruct((s, d), F32)] + _chip_exchange_shapes(exchange),
                         scratch_shapes=[pltpu.VMEM((tm, tn), F32)] + (_chip_exchange_sems(n) if n else []),
                         compiler_params=_params(*(["arbitrary"] * 3 if n else ["parallel", "parallel", "arbitrary"])))(
                             dgt, dup, wgate, wup, *exchange)
    return res[0], list(res[1:])


def _halo_specs(tm, s, cols, cb, halo=HALO):
    r = tm // halo
    last = s // halo - 1
    return [pl.BlockSpec((halo, cols), lambda i: (jnp.maximum(i * r - 1, 0), cb)),
            pl.BlockSpec((tm, cols), lambda i: (i, cb)),
            pl.BlockSpec((halo, cols), lambda i: (jnp.minimum((i + 1) * r, last), cb))]


def _edge_masked(prev_ref, next_ref, i, nt):
    return jnp.where(i > 0, prev_ref[...], 0.0), jnp.where(i < nt - 1, next_ref[...], 0.0)


def _glu(a, gate):
    return a * jax.nn.sigmoid(gate)


def _conv_post(conv, lg, lb, gm):
    mu = jnp.mean(conv, axis=-1, keepdims=True)
    xc = conv - mu
    rs = lax.rsqrt(jnp.mean(xc * xc, axis=-1, keepdims=True) + EPS)
    xhat = xc * rs
    yl = xhat * lg + lb
    ya, sg = _silu_parts(yl)
    return xhat, rs, yl, sg, ya


def _fill_glu(hg, ap, am, an, gp, gm_, gn, i, nt, tm):
    a0, a2 = _edge_masked(ap, an, i, nt)
    g0, g2 = _edge_masked(gp, gn, i, nt)
    hg[pl.ds(0, HALO), :] = _glu(a0, g0)
    hg[pl.ds(HALO, tm), :] = _glu(am[...], gm_[...])
    hg[pl.ds(HALO + tm, HALO), :] = _glu(a2, g2)


def conv_fwd(name, proj, cw, cb, lg, lb, gmix):
    s = proj.shape[0]
    tm = _pick(s, (256, 128))
    nt = s // tm

    def body(ap, am, an, gp, gm_, gn, cw_ref, cb_ref, lg_ref, lb_ref, gx_ref, o_ref, conv_ref, hg):
        i = pl.program_id(0)
        _fill_glu(hg, ap, am, an, gp, gm_, gn, i, nt, tm)
        acc = jnp.zeros((tm, C_A), F32) + cb_ref[...]
        for t in range(CONV_WIDTH):
            acc = acc + cw_ref[pl.ds(t, 1), :] * hg[pl.ds(HALO - CONV_HALF + t, tm), :]
        conv_ref[...] = acc
        ya = _conv_post(acc, lg_ref[...], lb_ref[...], None)[4]
        o_ref[...] = (ya * _rms_scale(ya) * gx_ref[...]).astype(BF16)

    vec = _const_spec((1, C_A))
    return pl.pallas_call(body, name=name, grid=(nt,),
                          in_specs=_halo_specs(tm, s, C_A, 0) + _halo_specs(tm, s, C_A, 1)
                          + [_const_spec((32, C_A)), vec, vec, vec, vec],
                          out_specs=[_row_spec(tm, C_A), _row_spec(tm, C_A)],
                          out_shape=[jax.ShapeDtypeStruct((s, C_A), BF16), jax.ShapeDtypeStruct((s, C_A), F32)],
                          scratch_shapes=[pltpu.VMEM((tm + 2 * HALO, C_A), F32)],
                          compiler_params=_params("parallel"))(proj, proj, proj, proj, proj, proj, cw, cb, lg, lb, gmix)


def conv_bwd(name, proj, conv_out, dy, cw, lg, lb, gmix):
    s = proj.shape[0]
    tm = _pick(s, (256, 128))
    nt = s // tm
    te = tm + HALO
    off = HALO // 2

    def extended(prev_ref, main_ref, next_ref, i):
        p0, p2 = _edge_masked(prev_ref, next_ref, i, nt)
        return jnp.concatenate([p0[off:], main_ref[...], p2[:off]], axis=0)

    def body(ap, am, an, gp, gm_, gn, cp, cm, cn, dp, dm, dn, cw_ref, lg_ref, lb_ref, gx_ref,
             dag_ref, dcw_ref, dcb_ref, dlg_ref, dlb_ref, dgx_ref, hg, dc):
        i = pl.program_id(0)
        _fill_glu(hg, ap, am, an, gp, gm_, gn, i, nt, tm)
        lg, lb, gx = lg_ref[...], lb_ref[...], gx_ref[...]
        xhat, rs, yl, sg, ya = _conv_post(extended(cp, cm, cn, i), lg, lb, gx)
        dout = extended(dp, dm, dn, i)
        dya, dgx_rows = _rms_bwd(dout, ya, gx)
        dyl = dya * (sg * (1.0 + yl * (1.0 - sg)))
        dxh = dyl * lg
        dconv = rs * (dxh - jnp.mean(dxh, axis=-1, keepdims=True)
                      - xhat * jnp.mean(dxh * xhat, axis=-1, keepdims=True))
        dc[...] = dconv

        @pl.when(i == 0)
        def _():
            for r in (dcw_ref, dcb_ref, dlg_ref, dlb_ref, dgx_ref):
                r[...] = jnp.zeros_like(r)

        dcb_ref[...] += _sum8(dconv[off:off + tm])
        dlg_ref[...] += _sum8((dyl * xhat)[off:off + tm])
        dlb_ref[...] += _sum8(dyl[off:off + tm])
        dgx_ref[...] += _sum8(dgx_rows[off:off + tm])
        dcm = dconv[off:off + tm]
        dhg = jnp.zeros((tm, C_A), F32)
        for t in range(CONV_WIDTH):
            dhg = dhg + cw_ref[pl.ds(t, 1), :] * dc[pl.ds(HALO - 1 - t, tm), :]
            dcw_ref[pl.ds(8 * t, 8), :] += _sum8(dcm * hg[pl.ds(HALO - CONV_HALF + t, tm), :])
        a, gate = am[...], gm_[...]
        sgate = jax.nn.sigmoid(gate)
        dag_ref[:, pl.ds(0, C_A)] = (dhg * sgate).astype(BF16)
        dag_ref[:, pl.ds(C_A, C_A)] = (dhg * a * sgate * (1.0 - sgate)).astype(BF16)

    vec = _const_spec((1, C_A))
    acc8 = _const_spec((8, C_A))
    sh8 = jax.ShapeDtypeStruct((8, C_A), F32)
    halo = functools.partial(_halo_specs, tm, s, C_A)
    return pl.pallas_call(body, name=name, grid=(nt,),
                          in_specs=halo(0) + halo(1) + halo(0) + halo(0) + [_const_spec((32, C_A)), vec, vec, vec],
                          out_specs=[_row_spec(tm, 2 * C_A), _const_spec((32 * 8, C_A)), acc8, acc8, acc8, acc8],
                          out_shape=[jax.ShapeDtypeStruct((s, 2 * C_A), BF16), jax.ShapeDtypeStruct((32 * 8, C_A), F32),
                                     sh8, sh8, sh8, sh8],
                          scratch_shapes=[pltpu.VMEM((tm + 2 * HALO, C_A), F32), pltpu.VMEM((te, C_A), F32)],
                          compiler_params=_params("arbitrary"))(proj, proj, proj, proj, proj, proj,
                                                                conv_out, conv_out, conv_out, dy, dy, dy, cw, lg, lb, gmix)


def _pool_counts(pos, win, s):
    lo = jnp.maximum(pos - win // 2, 0)
    hi = jnp.minimum(pos + win - win // 2, s)
    return jnp.maximum(hi - lo, 1).astype(F32)


def _pooled(uext, base, rows, pos, s):
    outs = []
    for gi, win in enumerate(POOL_WINDOWS):
        lanes = pl.ds(gi * C_G, C_G)
        acc = jnp.zeros((rows, C_G), F32)
        for o in range(-(win // 2), win - win // 2):
            acc = acc + uext[pl.ds(base + o, rows), lanes]
        outs.append(acc / _pool_counts(pos, win, s) - uext[pl.ds(base, rows), lanes])
    return outs


def pool_fwd(name, proj, pw, scale, gmix):
    s = proj.shape[0]
    tm = _pick(s, (256, 128))
    nt = s // tm
    ucol = (proj.shape[1] - C_C) // C_C

    def body(up, um, un, pw_ref, sc_ref, gx_ref, o_ref, uext):
        i = pl.program_id(0)
        u0, u2 = _edge_masked(up, un, i, nt)
        uext[pl.ds(0, HALO), :] = u0
        uext[pl.ds(HALO, tm), :] = um[...]
        uext[pl.ds(HALO + tm, HALO), :] = u2
        pos = i * tm + lax.broadcasted_iota(jnp.int32, (tm, 1), 0)
        pooled = _pooled(uext, HALO, tm, pos, s)
        mixed = jnp.concatenate([_dot(pooled[g].astype(BF16), pw_ref[g], "nn") for g in range(4)], axis=1)
        yc = mixed * sc_ref[...]
        o_ref[...] = (yc * _rms_scale(yc) * gx_ref[...]).astype(BF16)

    vec = _const_spec((1, C_C))
    return pl.pallas_call(body, name=name, grid=(nt,),
                          in_specs=_halo_specs(tm, s, C_C, ucol) + [_const_spec((4, C_G, C_G)), vec, vec],
                          out_specs=_row_spec(tm, C_C), out_shape=jax.ShapeDtypeStruct((s, C_C), BF16),
                          scratch_shapes=[pltpu.VMEM((tm + 2 * HALO, C_C), F32)],
                          compiler_params=_params("parallel"))(proj, proj, proj, pw, scale, gmix)


def pool_bwd(name, proj, dy, pw, scale, gmix):
    s = proj.shape[0]
    tm = _pick(s, (256, 128))
    nt = s // tm
    te = tm + HALO
    off = HALO // 2
    ucol = (proj.shape[1] - C_C) // C_C
    dcol = (dy.shape[1] - C_C) // C_C

    def body(up, um, un, dp, dm, dn, pw_ref, sc_ref, gx_ref, du_ref, dpw_ref, dsc_ref, dgx_ref, uext, dyx, qs, dps):
        i = pl.program_id(0)
        u0, u2 = _edge_masked(up, un, i, nt)
        uext[pl.ds(0, HALO), :] = u0
        uext[pl.ds(HALO, tm), :] = um[...]
        uext[pl.ds(HALO + tm, HALO), :] = u2
        d0, d2 = _edge_masked(dp, dn, i, nt)
        dyx[pl.ds(0, HALO), :] = d0
        dyx[pl.ds(HALO, tm), :] = dm[...]
        dyx[pl.ds(HALO + tm, HALO), :] = d2
        pos = i * tm - off + lax.broadcasted_iota(jnp.int32, (te, 1), 0)
        pooled = _pooled(uext, off, te, pos, s)
        mixed = jnp.concatenate([_dot(pooled[g].astype(BF16), pw_ref[g], "nn") for g in range(4)], axis=1)
        sc = sc_ref[...]
        yc = mixed * sc
        dyc, dgx_rows = _rms_bwd(dyx[pl.ds(off, te), :], yc, gx_ref[...])
        dmixed = dyc * sc

        @pl.when(i == 0)
        def _():
            for r in (dpw_ref, dsc_ref, dgx_ref):
                r[...] = jnp.zeros_like(r)

        dsc_ref[...] += _sum8((dyc * mixed)[off:off + tm])
        dgx_ref[...] += _sum8(dgx_rows[off:off + tm])
        for gi, win in enumerate(POOL_WINDOWS):
            lanes = pl.ds(gi * C_G, C_G)
            dmg = dmixed[:, gi * C_G:(gi + 1) * C_G].astype(BF16)
            dpw_ref[gi] += _dot(pooled[gi][off:off + tm].astype(BF16), dmg[off:off + tm], "tn")
            dpl = _dot(dmg, pw_ref[gi], "nt")
            dps[:, lanes] = dpl
            qs[:, lanes] = dpl / _pool_counts(pos, win, s)
        for gi, win in enumerate(POOL_WINDOWS):
            lanes = pl.ds(gi * C_G, C_G)
            acc = jnp.zeros((tm, C_G), F32) - dps[pl.ds(off, tm), lanes]
            for o in range(-(win // 2) + 1, win // 2 + 1):
                acc = acc + qs[pl.ds(off + o, tm), lanes]
            du_ref[:, lanes] = acc.astype(BF16)

    vec = _const_spec((1, C_C))
    acc8 = _const_spec((8, C_C))
    sh8 = jax.ShapeDtypeStruct((8, C_C), F32)
    return pl.pallas_call(body, name=name, grid=(nt,),
                          in_specs=_halo_specs(tm, s, C_C, ucol) + _halo_specs(tm, s, C_C, dcol)
                          + [_const_spec((4, C_G, C_G)), vec, vec],
                          out_specs=[_row_spec(tm, C_C), _const_spec((4, C_G, C_G)), acc8, acc8],
                          out_shape=[jax.ShapeDtypeStruct((s, C_C), BF16), jax.ShapeDtypeStruct((4, C_G, C_G), F32),
                                     sh8, sh8],
                          scratch_shapes=[pltpu.VMEM((tm + 2 * HALO, C_C), F32), pltpu.VMEM((tm + 2 * HALO, C_C), F32),
                                          pltpu.VMEM((te, C_C), F32), pltpu.VMEM((te, C_C), F32)],
                          compiler_params=_params("arbitrary"))(proj, proj, proj, dy, dy, dy, pw, scale, gmix)


def rope_tables(s):
    pos = jnp.arange(s, dtype=F32)
    inv = ROPE_THETA ** (-jnp.arange(0, ROT_DIM, 2, dtype=F32) / ROT_DIM)
    ang = pos[:, None] * inv[None, :]
    cos, sin = jnp.cos(ang), jnp.sin(ang)
    half = ROT_DIM // 2
    rest = HEAD_DIM - ROT_DIM
    c = jnp.concatenate([cos, cos, jnp.ones((s, rest), F32)], axis=1)
    sa = jnp.concatenate([-sin, jnp.zeros((s, HEAD_DIM - half), F32)], axis=1)
    sb = jnp.concatenate([jnp.zeros((s, half), F32), sin, jnp.zeros((s, rest), F32)], axis=1)
    return tuple(jnp.concatenate([t, t], axis=1) for t in (c, sa, sb))


DIL_TILE = 256
N_CHUNKS = C_B // LANES


def _dilated_shape(s, d, dtype):
    return jax.ShapeDtypeStruct((s, C_B) if d == 1 else (d, s // d, C_B), dtype)


def _dilated_spec(tm, d):
    return _row_spec(tm, C_B) if d == 1 else pl.BlockSpec((d, tm // d, C_B), lambda i: (0, i, 0))


def _as_dilated(t, d):
    return t if d == 1 else t.reshape(d, t.shape[0] // d, t.shape[1])


def _dil_scratch(tm):
    return pltpu.VMEM((N_CHUNKS, tm, LANES), F32)


def _store_dilated(o_ref, scr, d, tm):
    for r in range(d):
        for j in range(N_CHUNKS):
            o_ref[r, :, pl.ds(j * LANES, LANES)] = scr[j, pl.ds(r, tm // d, stride=d), :].astype(o_ref.dtype)


def _load_dilated(in_ref, scr, d, tm):
    for r in range(d):
        for j in range(N_CHUNKS):
            scr[j, pl.ds(r, tm // d, stride=d), :] = in_ref[r, :, pl.ds(j * LANES, LANES)].astype(F32)


def rope_fwd(name, proj, tabs):
    s = proj.shape[0]
    tm = DIL_TILE
    scale = HEAD_DIM ** -0.5

    def body(q_ref, k_ref, v_ref, c_ref, sa_ref, sb_ref, *rest):
        outs, scr = rest[:9], rest[9:]
        c, sa, sb = c_ref[...], sa_ref[...], sb_ref[...]
        for j in range(N_CHUNKS):
            lanes = pl.ds(j * LANES, LANES)
            for which, (src, mul) in enumerate(((q_ref, scale), (k_ref, 1.0))):
                t = src[:, lanes]
                r = t * c + pltpu.roll(t, LANES - ROT_DIM // 2, 1) * sa + pltpu.roll(t, ROT_DIM // 2, 1) * sb
                scr[which][j] = r * mul
            scr[2][j] = v_ref[:, lanes]
            for which in range(3):
                outs[which][:, lanes] = scr[which][j].astype(BF16)
        for pi, d in enumerate(DILATIONS[1:]):
            for which in range(3):
                _store_dilated(outs[3 * (pi + 1) + which], scr[which], d, tm)

    tab = _row_spec(tm, LANES)
    res = pl.pallas_call(body, name=name, grid=(s // tm,),
                         in_specs=[_row_spec(tm, C_B, 1), _row_spec(tm, C_B, 2), _row_spec(tm, C_B, 3), tab, tab, tab],
                         out_specs=[_dilated_spec(tm, d) for d in DILATIONS for _ in range(3)],
                         out_shape=[_dilated_shape(s, d, BF16) for d in DILATIONS for _ in range(3)],
                         scratch_shapes=[_dil_scratch(tm)] * 3,
                         compiler_params=_params("parallel"))(proj, proj, proj, *tabs)
    return [tuple(t.reshape(s, C_B) for t in res[3 * pi:3 * pi + 3]) for pi in range(len(DILATIONS))]


def rope_bwd(name, dqs, dks, dvs, tabs):
    s = dqs[0].shape[0]
    tm = DIL_TILE
    scale = HEAD_DIM ** -0.5
    n_dil = len(DILATIONS)

    def body(*refs):
        ins = [refs[n_dil * which:n_dil * (which + 1)] for which in range(3)]
        c_ref, sa_ref, sb_ref, o_ref = refs[3 * n_dil:3 * n_dil + 4]
        scr = refs[3 * n_dil + 4:]
        for which in range(3):
            for pi, d in enumerate(DILATIONS[1:]):
                _load_dilated(ins[which][pi + 1], scr[which * (n_dil - 1) + pi], d, tm)
        c, sa, sb = c_ref[...], sa_ref[...], sb_ref[...]
        for j in range(N_CHUNKS):
            lanes = pl.ds(j * LANES, LANES)
            for which, mul in ((0, scale), (1, 1.0), (2, None)):
                d = ins[which][0][:, lanes].astype(F32)
                for pi in range(n_dil - 1):
                    d = d + scr[which * (n_dil - 1) + pi][j]
                if mul is not None:
                    d = (d * c + pltpu.roll(d * sa, ROT_DIM // 2, 1) + pltpu.roll(d * sb, LANES - ROT_DIM // 2, 1)) * mul
                o_ref[:, pl.ds(which * C_B + j * LANES, LANES)] = d.astype(BF16)

    tab = _row_spec(tm, LANES)
    args = [_as_dilated(t, d) for ts in (dqs, dks, dvs) for t, d in zip(ts, DILATIONS)]
    return pl.pallas_call(body, name=name, grid=(s // tm,),
                          in_specs=[_dilated_spec(tm, d) for _ in range(3) for d in DILATIONS] + [tab, tab, tab],
                          out_specs=_row_spec(tm, 3 * C_B), out_shape=jax.ShapeDtypeStruct((s, 3 * C_B), BF16),
                          scratch_shapes=[_dil_scratch(tm)] * (3 * (n_dil - 1)),
                          compiler_params=_params("parallel"))(*args, *tabs)


def _window_specs(tq, s, cols):
    r = tq // ATT_HALF
    last = s // ATT_HALF - 1
    return [pl.BlockSpec((ATT_HALF, cols), lambda i: (jnp.maximum(i * r - 1, 0), 0)),
            pl.BlockSpec((tq, cols), lambda i: (i, 0)),
            pl.BlockSpec((ATT_HALF, cols), lambda i: (jnp.minimum((i + 1) * r, last), 0))]


def _fill_window(win, prev_ref, main_ref, next_ref, tq):
    win[pl.ds(0, ATT_HALF), :] = prev_ref[...]
    win[pl.ds(ATT_HALF, tq), :] = main_ref[...]
    win[pl.ds(ATT_HALF + tq, ATT_HALF), :] = next_ref[...]


def _band_valid(tile, window, j0, seg, tile_is_rows):
    shape = (2 * tile, window) if tile_is_rows else (2 * window, tile)
    ri = lax.broadcasted_iota(jnp.int32, shape, 0)
    ci = lax.broadcasted_iota(jnp.int32, shape, 1)
    per_head = tile if tile_is_rows else window
    ri = jnp.where(ri >= per_head, ri - per_head, ri)
    ti, wi = (ri, ci) if tile_is_rows else (ci, ri)
    jw = j0 - ATT_HALF + wi
    return (jnp.abs(wi - ATT_HALF - ti) <= ATT_HALF) & (jw >= 0) & (jw < seg)


def _first_head():
    return lax.broadcasted_iota(jnp.int32, (1, LANES), 1) < HEAD_DIM


def _stack_heads(t, first):
    z = jnp.zeros_like(t)
    return jnp.concatenate([jnp.where(first, t, z), jnp.where(first, z, t)], axis=0)


def _unstack_heads(t2, first, rows):
    return jnp.where(first, t2[:rows], t2[rows:])


def _head_columns(ref, pair, rows):
    return jnp.concatenate([ref[rows, pl.ds(pair * LANES, 1)], ref[rows, pl.ds(pair * LANES + HEAD_DIM, 1)]], axis=0)


def _sub_tiles(tile):
    sub = min(tile, 2 * ATT_HALF)
    return sub, tile // sub, sub + 2 * ATT_HALF


def band_attn_fwd(name, q, k, v, seg):
    s, c = q.shape
    tq = _pick(seg, (256, 128))
    per_seg = seg // tq

    wrows = tq + 2 * ATT_HALF

    sub, n_sub, wsub = _sub_tiles(tq)

    def body(q_ref, kp, km, kn, vp, vm, vn, o_ref, lse_ref, kw, vw):
        j0 = (pl.program_id(0) % per_seg) * tq
        _fill_window(kw, kp, km, kn, tq)
        _fill_window(vw, vp, vm, vn, tq)
        first = _first_head()
        for h in range(n_sub):
            rows, wnd = pl.ds(h * sub, sub), pl.ds(h * sub, wsub)
            valid = _band_valid(sub, wsub, j0 + h * sub, seg, True)
            for pair in range(c // LANES):
                lanes = pl.ds(pair * LANES, LANES)
                sc = jnp.where(valid, _dot(_stack_heads(q_ref[rows, lanes], first), kw[wnd, lanes], "nt"), NEG)
                m = jnp.max(sc, axis=-1, keepdims=True)
                p = jnp.exp(sc - m)
                l = jnp.sum(p, axis=-1, keepdims=True)
                o = _dot(p.astype(BF16), vw[wnd, lanes], "nn") / l
                o_ref[rows, lanes] = _unstack_heads(o, first, sub).astype(BF16)
                lse_ref[rows, lanes] = _unstack_heads(m + jnp.log(l), first, sub)

    win = _window_specs(tq, s, c)
    tile = _row_spec(tq, c)
    return pl.pallas_call(body, name=name, grid=(s // tq,), in_specs=[tile] + win + win, out_specs=[tile, tile],
                          out_shape=[jax.ShapeDtypeStruct((s, c), BF16), jax.ShapeDtypeStruct((s, c), F32)],
                          scratch_shapes=[pltpu.VMEM((wrows, c), BF16)] * 2,
                          compiler_params=_params("parallel"))(q, k, k, k, v, v, v)


def band_attn_dq(name, q, k, v, do, lse, delta, seg):
    s, c = q.shape
    tq = _pick(seg, (256, 128))
    per_seg = seg // tq

    wrows = tq + 2 * ATT_HALF

    sub, n_sub, wsub = _sub_tiles(tq)

    def body(q_ref, kp, km, kn, vp, vm, vn, do_ref, lse_ref, dl_ref, dq_ref, kw, vw):
        j0 = (pl.program_id(0) % per_seg) * tq
        _fill_window(kw, kp, km, kn, tq)
        _fill_window(vw, vp, vm, vn, tq)
        first = _first_head()
        for h in range(n_sub):
            rows, wnd = pl.ds(h * sub, sub), pl.ds(h * sub, wsub)
            valid = _band_valid(sub, wsub, j0 + h * sub, seg, True)
            for pair in range(c // LANES):
                lanes = pl.ds(pair * LANES, LANES)
                sc = _dot(_stack_heads(q_ref[rows, lanes], first), kw[wnd, lanes], "nt")
                p = jnp.where(valid, jnp.exp(sc - _head_columns(lse_ref, pair, rows)), 0.0)
                dp = _dot(_stack_heads(do_ref[rows, lanes], first), vw[wnd, lanes], "nt")
                ds = p * (dp - _head_columns(dl_ref, pair, rows))
                dq = _dot(ds.astype(BF16), kw[wnd, lanes], "nn")
                dq_ref[rows, lanes] = _unstack_heads(dq, first, sub).astype(BF16)

    win = _window_specs(tq, s, c)
    tile = _row_spec(tq, c)
    return pl.pallas_call(body, name=name, grid=(s // tq,), in_specs=[tile] + win + win + [tile, tile, tile],
                          out_specs=tile, out_shape=jax.ShapeDtypeStruct((s, c), BF16),
                          scratch_shapes=[pltpu.VMEM((wrows, c), BF16)] * 2,
                          compiler_params=_params("parallel"))(q, k, k, k, v, v, v, do, lse, delta)


def band_attn_dkv(name, q, k, v, do, lse, delta, seg):
    s, c = q.shape
    tk = _pick(seg, (256, 128))
    per_seg = seg // tk

    wrows = tk + 2 * ATT_HALF
    sub, n_sub, wsub = _sub_tiles(tk)

    def body(k_ref, v_ref, qp, qm, qn, dop, dom, don, lp, lm, ln, dlp, dlm, dln, dk_ref, dv_ref, qw, dow, lsew, dlw):
        j0 = (pl.program_id(0) % per_seg) * tk
        _fill_window(qw, qp, qm, qn, tk)
        _fill_window(dow, dop, dom, don, tk)
        _fill_window(lsew, lp, lm, ln, tk)
        _fill_window(dlw, dlp, dlm, dln, tk)
        first = _first_head()
        top = lax.broadcasted_iota(jnp.int32, (2 * sub, 1), 0) < sub

        def head_rows(win, wnd, lanes):
            t = win[wnd, lanes].T
            return jnp.where(top, t[0:1, :], t[HEAD_DIM:HEAD_DIM + 1, :])

        for h in range(n_sub):
            rows, wnd = pl.ds(h * sub, sub), pl.ds(h * sub, wsub)
            valid = _band_valid(sub, wsub, j0 + h * sub, seg, True)
            for pair in range(c // LANES):
                lanes = pl.ds(pair * LANES, LANES)
                q2, do2 = qw[wnd, lanes], dow[wnd, lanes]
                sc = _dot(_stack_heads(k_ref[rows, lanes], first), q2, "nt")
                p = jnp.where(valid, jnp.exp(sc - head_rows(lsew, wnd, lanes)), 0.0)
                dv = _dot(p.astype(BF16), do2, "nn")
                dp = _dot(_stack_heads(v_ref[rows, lanes], first), do2, "nt")
                ds = p * (dp - head_rows(dlw, wnd, lanes))
                dk = _dot(ds.astype(BF16), q2, "nn")
                dv_ref[rows, lanes] = _unstack_heads(dv, first, sub).astype(BF16)
                dk_ref[rows, lanes] = _unstack_heads(dk, first, sub).astype(BF16)

    win = _window_specs(tk, s, c)
    tile = _row_spec(tk, c)
    osh = jax.ShapeDtypeStruct((s, c), BF16)
    return pl.pallas_call(body, name=name, grid=(s // tk,), in_specs=[tile, tile] + win * 4,
                          out_specs=[tile, tile], out_shape=[osh, osh],
                          scratch_shapes=[pltpu.VMEM((wrows, c), BF16)] * 2 + [pltpu.VMEM((wrows, c), F32)] * 2,
                          compiler_params=_params("parallel"))(k, v, q, q, q, do, do, do, lse, lse, lse,
                                                               delta, delta, delta)


def attn_combine_fwd(name, os_, lses, gmix):
    s, c = os_[0].shape
    tm = DIL_TILE
    n_dil = len(DILATIONS)

    def body(*refs):
        o_refs, l_refs, gx_ref = refs[:n_dil], refs[n_dil:2 * n_dil], refs[2 * n_dil]
        yb_ref, y_ref = refs[2 * n_dil + 1:2 * n_dil + 3]
        lse_refs = refs[2 * n_dil + 3:3 * n_dil + 3]
        scr = refs[3 * n_dil + 3:]
        scr_o, scr_l, scr_lse = scr[:n_dil - 1], scr[n_dil - 1:2 * (n_dil - 1)], scr[-1]
        for pi, d in enumerate(DILATIONS[1:]):
            _load_dilated(o_refs[pi + 1], scr_o[pi], d, tm)
            _load_dilated(l_refs[pi + 1], scr_l[pi], d, tm)
        sumsq = jnp.zeros((tm, 1), F32)
        for j in range(N_CHUNKS):
            lanes = pl.ds(j * LANES, LANES)
            ls = [l_refs[0][:, lanes]] + [t[j] for t in scr_l]
            vals = [o_refs[0][:, lanes].astype(F32)] + [t[j] for t in scr_o]
            m = functools.reduce(jnp.maximum, ls)
            es = [jnp.exp(l - m) for l in ls]
            den = functools.reduce(lambda a, b: a + b, es)
            yb = functools.reduce(lambda a, b: a + b, [e * v for e, v in zip(es, vals)]) / den
            yb_ref[:, lanes] = yb
            lse = m + jnp.log(den)
            lse_refs[0][:, lanes] = lse
            scr_lse[j] = lse
            sumsq = sumsq + jnp.sum(yb * yb, axis=-1, keepdims=True)
        r = lax.rsqrt(sumsq / c + EPS)
        y_ref[...] = (yb_ref[...] * r * gx_ref[...]).astype(BF16)
        for pi, d in enumerate(DILATIONS[1:]):
            _store_dilated(lse_refs[pi + 1], scr_lse, d, tm)

    row = _row_spec(tm, c)
    dil = [_dilated_spec(tm, d) for d in DILATIONS]
    res = pl.pallas_call(body, name=name, grid=(s // tm,), in_specs=dil + dil + [_const_spec((1, c))],
                         out_specs=[row, row] + dil,
                         out_shape=[jax.ShapeDtypeStruct((s, c), F32), jax.ShapeDtypeStruct((s, c), BF16)]
                         + [_dilated_shape(s, d, F32) for d in DILATIONS],
                         scratch_shapes=[_dil_scratch(tm)] * (2 * (n_dil - 1) + 1),
                         compiler_params=_params("parallel"))(
                             *[_as_dilated(t, d) for t, d in zip(os_, DILATIONS)],
                             *[_as_dilated(t, d) for t, d in zip(lses, DILATIONS)], gmix)
    return res[0], res[1], [t.reshape(s, c) for t in res[2:]]


def attn_combine_bwd(name, dy, yb, gmix):
    s, c = yb.shape
    tm = DIL_TILE
    half = c // 2
    n_dil = len(DILATIONS)

    def body(d1, d2, yb_ref, gx_ref, *rest):
        dyb_refs, dl_refs, dgx_ref = rest[:n_dil], rest[n_dil:2 * n_dil], rest[2 * n_dil]
        scr_dyb, scr_dl = rest[2 * n_dil + 1:]
        i = pl.program_id(0)
        d = jnp.concatenate([d1[...], d2[...]], axis=1)
        yb_ = yb_ref[...]
        dyb, dgx_rows = _rms_bwd(d, yb_, gx_ref[...])
        dyb_refs[0][...] = dyb.astype(BF16)

        @pl.when(i == 0)
        def _():
            dgx_ref[...] = jnp.zeros_like(dgx_ref)

        dgx_ref[...] += _sum8(dgx_rows)
        ri = lax.broadcasted_iota(jnp.int32, (LANES, LANES), 0) // HEAD_DIM
        ci = lax.broadcasted_iota(jnp.int32, (LANES, LANES), 1) // HEAD_DIM
        same_head = jnp.where(ri == ci, 1.0, 0.0).astype(BF16)
        prod = dyb * yb_
        for j in range(N_CHUNKS):
            pj = prod[:, j * LANES:(j + 1) * LANES]
            hi = pj.astype(BF16)
            lo = (pj - hi.astype(F32)).astype(BF16)
            delta = _dot(hi, same_head, "nn") + _dot(lo, same_head, "nn")
            dl_refs[0][:, pl.ds(j * LANES, LANES)] = delta
            scr_dl[j] = delta
            scr_dyb[j] = dyb[:, j * LANES:(j + 1) * LANES]
        for pi, dil in enumerate(DILATIONS[1:]):
            _store_dilated(dyb_refs[pi + 1], scr_dyb, dil, tm)
            _store_dilated(dl_refs[pi + 1], scr_dl, dil, tm)

    row = _row_spec(tm, c)
    dil_specs = [_dilated_spec(tm, d) for d in DILATIONS]
    res = pl.pallas_call(body, name=name, grid=(s // tm,),
                         in_specs=[_row_spec(tm, half, 1), _row_spec(tm, half, 2), row, _const_spec((1, c))],
                         out_specs=dil_specs + dil_specs + [_const_spec((8, c))],
                         out_shape=[_dilated_shape(s, d, BF16) for d in DILATIONS]
                         + [_dilated_shape(s, d, F32) for d in DILATIONS] + [jax.ShapeDtypeStruct((8, c), F32)],
                         scratch_shapes=[_dil_scratch(tm)] * 2,
                         compiler_params=_params("arbitrary"))(dy, dy, yb, gmix)
    return ([t.reshape(s, c) for t in res[:n_dil]], [t.reshape(s, c) for t in res[n_dil:2 * n_dil]], res[2 * n_dil])


def _ew_rows(rows, cols, n_bufs):
    budget = 24 * 1024 * 1024 // (8 * n_bufs * cols)
    return _pick(rows, tuple(t for t in (2048, 1024, 512, 256, 128, 64, 32, 16, 8) if t <= max(budget, 8)))


def elementwise(name, fn, ins, out_dtypes):
    shape = ins[0].shape
    cols = shape[-1]
    rows = math.prod(shape[:-1])
    tr = _ew_rows(rows, cols, len(ins) + len(out_dtypes))
    n_in = len(ins)

    def body(*refs):
        outs = fn(*[r[...] for r in refs[:n_in]])
        for o_ref, o in zip(refs[n_in:], outs):
            o_ref[...] = o.astype(o_ref.dtype)

    spec = _row_spec(tr, cols)
    res = pl.pallas_call(body, name=name, grid=(rows // tr,), in_specs=[spec] * n_in, out_specs=[spec] * len(out_dtypes),
                         out_shape=[jax.ShapeDtypeStruct((rows, cols), dt) for dt in out_dtypes],
                         compiler_params=_params("parallel"))(*[t.reshape(rows, cols) for t in ins])
    return [r.reshape(shape) for r in res]


def _adamw(g, w, m, v):
    m = ADAM_B1 * m + (1.0 - ADAM_B1) * g
    v = ADAM_B2 * v + (1.0 - ADAM_B2) * (g * g)
    m_hat = m / (1.0 - ADAM_B1 ** ADAM_STEP)
    v_hat = v / (1.0 - ADAM_B2 ** ADAM_STEP)
    delta = -ADAM_LR * (m_hat / (jnp.sqrt(v_hat) + ADAM_EPS) + ADAM_WD * w)
    return delta, m, v


def cast_into_gathered(w, place, layer):
    _, r, c = w.shape
    tr = _ew_rows(r, c, 2)

    def body(p_ref, w_ref, o_ref):
        o_ref[...] = w_ref[...].astype(BF16)

    grid_spec = pltpu.PrefetchScalarGridSpec(
        num_scalar_prefetch=1, grid=(r // tr,),
        in_specs=[pl.BlockSpec((None, tr, c), lambda i, p: (layer, i, 0))],
        out_specs=pl.BlockSpec((None, None, tr, c), lambda i, p: (p[0], 0, i, 0)))
    return pl.pallas_call(body, name="cast_into_gathered", grid_spec=grid_spec,
                          out_shape=jax.ShapeDtypeStruct((N_GROUPS, 1, r, c), BF16),
                          compiler_params=_params("parallel"))(place, w)


def sum_core_halves(name, grad, theirs, place):
    g, r, c = grad.shape
    h = r // 2
    tr = _ew_rows(h, c, 3)
    nh = h // tr

    def body(p_ref, a_ref, b_ref, o_ref):
        o_ref[...] = (a_ref[...].astype(F32) + b_ref[...].astype(F32)).astype(BF16)

    blk = pl.BlockSpec((None, tr, c), lambda gi, i, p: (gi, i, 0))
    grid_spec = pltpu.PrefetchScalarGridSpec(
        num_scalar_prefetch=1, grid=(g, nh),
        in_specs=[pl.BlockSpec((None, tr, c), lambda gi, i, p: (gi, p[1] * nh + i, 0)), blk], out_specs=blk)
    return pl.pallas_call(body, name=name, grid_spec=grid_spec, out_shape=jax.ShapeDtypeStruct((g, h, c), BF16),
                          compiler_params=_params("parallel", "parallel"))(place, grad, theirs)


def sum_chip_partials(name, parts, recv, place, layer, n_layers, into):
    _, h, c = parts.shape
    tr = _ew_rows(h, c, 5)
    nh = h // tr

    def body(p_ref, o_ref, r0, r1, r2, *rest):
        rest[-1][...] = ((o_ref[...].astype(F32) + r0[...].astype(F32)) + r1[...].astype(F32)) + r2[...].astype(F32)

    in_specs = [pl.BlockSpec((None, tr, c), lambda i, p: (p[0], i, 0))]
    in_specs += [pl.BlockSpec((None, tr, c), lambda i, p, j=j: (j, i, 0)) for j in range(3)]
    args = [place, parts, recv, recv, recv]
    aliases = {}
    if into is not None:
        in_specs.append(ANY)
        args.append(into)
        aliases = {5: 0}
    grid_spec = pltpu.PrefetchScalarGridSpec(
        num_scalar_prefetch=1, grid=(nh,), in_specs=in_specs,
        out_specs=pl.BlockSpec((None, tr, c), lambda i, p: (layer, p[1] * nh + i, 0)))
    return pl.pallas_call(body, name=name, grid_spec=grid_spec, input_output_aliases=aliases,
                          out_shape=jax.ShapeDtypeStruct((n_layers, 2 * h, c), F32),
                          compiler_params=_params("parallel"))(*args)


def small_update(name, gall, w, m, v):
    _, rows, cols = gall.shape
    tr = _ew_rows(rows, cols, 16)

    def body(g_ref, w_ref, m_ref, v_ref, go, do, mo, vo):
        g = g_ref[0]
        for dev in range(1, 8):
            g = g + g_ref[dev]
        delta, mn, vn = _adamw(g, w_ref[...], m_ref[...], v_ref[...])
        go[...] = g
        do[...] = delta
        mo[...] = mn
        vo[...] = vn

    spec = _row_spec(tr, cols)
    sh = jax.ShapeDtypeStruct((rows, cols), F32)
    return pl.pallas_call(body, name=name, grid=(rows // tr,),
                          in_specs=[pl.BlockSpec((8, tr, cols), lambda i: (0, i, 0)), spec, spec, spec],
                          out_specs=[spec] * 4, out_shape=[sh] * 4, compiler_params=_params("parallel"))(gall, w, m, v)


def _place():
    x, y, c = lax.axis_index("x"), lax.axis_index("y"), lax.axis_index("c")
    chips = [(1 - x, y), (x, 1 - y), (1 - x, 1 - y)]
    return x, y, c, chips


def _remote(src, dst, send_sems, recv_sems, k, to):
    return pltpu.make_async_remote_copy(src_ref=src, dst_ref=dst, send_sem=send_sems.at[k], recv_sem=recv_sems.at[k],
                                        device_id=to, device_id_type=MESH)


ANY = pl.BlockSpec(memory_space=pl.ANY)


def allgather_small(name, block):
    m_per, n = block.shape

    def body(x_ref, out_ref, send_sems, recv_sems, local_sem):
        x, y, c, chips = _place()
        me, sibling = (x, y, c), (x, y, 1 - c)

        def rows(px, py, pc):
            return out_ref.at[pl.ds((4 * px + 2 * py + pc) * m_per, m_per), :]

        def copy(k, blk, to, src=None):
            return _remote(rows(*blk) if src is None else src, rows(*blk), send_sems, recv_sems, k, to)

        mine = pltpu.make_async_copy(x_ref, rows(*me), local_sem)
        mine.start()
        first = [copy(0, me, sibling, src=x_ref)]
        first += [copy(1 + j, me, (*chip, c), src=x_ref) for j, chip in enumerate(chips)]
        for cp in first:
            cp.start()
        passed = [copy(4 + j, (*chip, c), sibling) for j, chip in enumerate(chips)]
        for j, chip in enumerate(chips):
            copy(1 + j, (*chip, c), me).wait_recv()
            passed[j].start()
        copy(0, sibling, me).wait_recv()
        for j, chip in enumerate(chips):
            copy(4 + j, (*chip, 1 - c), me).wait_recv()
        for cp in first + passed:
            cp.wait_send()
        mine.wait()

    return pl.pallas_call(body, name=name, out_shape=jax.ShapeDtypeStruct((8 * m_per, n), block.dtype),
                          in_specs=[pl.BlockSpec(memory_space=pltpu.VMEM)], out_specs=pl.BlockSpec(memory_space=pltpu.VMEM),
                          scratch_shapes=[pltpu.SemaphoreType.DMA((7,)), pltpu.SemaphoreType.DMA((7,)),
                                          pltpu.SemaphoreType.DMA],
                          compiler_params=pltpu.CompilerParams(vmem_limit_bytes=VMEM_LIMIT))(block)


def gather_weights(bufs):
    n = len(bufs)

    def body(*refs):
        start, finish = _gather_plan(refs[n:2 * n], *refs[2 * n:])
        start()
        finish()

    return pl.pallas_call(body, name="gather_weights",
                          out_shape=[jax.ShapeDtypeStruct(t.shape, t.dtype) for t in bufs],
                          in_specs=[ANY] * n, out_specs=[ANY] * n, input_output_aliases={a: a for a in range(n)},
                          scratch_shapes=_gather_sems(n))(*bufs)


def _gather_sems(n):
    return [pltpu.SemaphoreType.DMA((6 * n,)), pltpu.SemaphoreType.DMA((6 * n,))]


def _gather_plan(outs, send_sems, recv_sems):
    n = len(outs)
    x, y, c, chips = _place()
    g0 = 2 * x + y
    sibling = (x, y, 1 - c)
    groups = [2 * cx + cy for cx, cy in chips]

    def part(a, g, cc):
        h = outs[a].shape[2] // 2
        return outs[a].at[g, :, pl.ds(cc * h, h), :]

    def ici(a, j):
        return _remote(part(a, g0, c), part(a, g0, c), send_sems, recv_sems, 3 * a + j, (*chips[j], c))

    def ici_arrival(a, j):
        return _remote(part(a, groups[j], c), part(a, groups[j], c), send_sems, recv_sems, 3 * a + j, (*chips[j], c))

    def passed(a, j, cc):
        return _remote(part(a, groups[j], cc), part(a, groups[j], cc), send_sems, recv_sems, 3 * n + 3 * a + j, sibling)

    def start():
        for a in range(n):
            for j in range(3):
                ici(a, j).start()

    def finish():
        for a in range(n):
            for j in range(3):
                ici_arrival(a, j).wait_recv()
                passed(a, j, c).start()
        for a in range(n):
            for j in range(3):
                passed(a, j, 1 - c).wait_recv()
        for a in range(n):
            for j in range(3):
                ici(a, j).wait_send()
                passed(a, j, c).wait_send()

    return start, finish


def exchange_core_halves(name, grads):
    n = len(grads)
    halves = [t.shape[1] // 2 for t in grads]

    def body(*refs):
        ins, theirs = refs[:n], refs[n:2 * n]
        send_sems, recv_sems = refs[2 * n:]
        x, y, c, _ = _place()
        sibling = (x, y, 1 - c)

        def give(a, g):
            return _remote(ins[a].at[g, pl.ds((1 - c) * halves[a], halves[a]), :], theirs[a].at[g], send_sems,
                           recv_sems, N_GROUPS * a + g, sibling)

        for a in range(n):
            for g in range(N_GROUPS):
                give(a, g).start()
        for a in range(n):
            for g in range(N_GROUPS):
                give(a, g).wait_recv()
        for a in range(n):
            for g in range(N_GROUPS):
                give(a, g).wait_send()

    shapes = [jax.ShapeDtypeStruct((t.shape[0], h, t.shape[2]), t.dtype) for t, h in zip(grads, halves)]
    k = N_GROUPS * n
    return pl.pallas_call(body, name=name, out_shape=shapes, in_specs=[ANY] * n, out_specs=[ANY] * n,
                          scratch_shapes=[pltpu.SemaphoreType.DMA((k,)), pltpu.SemaphoreType.DMA((k,))])(*grads)


def exchange_chip_partials(name, parts):
    n = len(parts)

    def body(*refs):
        start, finish = _chip_exchange_plan(refs[:n], refs[n:2 * n], *refs[2 * n:])
        start()
        finish()

    return pl.pallas_call(body, name=name, out_shape=_chip_exchange_shapes(parts), in_specs=[ANY] * n,
                          out_specs=[ANY] * n, scratch_shapes=_chip_exchange_sems(n))(*parts)


def _chip_exchange_shapes(parts):
    return [jax.ShapeDtypeStruct((3,) + t.shape[1:], t.dtype) for t in parts]


def _chip_exchange_sems(n):
    return [pltpu.SemaphoreType.DMA((3 * n,)), pltpu.SemaphoreType.DMA((3 * n,))]


def _chip_exchange_plan(ins, recv, send_sems, recv_sems):
    n = len(ins)
    x, y, c, chips = _place()

    def give(a, j):
        return _remote(ins[a].at[2 * chips[j][0] + chips[j][1]], recv[a].at[j], send_sems, recv_sems, 3 * a + j,
                       (*chips[j], c))

    def start():
        for a in range(n):
            for j in range(3):
                give(a, j).start()

    def finish():
        for a in range(n):
            for j in range(3):
                give(a, j).wait_recv()
        for a in range(n):
            for j in range(3):
                give(a, j).wait_send()

    return start, finish


def share_reduced_halves(bufs):
    n = len(bufs)
    n_layers = bufs[0].shape[0]
    halves = [t.shape[1] // 2 for t in bufs]

    def body(*refs):
        outs = refs[n:2 * n]
        send_sems, recv_sems = refs[2 * n:]
        x, y, c, _ = _place()
        sibling = (x, y, 1 - c)

        def give(l, a, cc):
            part = outs[a].at[l, pl.ds(cc * halves[a], halves[a]), :]
            return _remote(part, part, send_sems, recv_sems, l * n + a, sibling)

        for l in range(n_layers):
            for a in range(n):
                give(l, a, c).start()
        for l in range(n_layers):
            for a in range(n):
                give(l, a, 1 - c).wait_recv()
        for l in range(n_layers):
            for a in range(n):
                give(l, a, c).wait_send()

    k = n_layers * n
    return pl.pallas_call(body, name="share_reduced_halves", out_shape=[jax.ShapeDtypeStruct(t.shape, t.dtype) for t in bufs],
                          in_specs=[ANY] * n, out_specs=[ANY] * n, input_output_aliases={a: a for a in range(n)},
                          scratch_shapes=[pltpu.SemaphoreType.DMA((k,)), pltpu.SemaphoreType.DMA((k,))])(*bufs)


def reduce_in_chip(grads, place):
    theirs = exchange_core_halves("exchange_core_halves", grads)
    return [sum_core_halves("sum_core_halves", g, t, place) for g, t in zip(grads, theirs)]


def reduce_over_chips(parts, recv, place, layer, n_layers, into):
    return [sum_chip_partials("sum_chip_partials", p, r, place, layer, n_layers, None if into is None else into[a])
            for a, (p, r) in enumerate(zip(parts, recv))]


def _pack(parts):
    flat = jnp.concatenate([p.reshape(-1) for p in parts])
    pad = (-flat.shape[0]) % (8 * LANES)
    if pad:
        flat = jnp.concatenate([flat, jnp.zeros((pad,), flat.dtype)])
    return flat.reshape(-1, LANES)


def _unpack(buf, shapes):
    flat = buf.reshape(-1)
    out, at = [], 0
    for sh in shapes:
        size = math.prod(sh)
        out.append(flat[at:at + size].reshape(sh))
        at += size
    return out


def kernel(x, w_in, conv_w, conv_b, conv_ln_g, conv_ln_b, pool_w, pool_scale, g_mix, w_out, g_pre_mix, g_post_mix, g_pre_ffn, g_post_ffn, w_gate, w_up, w_down, loss_target, m_w_in, m_conv_w, m_conv_b, m_conv_ln_g, m_conv_ln_b, m_pool_w, m_pool_scale, m_g_mix, m_w_out, m_g_pre_mix, m_g_post_mix, m_g_pre_ffn, m_g_post_ffn, m_w_gate, m_w_up, m_w_down, v_w_in, v_conv_w, v_conv_b, v_conv_ln_g, v_conv_ln_b, v_pool_w, v_pool_scale, v_g_mix, v_w_out, v_g_pre_mix, v_g_post_mix, v_g_pre_ffn, v_g_post_ffn, v_w_gate, v_w_up, v_w_down):
    xs, target = x[0], loss_target[0]
    s, d = xs.shape
    n_layers = w_in.shape[0]
    my_group = 2 * lax.axis_index("x") + lax.axis_index("y")

    big = [w_in, w_out, w_gate, w_up, w_down]
    place = jnp.stack([my_group, lax.axis_index("c")]).astype(jnp.int32)
    gathered = [[cast_into_gathered(w, place, l) for w in big] for l in range(n_layers)]
    gathered[0] = gather_weights(gathered[0])
    cw_pad = jnp.pad(conv_w, ((0, 0), (0, 32 - CONV_WIDTH), (0, 0)))
    cw_all = allgather_small("allgather_conv_w", cw_pad.reshape(n_layers * 32, LANES))
    cw_full = cw_all.reshape(N_GROUPS, 2, n_layers, 32, LANES)[:, 0].transpose(1, 2, 0, 3).reshape(n_layers, 32, C_A)
    pw_bf16 = pool_w.astype(BF16)
    tabs = rope_tables(s)
    gmix_a, gmix_b, gmix_c = g_mix[:, :C_A], g_mix[:, C_A:C_A + C_B], g_mix[:, C_A + C_B:]
    row = lambda t: t.reshape(1, -1)

    saved = []
    (h,) = norm_fwd("norm_first", xs, None, None, g_pre_mix[0])
    xin = xs
    for l in range(n_layers):
        win_g, wout_g, wgate_g, wup_g, wdown_g = gathered[l]
        proj = mm_act_wcols("proj", h, win_g, 0, F32)
        ya, conv_out = conv_fwd("conv_fwd", proj, cw_full[l], row(conv_b[l]), row(conv_ln_g[l]), row(conv_ln_b[l]),
                                row(gmix_a[l]))
        qkv_d = rope_fwd("rope_fwd", proj, tabs)
        os_, lses = [], []
        for dil, (qd, kd, vd) in zip(DILATIONS, qkv_d):
            o, lse = band_attn_fwd(f"band_attn_fwd_d{dil}", qd, kd, vd, s // dil)
            os_.append(o)
            lses.append(lse)
        yb, ybn, lse_joint = attn_combine_fwd("attn_combine_fwd", os_, lses, row(gmix_b[l]))
        yc = pool_fwd("pool_fwd", proj, pw_bf16[l], row(pool_scale[l]), row(gmix_c[l]))
        y = jnp.concatenate([ya, ybn, yc], axis=1)
        z = mm_act_wrows("mix_out", y, wout_g, 0, F32)
        x1, h2 = norm_fwd("norm_mid", xin, z, g_post_mix[l], g_pre_ffn[l])
        if l + 1 < n_layers:
            (gt, up, act), gathered[l + 1] = ffn_up("ffn_up_gather", h2, wgate_g, wup_g, 0, gathered[l + 1])
        else:
            (gt, up, act), _ = ffn_up("ffn_up", h2, wgate_g, wup_g, 0)
        f = mm_act_wrows("ffn_down", act, wdown_g, 0, F32)
        if l + 1 < n_layers:
            x2, h_next = norm_fwd("norm_next", x1, f, g_post_ffn[l], g_pre_mix[l + 1])
        else:
            (x2,), h_next = norm_fwd("norm_last", x1, f, g_post_ffn[l], None), None
        saved.append(dict(xin=xin, h=h, proj=proj, conv=conv_out, qkv_d=qkv_d, yb=yb, lse=lse_joint, y=y, z=z, x1=x1, h2=h2,
                          gt=gt, up=up, act=act, f=f))
        xin, h = x2, h_next

    dx, sq = loss_head(xin, target)
    loss = lax.psum(0.5 * jnp.sum(sq) / d, ("x", "y", "c"))

    small = {k: [None] * n_layers for k in ("cw", "cb", "lg", "lb", "pw", "ps", "gmix", "gpre_mix", "gpost_mix",
                                            "gpre_ffn", "gpost_ffn")}
    reduced = None
    df, dg = norm_bwd("norm_bwd_top", dx, None, None, None, saved[-1]["f"], g_post_ffn[n_layers - 1])
    small["gpost_ffn"][n_layers - 1] = dg
    pending = None
    for l in reversed(range(n_layers)):
        sv = saved[l]
        win_g, wout_g, wgate_g, wup_g, wdown_g = gathered[l]
        dgt, dup = ffn_down_bwd("ffn_down_bwd", df, wdown_g, 0, sv["gt"], sv["up"])
        d_wdown = mm_plain("wgrad_rows_ffn", "tn", sv["act"], df, BF16)
        if pending is None:
            dh2, _ = ffn_in_bwd("ffn_in_bwd", dgt, dup, wgate_g, wup_g, 0)
        else:
            dh2, recv = ffn_in_bwd("ffn_in_bwd_exchange", dgt, dup, wgate_g, wup_g, 0, pending[1])
            reduced = reduce_over_chips(pending[1], recv, place, pending[0], n_layers, reduced)
        d_wgate = mm_wgrad_cols("wgrad_cols_ffn", sv["h2"], dgt, wgate_g.shape[3])
        d_wup = mm_wgrad_cols("wgrad_cols_ffn", sv["h2"], dup, wup_g.shape[3])
        dx1, small["gpre_ffn"][l], dz, small["gpost_mix"][l] = norm_bwd(
            "norm_bwd_mid", dx, dh2, sv["x1"], g_pre_ffn[l], sv["z"], g_post_mix[l])
        dy = mm_act_wrows_t("mix_out_bwd", dz, wout_g, 0, F32)
        d_wout = mm_plain("wgrad_rows_mix", "tn", sv["y"], dz, BF16)
        proj = sv["proj"]
        dag, dcw, small["cb"][l], small["lg"][l], small["lb"][l], dgx_a = conv_bwd(
            "conv_bwd", proj, sv["conv"], dy, cw_full[l], row(conv_ln_g[l]), row(conv_ln_b[l]), row(gmix_a[l]))
        small["cw"][l] = dcw.reshape(32, 8, C_A).sum(axis=1)
        dybs, deltas, dgx_b = attn_combine_bwd("attn_combine_bwd", dy, sv["yb"], row(gmix_b[l]))
        dqs, dks, dvs = [], [], []
        for dil, (qd, kd, vd), do_d, lse_d, delta_d in zip(DILATIONS, sv["qkv_d"], dybs, sv["lse"], deltas):
            dqs.append(band_attn_dq(f"band_attn_dq_d{dil}", qd, kd, vd, do_d, lse_d, delta_d, s // dil))
            dk, dv = band_attn_dkv(f"band_attn_dkv_d{dil}", qd, kd, vd, do_d, lse_d, delta_d, s // dil)
            dks.append(dk)
            dvs.append(dv)
        dqkv = rope_bwd("rope_bwd", dqs, dks, dvs, tabs)
        du, small["pw"][l], small["ps"][l], dgx_c = pool_bwd("pool_bwd", proj, dy, pw_bf16[l], row(pool_scale[l]),
                                                              row(gmix_c[l]))
        small["gmix"][l] = jnp.concatenate([dgx_a, dgx_b, dgx_c], axis=1)
        dproj = jnp.concatenate([dag, dqkv, du], axis=1)
        dh1 = mm_act_wcols_t("proj_bwd", dproj, win_g, 0, F32)
        d_win = mm_wgrad_cols("wgrad_cols_proj", sv["h"], dproj, win_g.shape[3])
        if l > 0:
            dx, small["gpre_mix"][l], df, small["gpost_ffn"][l - 1] = norm_bwd(
                "norm_bwd_next", dx1, dh1, sv["xin"], g_pre_mix[l], saved[l - 1]["f"], g_post_ffn[l - 1])
        else:
            dx, small["gpre_mix"][l] = norm_bwd("norm_bwd_first", dx1, dh1, sv["xin"], g_pre_mix[l], None, None)
        d_wout = d_wout.reshape(N_GROUPS, -1, d_wout.shape[1])
        d_wdown = d_wdown.reshape(N_GROUPS, -1, d_wdown.shape[1])
        pending = (l, reduce_in_chip([d_win, d_wout, d_wgate, d_wup, d_wdown], place))
    recv = exchange_chip_partials("exchange_chip_partials", pending[1])
    reduced = reduce_over_chips(pending[1], recv, place, pending[0], n_layers, reduced)

    big_grads = share_reduced_halves(reduced)
    big_m = [m_w_in, m_w_out, m_w_gate, m_w_up, m_w_down]
    big_v = [v_w_in, v_w_out, v_w_gate, v_w_up, v_w_down]
    big_upd = [elementwise("adamw", _adamw, [g, w, m, v], [F32, F32, F32])
               for g, w, m, v in zip(big_grads, big, big_m, big_v)]

    def stack8(k):
        return jnp.stack([t.sum(axis=0) if t.shape[0] == 8 and t.ndim == 2 else t for t in small[k]])

    order = ("cw", "cb", "lg", "lb", "pw", "ps", "gmix", "gpre_mix", "gpost_mix", "gpre_ffn", "gpost_ffn")
    partial = [jnp.stack(small["cw"])] + [stack8(k) for k in order[1:]]
    rep_w = [conv_b, conv_ln_g, conv_ln_b, pool_w, pool_scale, g_mix, g_pre_mix, g_post_mix, g_pre_ffn, g_post_ffn]
    rep_m = [m_conv_b, m_conv_ln_g, m_conv_ln_b, m_pool_w, m_pool_scale, m_g_mix, m_g_pre_mix, m_g_post_mix,
             m_g_pre_ffn, m_g_post_ffn]
    rep_v = [v_conv_b, v_conv_ln_g, v_conv_ln_b, v_pool_w, v_pool_scale, v_g_mix, v_g_pre_mix, v_g_post_mix,
             v_g_pre_ffn, v_g_post_ffn]
    packed = _pack(partial)
    rows = packed.shape[0]
    gall = allgather_small("allgather_small_grads", packed).reshape(8, rows, LANES)
    cw_hole = jnp.zeros((n_layers, 32, C_A), F32)
    rep_out = small_update("small_update", gall, _pack([cw_hole] + rep_w), _pack([cw_hole] + rep_m),
                           _pack([cw_hole] + rep_v))
    shapes = [cw_hole.shape] + [t.shape for t in rep_w]
    rep_g, rep_d, rep_mn, rep_vn = ([t for t in _unpack(buf, shapes)[1:]] for buf in rep_out)
    cw_rows = n_layers * 32
    gall_cw = gall[:, :cw_rows * C_A // LANES].reshape(8, cw_rows, C_A // LANES, LANES)
    gall_cw = lax.dynamic_index_in_dim(gall_cw, my_group, axis=2, keepdims=False)
    pad_cw = lambda t: jnp.pad(t, ((0, 0), (0, 32 - CONV_WIDTH), (0, 0))).reshape(cw_rows, LANES)
    cw_out = small_update("conv_w_update", gall_cw, pad_cw(conv_w), pad_cw(m_conv_w), pad_cw(v_conv_w))
    cw_g, cw_d, cw_mn, cw_vn = (t.reshape(n_layers, 32, LANES)[:, :CONV_WIDTH] for t in cw_out)

    def assemble(bigs, cw, reps):
        return [bigs[0], cw] + list(reps[:6]) + [bigs[1]] + list(reps[6:]) + list(bigs[2:])

    grads = assemble(big_grads, cw_g, rep_g)
    deltas = assemble([u[0] for u in big_upd], cw_d, rep_d)
    new_m = assemble([u[1] for u in big_upd], cw_mn, rep_mn)
    new_v = assemble([u[2] for u in big_upd], cw_vn, rep_vn)
    return (loss, dx[None], *grads, *deltas, *new_m, *new_v)
```

```python
import functools
import math

import jax
import jax.numpy as jnp
from jax import lax
from jax.experimental import pallas as pl
from jax.experimental.pallas import tpu as pltpu

F32 = jnp.float32
BF16 = jnp.bfloat16
MESH = pl.DeviceIdType.MESH

EPS = 1e-6
NEG = -1e30
C_A = 512
C_B = 1024
C_C = 512
HEAD_DIM = 64
POOL_WINDOWS = (2, 4, 8, 16)
C_G = 128
CONV_WIDTH = 31
CONV_HALF = 15
DILATIONS = (1, 4, 16)
ATT_HALF = 64
ATT_TILES = (512, 256, 128)
ROT_DIM = 16
ROPE_THETA = 500000.0
ADAM_LR, ADAM_B1, ADAM_B2, ADAM_EPS, ADAM_WD, ADAM_STEP = 0.001, 0.9, 0.999, 1e-08, 0.01, 10
N_GROUPS = 4
HALO = 32
LANES = 128
VMEM_LIMIT = 48 * 1024 * 1024


def _params(*sem):
    return pltpu.CompilerParams(dimension_semantics=sem, vmem_limit_bytes=VMEM_LIMIT)


def _pick(n, prefs):
    for p in prefs:
        if p <= n and n % p == 0:
            return p
    return n


def _rms_scale(t):
    return lax.rsqrt(jnp.mean(t * t, axis=-1, keepdims=True) + EPS)


def _rms_bwd(d, x, g):
    r = _rms_scale(x)
    u = d * g
    dx = r * u - x * (r * r * r) * jnp.mean(u * x, axis=-1, keepdims=True)
    return dx, d * x * r


def _sum8(t):
    rows, cols = t.shape
    return jnp.sum(t.reshape(rows // 8, 8, cols), axis=0)


def _dot(a, b, mode):
    dn = {"nn": (((1,), (0,)), ((), ())), "nt": (((1,), (1,)), ((), ())), "tn": (((0,), (0,)), ((), ()))}[mode]
    return lax.dot_general(a, b, dn, preferred_element_type=F32)


def _row_spec(tm, cols, col_block=0):
    return pl.BlockSpec((tm, cols), lambda i, cb=col_block: (i, cb))


def _const_spec(shape):
    return pl.BlockSpec(shape, lambda i: tuple(0 for _ in shape))


def norm_fwd(name, x, z, g_post, g_next):
    s, d = x.shape
    tm = _pick(s, (512, 256, 128))
    has_z, has_h = z is not None, g_next is not None

    def body(*refs):
        refs = list(refs)
        x_ref = refs.pop(0)
        xn = x_ref[...]
        if has_z:
            z_ref, gp_ref = refs.pop(0), refs.pop(0)
            zz = z_ref[...]
            xn = xn + zz * _rms_scale(zz) * gp_ref[...]
        if has_h:
            gn_ref = refs.pop(0)
        if has_z:
            refs.pop(0)[...] = xn
        if has_h:
            refs.pop(0)[...] = (xn * _rms_scale(xn) * gn_ref[...]).astype(BF16)

    ins, specs, outs, ospecs = [x], [_row_spec(tm, d)], [], []
    if has_z:
        ins += [z, g_post.reshape(1, d)]
        specs += [_row_spec(tm, d), _const_spec((1, d))]
        outs.append(jax.ShapeDtypeStruct((s, d), F32))
        ospecs.append(_row_spec(tm, d))
    if has_h:
        ins.append(g_next.reshape(1, d))
        specs.append(_const_spec((1, d)))
        outs.append(jax.ShapeDtypeStruct((s, d), BF16))
        ospecs.append(_row_spec(tm, d))
    res = pl.pallas_call(body, name=name, grid=(s // tm,), in_specs=specs, out_specs=ospecs, out_shape=outs,
                         compiler_params=_params("parallel"))(*ins)
    return list(res)


def norm_bwd(name, dres, dh, xin, g_pre, zin, g_post):
    s, d = dres.shape
    tm = _pick(s, (256, 128))
    has_pre, has_post = dh is not None, zin is not None

    def body(*refs):
        refs = list(refs)
        i = pl.program_id(0)
        dx = refs.pop(0)[...]
        if has_pre:
            dh_ref, x_ref, g_ref = refs.pop(0), refs.pop(0), refs.pop(0)
            ddx, dg = _rms_bwd(dh_ref[...].astype(F32), x_ref[...], g_ref[...])
            dx = dx + ddx
        if has_post:
            z_ref, gp_ref = refs.pop(0), refs.pop(0)
            dz, dgp = _rms_bwd(dx, z_ref[...], gp_ref[...])
        if has_pre:
            refs.pop(0)[...] = dx
            dg_ref = refs.pop(0)

            @pl.when(i == 0)
            def _():
                dg_ref[...] = jnp.zeros_like(dg_ref)

            dg_ref[...] += _sum8(dg)
        if has_post:
            refs.pop(0)[...] = dz.astype(BF16)
            dgp_ref = refs.pop(0)

            @pl.when(i == 0)
            def _():
                dgp_ref[...] = jnp.zeros_like(dgp_ref)

            dgp_ref[...] += _sum8(dgp)

    ins, specs, outs, ospecs = [dres], [_row_spec(tm, d)], [], []
    if has_pre:
        ins += [dh, xin, g_pre.reshape(1, d)]
        specs += [_row_spec(tm, d), _row_spec(tm, d), _const_spec((1, d))]
        outs += [jax.ShapeDtypeStruct((s, d), F32), jax.ShapeDtypeStruct((8, d), F32)]
        ospecs += [_row_spec(tm, d), _const_spec((8, d))]
    if has_post:
        ins += [zin, g_post.reshape(1, d)]
        specs += [_row_spec(tm, d), _const_spec((1, d))]
        outs += [jax.ShapeDtypeStruct((s, d), BF16), jax.ShapeDtypeStruct((8, d), F32)]
        ospecs += [_row_spec(tm, d), _const_spec((8, d))]
    res = pl.pallas_call(body, name=name, grid=(s // tm,), in_specs=specs, out_specs=ospecs, out_shape=outs,
                         compiler_params=_params("arbitrary"))(*ins)
    return list(res)


def loss_head(x, target):
    s, d = x.shape
    tm = _pick(s, (512, 256, 128))

    def body(x_ref, t_ref, dx_ref, sq_ref):
        i = pl.program_id(0)
        e = x_ref[...] - t_ref[...]
        dx_ref[...] = e * (1.0 / d)

        @pl.when(i == 0)
        def _():
            sq_ref[...] = jnp.zeros_like(sq_ref)

        sq_ref[...] += _sum8(e * e)

    return pl.pallas_call(body, name="loss_head", grid=(s // tm,),
                          in_specs=[_row_spec(tm, d), _row_spec(tm, d)],
                          out_specs=[_row_spec(tm, d), _const_spec((8, d))],
                          out_shape=[jax.ShapeDtypeStruct((s, d), F32), jax.ShapeDtypeStruct((8, d), F32)],
                          compiler_params=_params("arbitrary"))(x, target)


def matmul(name, mode, a, b, *, grid, tk_steps, a_spec, b_spec, o_spec, out_shape, acc_shape, b_flat=None, gather=()):
    nk = tk_steps
    n = len(gather)

    def rhs(b_ref):
        return b_ref[...] if b_flat is None else b_ref[...].reshape(b_flat)

    def body(a_ref, b_ref, *rest):
        o_ref, scratch = rest[n], rest[2 * n + 1:]
        pid = [pl.program_id(ax) for ax in range(3)]
        if n:
            start, finish = _gather_plan(rest[n + 1:2 * n + 1], *scratch[-2:])
            pl.when((pid[0] == 0) & (pid[1] == 0) & (pid[2] == 0))(start)
        if nk == 1:
            o_ref[...] = _dot(a_ref[...], rhs(b_ref), mode).astype(o_ref.dtype)
        else:
            acc = scratch[0]

            @pl.when(pid[2] == 0)
            def _():
                acc[...] = jnp.zeros_like(acc)

            acc[...] += _dot(a_ref[...], rhs(b_ref), mode)

            @pl.when(pid[2] == nk - 1)
            def _():
                o_ref[...] = acc[...].astype(o_ref.dtype)
        if n:
            pl.when((pid[0] == grid[0] - 1) & (pid[1] == grid[1] - 1) & (pid[2] == grid[2] - 1))(finish)

    acc_scratch = [] if nk == 1 else [pltpu.VMEM(acc_shape, F32)]
    if not n:
        return pl.pallas_call(body, name=name, grid=grid, in_specs=[a_spec, b_spec], out_specs=o_spec,
                              out_shape=out_shape, scratch_shapes=acc_scratch,
                              compiler_params=_params("parallel", "parallel", "arbitrary"))(a, b)
    res = pl.pallas_call(body, name=name, grid=grid, in_specs=[a_spec, b_spec] + [ANY] * n,
                         out_specs=[o_spec] + [ANY] * n,
                         out_shape=[out_shape] + [jax.ShapeDtypeStruct(t.shape, t.dtype) for t in gather],
                         input_output_aliases={2 + i: 1 + i for i in range(n)},
                         scratch_shapes=acc_scratch + _gather_sems(n),
                         compiler_params=_params("arbitrary", "arbitrary", "arbitrary"))(a, b, *gather)
    return res[0], list(res[1:])


TM_PREFS = (1024, 512, 256, 128)
TW_PREFS = (1408, 1152, 1024, 512, 384, 256, 128)
TK_PREFS = (512, 384, 256, 128)
FULL_K_PREFS = (2048, 1024) + TK_PREFS


def mm_act_wcols(name, a, wg, layer, out_dtype, gather=()):
    s, k = a.shape
    g, _, _, ng = wg.shape
    tm, tk, tn = _pick(s, TM_PREFS), _pick(k, FULL_K_PREFS), _pick(ng, TW_PREFS)
    per = ng // tn
    return matmul(name, "nn", a, wg, grid=(g * per, s // tm, k // tk), tk_steps=k // tk,
                  a_spec=pl.BlockSpec((tm, tk), lambda j, i, kk: (i, kk)),
                  b_spec=pl.BlockSpec((None, None, tk, tn), lambda j, i, kk: (j // per, layer, kk, j % per)),
                  o_spec=pl.BlockSpec((tm, tn), lambda j, i, kk: (i, j)),
                  out_shape=jax.ShapeDtypeStruct((s, g * ng), out_dtype), acc_shape=(tm, tn), gather=gather)


def mm_act_wcols_t(name, a, wg, layer, out_dtype):
    s, n = a.shape
    g, _, k, ng = wg.shape
    tm, tn, tk = _pick(s, TM_PREFS), _pick(k, TM_PREFS), _pick(ng, (1152, 384, 128))
    per = ng // tk
    return matmul(name, "nt", a, wg, grid=(s // tm, k // tn, g * per), tk_steps=g * per,
                  a_spec=pl.BlockSpec((tm, tk), lambda i, j, kk: (i, kk)),
                  b_spec=pl.BlockSpec((None, None, tn, tk), lambda i, j, kk: (kk // per, layer, j, kk % per)),
                  o_spec=pl.BlockSpec((tm, tn), lambda i, j, kk: (i, j)),
                  out_shape=jax.ShapeDtypeStruct((s, k), out_dtype), acc_shape=(tm, tn))


def mm_act_wrows(name, a, wr, layer, out_dtype):
    s, k = a.shape
    g, _, kg, n = wr.shape
    tm, tn, tk = _pick(s, TM_PREFS), _pick(n, TM_PREFS), _pick(kg, (1408, 512, 256, 128))
    if k <= FULL_K_PREFS[0]:
        return matmul(name, "nn", a, wr, grid=(n // tn, s // tm, 1), tk_steps=1,
                      a_spec=pl.BlockSpec((tm, k), lambda j, i, kk: (i, 0)),
                      b_spec=pl.BlockSpec((g, None, kg, tn), lambda j, i, kk: (0, layer, 0, j)),
                      o_spec=pl.BlockSpec((tm, tn), lambda j, i, kk: (i, j)),
                      out_shape=jax.ShapeDtypeStruct((s, n), out_dtype), acc_shape=(tm, tn), b_flat=(k, tn))
    per = kg // tk
    return matmul(name, "nn", a, wr, grid=(s // tm, n // tn, g * per), tk_steps=g * per,
                  a_spec=pl.BlockSpec((tm, tk), lambda i, j, kk: (i, kk)),
                  b_spec=pl.BlockSpec((None, None, tk, tn), lambda i, j, kk: (kk // per, layer, kk % per, j)),
                  o_spec=pl.BlockSpec((tm, tn), lambda i, j, kk: (i, j)),
                  out_shape=jax.ShapeDtypeStruct((s, n), out_dtype), acc_shape=(tm, tn))


def mm_act_wrows_t(name, a, wr, layer, out_dtype):
    s, n = a.shape
    g, _, kg, _ = wr.shape
    tm, tn, tk = _pick(s, TM_PREFS), _pick(kg, (1408, 512, 256, 128)), _pick(n, FULL_K_PREFS)
    per = kg // tn
    return matmul(name, "nt", a, wr, grid=(g * per, s // tm, n // tk), tk_steps=n // tk,
                  a_spec=pl.BlockSpec((tm, tk), lambda j, i, kk: (i, kk)),
                  b_spec=pl.BlockSpec((None, None, tn, tk), lambda j, i, kk: (j // per, layer, j % per, kk)),
                  o_spec=pl.BlockSpec((tm, tn), lambda j, i, kk: (i, j)),
                  out_shape=jax.ShapeDtypeStruct((s, g * kg), out_dtype), acc_shape=(tm, tn))


def mm_wgrad_cols(name, a, dy, ng):
    s, k = a.shape
    n = dy.shape[1]
    g = n // ng
    tm = _pick(k, TM_PREFS)
    tk = _pick(s, TM_PREFS)
    tn = _pick(ng, TW_PREFS)
    per = ng // tn
    return matmul(name, "tn", a, dy, grid=(k // tm, g * per, s // tk), tk_steps=s // tk,
                  a_spec=pl.BlockSpec((tk, tm), lambda i, j, kk: (kk, i)),
                  b_spec=pl.BlockSpec((tk, tn), lambda i, j, kk: (kk, j)),
                  o_spec=pl.BlockSpec((None, tm, tn), lambda i, j, kk: (j // per, i, j % per)),
                  out_shape=jax.ShapeDtypeStruct((g, k, ng), BF16), acc_shape=(tm, tn))


def mm_plain(name, mode, a, b, out_dtype):
    if mode == "nn":
        (m, k), n = a.shape, b.shape[1]
    elif mode == "nt":
        (m, k), n = a.shape, b.shape[0]
    else:
        (k, m), n = a.shape, b.shape[1]
    tm = _pick(m, (1408,) + TM_PREFS)
    tn = _pick(n, TM_PREFS)
    tk = _pick(k, TM_PREFS)
    a_spec = (pl.BlockSpec((tk, tm), lambda i, j, kk: (kk, i)) if mode == "tn"
              else pl.BlockSpec((tm, tk), lambda i, j, kk: (i, kk)))
    b_spec = (pl.BlockSpec((tn, tk), lambda i, j, kk: (j, kk)) if mode == "nt"
              else pl.BlockSpec((tk, tn), lambda i, j, kk: (kk, j)))
    return matmul(name, mode, a, b, grid=(m // tm, n // tn, k // tk), tk_steps=k // tk, a_spec=a_spec, b_spec=b_spec,
                  o_spec=pl.BlockSpec((tm, tn), lambda i, j, kk: (i, j)),
                  out_shape=jax.ShapeDtypeStruct((m, n), out_dtype), acc_shape=(tm, tn))


def _silu_parts(gt):
    sg = jax.nn.sigmoid(gt)
    return gt * sg, sg


def ffn_up(name, h, wgate, wup, layer, gather=()):
    s, d = h.shape
    g, _, _, ng = wgate.shape
    assert d <= FULL_K_PREFS[0]
    tm = _pick(s, (256, 128))
    tn = _pick(ng, TW_PREFS)
    per = ng // tn

    n = len(gather)
    grid = (g * per, s // tm)

    def body(h_ref, wg_ref, wu_ref, *rest):
        gt_ref, up_ref, act_ref = rest[n:n + 3]
        if n:
            start, finish = _gather_plan(rest[n + 3:2 * n + 3], *rest[2 * n + 3:])
            pl.when((pl.program_id(0) == 0) & (pl.program_id(1) == 0))(start)
        hh = h_ref[...]
        gt = _dot(hh, wg_ref[...], "nn")
        up = _dot(hh, wu_ref[...], "nn")
        gt_ref[...] = gt.astype(BF16)
        up_ref[...] = up.astype(BF16)
        act_ref[...] = (_silu_parts(gt)[0] * up).astype(BF16)
        if n:
            pl.when((pl.program_id(0) == grid[0] - 1) & (pl.program_id(1) == grid[1] - 1))(finish)

    wspec = pl.BlockSpec((None, None, d, tn), lambda j, i: (j // per, layer, 0, j % per))
    ospec = pl.BlockSpec((tm, tn), lambda j, i: (i, j))
    osh = jax.ShapeDtypeStruct((s, g * ng), BF16)
    res = pl.pallas_call(body, name=name, grid=grid,
                         in_specs=[pl.BlockSpec((tm, d), lambda j, i: (i, 0)), wspec, wspec] + [ANY] * n,
                         out_specs=[ospec, ospec, ospec] + [ANY] * n,
                         out_shape=[osh, osh, osh] + [jax.ShapeDtypeStruct(t.shape, t.dtype) for t in gather],
                         input_output_aliases={3 + a: 3 + a for a in range(n)},
                         scratch_shapes=_gather_sems(n) if n else [],
                         compiler_params=_params(*(["arbitrary"] * 2 if n else ["parallel"] * 2)))(h, wgate, wup, *gather)
    return res[:3], list(res[3:])


def ffn_down_bwd(name, df, wdown, layer, gt, up):
    s, d = df.shape
    g, _, fg, _ = wdown.shape
    f = g * fg
    assert d <= FULL_K_PREFS[0]
    tm = _pick(s, (256, 128))
    tn = _pick(fg, TW_PREFS)
    per = fg // tn

    def body(df_ref, wd_ref, gt_ref, up_ref, dgt_ref, dup_ref):
        da = _dot(df_ref[...], wd_ref[...], "nt")
        gt = gt_ref[...].astype(F32)
        up = up_ref[...].astype(F32)
        silu, sg = _silu_parts(gt)
        dgt_ref[...] = (da * up * (sg * (1.0 + gt * (1.0 - sg)))).astype(BF16)
        dup_ref[...] = (da * silu).astype(BF16)

    ospec = pl.BlockSpec((tm, tn), lambda j, i: (i, j))
    osh = jax.ShapeDtypeStruct((s, f), BF16)
    return pl.pallas_call(body, name=name, grid=(f // tn, s // tm),
                          in_specs=[pl.BlockSpec((tm, d), lambda j, i: (i, 0)),
                                    pl.BlockSpec((None, None, tn, d), lambda j, i: (j // per, layer, j % per, 0)),
                                    ospec, ospec],
                          out_specs=[ospec, ospec], out_shape=[osh, osh],
                          compiler_params=_params("parallel", "parallel"))(df, wdown, gt, up)


def ffn_in_bwd(name, dgt, dup, wgate, wup, layer, exchange=()):
    s, f = dgt.shape
    g, _, d, ng = wgate.shape
    tm = _pick(s, (512, 256, 128))
    tn = _pick(d, TM_PREFS)
    tk = _pick(ng, TW_PREFS)
    per = ng // tk
    nk = g * per

    n = len(exchange)
    grid = (s // tm, d // tn, nk)

    def body(dg_ref, du_ref, wg_ref, wu_ref, *rest):
        o_ref, acc = rest[n], rest[2 * n + 1]
        kk = pl.program_id(2)
        if n:
            start, finish = _chip_exchange_plan(rest[:n], rest[n + 1:2 * n + 1], *rest[2 * n + 2:])
            pl.when((pl.program_id(0) == 0) & (pl.program_id(1) == 0) & (kk == 0))(start)

        @pl.when(kk == 0)
        def _():
            acc[...] = jnp.zeros_like(acc)

        acc[...] += _dot(dg_ref[...], wg_ref[...], "nt") + _dot(du_ref[...], wu_ref[...], "nt")

        @pl.when(kk == nk - 1)
        def _():
            o_ref[...] = acc[...]

        if n:
            pl.when((pl.program_id(0) == grid[0] - 1) & (pl.program_id(1) == grid[1] - 1) & (kk == nk - 1))(finish)

    aspec = pl.BlockSpec((tm, tk), lambda i, j, kk: (i, kk))
    wspec = pl.BlockSpec((None, None, tn, tk), lambda i, j, kk: (kk // per, layer, j, kk % per))
    res = pl.pallas_call(body, name=name, grid=grid, in_specs=[aspec, aspec, wspec, wspec] + [ANY] * n,
                         out_specs=[pl.BlockSpec((tm, tn), lambda i, j, kk: (i, j))] + [ANY] * n,
                         out_shape=[jax.ShapeDtypeStruct((s, d), F32)] + _chip_exchange_shapes(exchange),
                         scratch_shapes=[pltpu.VMEM((tm, tn), F32)] + (_chip_exchange_sems(n) if n else []),
                         compiler_params=_params(*(["arbitrary"] * 3 if n else ["parallel", "parallel", "arbitrary"])))(
                             dgt, dup, wgate, wup, *exchange)
    return res[0], list(res[1:])


def _halo_specs(tm, s, cols, cb, halo=HALO):
    r = tm // halo
    last = s // halo - 1
    return [pl.BlockSpec((halo, cols), lambda i: (jnp.maximum(i * r - 1, 0), cb)),
            pl.BlockSpec((tm, cols), lambda i: (i, cb)),
            pl.BlockSpec((halo, cols), lambda i: (jnp.minimum((i + 1) * r, last), cb))]


def _edge_masked(prev_ref, next_ref, i, nt):
    return jnp.where(i > 0, prev_ref[...], 0.0), jnp.where(i < nt - 1, next_ref[...], 0.0)


def _glu(a, gate):
    return a * jax.nn.sigmoid(gate)


def _conv_post(conv, lg, lb, gm):
    mu = jnp.mean(conv, axis=-1, keepdims=True)
    xc = conv - mu
    rs = lax.rsqrt(jnp.mean(xc * xc, axis=-1, keepdims=True) + EPS)
    xhat = xc * rs
    yl = xhat * lg + lb
    ya, sg = _silu_parts(yl)
    return xhat, rs, yl, sg, ya


def _fill_glu(hg, ap, am, an, gp, gm_, gn, i, nt, tm):
    a0, a2 = _edge_masked(ap, an, i, nt)
    g0, g2 = _edge_masked(gp, gn, i, nt)
    hg[pl.ds(0, HALO), :] = _glu(a0, g0)
    hg[pl.ds(HALO, tm), :] = _glu(am[...], gm_[...])
    hg[pl.ds(HALO + tm, HALO), :] = _glu(a2, g2)


def conv_fwd(name, proj, cw, cb, lg, lb, gmix):
    s = proj.shape[0]
    tm = _pick(s, (256, 128))
    nt = s // tm

    def body(ap, am, an, gp, gm_, gn, cw_ref, cb_ref, lg_ref, lb_ref, gx_ref, o_ref, conv_ref, hg):
        i = pl.program_id(0)
        _fill_glu(hg, ap, am, an, gp, gm_, gn, i, nt, tm)
        acc = jnp.zeros((tm, C_A), F32) + cb_ref[...]
        for t in range(CONV_WIDTH):
            acc = acc + cw_ref[pl.ds(t, 1), :] * hg[pl.ds(HALO - CONV_HALF + t, tm), :]
        conv_ref[...] = acc
        ya = _conv_post(acc, lg_ref[...], lb_ref[...], None)[4]
        o_ref[...] = (ya * _rms_scale(ya) * gx_ref[...]).astype(BF16)

    vec = _const_spec((1, C_A))
    return pl.pallas_call(body, name=name, grid=(nt,),
                          in_specs=_halo_specs(tm, s, C_A, 0) + _halo_specs(tm, s, C_A, 1)
                          + [_const_spec((32, C_A)), vec, vec, vec, vec],
                          out_specs=[_row_spec(tm, C_A), _row_spec(tm, C_A)],
                          out_shape=[jax.ShapeDtypeStruct((s, C_A), BF16), jax.ShapeDtypeStruct((s, C_A), F32)],
                          scratch_shapes=[pltpu.VMEM((tm + 2 * HALO, C_A), F32)],
                          compiler_params=_params("parallel"))(proj, proj, proj, proj, proj, proj, cw, cb, lg, lb, gmix)


def conv_bwd(name, proj, conv_out, dy, cw, lg, lb, gmix, exchange=()):
    s = proj.shape[0]
    tm = _pick(s, (256, 128))
    nt = s // tm
    te = tm + HALO
    off = HALO // 2

    def extended(prev_ref, main_ref, next_ref, i):
        p0, p2 = _edge_masked(prev_ref, next_ref, i, nt)
        return jnp.concatenate([p0[off:], main_ref[...], p2[:off]], axis=0)

    n = len(exchange)

    def body(ap, am, an, gp, gm_, gn, cp, cm, cn, dp, dm, dn, cw_ref, lg_ref, lb_ref, gx_ref, *rest):
        dag_ref, dcw_ref, dcb_ref, dlg_ref, dlb_ref, dgx_ref = rest[n:n + 6]
        hg, dc = rest[2 * n + 6:2 * n + 8]
        i = pl.program_id(0)
        if n:
            start, finish = _chip_exchange_plan(rest[:n], rest[n + 6:2 * n + 6], *rest[2 * n + 8:])
            pl.when(i == 0)(start)
        _fill_glu(hg, ap, am, an, gp, gm_, gn, i, nt, tm)
        lg, lb, gx = lg_ref[...], lb_ref[...], gx_ref[...]
        xhat, rs, yl, sg, ya = _conv_post(extended(cp, cm, cn, i), lg, lb, gx)
        dout = extended(dp, dm, dn, i)
        dya, dgx_rows = _rms_bwd(dout, ya, gx)
        dyl = dya * (sg * (1.0 + yl * (1.0 - sg)))
        dxh = dyl * lg
        dconv = rs * (dxh - jnp.mean(dxh, axis=-1, keepdims=True)
                      - xhat * jnp.mean(dxh * xhat, axis=-1, keepdims=True))
        dc[...] = dconv

        @pl.when(i == 0)
        def _():
            for r in (dcw_ref, dcb_ref, dlg_ref, dlb_ref, dgx_ref):
                r[...] = jnp.zeros_like(r)

        dcb_ref[...] += _sum8(dconv[off:off + tm])
        dlg_ref[...] += _sum8((dyl * xhat)[off:off + tm])
        dlb_ref[...] += _sum8(dyl[off:off + tm])
        dgx_ref[...] += _sum8(dgx_rows[off:off + tm])
        dcm = dconv[off:off + tm]
        dhg = jnp.zeros((tm, C_A), F32)
        for t in range(CONV_WIDTH):
            dhg = dhg + cw_ref[pl.ds(t, 1), :] * dc[pl.ds(HALO - 1 - t, tm), :]
            dcw_ref[pl.ds(8 * t, 8), :] += _sum8(dcm * hg[pl.ds(HALO - CONV_HALF + t, tm), :])
        a, gate = am[...], gm_[...]
        sgate = jax.nn.sigmoid(gate)
        dag_ref[:, pl.ds(0, C_A)] = (dhg * sgate).astype(BF16)
        dag_ref[:, pl.ds(C_A, C_A)] = (dhg * a * sgate * (1.0 - sgate)).astype(BF16)
        if n:
            pl.when(i == nt - 1)(finish)

    vec = _const_spec((1, C_A))
    acc8 = _const_spec((8, C_A))
    sh8 = jax.ShapeDtypeStruct((8, C_A), F32)
    halo = functools.partial(_halo_specs, tm, s, C_A)
    res = pl.pallas_call(body, name=name, grid=(nt,),
                         in_specs=halo(0) + halo(1) + halo(0) + halo(0) + [_const_spec((32, C_A)), vec, vec, vec]
                         + [ANY] * n,
                         out_specs=[_row_spec(tm, 2 * C_A), _const_spec((32 * 8, C_A)), acc8, acc8, acc8, acc8] + [ANY] * n,
                         out_shape=[jax.ShapeDtypeStruct((s, 2 * C_A), BF16), jax.ShapeDtypeStruct((32 * 8, C_A), F32),
                                    sh8, sh8, sh8, sh8] + _chip_exchange_shapes(exchange),
                         scratch_shapes=[pltpu.VMEM((tm + 2 * HALO, C_A), F32), pltpu.VMEM((te, C_A), F32)]
                         + (_chip_exchange_sems(n) if n else []),
                         compiler_params=_params("arbitrary"))(proj, proj, proj, proj, proj, proj, conv_out, conv_out,
                                                               conv_out, dy, dy, dy, cw, lg, lb, gmix, *exchange)
    return list(res[:6]), list(res[6:])


def _pool_counts(pos, win, s):
    lo = jnp.maximum(pos - win // 2, 0)
    hi = jnp.minimum(pos + win - win // 2, s)
    return jnp.maximum(hi - lo, 1).astype(F32)


def _pooled(uext, base, rows, pos, s):
    outs = []
    for gi, win in enumerate(POOL_WINDOWS):
        lanes = pl.ds(gi * C_G, C_G)
        acc = jnp.zeros((rows, C_G), F32)
        for o in range(-(win // 2), win - win // 2):
            acc = acc + uext[pl.ds(base + o, rows), lanes]
        outs.append(acc / _pool_counts(pos, win, s) - uext[pl.ds(base, rows), lanes])
    return outs


def pool_fwd(name, proj, pw, scale, gmix):
    s = proj.shape[0]
    tm = _pick(s, (256, 128))
    nt = s // tm
    ucol = (proj.shape[1] - C_C) // C_C

    def body(up, um, un, pw_ref, sc_ref, gx_ref, o_ref, uext):
        i = pl.program_id(0)
        u0, u2 = _edge_masked(up, un, i, nt)
        uext[pl.ds(0, HALO), :] = u0
        uext[pl.ds(HALO, tm), :] = um[...]
        uext[pl.ds(HALO + tm, HALO), :] = u2
        pos = i * tm + lax.broadcasted_iota(jnp.int32, (tm, 1), 0)
        pooled = _pooled(uext, HALO, tm, pos, s)
        mixed = jnp.concatenate([_dot(pooled[g].astype(BF16), pw_ref[g], "nn") for g in range(4)], axis=1)
        yc = mixed * sc_ref[...]
        o_ref[...] = (yc * _rms_scale(yc) * gx_ref[...]).astype(BF16)

    vec = _const_spec((1, C_C))
    return pl.pallas_call(body, name=name, grid=(nt,),
                          in_specs=_halo_specs(tm, s, C_C, ucol) + [_const_spec((4, C_G, C_G)), vec, vec],
                          out_specs=_row_spec(tm, C_C), out_shape=jax.ShapeDtypeStruct((s, C_C), BF16),
                          scratch_shapes=[pltpu.VMEM((tm + 2 * HALO, C_C), F32)],
                          compiler_params=_params("parallel"))(proj, proj, proj, pw, scale, gmix)


def pool_bwd(name, proj, dy, pw, scale, gmix):
    s = proj.shape[0]
    tm = _pick(s, (256, 128))
    nt = s // tm
    te = tm + HALO
    off = HALO // 2
    ucol = (proj.shape[1] - C_C) // C_C
    dcol = (dy.shape[1] - C_C) // C_C

    def body(up, um, un, dp, dm, dn, pw_ref, sc_ref, gx_ref, du_ref, dpw_ref, dsc_ref, dgx_ref, uext, dyx, qs, dps):
        i = pl.program_id(0)
        u0, u2 = _edge_masked(up, un, i, nt)
        uext[pl.ds(0, HALO), :] = u0
        uext[pl.ds(HALO, tm), :] = um[...]
        uext[pl.ds(HALO + tm, HALO), :] = u2
        d0, d2 = _edge_masked(dp, dn, i, nt)
        dyx[pl.ds(0, HALO), :] = d0
        dyx[pl.ds(HALO, tm), :] = dm[...]
        dyx[pl.ds(HALO + tm, HALO), :] = d2
        pos = i * tm - off + lax.broadcasted_iota(jnp.int32, (te, 1), 0)
        pooled = _pooled(uext, off, te, pos, s)
        mixed = jnp.concatenate([_dot(pooled[g].astype(BF16), pw_ref[g], "nn") for g in range(4)], axis=1)
        sc = sc_ref[...]
        yc = mixed * sc
        dyc, dgx_rows = _rms_bwd(dyx[pl.ds(off, te), :], yc, gx_ref[...])
        dmixed = dyc * sc

        @pl.when(i == 0)
        def _():
            for r in (dpw_ref, dsc_ref, dgx_ref):
                r[...] = jnp.zeros_like(r)

        dsc_ref[...] += _sum8((dyc * mixed)[off:off + tm])
        dgx_ref[...] += _sum8(dgx_rows[off:off + tm])
        for gi, win in enumerate(POOL_WINDOWS):
            lanes = pl.ds(gi * C_G, C_G)
            dmg = dmixed[:, gi * C_G:(gi + 1) * C_G].astype(BF16)
            dpw_ref[gi] += _dot(pooled[gi][off:off + tm].astype(BF16), dmg[off:off + tm], "tn")
            dpl = _dot(dmg, pw_ref[gi], "nt")
            dps[:, lanes] = dpl
            qs[:, lanes] = dpl / _pool_counts(pos, win, s)
        for gi, win in enumerate(POOL_WINDOWS):
            lanes = pl.ds(gi * C_G, C_G)
            acc = jnp.zeros((tm, C_G), F32) - dps[pl.ds(off, tm), lanes]
            for o in range(-(win // 2) + 1, win // 2 + 1):
                acc = acc + qs[pl.ds(off + o, tm), lanes]
            du_ref[:, lanes] = acc.astype(BF16)

    vec = _const_spec((1, C_C))
    acc8 = _const_spec((8, C_C))
    sh8 = jax.ShapeDtypeStruct((8, C_C), F32)
    return pl.pallas_call(body, name=name, grid=(nt,),
                          in_specs=_halo_specs(tm, s, C_C, ucol) + _halo_specs(tm, s, C_C, dcol)
                          + [_const_spec((4, C_G, C_G)), vec, vec],
                          out_specs=[_row_spec(tm, C_C), _const_spec((4, C_G, C_G)), acc8, acc8],
                          out_shape=[jax.ShapeDtypeStruct((s, C_C), BF16), jax.ShapeDtypeStruct((4, C_G, C_G), F32),
                                     sh8, sh8],
                          scratch_shapes=[pltpu.VMEM((tm + 2 * HALO, C_C), F32), pltpu.VMEM((tm + 2 * HALO, C_C), F32),
                                          pltpu.VMEM((te, C_C), F32), pltpu.VMEM((te, C_C), F32)],
                          compiler_params=_params("arbitrary"))(proj, proj, proj, dy, dy, dy, pw, scale, gmix)


def rope_tables(s):
    pos = jnp.arange(s, dtype=F32)
    inv = ROPE_THETA ** (-jnp.arange(0, ROT_DIM, 2, dtype=F32) / ROT_DIM)
    ang = pos[:, None] * inv[None, :]
    cos, sin = jnp.cos(ang), jnp.sin(ang)
    half = ROT_DIM // 2
    rest = HEAD_DIM - ROT_DIM
    c = jnp.concatenate([cos, cos, jnp.ones((s, rest), F32)], axis=1)
    sa = jnp.concatenate([-sin, jnp.zeros((s, HEAD_DIM - half), F32)], axis=1)
    sb = jnp.concatenate([jnp.zeros((s, half), F32), sin, jnp.zeros((s, rest), F32)], axis=1)
    return tuple(jnp.concatenate([t, t], axis=1) for t in (c, sa, sb))


DIL_TILE = 256
N_CHUNKS = C_B // LANES


def _dilated_shape(s, d, dtype):
    return jax.ShapeDtypeStruct((s, C_B) if d == 1 else (d, s // d, C_B), dtype)


def _dilated_spec(tm, d):
    return _row_spec(tm, C_B) if d == 1 else pl.BlockSpec((d, tm // d, C_B), lambda i: (0, i, 0))


def _as_dilated(t, d):
    return t if d == 1 else t.reshape(d, t.shape[0] // d, t.shape[1])


def _dil_scratch(tm):
    return pltpu.VMEM((N_CHUNKS, tm, LANES), F32)


def _store_dilated(o_ref, scr, d, tm):
    for r in range(d):
        for j in range(N_CHUNKS):
            o_ref[r, :, pl.ds(j * LANES, LANES)] = scr[j, pl.ds(r, tm // d, stride=d), :].astype(o_ref.dtype)


def _load_dilated(in_ref, scr, d, tm):
    for r in range(d):
        for j in range(N_CHUNKS):
            scr[j, pl.ds(r, tm // d, stride=d), :] = in_ref[r, :, pl.ds(j * LANES, LANES)].astype(F32)


def rope_fwd(name, proj, tabs):
    s = proj.shape[0]
    tm = DIL_TILE
    scale = HEAD_DIM ** -0.5

    def body(q_ref, k_ref, v_ref, c_ref, sa_ref, sb_ref, *rest):
        outs, scr = rest[:9], rest[9:]
        c, sa, sb = c_ref[...], sa_ref[...], sb_ref[...]
        for j in range(N_CHUNKS):
            lanes = pl.ds(j * LANES, LANES)
            for which, (src, mul) in enumerate(((q_ref, scale), (k_ref, 1.0))):
                t = src[:, lanes]
                r = t * c + pltpu.roll(t, LANES - ROT_DIM // 2, 1) * sa + pltpu.roll(t, ROT_DIM // 2, 1) * sb
                scr[which][j] = r * mul
            scr[2][j] = v_ref[:, lanes]
            for which in range(3):
                outs[which][:, lanes] = scr[which][j].astype(BF16)
        for pi, d in enumerate(DILATIONS[1:]):
            for which in range(3):
                _store_dilated(outs[3 * (pi + 1) + which], scr[which], d, tm)

    tab = _row_spec(tm, LANES)
    res = pl.pallas_call(body, name=name, grid=(s // tm,),
                         in_specs=[_row_spec(tm, C_B, 1), _row_spec(tm, C_B, 2), _row_spec(tm, C_B, 3), tab, tab, tab],
                         out_specs=[_dilated_spec(tm, d) for d in DILATIONS for _ in range(3)],
                         out_shape=[_dilated_shape(s, d, BF16) for d in DILATIONS for _ in range(3)],
                         scratch_shapes=[_dil_scratch(tm)] * 3,
                         compiler_params=_params("parallel"))(proj, proj, proj, *tabs)
    return [tuple(t.reshape(s, C_B) for t in res[3 * pi:3 * pi + 3]) for pi in range(len(DILATIONS))]


def rope_bwd(name, dqs, dks, dvs, tabs):
    s = dqs[0].shape[0]
    tm = DIL_TILE
    scale = HEAD_DIM ** -0.5
    n_dil = len(DILATIONS)

    def body(*refs):
        ins = [refs[n_dil * which:n_dil * (which + 1)] for which in range(3)]
        c_ref, sa_ref, sb_ref, o_ref = refs[3 * n_dil:3 * n_dil + 4]
        scr = refs[3 * n_dil + 4:]
        for which in range(3):
            for pi, d in enumerate(DILATIONS[1:]):
                _load_dilated(ins[which][pi + 1], scr[which * (n_dil - 1) + pi], d, tm)
        c, sa, sb = c_ref[...], sa_ref[...], sb_ref[...]
        for j in range(N_CHUNKS):
            lanes = pl.ds(j * LANES, LANES)
            for which, mul in ((0, scale), (1, 1.0), (2, None)):
                d = ins[which][0][:, lanes].astype(F32)
                for pi in range(n_dil - 1):
                    d = d + scr[which * (n_dil - 1) + pi][j]
                if mul is not None:
                    d = (d * c + pltpu.roll(d * sa, ROT_DIM // 2, 1) + pltpu.roll(d * sb, LANES - ROT_DIM // 2, 1)) * mul
                o_ref[:, pl.ds(which * C_B + j * LANES, LANES)] = d.astype(BF16)

    tab = _row_spec(tm, LANES)
    args = [_as_dilated(t, d) for ts in (dqs, dks, dvs) for t, d in zip(ts, DILATIONS)]
    return pl.pallas_call(body, name=name, grid=(s // tm,),
                          in_specs=[_dilated_spec(tm, d) for _ in range(3) for d in DILATIONS] + [tab, tab, tab],
                          out_specs=_row_spec(tm, 3 * C_B), out_shape=jax.ShapeDtypeStruct((s, 3 * C_B), BF16),
                          scratch_shapes=[_dil_scratch(tm)] * (3 * (n_dil - 1)),
                          compiler_params=_params("parallel"))(*args, *tabs)


def _window_specs(tq, s, cols):
    r = tq // ATT_HALF
    last = s // ATT_HALF - 1
    return [pl.BlockSpec((ATT_HALF, cols), lambda i: (jnp.maximum(i * r - 1, 0), 0)),
            pl.BlockSpec((tq, cols), lambda i: (i, 0)),
            pl.BlockSpec((ATT_HALF, cols), lambda i: (jnp.minimum((i + 1) * r, last), 0))]


def _fill_window(win, prev_ref, main_ref, next_ref, tq):
    win[pl.ds(0, ATT_HALF), :] = prev_ref[...]
    win[pl.ds(ATT_HALF, tq), :] = main_ref[...]
    win[pl.ds(ATT_HALF + tq, ATT_HALF), :] = next_ref[...]


def _band_valid(tile, window, j0, seg, tile_is_rows):
    shape = (2 * tile, window) if tile_is_rows else (2 * window, tile)
    ri = lax.broadcasted_iota(jnp.int32, shape, 0)
    ci = lax.broadcasted_iota(jnp.int32, shape, 1)
    per_head = tile if tile_is_rows else window
    ri = jnp.where(ri >= per_head, ri - per_head, ri)
    ti, wi = (ri, ci) if tile_is_rows else (ci, ri)
    jw = j0 - ATT_HALF + wi
    return (jnp.abs(wi - ATT_HALF - ti) <= ATT_HALF) & (jw >= 0) & (jw < seg)


def _first_head():
    return lax.broadcasted_iota(jnp.int32, (1, LANES), 1) < HEAD_DIM


def _stack_heads(t, first):
    z = jnp.zeros_like(t)
    return jnp.concatenate([jnp.where(first, t, z), jnp.where(first, z, t)], axis=0)


def _unstack_heads(t2, first, rows):
    return jnp.where(first, t2[:rows], t2[rows:])


def _head_columns(ref, pair, rows):
    return jnp.concatenate([ref[rows, pl.ds(pair * LANES, 1)], ref[rows, pl.ds(pair * LANES + HEAD_DIM, 1)]], axis=0)


def _sub_tiles(tile):
    sub = min(tile, 2 * ATT_HALF)
    return sub, tile // sub, sub + 2 * ATT_HALF


def band_attn_fwd(name, q, k, v, seg):
    s, c = q.shape
    tq = _pick(seg, ATT_TILES)
    per_seg = seg // tq

    wrows = tq + 2 * ATT_HALF

    sub, n_sub, wsub = _sub_tiles(tq)

    def body(q_ref, kp, km, kn, vp, vm, vn, o_ref, lse_ref, kw, vw):
        j0 = (pl.program_id(0) % per_seg) * tq
        _fill_window(kw, kp, km, kn, tq)
        _fill_window(vw, vp, vm, vn, tq)
        first = _first_head()
        for h in range(n_sub):
            rows, wnd = pl.ds(h * sub, sub), pl.ds(h * sub, wsub)
            valid = _band_valid(sub, wsub, j0 + h * sub, seg, True)
            for pair in range(c // LANES):
                lanes = pl.ds(pair * LANES, LANES)
                sc = jnp.where(valid, _dot(_stack_heads(q_ref[rows, lanes], first), kw[wnd, lanes], "nt"), NEG)
                m = jnp.max(sc, axis=-1, keepdims=True)
                p = jnp.exp(sc - m)
                l = jnp.sum(p, axis=-1, keepdims=True)
                o = _dot(p.astype(BF16), vw[wnd, lanes], "nn") / l
                o_ref[rows, lanes] = _unstack_heads(o, first, sub).astype(BF16)
                lse_ref[rows, lanes] = _unstack_heads(m + jnp.log(l), first, sub)

    win = _window_specs(tq, s, c)
    tile = _row_spec(tq, c)
    return pl.pallas_call(body, name=name, grid=(s // tq,), in_specs=[tile] + win + win, out_specs=[tile, tile],
                          out_shape=[jax.ShapeDtypeStruct((s, c), BF16), jax.ShapeDtypeStruct((s, c), F32)],
                          scratch_shapes=[pltpu.VMEM((wrows, c), BF16)] * 2,
                          compiler_params=_params("parallel"))(q, k, k, k, v, v, v)


def band_attn_dq(name, q, k, v, do, lse, delta, seg):
    s, c = q.shape
    tq = _pick(seg, ATT_TILES)
    per_seg = seg // tq

    wrows = tq + 2 * ATT_HALF

    sub, n_sub, wsub = _sub_tiles(tq)

    def body(q_ref, kp, km, kn, vp, vm, vn, do_ref, lse_ref, dl_ref, dq_ref, kw, vw):
        j0 = (pl.program_id(0) % per_seg) * tq
        _fill_window(kw, kp, km, kn, tq)
        _fill_window(vw, vp, vm, vn, tq)
        first = _first_head()
        for h in range(n_sub):
            rows, wnd = pl.ds(h * sub, sub), pl.ds(h * sub, wsub)
            valid = _band_valid(sub, wsub, j0 + h * sub, seg, True)
            for pair in range(c // LANES):
                lanes = pl.ds(pair * LANES, LANES)
                sc = _dot(_stack_heads(q_ref[rows, lanes], first), kw[wnd, lanes], "nt")
                p = jnp.where(valid, jnp.exp(sc - _head_columns(lse_ref, pair, rows)), 0.0)
                dp = _dot(_stack_heads(do_ref[rows, lanes], first), vw[wnd, lanes], "nt")
                ds = p * (dp - _head_columns(dl_ref, pair, rows))
                dq = _dot(ds.astype(BF16), kw[wnd, lanes], "nn")
                dq_ref[rows, lanes] = _unstack_heads(dq, first, sub).astype(BF16)

    win = _window_specs(tq, s, c)
    tile = _row_spec(tq, c)
    return pl.pallas_call(body, name=name, grid=(s // tq,), in_specs=[tile] + win + win + [tile, tile, tile],
                          out_specs=tile, out_shape=jax.ShapeDtypeStruct((s, c), BF16),
                          scratch_shapes=[pltpu.VMEM((wrows, c), BF16)] * 2,
                          compiler_params=_params("parallel"))(q, k, k, k, v, v, v, do, lse, delta)


def band_attn_dkv(name, q, k, v, do, lse, delta, seg):
    s, c = q.shape
    tk = _pick(seg, ATT_TILES)
    per_seg = seg // tk

    wrows = tk + 2 * ATT_HALF
    sub, n_sub, wsub = _sub_tiles(tk)

    def body(k_ref, v_ref, qp, qm, qn, dop, dom, don, lp, lm, ln, dlp, dlm, dln, dk_ref, dv_ref, qw, dow, lsew, dlw):
        j0 = (pl.program_id(0) % per_seg) * tk
        _fill_window(qw, qp, qm, qn, tk)
        _fill_window(dow, dop, dom, don, tk)
        _fill_window(lsew, lp, lm, ln, tk)
        _fill_window(dlw, dlp, dlm, dln, tk)
        first = _first_head()
        top = lax.broadcasted_iota(jnp.int32, (2 * sub, 1), 0) < sub

        def head_rows(win, wnd, lanes):
            t = win[wnd, lanes].T
            return jnp.where(top, t[0:1, :], t[HEAD_DIM:HEAD_DIM + 1, :])

        for h in range(n_sub):
            rows, wnd = pl.ds(h * sub, sub), pl.ds(h * sub, wsub)
            valid = _band_valid(sub, wsub, j0 + h * sub, seg, True)
            for pair in range(c // LANES):
                lanes = pl.ds(pair * LANES, LANES)
                q2, do2 = qw[wnd, lanes], dow[wnd, lanes]
                sc = _dot(_stack_heads(k_ref[rows, lanes], first), q2, "nt")
                p = jnp.where(valid, jnp.exp(sc - head_rows(lsew, wnd, lanes)), 0.0)
                dv = _dot(p.astype(BF16), do2, "nn")
                dp = _dot(_stack_heads(v_ref[rows, lanes], first), do2, "nt")
                ds = p * (dp - head_rows(dlw, wnd, lanes))
                dk = _dot(ds.astype(BF16), q2, "nn")
                dv_ref[rows, lanes] = _unstack_heads(dv, first, sub).astype(BF16)
                dk_ref[rows, lanes] = _unstack_heads(dk, first, sub).astype(BF16)

    win = _window_specs(tk, s, c)
    tile = _row_spec(tk, c)
    osh = jax.ShapeDtypeStruct((s, c), BF16)
    return pl.pallas_call(body, name=name, grid=(s // tk,), in_specs=[tile, tile] + win * 4,
                          out_specs=[tile, tile], out_shape=[osh, osh],
                          scratch_shapes=[pltpu.VMEM((wrows, c), BF16)] * 2 + [pltpu.VMEM((wrows, c), F32)] * 2,
                          compiler_params=_params("parallel"))(k, v, q, q, q, do, do, do, lse, lse, lse,
                                                               delta, delta, delta)


def attn_combine_fwd(name, os_, lses, gmix):
    s, c = os_[0].shape
    tm = DIL_TILE
    n_dil = len(DILATIONS)

    def body(*refs):
        o_refs, l_refs, gx_ref = refs[:n_dil], refs[n_dil:2 * n_dil], refs[2 * n_dil]
        yb_ref, y_ref = refs[2 * n_dil + 1:2 * n_dil + 3]
        lse_refs = refs[2 * n_dil + 3:3 * n_dil + 3]
        scr = refs[3 * n_dil + 3:]
        scr_o, scr_l, scr_lse = scr[:n_dil - 1], scr[n_dil - 1:2 * (n_dil - 1)], scr[-1]
        for pi, d in enumerate(DILATIONS[1:]):
            _load_dilated(o_refs[pi + 1], scr_o[pi], d, tm)
            _load_dilated(l_refs[pi + 1], scr_l[pi], d, tm)
        sumsq = jnp.zeros((tm, 1), F32)
        for j in range(N_CHUNKS):
            lanes = pl.ds(j * LANES, LANES)
            ls = [l_refs[0][:, lanes]] + [t[j] for t in scr_l]
            vals = [o_refs[0][:, lanes].astype(F32)] + [t[j] for t in scr_o]
            m = functools.reduce(jnp.maximum, ls)
            es = [jnp.exp(l - m) for l in ls]
            den = functools.reduce(lambda a, b: a + b, es)
            yb = functools.reduce(lambda a, b: a + b, [e * v for e, v in zip(es, vals)]) / den
            yb_ref[:, lanes] = yb
            lse = m + jnp.log(den)
            lse_refs[0][:, lanes] = lse
            scr_lse[j] = lse
            sumsq = sumsq + jnp.sum(yb * yb, axis=-1, keepdims=True)
        r = lax.rsqrt(sumsq / c + EPS)
        y_ref[...] = (yb_ref[...] * r * gx_ref[...]).astype(BF16)
        for pi, d in enumerate(DILATIONS[1:]):
            _store_dilated(lse_refs[pi + 1], scr_lse, d, tm)

    row = _row_spec(tm, c)
    dil = [_dilated_spec(tm, d) for d in DILATIONS]
    res = pl.pallas_call(body, name=name, grid=(s // tm,), in_specs=dil + dil + [_const_spec((1, c))],
                         out_specs=[row, row] + dil,
                         out_shape=[jax.ShapeDtypeStruct((s, c), F32), jax.ShapeDtypeStruct((s, c), BF16)]
                         + [_dilated_shape(s, d, F32) for d in DILATIONS],
                         scratch_shapes=[_dil_scratch(tm)] * (2 * (n_dil - 1) + 1),
                         compiler_params=_params("parallel"))(
                             *[_as_dilated(t, d) for t, d in zip(os_, DILATIONS)],
                             *[_as_dilated(t, d) for t, d in zip(lses, DILATIONS)], gmix)
    return res[0], res[1], [t.reshape(s, c) for t in res[2:]]


def attn_combine_bwd(name, dy, yb, gmix):
    s, c = yb.shape
    tm = DIL_TILE
    half = c // 2
    n_dil = len(DILATIONS)

    def body(d1, d2, yb_ref, gx_ref, *rest):
        dyb_refs, dl_refs, dgx_ref = rest[:n_dil], rest[n_dil:2 * n_dil], rest[2 * n_dil]
        scr_dyb, scr_dl = rest[2 * n_dil + 1:]
        i = pl.program_id(0)
        d = jnp.concatenate([d1[...], d2[...]], axis=1)
        yb_ = yb_ref[...]
        dyb, dgx_rows = _rms_bwd(d, yb_, gx_ref[...])
        dyb_refs[0][...] = dyb.astype(BF16)

        @pl.when(i == 0)
        def _():
            dgx_ref[...] = jnp.zeros_like(dgx_ref)

        dgx_ref[...] += _sum8(dgx_rows)
        ri = lax.broadcasted_iota(jnp.int32, (LANES, LANES), 0) // HEAD_DIM
        ci = lax.broadcasted_iota(jnp.int32, (LANES, LANES), 1) // HEAD_DIM
        same_head = jnp.where(ri == ci, 1.0, 0.0).astype(BF16)
        prod = dyb * yb_
        for j in range(N_CHUNKS):
            pj = prod[:, j * LANES:(j + 1) * LANES]
            hi = pj.astype(BF16)
            lo = (pj - hi.astype(F32)).astype(BF16)
            delta = _dot(hi, same_head, "nn") + _dot(lo, same_head, "nn")
            dl_refs[0][:, pl.ds(j * LANES, LANES)] = delta
            scr_dl[j] = delta
            scr_dyb[j] = dyb[:, j * LANES:(j + 1) * LANES]
        for pi, dil in enumerate(DILATIONS[1:]):
            _store_dilated(dyb_refs[pi + 1], scr_dyb, dil, tm)
            _store_dilated(dl_refs[pi + 1], scr_dl, dil, tm)

    row = _row_spec(tm, c)
    dil_specs = [_dilated_spec(tm, d) for d in DILATIONS]
    res = pl.pallas_call(body, name=name, grid=(s // tm,),
                         in_specs=[_row_spec(tm, half, 1), _row_spec(tm, half, 2), row, _const_spec((1, c))],
                         out_specs=dil_specs + dil_specs + [_const_spec((8, c))],
                         out_shape=[_dilated_shape(s, d, BF16) for d in DILATIONS]
                         + [_dilated_shape(s, d, F32) for d in DILATIONS] + [jax.ShapeDtypeStruct((8, c), F32)],
                         scratch_shapes=[_dil_scratch(tm)] * 2,
                         compiler_params=_params("arbitrary"))(dy, dy, yb, gmix)
    return ([t.reshape(s, c) for t in res[:n_dil]], [t.reshape(s, c) for t in res[n_dil:2 * n_dil]], res[2 * n_dil])


def _ew_rows(rows, cols, n_bufs):
    budget = 24 * 1024 * 1024 // (8 * n_bufs * cols)
    return _pick(rows, tuple(t for t in (2048, 1024, 512, 256, 128, 64, 32, 16, 8) if t <= max(budget, 8)))


def elementwise(name, fn, ins, out_dtypes):
    shape = ins[0].shape
    cols = shape[-1]
    rows = math.prod(shape[:-1])
    tr = _ew_rows(rows, cols, len(ins) + len(out_dtypes))
    n_in = len(ins)

    def body(*refs):
        outs = fn(*[r[...] for r in refs[:n_in]])
        for o_ref, o in zip(refs[n_in:], outs):
            o_ref[...] = o.astype(o_ref.dtype)

    spec = _row_spec(tr, cols)
    res = pl.pallas_call(body, name=name, grid=(rows // tr,), in_specs=[spec] * n_in, out_specs=[spec] * len(out_dtypes),
                         out_shape=[jax.ShapeDtypeStruct((rows, cols), dt) for dt in out_dtypes],
                         compiler_params=_params("parallel"))(*[t.reshape(rows, cols) for t in ins])
    return [r.reshape(shape) for r in res]


def _adamw(g, w, m, v):
    m = ADAM_B1 * m + (1.0 - ADAM_B1) * g
    v = ADAM_B2 * v + (1.0 - ADAM_B2) * (g * g)
    m_hat = m / (1.0 - ADAM_B1 ** ADAM_STEP)
    v_hat = v / (1.0 - ADAM_B2 ** ADAM_STEP)
    delta = -ADAM_LR * (m_hat / (jnp.sqrt(v_hat) + ADAM_EPS) + ADAM_WD * w)
    return delta, m, v


def cast_into_gathered(w, place, layer):
    _, r, c = w.shape
    tr = _ew_rows(r, c, 2)

    def body(p_ref, w_ref, o_ref):
        o_ref[...] = w_ref[...].astype(BF16)

    grid_spec = pltpu.PrefetchScalarGridSpec(
        num_scalar_prefetch=1, grid=(r // tr,),
        in_specs=[pl.BlockSpec((None, tr, c), lambda i, p: (layer, i, 0))],
        out_specs=pl.BlockSpec((None, None, tr, c), lambda i, p: (p[0], 0, i, 0)))
    return pl.pallas_call(body, name="cast_into_gathered", grid_spec=grid_spec,
                          out_shape=jax.ShapeDtypeStruct((N_GROUPS, 1, r, c), BF16),
                          compiler_params=_params("parallel"))(place, w)


def sum_core_halves(name, grad, theirs, place):
    g, r, c = grad.shape
    h = r // 2
    tr = _ew_rows(h, c, 3)
    nh = h // tr

    def body(p_ref, a_ref, b_ref, o_ref):
        o_ref[...] = (a_ref[...].astype(F32) + b_ref[...].astype(F32)).astype(BF16)

    blk = pl.BlockSpec((None, tr, c), lambda gi, i, p: (gi, i, 0))
    grid_spec = pltpu.PrefetchScalarGridSpec(
        num_scalar_prefetch=1, grid=(g, nh),
        in_specs=[pl.BlockSpec((None, tr, c), lambda gi, i, p: (gi, p[1] * nh + i, 0)), blk], out_specs=blk)
    return pl.pallas_call(body, name=name, grid_spec=grid_spec, out_shape=jax.ShapeDtypeStruct((g, h, c), BF16),
                          compiler_params=_params("parallel", "parallel"))(place, grad, theirs)


def sum_chip_partials(name, parts, recv, place, layer, n_layers, into):
    _, h, c = parts.shape
    tr = _ew_rows(h, c, 5)
    nh = h // tr

    def body(p_ref, o_ref, r0, r1, r2, *rest):
        rest[-1][...] = ((o_ref[...].astype(F32) + r0[...].astype(F32)) + r1[...].astype(F32)) + r2[...].astype(F32)

    in_specs = [pl.BlockSpec((None, tr, c), lambda i, p: (p[0], i, 0))]
    in_specs += [pl.BlockSpec((None, tr, c), lambda i, p, j=j: (j, i, 0)) for j in range(3)]
    args = [place, parts, recv, recv, recv]
    aliases = {}
    if into is not None:
        in_specs.append(ANY)
        args.append(into)
        aliases = {5: 0}
    grid_spec = pltpu.PrefetchScalarGridSpec(
        num_scalar_prefetch=1, grid=(nh,), in_specs=in_specs,
        out_specs=pl.BlockSpec((None, tr, c), lambda i, p: (layer, p[1] * nh + i, 0)))
    return pl.pallas_call(body, name=name, grid_spec=grid_spec, input_output_aliases=aliases,
                          out_shape=jax.ShapeDtypeStruct((n_layers, 2 * h, c), F32),
                          compiler_params=_params("parallel"))(*args)


def small_update(name, gall, w, m, v):
    _, rows, cols = gall.shape
    tr = _ew_rows(rows, cols, 16)

    def body(g_ref, w_ref, m_ref, v_ref, go, do, mo, vo):
        g = g_ref[0]
        for dev in range(1, 8):
            g = g + g_ref[dev]
        delta, mn, vn = _adamw(g, w_ref[...], m_ref[...], v_ref[...])
        go[...] = g
        do[...] = delta
        mo[...] = mn
        vo[...] = vn

    spec = _row_spec(tr, cols)
    sh = jax.ShapeDtypeStruct((rows, cols), F32)
    return pl.pallas_call(body, name=name, grid=(rows // tr,),
                          in_specs=[pl.BlockSpec((8, tr, cols), lambda i: (0, i, 0)), spec, spec, spec],
                          out_specs=[spec] * 4, out_shape=[sh] * 4, compiler_params=_params("parallel"))(gall, w, m, v)


def _place():
    x, y, c = lax.axis_index("x"), lax.axis_index("y"), lax.axis_index("c")
    chips = [(1 - x, y), (x, 1 - y), (1 - x, 1 - y)]
    return x, y, c, chips


def _remote(src, dst, send_sems, recv_sems, k, to):
    return pltpu.make_async_remote_copy(src_ref=src, dst_ref=dst, send_sem=send_sems.at[k], recv_sem=recv_sems.at[k],
                                        device_id=to, device_id_type=MESH)


ANY = pl.BlockSpec(memory_space=pl.ANY)


def allgather_small(name, block):
    m_per, n = block.shape

    def body(x_ref, out_ref, send_sems, recv_sems, local_sem):
        x, y, c, chips = _place()
        me, sibling = (x, y, c), (x, y, 1 - c)

        def rows(px, py, pc):
            return out_ref.at[pl.ds((4 * px + 2 * py + pc) * m_per, m_per), :]

        def copy(k, blk, to, src=None):
            return _remote(rows(*blk) if src is None else src, rows(*blk), send_sems, recv_sems, k, to)

        mine = pltpu.make_async_copy(x_ref, rows(*me), local_sem)
        mine.start()
        first = [copy(0, me, sibling, src=x_ref)]
        first += [copy(1 + j, me, (*chip, c), src=x_ref) for j, chip in enumerate(chips)]
        for cp in first:
            cp.start()
        passed = [copy(4 + j, (*chip, c), sibling) for j, chip in enumerate(chips)]
        for j, chip in enumerate(chips):
            copy(1 + j, (*chip, c), me).wait_recv()
            passed[j].start()
        copy(0, sibling, me).wait_recv()
        for j, chip in enumerate(chips):
            copy(4 + j, (*chip, 1 - c), me).wait_recv()
        for cp in first + passed:
            cp.wait_send()
        mine.wait()

    return pl.pallas_call(body, name=name, out_shape=jax.ShapeDtypeStruct((8 * m_per, n), block.dtype),
                          in_specs=[pl.BlockSpec(memory_space=pltpu.VMEM)], out_specs=pl.BlockSpec(memory_space=pltpu.VMEM),
                          scratch_shapes=[pltpu.SemaphoreType.DMA((7,)), pltpu.SemaphoreType.DMA((7,)),
                                          pltpu.SemaphoreType.DMA],
                          compiler_params=pltpu.CompilerParams(vmem_limit_bytes=VMEM_LIMIT))(block)


def gather_weights(bufs):
    n = len(bufs)

    def body(*refs):
        start, finish = _gather_plan(refs[n:2 * n], *refs[2 * n:])
        start()
        finish()

    return pl.pallas_call(body, name="gather_weights",
                          out_shape=[jax.ShapeDtypeStruct(t.shape, t.dtype) for t in bufs],
                          in_specs=[ANY] * n, out_specs=[ANY] * n, input_output_aliases={a: a for a in range(n)},
                          scratch_shapes=_gather_sems(n))(*bufs)


def _gather_sems(n):
    return [pltpu.SemaphoreType.DMA((6 * n,)), pltpu.SemaphoreType.DMA((6 * n,))]


def _gather_plan(outs, send_sems, recv_sems):
    n = len(outs)
    x, y, c, chips = _place()
    g0 = 2 * x + y
    sibling = (x, y, 1 - c)
    groups = [2 * cx + cy for cx, cy in chips]

    def part(a, g, cc):
        h = outs[a].shape[2] // 2
        return outs[a].at[g, :, pl.ds(cc * h, h), :]

    def ici(a, j):
        return _remote(part(a, g0, c), part(a, g0, c), send_sems, recv_sems, 3 * a + j, (*chips[j], c))

    def ici_arrival(a, j):
        return _remote(part(a, groups[j], c), part(a, groups[j], c), send_sems, recv_sems, 3 * a + j, (*chips[j], c))

    def passed(a, j, cc):
        return _remote(part(a, groups[j], cc), part(a, groups[j], cc), send_sems, recv_sems, 3 * n + 3 * a + j, sibling)

    def start():
        for a in range(n):
            for j in range(3):
                ici(a, j).start()

    def finish():
        for a in range(n):
            for j in range(3):
                ici_arrival(a, j).wait_recv()
                passed(a, j, c).start()
        for a in range(n):
            for j in range(3):
                passed(a, j, 1 - c).wait_recv()
        for a in range(n):
            for j in range(3):
                ici(a, j).wait_send()
                passed(a, j, c).wait_send()

    return start, finish


def exchange_core_halves(name, grads):
    n = len(grads)
    halves = [t.shape[1] // 2 for t in grads]

    def body(*refs):
        ins, theirs = refs[:n], refs[n:2 * n]
        send_sems, recv_sems = refs[2 * n:]
        x, y, c, _ = _place()
        sibling = (x, y, 1 - c)

        def give(a, g):
            return _remote(ins[a].at[g, pl.ds((1 - c) * halves[a], halves[a]), :], theirs[a].at[g], send_sems,
                           recv_sems, N_GROUPS * a + g, sibling)

        for a in range(n):
            for g in range(N_GROUPS):
                give(a, g).start()
        for a in range(n):
            for g in range(N_GROUPS):
                give(a, g).wait_recv()
        for a in range(n):
            for g in range(N_GROUPS):
                give(a, g).wait_send()

    shapes = [jax.ShapeDtypeStruct((t.shape[0], h, t.shape[2]), t.dtype) for t, h in zip(grads, halves)]
    k = N_GROUPS * n
    return pl.pallas_call(body, name=name, out_shape=shapes, in_specs=[ANY] * n, out_specs=[ANY] * n,
                          scratch_shapes=[pltpu.SemaphoreType.DMA((k,)), pltpu.SemaphoreType.DMA((k,))])(*grads)


def exchange_chip_partials(name, parts):
    n = len(parts)

    def body(*refs):
        start, finish = _chip_exchange_plan(refs[:n], refs[n:2 * n], *refs[2 * n:])
        start()
        finish()

    return pl.pallas_call(body, name=name, out_shape=_chip_exchange_shapes(parts), in_specs=[ANY] * n,
                          out_specs=[ANY] * n, scratch_shapes=_chip_exchange_sems(n))(*parts)


def _chip_exchange_shapes(parts):
    return [jax.ShapeDtypeStruct((3,) + t.shape[1:], t.dtype) for t in parts]


def _chip_exchange_sems(n):
    return [pltpu.SemaphoreType.DMA((3 * n,)), pltpu.SemaphoreType.DMA((3 * n,))]


def _chip_exchange_plan(ins, recv, send_sems, recv_sems):
    n = len(ins)
    x, y, c, chips = _place()

    def give(a, j):
        return _remote(ins[a].at[2 * chips[j][0] + chips[j][1]], recv[a].at[j], send_sems, recv_sems, 3 * a + j,
                       (*chips[j], c))

    def start():
        for a in range(n):
            for j in range(3):
                give(a, j).start()

    def finish():
        for a in range(n):
            for j in range(3):
                give(a, j).wait_recv()
        for a in range(n):
            for j in range(3):
                give(a, j).wait_send()

    return start, finish


def share_reduced_halves(bufs):
    n = len(bufs)
    n_layers = bufs[0].shape[0]
    halves = [t.shape[1] // 2 for t in bufs]

    def body(*refs):
        outs = refs[n:2 * n]
        send_sems, recv_sems = refs[2 * n:]
        x, y, c, _ = _place()
        sibling = (x, y, 1 - c)

        def give(l, a, cc):
            part = outs[a].at[l, pl.ds(cc * halves[a], halves[a]), :]
            return _remote(part, part, send_sems, recv_sems, l * n + a, sibling)

        for l in range(n_layers):
            for a in range(n):
                give(l, a, c).start()
        for l in range(n_layers):
            for a in range(n):
                give(l, a, 1 - c).wait_recv()
        for l in range(n_layers):
            for a in range(n):
                give(l, a, c).wait_send()

    k = n_layers * n
    return pl.pallas_call(body, name="share_reduced_halves", out_shape=[jax.ShapeDtypeStruct(t.shape, t.dtype) for t in bufs],
                          in_specs=[ANY] * n, out_specs=[ANY] * n, input_output_aliases={a: a for a in range(n)},
                          scratch_shapes=[pltpu.SemaphoreType.DMA((k,)), pltpu.SemaphoreType.DMA((k,))])(*bufs)


def reduce_in_chip(grads, place):
    theirs = exchange_core_halves("exchange_core_halves", grads)
    return [sum_core_halves("sum_core_halves", g, t, place) for g, t in zip(grads, theirs)]


def reduce_over_chips(parts, recv, place, layer, n_layers, into):
    return [sum_chip_partials("sum_chip_partials", p, r, place, layer, n_layers, None if into is None else into[a])
            for a, (p, r) in enumerate(zip(parts, recv))]


def _pack(parts):
    flat = jnp.concatenate([p.reshape(-1) for p in parts])
    pad = (-flat.shape[0]) % (8 * LANES)
    if pad:
        flat = jnp.concatenate([flat, jnp.zeros((pad,), flat.dtype)])
    return flat.reshape(-1, LANES)


def _unpack(buf, shapes):
    flat = buf.reshape(-1)
    out, at = [], 0
    for sh in shapes:
        size = math.prod(sh)
        out.append(flat[at:at + size].reshape(sh))
        at += size
    return out


def kernel(x, w_in, conv_w, conv_b, conv_ln_g, conv_ln_b, pool_w, pool_scale, g_mix, w_out, g_pre_mix, g_post_mix, g_pre_ffn, g_post_ffn, w_gate, w_up, w_down, loss_target, m_w_in, m_conv_w, m_conv_b, m_conv_ln_g, m_conv_ln_b, m_pool_w, m_pool_scale, m_g_mix, m_w_out, m_g_pre_mix, m_g_post_mix, m_g_pre_ffn, m_g_post_ffn, m_w_gate, m_w_up, m_w_down, v_w_in, v_conv_w, v_conv_b, v_conv_ln_g, v_conv_ln_b, v_pool_w, v_pool_scale, v_g_mix, v_w_out, v_g_pre_mix, v_g_post_mix, v_g_pre_ffn, v_g_post_ffn, v_w_gate, v_w_up, v_w_down):
    xs, target = x[0], loss_target[0]
    s, d = xs.shape
    n_layers = w_in.shape[0]
    my_group = 2 * lax.axis_index("x") + lax.axis_index("y")

    big = [w_in, w_out, w_gate, w_up, w_down]
    place = jnp.stack([my_group, lax.axis_index("c")]).astype(jnp.int32)
    gathered = [[cast_into_gathered(w, place, l) for w in big] for l in range(n_layers)]
    gathered[0][:1] = gather_weights(gathered[0][:1])
    cw_pad = jnp.pad(conv_w, ((0, 0), (0, 32 - CONV_WIDTH), (0, 0)))
    cw_all = allgather_small("allgather_conv_w", cw_pad.reshape(n_layers * 32, LANES))
    cw_full = cw_all.reshape(N_GROUPS, 2, n_layers, 32, LANES)[:, 0].transpose(1, 2, 0, 3).reshape(n_layers, 32, C_A)
    pw_bf16 = pool_w.astype(BF16)
    tabs = rope_tables(s)
    gmix_a, gmix_b, gmix_c = g_mix[:, :C_A], g_mix[:, C_A:C_A + C_B], g_mix[:, C_A + C_B:]
    row = lambda t: t.reshape(1, -1)

    saved = []
    (h,) = norm_fwd("norm_first", xs, None, None, g_pre_mix[0])
    xin = xs
    for l in range(n_layers):
        if l == 0:
            proj, gathered[0][1:] = mm_act_wcols("proj_gather", h, gathered[0][0], 0, F32, gathered[0][1:])
        else:
            proj = mm_act_wcols("proj", h, gathered[l][0], 0, F32)
        win_g, wout_g, wgate_g, wup_g, wdown_g = gathered[l]
        ya, conv_out = conv_fwd("conv_fwd", proj, cw_full[l], row(conv_b[l]), row(conv_ln_g[l]), row(conv_ln_b[l]),
                                row(gmix_a[l]))
        qkv_d = rope_fwd("rope_fwd", proj, tabs)
        os_, lses = [], []
        for dil, (qd, kd, vd) in zip(DILATIONS, qkv_d):
            o, lse = band_attn_fwd(f"band_attn_fwd_d{dil}", qd, kd, vd, s // dil)
            os_.append(o)
            lses.append(lse)
        yb, ybn, lse_joint = attn_combine_fwd("attn_combine_fwd", os_, lses, row(gmix_b[l]))
        yc = pool_fwd("pool_fwd", proj, pw_bf16[l], row(pool_scale[l]), row(gmix_c[l]))
        y = jnp.concatenate([ya, ybn, yc], axis=1)
        z = mm_act_wrows("mix_out", y, wout_g, 0, F32)
        x1, h2 = norm_fwd("norm_mid", xin, z, g_post_mix[l], g_pre_ffn[l])
        if l + 1 < n_layers:
            (gt, up, act), gathered[l + 1] = ffn_up("ffn_up_gather", h2, wgate_g, wup_g, 0, gathered[l + 1])
        else:
            (gt, up, act), _ = ffn_up("ffn_up", h2, wgate_g, wup_g, 0)
        f = mm_act_wrows("ffn_down", act, wdown_g, 0, F32)
        if l + 1 < n_layers:
            x2, h_next = norm_fwd("norm_next", x1, f, g_post_ffn[l], g_pre_mix[l + 1])
        else:
            (x2,), h_next = norm_fwd("norm_last", x1, f, g_post_ffn[l], None), None
        saved.append(dict(xin=xin, h=h, proj=proj, conv=conv_out, qkv_d=qkv_d, yb=yb, lse=lse_joint, y=y, z=z, x1=x1, h2=h2,
                          gt=gt, up=up, act=act, f=f))
        xin, h = x2, h_next

    dx, sq = loss_head(xin, target)
    loss = lax.psum(0.5 * jnp.sum(sq) / d, ("x", "y", "c"))

    small = {k: [None] * n_layers for k in ("cw", "cb", "lg", "lb", "pw", "ps", "gmix", "gpre_mix", "gpost_mix",
                                            "gpre_ffn", "gpost_ffn")}
    reduced = None
    df, dg = norm_bwd("norm_bwd_top", dx, None, None, None, saved[-1]["f"], g_post_ffn[n_layers - 1])
    small["gpost_ffn"][n_layers - 1] = dg
    pending = None
    for l in reversed(range(n_layers)):
        sv = saved[l]
        win_g, wout_g, wgate_g, wup_g, wdown_g = gathered[l]
        dgt, dup = ffn_down_bwd("ffn_down_bwd", df, wdown_g, 0, sv["gt"], sv["up"])
        d_wdown = mm_plain("wgrad_rows_ffn", "tn", sv["act"], df, BF16)
        if pending is None:
            dh2, _ = ffn_in_bwd("ffn_in_bwd", dgt, dup, wgate_g, wup_g, 0)
        else:
            dh2, recv = ffn_in_bwd("ffn_in_bwd_exchange", dgt, dup, wgate_g, wup_g, 0, pending[1])
            reduced = reduce_over_chips(pending[1], recv, place, pending[0], n_layers, reduced)
        d_wgate = mm_wgrad_cols("wgrad_cols_ffn", sv["h2"], dgt, wgate_g.shape[3])
        d_wup = mm_wgrad_cols("wgrad_cols_ffn", sv["h2"], dup, wup_g.shape[3])
        dx1, small["gpre_ffn"][l], dz, small["gpost_mix"][l] = norm_bwd(
            "norm_bwd_mid", dx, dh2, sv["x1"], g_pre_ffn[l], sv["z"], g_post_mix[l])
        dy = mm_act_wrows_t("mix_out_bwd", dz, wout_g, 0, F32)
        d_wout = mm_plain("wgrad_rows_mix", "tn", sv["y"], dz, BF16)
        proj = sv["proj"]
        if l == 0:
            ffn_parts = reduce_in_chip([d_wgate, d_wup, d_wdown.reshape(N_GROUPS, -1, d_wdown.shape[1])], place)
        (dag, dcw, small["cb"][l], small["lg"][l], small["lb"][l], dgx_a), ffn_recv = conv_bwd(
            "conv_bwd_exchange" if l == 0 else "conv_bwd", proj, sv["conv"], dy, cw_full[l], row(conv_ln_g[l]),
            row(conv_ln_b[l]), row(gmix_a[l]), ffn_parts if l == 0 else ())
        small["cw"][l] = dcw.reshape(32, 8, C_A).sum(axis=1)
        dybs, deltas, dgx_b = attn_combine_bwd("attn_combine_bwd", dy, sv["yb"], row(gmix_b[l]))
        dqs, dks, dvs = [], [], []
        for dil, (qd, kd, vd), do_d, lse_d, delta_d in zip(DILATIONS, sv["qkv_d"], dybs, sv["lse"], deltas):
            dqs.append(band_attn_dq(f"band_attn_dq_d{dil}", qd, kd, vd, do_d, lse_d, delta_d, s // dil))
            dk, dv = band_attn_dkv(f"band_attn_dkv_d{dil}", qd, kd, vd, do_d, lse_d, delta_d, s // dil)
            dks.append(dk)
            dvs.append(dv)
        dqkv = rope_bwd("rope_bwd", dqs, dks, dvs, tabs)
        du, small["pw"][l], small["ps"][l], dgx_c = pool_bwd("pool_bwd", proj, dy, pw_bf16[l], row(pool_scale[l]),
                                                              row(gmix_c[l]))
        small["gmix"][l] = jnp.concatenate([dgx_a, dgx_b, dgx_c], axis=1)
        dproj = jnp.concatenate([dag, dqkv, du], axis=1)
        dh1 = mm_act_wcols_t("proj_bwd", dproj, win_g, 0, F32)
        d_win = mm_wgrad_cols("wgrad_cols_proj", sv["h"], dproj, win_g.shape[3])
        if l > 0:
            dx, small["gpre_mix"][l], df, small["gpost_ffn"][l - 1] = norm_bwd(
                "norm_bwd_next", dx1, dh1, sv["xin"], g_pre_mix[l], saved[l - 1]["f"], g_post_ffn[l - 1])
        else:
            dx, small["gpre_mix"][l] = norm_bwd("norm_bwd_first", dx1, dh1, sv["xin"], g_pre_mix[l], None, None)
        d_wout = d_wout.reshape(N_GROUPS, -1, d_wout.shape[1])
        d_wdown = d_wdown.reshape(N_GROUPS, -1, d_wdown.shape[1])
        if l > 0:
            pending = (l, reduce_in_chip([d_win, d_wout, d_wgate, d_wup, d_wdown], place))
    rest_parts = reduce_in_chip([d_win, d_wout], place)
    rest_recv = exchange_chip_partials("exchange_chip_partials", rest_parts)
    into = [None] * 5 if reduced is None else reduced
    reduced = (reduce_over_chips(rest_parts, rest_recv, place, 0, n_layers, into[:2])
               + reduce_over_chips(ffn_parts, ffn_recv, place, 0, n_layers, into[2:]))

    big_grads = share_reduced_halves(reduced)
    big_m = [m_w_in, m_w_out, m_w_gate, m_w_up, m_w_down]
    big_v = [v_w_in, v_w_out, v_w_gate, v_w_up, v_w_down]
    big_upd = [elementwise("adamw", _adamw, [g, w, m, v], [F32, F32, F32])
               for g, w, m, v in zip(big_grads, big, big_m, big_v)]

    def stack8(k):
        return jnp.stack([t.sum(axis=0) if t.shape[0] == 8 and t.ndim == 2 else t for t in small[k]])

    order = ("cw", "cb", "lg", "lb", "pw", "ps", "gmix", "gpre_mix", "gpost_mix", "gpre_ffn", "gpost_ffn")
    partial = [jnp.stack(small["cw"])] + [stack8(k) for k in order[1:]]
    rep_w = [conv_b, conv_ln_g, conv_ln_b, pool_w, pool_scale, g_mix, g_pre_mix, g_post_mix, g_pre_ffn, g_post_ffn]
    rep_m = [m_conv_b, m_conv_ln_g, m_conv_ln_b, m_pool_w, m_pool_scale, m_g_mix, m_g_pre_mix, m_g_post_mix,
             m_g_pre_ffn, m_g_post_ffn]
    rep_v = [v_conv_b, v_conv_ln_g, v_conv_ln_b, v_pool_w, v_pool_scale, v_g_mix, v_g_pre_mix, v_g_post_mix,
             v_g_pre_ffn, v_g_post_ffn]
    packed = _pack(partial)
    rows = packed.shape[0]
    gall = allgather_small("allgather_small_grads", packed).reshape(8, rows, LANES)
    cw_hole = jnp.zeros((n_layers, 32, C_A), F32)
    rep_out = small_update("small_update", gall, _pack([cw_hole] + rep_w), _pack([cw_hole] + rep_m),
                           _pack([cw_hole] + rep_v))
    shapes = [cw_hole.shape] + [t.shape for t in rep_w]
    rep_g, rep_d, rep_mn, rep_vn = ([t for t in _unpack(buf, shapes)[1:]] for buf in rep_out)
    cw_rows = n_layers * 32
    gall_cw = gall[:, :cw_rows * C_A // LANES].reshape(8, cw_rows, C_A // LANES, LANES)
    gall_cw = lax.dynamic_index_in_dim(gall_cw, my_group, axis=2, keepdims=False)
    pad_cw = lambda t: jnp.pad(t, ((0, 0), (0, 32 - CONV_WIDTH), (0, 0))).reshape(cw_rows, LANES)
    cw_out = small_update("conv_w_update", gall_cw, pad_cw(conv_w), pad_cw(m_conv_w), pad_cw(v_conv_w))
    cw_g, cw_d, cw_mn, cw_vn = (t.reshape(n_layers, 32, LANES)[:, :CONV_WIDTH] for t in cw_out)

    def assemble(bigs, cw, reps):
        return [bigs[0], cw] + list(reps[:6]) + [bigs[1]] + list(reps[6:]) + list(bigs[2:])

    grads = assemble(big_grads, cw_g, rep_g)
    deltas = assemble([u[0] for u in big_upd], cw_d, rep_d)
    new_m = assemble([u[1] for u in big_upd], cw_mn, rep_mn)
    new_v = assemble([u[2] for u in big_upd], cw_vn, rep_vn)
    return (loss, dx[None], *grads, *deltas, *new_m, *new_v)
```

```python
import functools
import math

import jax
import jax.numpy as jnp
from jax import lax
from jax.experimental import pallas as pl
from jax.experimental.pallas import tpu as pltpu

F32 = jnp.float32
BF16 = jnp.bfloat16
MESH = pl.DeviceIdType.MESH

EPS = 1e-6
NEG = -1e30
C_A = 512
C_B = 1024
C_C = 512
HEAD_DIM = 64
POOL_WINDOWS = (2, 4, 8, 16)
C_G = 128
CONV_WIDTH = 31
CONV_HALF = 15
DILATIONS = (1, 4, 16)
ATT_HALF = 64
ATT_TILES = (512, 256, 128)
ROT_DIM = 16
ROPE_THETA = 500000.0
ADAM_LR, ADAM_B1, ADAM_B2, ADAM_EPS, ADAM_WD, ADAM_STEP = 0.001, 0.9, 0.999, 1e-08, 0.01, 10
N_GROUPS = 4
HALO = 32
LANES = 128
VMEM_LIMIT = 48 * 1024 * 1024


def _params(*sem):
    return pltpu.CompilerParams(dimension_semantics=sem, vmem_limit_bytes=VMEM_LIMIT)


def _pick(n, prefs):
    for p in prefs:
        if p <= n and n % p == 0:
            return p
    return n


def _rms_scale(t):
    return lax.rsqrt(jnp.mean(t * t, axis=-1, keepdims=True) + EPS)


def _rms_bwd(d, x, g):
    r = _rms_scale(x)
    u = d * g
    dx = r * u - x * (r * r * r) * jnp.mean(u * x, axis=-1, keepdims=True)
    return dx, d * x * r


def _sum8(t):
    rows, cols = t.shape
    return jnp.sum(t.reshape(rows // 8, 8, cols), axis=0)


def _dot(a, b, mode):
    dn = {"nn": (((1,), (0,)), ((), ())), "nt": (((1,), (1,)), ((), ())), "tn": (((0,), (0,)), ((), ()))}[mode]
    return lax.dot_general(a, b, dn, preferred_element_type=F32)


def _row_spec(tm, cols, col_block=0):
    return pl.BlockSpec((tm, cols), lambda i, cb=col_block: (i, cb))


def _const_spec(shape):
    return pl.BlockSpec(shape, lambda i: tuple(0 for _ in shape))


def norm_fwd(name, x, z, g_post, g_next):
    s, d = x.shape
    tm = _pick(s, (512, 256, 128))
    has_z, has_h = z is not None, g_next is not None

    def body(*refs):
        refs = list(refs)
        x_ref = refs.pop(0)
        xn = x_ref[...]
        if has_z:
            z_ref, gp_ref = refs.pop(0), refs.pop(0)
            zz = z_ref[...]
            xn = xn + zz * _rms_scale(zz) * gp_ref[...]
        if has_h:
            gn_ref = refs.pop(0)
        if has_z:
            refs.pop(0)[...] = xn
        if has_h:
            refs.pop(0)[...] = (xn * _rms_scale(xn) * gn_ref[...]).astype(BF16)

    ins, specs, outs, ospecs = [x], [_row_spec(tm, d)], [], []
    if has_z:
        ins += [z, g_post.reshape(1, d)]
        specs += [_row_spec(tm, d), _const_spec((1, d))]
        outs.append(jax.ShapeDtypeStruct((s, d), F32))
        ospecs.append(_row_spec(tm, d))
    if has_h:
        ins.append(g_next.reshape(1, d))
        specs.append(_const_spec((1, d)))
        outs.append(jax.ShapeDtypeStruct((s, d), BF16))
        ospecs.append(_row_spec(tm, d))
    res = pl.pallas_call(body, name=name, grid=(s // tm,), in_specs=specs, out_specs=ospecs, out_shape=outs,
                         compiler_params=_params("parallel"))(*ins)
    return list(res)


def norm_bwd(name, dres, dh, xin, g_pre, zin, g_post):
    s, d = dres.shape
    tm = _pick(s, (256, 128))
    has_pre, has_post = dh is not None, zin is not None

    def body(*refs):
        refs = list(refs)
        i = pl.program_id(0)
        dx = refs.pop(0)[...]
        if has_pre:
            dh_ref, x_ref, g_ref = refs.pop(0), refs.pop(0), refs.pop(0)
            ddx, dg = _rms_bwd(dh_ref[...].astype(F32), x_ref[...], g_ref[...])
            dx = dx + ddx
        if has_post:
            z_ref, gp_ref = refs.pop(0), refs.pop(0)
            dz, dgp = _rms_bwd(dx, z_ref[...], gp_ref[...])
        if has_pre:
            refs.pop(0)[...] = dx
            dg_ref = refs.pop(0)

            @pl.when(i == 0)
            def _():
                dg_ref[...] = jnp.zeros_like(dg_ref)

            dg_ref[...] += _sum8(dg)
        if has_post:
            refs.pop(0)[...] = dz.astype(BF16)
            dgp_ref = refs.pop(0)

            @pl.when(i == 0)
            def _():
                dgp_ref[...] = jnp.zeros_like(dgp_ref)

            dgp_ref[...] += _sum8(dgp)

    ins, specs, outs, ospecs = [dres], [_row_spec(tm, d)], [], []
    if has_pre:
        ins += [dh, xin, g_pre.reshape(1, d)]
        specs += [_row_spec(tm, d), _row_spec(tm, d), _const_spec((1, d))]
        outs += [jax.ShapeDtypeStruct((s, d), F32), jax.ShapeDtypeStruct((8, d), F32)]
        ospecs += [_row_spec(tm, d), _const_spec((8, d))]
    if has_post:
        ins += [zin, g_post.reshape(1, d)]
        specs += [_row_spec(tm, d), _const_spec((1, d))]
        outs += [jax.ShapeDtypeStruct((s, d), BF16), jax.ShapeDtypeStruct((8, d), F32)]
        ospecs += [_row_spec(tm, d), _const_spec((8, d))]
    res = pl.pallas_call(body, name=name, grid=(s // tm,), in_specs=specs, out_specs=ospecs, out_shape=outs,
                         compiler_params=_params("arbitrary"))(*ins)
    return list(res)


def loss_head(x, target):
    s, d = x.shape
    tm = _pick(s, (512, 256, 128))

    def body(x_ref, t_ref, dx_ref, sq_ref):
        i = pl.program_id(0)
        e = x_ref[...] - t_ref[...]
        dx_ref[...] = e * (1.0 / d)

        @pl.when(i == 0)
        def _():
            sq_ref[...] = jnp.zeros_like(sq_ref)

        sq_ref[...] += _sum8(e * e)

    return pl.pallas_call(body, name="loss_head", grid=(s // tm,),
                          in_specs=[_row_spec(tm, d), _row_spec(tm, d)],
                          out_specs=[_row_spec(tm, d), _const_spec((8, d))],
                          out_shape=[jax.ShapeDtypeStruct((s, d), F32), jax.ShapeDtypeStruct((8, d), F32)],
                          compiler_params=_params("arbitrary"))(x, target)


def matmul(name, mode, a, b, *, grid, tk_steps, a_spec, b_spec, o_spec, out_shape, acc_shape, b_flat=None, gather=()):
    nk = tk_steps
    n = len(gather)

    def rhs(b_ref):
        return b_ref[...] if b_flat is None else b_ref[...].reshape(b_flat)

    def body(a_ref, b_ref, *rest):
        o_ref, scratch = rest[n], rest[2 * n + 1:]
        pid = [pl.program_id(ax) for ax in range(3)]
        if n:
            start, finish = _gather_plan(rest[n + 1:2 * n + 1], *scratch[-2:])
            pl.when((pid[0] == 0) & (pid[1] == 0) & (pid[2] == 0))(start)
        if nk == 1:
            o_ref[...] = _dot(a_ref[...], rhs(b_ref), mode).astype(o_ref.dtype)
        else:
            acc = scratch[0]

            @pl.when(pid[2] == 0)
            def _():
                acc[...] = jnp.zeros_like(acc)

            acc[...] += _dot(a_ref[...], rhs(b_ref), mode)

            @pl.when(pid[2] == nk - 1)
            def _():
                o_ref[...] = acc[...].astype(o_ref.dtype)
        if n:
            pl.when((pid[0] == grid[0] - 1) & (pid[1] == grid[1] - 1) & (pid[2] == grid[2] - 1))(finish)

    acc_scratch = [] if nk == 1 else [pltpu.VMEM(acc_shape, F32)]
    if not n:
        return pl.pallas_call(body, name=name, grid=grid, in_specs=[a_spec, b_spec], out_specs=o_spec,
                              out_shape=out_shape, scratch_shapes=acc_scratch,
                              compiler_params=_params("parallel", "parallel", "arbitrary"))(a, b)
    res = pl.pallas_call(body, name=name, grid=grid, in_specs=[a_spec, b_spec] + [ANY] * n,
                         out_specs=[o_spec] + [ANY] * n,
                         out_shape=[out_shape] + [jax.ShapeDtypeStruct(t.shape, t.dtype) for t in gather],
                         input_output_aliases={2 + i: 1 + i for i in range(n)},
                         scratch_shapes=acc_scratch + _gather_sems(n),
                         compiler_params=_params("arbitrary", "arbitrary", "arbitrary"))(a, b, *gather)
    return res[0], list(res[1:])


TM_PREFS = (1024, 512, 256, 128)
TW_PREFS = (1408, 1152, 1024, 512, 384, 256, 128)
TK_PREFS = (512, 384, 256, 128)
FULL_K_PREFS = (2048, 1024) + TK_PREFS


def mm_act_wcols(name, a, wg, layer, out_dtype, gather=()):
    s, k = a.shape
    g, _, _, ng = wg.shape
    tm, tk, tn = _pick(s, TM_PREFS), _pick(k, FULL_K_PREFS), _pick(ng, TW_PREFS)
    per = ng // tn
    return matmul(name, "nn", a, wg, grid=(g * per, s // tm, k // tk), tk_steps=k // tk,
                  a_spec=pl.BlockSpec((tm, tk), lambda j, i, kk: (i, kk)),
                  b_spec=pl.BlockSpec((None, None, tk, tn), lambda j, i, kk: (j // per, layer, kk, j % per)),
                  o_spec=pl.BlockSpec((tm, tn), lambda j, i, kk: (i, j)),
                  out_shape=jax.ShapeDtypeStruct((s, g * ng), out_dtype), acc_shape=(tm, tn), gather=gather)


def mm_act_wcols_t(name, a, wg, layer, out_dtype):
    s, n = a.shape
    g, _, k, ng = wg.shape
    tm, tn, tk = _pick(s, TM_PREFS), _pick(k, TM_PREFS), _pick(ng, (1152, 384, 128))
    per = ng // tk
    return matmul(name, "nt", a, wg, grid=(s // tm, k // tn, g * per), tk_steps=g * per,
                  a_spec=pl.BlockSpec((tm, tk), lambda i, j, kk: (i, kk)),
                  b_spec=pl.BlockSpec((None, None, tn, tk), lambda i, j, kk: (kk // per, layer, j, kk % per)),
                  o_spec=pl.BlockSpec((tm, tn), lambda i, j, kk: (i, j)),
                  out_shape=jax.ShapeDtypeStruct((s, k), out_dtype), acc_shape=(tm, tn))


def mm_act_wrows(name, a, wr, layer, out_dtype):
    s, k = a.shape
    g, _, kg, n = wr.shape
    tm, tn, tk = _pick(s, TM_PREFS), _pick(n, TM_PREFS), _pick(kg, (1408, 512, 256, 128))
    if k <= FULL_K_PREFS[0]:
        return matmul(name, "nn", a, wr, grid=(n // tn, s // tm, 1), tk_steps=1,
                      a_spec=pl.BlockSpec((tm, k), lambda j, i, kk: (i, 0)),
                      b_spec=pl.BlockSpec((g, None, kg, tn), lambda j, i, kk: (0, layer, 0, j)),
                      o_spec=pl.BlockSpec((tm, tn), lambda j, i, kk: (i, j)),
                      out_shape=jax.ShapeDtypeStruct((s, n), out_dtype), acc_shape=(tm, tn), b_flat=(k, tn))
    per = kg // tk
    return matmul(name, "nn", a, wr, grid=(s // tm, n // tn, g * per), tk_steps=g * per,
                  a_spec=pl.BlockSpec((tm, tk), lambda i, j, kk: (i, kk)),
                  b_spec=pl.BlockSpec((None, None, tk, tn), lambda i, j, kk: (kk // per, layer, kk % per, j)),
                  o_spec=pl.BlockSpec((tm, tn), lambda i, j, kk: (i, j)),
                  out_shape=jax.ShapeDtypeStruct((s, n), out_dtype), acc_shape=(tm, tn))


def mm_act_wrows_t(name, a, wr, layer, out_dtype):
    s, n = a.shape
    g, _, kg, _ = wr.shape
    tm, tn, tk = _pick(s, TM_PREFS), _pick(kg, (1408, 512, 256, 128)), _pick(n, FULL_K_PREFS)
    per = kg // tn
    return matmul(name, "nt", a, wr, grid=(g * per, s // tm, n // tk), tk_steps=n // tk,
                  a_spec=pl.BlockSpec((tm, tk), lambda j, i, kk: (i, kk)),
                  b_spec=pl.BlockSpec((None, None, tn, tk), lambda j, i, kk: (j // per, layer, j % per, kk)),
                  o_spec=pl.BlockSpec((tm, tn), lambda j, i, kk: (i, j)),
                  out_shape=jax.ShapeDtypeStruct((s, g * kg), out_dtype), acc_shape=(tm, tn))


def mm_wgrad_cols(name, a, dy, ng):
    s, k = a.shape
    n = dy.shape[1]
    g = n // ng
    tm = _pick(k, TM_PREFS)
    tk = _pick(s, TM_PREFS)
    tn = _pick(ng, TW_PREFS)
    per = ng // tn
    return matmul(name, "tn", a, dy, grid=(k // tm, g * per, s // tk), tk_steps=s // tk,
                  a_spec=pl.BlockSpec((tk, tm), lambda i, j, kk: (kk, i)),
                  b_spec=pl.BlockSpec((tk, tn), lambda i, j, kk: (kk, j)),
                  o_spec=pl.BlockSpec((None, tm, tn), lambda i, j, kk: (j // per, i, j % per)),
                  out_shape=jax.ShapeDtypeStruct((g, k, ng), BF16), acc_shape=(tm, tn))


def mm_plain(name, mode, a, b, out_dtype):
    if mode == "nn":
        (m, k), n = a.shape, b.shape[1]
    elif mode == "nt":
        (m, k), n = a.shape, b.shape[0]
    else:
        (k, m), n = a.shape, b.shape[1]
    tm = _pick(m, (1408,) + TM_PREFS)
    tn = _pick(n, TM_PREFS)
    tk = _pick(k, TM_PREFS)
    a_spec = (pl.BlockSpec((tk, tm), lambda i, j, kk: (kk, i)) if mode == "tn"
              else pl.BlockSpec((tm, tk), lambda i, j, kk: (i, kk)))
    b_spec = (pl.BlockSpec((tn, tk), lambda i, j, kk: (j, kk)) if mode == "nt"
              else pl.BlockSpec((tk, tn), lambda i, j, kk: (kk, j)))
    return matmul(name, mode, a, b, grid=(m // tm, n // tn, k // tk), tk_steps=k // tk, a_spec=a_spec, b_spec=b_spec,
                  o_spec=pl.BlockSpec((tm, tn), lambda i, j, kk: (i, j)),
                  out_shape=jax.ShapeDtypeStruct((m, n), out_dtype), acc_shape=(tm, tn))


def _silu_parts(gt):
    sg = jax.nn.sigmoid(gt)
    return gt * sg, sg


def ffn_up(name, h, wgate, wup, layer, gather=()):
    s, d = h.shape
    g, _, _, ng = wgate.shape
    assert d <= FULL_K_PREFS[0]
    tm = _pick(s, (256, 128))
    tn = _pick(ng, TW_PREFS)
    per = ng // tn

    n = len(gather)
    grid = (g * per, s // tm)

    def body(h_ref, wg_ref, wu_ref, *rest):
        gt_ref, up_ref, act_ref = rest[n:n + 3]
        if n:
            start, finish = _gather_plan(rest[n + 3:2 * n + 3], *rest[2 * n + 3:])
            pl.when((pl.program_id(0) == 0) & (pl.program_id(1) == 0))(start)
        hh = h_ref[...]
        gt = _dot(hh, wg_ref[...], "nn")
        up = _dot(hh, wu_ref[...], "nn")
        gt_ref[...] = gt.astype(BF16)
        up_ref[...] = up.astype(BF16)
        act_ref[...] = (_silu_parts(gt)[0] * up).astype(BF16)
        if n:
            pl.when((pl.program_id(0) == grid[0] - 1) & (pl.program_id(1) == grid[1] - 1))(finish)

    wspec = pl.BlockSpec((None, None, d, tn), lambda j, i: (j // per, layer, 0, j % per))
    ospec = pl.BlockSpec((tm, tn), lambda j, i: (i, j))
    osh = jax.ShapeDtypeStruct((s, g * ng), BF16)
    res = pl.pallas_call(body, name=name, grid=grid,
                         in_specs=[pl.BlockSpec((tm, d), lambda j, i: (i, 0)), wspec, wspec] + [ANY] * n,
                         out_specs=[ospec, ospec, ospec] + [ANY] * n,
                         out_shape=[osh, osh, osh] + [jax.ShapeDtypeStruct(t.shape, t.dtype) for t in gather],
                         input_output_aliases={3 + a: 3 + a for a in range(n)},
                         scratch_shapes=_gather_sems(n) if n else [],
                         compiler_params=_params(*(["arbitrary"] * 2 if n else ["parallel"] * 2)))(h, wgate, wup, *gather)
    return res[:3], list(res[3:])


def ffn_down_bwd(name, df, wdown, layer, gt, up):
    s, d = df.shape
    g, _, fg, _ = wdown.shape
    f = g * fg
    assert d <= FULL_K_PREFS[0]
    tm = _pick(s, (256, 128))
    tn = _pick(fg, TW_PREFS)
    per = fg // tn

    def body(df_ref, wd_ref, gt_ref, up_ref, dgt_ref, dup_ref):
        da = _dot(df_ref[...], wd_ref[...], "nt")
        gt = gt_ref[...].astype(F32)
        up = up_ref[...].astype(F32)
        silu, sg = _silu_parts(gt)
        dgt_ref[...] = (da * up * (sg * (1.0 + gt * (1.0 - sg)))).astype(BF16)
        dup_ref[...] = (da * silu).astype(BF16)

    ospec = pl.BlockSpec((tm, tn), lambda j, i: (i, j))
    osh = jax.ShapeDtypeStruct((s, f), BF16)
    return pl.pallas_call(body, name=name, grid=(f // tn, s // tm),
                          in_specs=[pl.BlockSpec((tm, d), lambda j, i: (i, 0)),
                                    pl.BlockSpec((None, None, tn, d), lambda j, i: (j // per, layer, j % per, 0)),
                                    ospec, ospec],
                          out_specs=[ospec, ospec], out_shape=[osh, osh],
                          compiler_params=_params("parallel", "parallel"))(df, wdown, gt, up)


def ffn_in_bwd(name, dgt, dup, wgate, wup, layer, exchange=()):
    s, f = dgt.shape
    g, _, d, ng = wgate.shape
    tm = _pick(s, (512, 256, 128))
    tn = _pick(d, TM_PREFS)
    tk = _pick(ng, TW_PREFS)
    per = ng // tk
    nk = g * per

    n = len(exchange)
    grid = (s // tm, d // tn, nk)

    def body(dg_ref, du_ref, wg_ref, wu_ref, *rest):
        o_ref, acc = rest[n], rest[2 * n + 1]
        kk = pl.program_id(2)
        if n:
            start, finish = _chip_exchange_plan(rest[:n], rest[n + 1:2 * n + 1], *rest[2 * n + 2:])
            pl.when((pl.program_id(0) == 0) & (pl.program_id(1) == 0) & (kk == 0))(start)

        @pl.when(kk == 0)
        def _():
            acc[...] = jnp.zeros_like(acc)

        acc[...] += _dot(dg_ref[...], wg_ref[...], "nt") + _dot(du_ref[...], wu_ref[...], "nt")

        @pl.when(kk == nk - 1)
        def _():
            o_ref[...] = acc[...]

        if n:
            pl.when((pl.program_id(0) == grid[0] - 1) & (pl.program_id(1) == grid[1] - 1) & (kk == nk - 1))(finish)

    aspec = pl.BlockSpec((tm, tk), lambda i, j, kk: (i, kk))
    wspec = pl.BlockSpec((None, None, tn, tk), lambda i, j, kk: (kk // per, layer, j, kk % per))
    res = pl.pallas_call(body, name=name, grid=grid, in_specs=[aspec, aspec, wspec, wspec] + [ANY] * n,
                         out_specs=[pl.BlockSpec((tm, tn), lambda i, j, kk: (i, j))] + [ANY] * n,
                         out_shape=[jax.ShapeDtypeStruct((s, d), F32)] + _chip_exchange_shapes(exchange),
                         scratch_shapes=[pltpu.VMEM((tm, tn), F32)] + (_chip_exchange_sems(n) if n else []),
                         compiler_params=_params(*(["arbitrary"] * 3 if n else ["parallel", "parallel", "arbitrary"])))(
                             dgt, dup, wgate, wup, *exchange)
    return res[0], list(res[1:])


def _halo_specs(tm, s, cols, cb, halo=HALO):
    r = tm // halo
    last = s // halo - 1
    return [pl.BlockSpec((halo, cols), lambda i: (jnp.maximum(i * r - 1, 0), cb)),
            pl.BlockSpec((tm, cols), lambda i: (i, cb)),
            pl.BlockSpec((halo, cols), lambda i: (jnp.minimum((i + 1) * r, last), cb))]


def _edge_masked(prev_ref, next_ref, i, nt):
    return jnp.where(i > 0, prev_ref[...], 0.0), jnp.where(i < nt - 1, next_ref[...], 0.0)


def _glu(a, gate):
    return a * jax.nn.sigmoid(gate)


def _conv_post(conv, lg, lb, gm):
    mu = jnp.mean(conv, axis=-1, keepdims=True)
    xc = conv - mu
    rs = lax.rsqrt(jnp.mean(xc * xc, axis=-1, keepdims=True) + EPS)
    xhat = xc * rs
    yl = xhat * lg + lb
    ya, sg = _silu_parts(yl)
    return xhat, rs, yl, sg, ya


def _fill_glu(hg, ap, am, an, gp, gm_, gn, i, nt, tm):
    a0, a2 = _edge_masked(ap, an, i, nt)
    g0, g2 = _edge_masked(gp, gn, i, nt)
    hg[pl.ds(0, HALO), :] = _glu(a0, g0)
    hg[pl.ds(HALO, tm), :] = _glu(am[...], gm_[...])
    hg[pl.ds(HALO + tm, HALO), :] = _glu(a2, g2)


def conv_fwd(name, proj, cw, cb, lg, lb, gmix):
    s = proj.shape[0]
    tm = _pick(s, (256, 128))
    nt = s // tm

    def body(ap, am, an, gp, gm_, gn, cw_ref, cb_ref, lg_ref, lb_ref, gx_ref, o_ref, conv_ref, hg):
        i = pl.program_id(0)
        _fill_glu(hg, ap, am, an, gp, gm_, gn, i, nt, tm)
        acc = jnp.zeros((tm, C_A), F32) + cb_ref[...]
        for t in range(CONV_WIDTH):
            acc = acc + cw_ref[pl.ds(t, 1), :] * hg[pl.ds(HALO - CONV_HALF + t, tm), :]
        conv_ref[...] = acc
        ya = _conv_post(acc, lg_ref[...], lb_ref[...], None)[4]
        o_ref[...] = (ya * _rms_scale(ya) * gx_ref[...]).astype(BF16)

    vec = _const_spec((1, C_A))
    return pl.pallas_call(body, name=name, grid=(nt,),
                          in_specs=_halo_specs(tm, s, C_A, 0) + _halo_specs(tm, s, C_A, 1)
                          + [_const_spec((32, C_A)), vec, vec, vec, vec],
                          out_specs=[_row_spec(tm, C_A), _row_spec(tm, C_A)],
                          out_shape=[jax.ShapeDtypeStruct((s, C_A), BF16), jax.ShapeDtypeStruct((s, C_A), F32)],
                          scratch_shapes=[pltpu.VMEM((tm + 2 * HALO, C_A), F32)],
                          compiler_params=_params("parallel"))(proj, proj, proj, proj, proj, proj, cw, cb, lg, lb, gmix)


def conv_bwd(name, proj, conv_out, dy, cw, lg, lb, gmix, exchange=()):
    s = proj.shape[0]
    tm = _pick(s, (256, 128))
    nt = s // tm
    te = tm + HALO
    off = HALO // 2

    def extended(prev_ref, main_ref, next_ref, i):
        p0, p2 = _edge_masked(prev_ref, next_ref, i, nt)
        return jnp.concatenate([p0[off:], main_ref[...], p2[:off]], axis=0)

    n = len(exchange)

    def body(ap, am, an, gp, gm_, gn, cp, cm, cn, dp, dm, dn, cw_ref, lg_ref, lb_ref, gx_ref, *rest):
        dag_ref, dcw_ref, dcb_ref, dlg_ref, dlb_ref, dgx_ref = rest[n:n + 6]
        hg, dc = rest[2 * n + 6:2 * n + 8]
        i = pl.program_id(0)
        if n:
            start, finish = _chip_exchange_plan(rest[:n], rest[n + 6:2 * n + 6], *rest[2 * n + 8:])
            pl.when(i == 0)(start)
        _fill_glu(hg, ap, am, an, gp, gm_, gn, i, nt, tm)
        lg, lb, gx = lg_ref[...], lb_ref[...], gx_ref[...]
        xhat, rs, yl, sg, ya = _conv_post(extended(cp, cm, cn, i), lg, lb, gx)
        dout = extended(dp, dm, dn, i)
        dya, dgx_rows = _rms_bwd(dout, ya, gx)
        dyl = dya * (sg * (1.0 + yl * (1.0 - sg)))
        dxh = dyl * lg
        dconv = rs * (dxh - jnp.mean(dxh, axis=-1, keepdims=True)
                      - xhat * jnp.mean(dxh * xhat, axis=-1, keepdims=True))
        dc[...] = dconv

        @pl.when(i == 0)
        def _():
            for r in (dcw_ref, dcb_ref, dlg_ref, dlb_ref, dgx_ref):
                r[...] = jnp.zeros_like(r)

        dcb_ref[...] += _sum8(dconv[off:off + tm])
        dlg_ref[...] += _sum8((dyl * xhat)[off:off + tm])
        dlb_ref[...] += _sum8(dyl[off:off + tm])
        dgx_ref[...] += _sum8(dgx_rows[off:off + tm])
        dcm = dconv[off:off + tm]
        dhg = jnp.zeros((tm, C_A), F32)
        for t in range(CONV_WIDTH):
            dhg = dhg + cw_ref[pl.ds(t, 1), :] * dc[pl.ds(HALO - 1 - t, tm), :]
            dcw_ref[pl.ds(8 * t, 8), :] += _sum8(dcm * hg[pl.ds(HALO - CONV_HALF + t, tm), :])
        a, gate = am[...], gm_[...]
        sgate = jax.nn.sigmoid(gate)
        dag_ref[:, pl.ds(0, C_A)] = (dhg * sgate).astype(BF16)
        dag_ref[:, pl.ds(C_A, C_A)] = (dhg * a * sgate * (1.0 - sgate)).astype(BF16)
        if n:
            pl.when(i == nt - 1)(finish)

    vec = _const_spec((1, C_A))
    acc8 = _const_spec((8, C_A))
    sh8 = jax.ShapeDtypeStruct((8, C_A), F32)
    halo = functools.partial(_halo_specs, tm, s, C_A)
    res = pl.pallas_call(body, name=name, grid=(nt,),
                         in_specs=halo(0) + halo(1) + halo(0) + halo(0) + [_const_spec((32, C_A)), vec, vec, vec]
                         + [ANY] * n,
                         out_specs=[_row_spec(tm, 2 * C_A), _const_spec((32 * 8, C_A)), acc8, acc8, acc8, acc8] + [ANY] * n,
                         out_shape=[jax.ShapeDtypeStruct((s, 2 * C_A), BF16), jax.ShapeDtypeStruct((32 * 8, C_A), F32),
                                    sh8, sh8, sh8, sh8] + _chip_exchange_shapes(exchange),
                         scratch_shapes=[pltpu.VMEM((tm + 2 * HALO, C_A), F32), pltpu.VMEM((te, C_A), F32)]
                         + (_chip_exchange_sems(n) if n else []),
                         compiler_params=_params("arbitrary"))(proj, proj, proj, proj, proj, proj, conv_out, conv_out,
                                                               conv_out, dy, dy, dy, cw, lg, lb, gmix, *exchange)
    return list(res[:6]), list(res[6:])


def _pool_counts(pos, win, s):
    lo = jnp.maximum(pos - win // 2, 0)
    hi = jnp.minimum(pos + win - win // 2, s)
    return jnp.maximum(hi - lo, 1).astype(F32)


def _pooled(uext, base, rows, pos, s):
    outs = []
    for gi, win in enumerate(POOL_WINDOWS):
        lanes = pl.ds(gi * C_G, C_G)
        acc = jnp.zeros((rows, C_G), F32)
        for o in range(-(win // 2), win - win // 2):
            acc = acc + uext[pl.ds(base + o, rows), lanes]
        outs.append(acc / _pool_counts(pos, win, s) - uext[pl.ds(base, rows), lanes])
    return outs


def pool_fwd(name, proj, pw, scale, gmix):
    s = proj.shape[0]
    tm = _pick(s, (256, 128))
    nt = s // tm
    ucol = (proj.shape[1] - C_C) // C_C

    def body(up, um, un, pw_ref, sc_ref, gx_ref, o_ref, uext):
        i = pl.program_id(0)
        u0, u2 = _edge_masked(up, un, i, nt)
        uext[pl.ds(0, HALO), :] = u0
        uext[pl.ds(HALO, tm), :] = um[...]
        uext[pl.ds(HALO + tm, HALO), :] = u2
        pos = i * tm + lax.broadcasted_iota(jnp.int32, (tm, 1), 0)
        pooled = _pooled(uext, HALO, tm, pos, s)
        mixed = jnp.concatenate([_dot(pooled[g].astype(BF16), pw_ref[g], "nn") for g in range(4)], axis=1)
        yc = mixed * sc_ref[...]
        o_ref[...] = (yc * _rms_scale(yc) * gx_ref[...]).astype(BF16)

    vec = _const_spec((1, C_C))
    return pl.pallas_call(body, name=name, grid=(nt,),
                          in_specs=_halo_specs(tm, s, C_C, ucol) + [_const_spec((4, C_G, C_G)), vec, vec],
                          out_specs=_row_spec(tm, C_C), out_shape=jax.ShapeDtypeStruct((s, C_C), BF16),
                          scratch_shapes=[pltpu.VMEM((tm + 2 * HALO, C_C), F32)],
                          compiler_params=_params("parallel"))(proj, proj, proj, pw, scale, gmix)


def pool_bwd(name, proj, dy, pw, scale, gmix):
    s = proj.shape[0]
    tm = _pick(s, (256, 128))
    nt = s // tm
    te = tm + HALO
    off = HALO // 2
    ucol = (proj.shape[1] - C_C) // C_C
    dcol = (dy.shape[1] - C_C) // C_C

    def body(up, um, un, dp, dm, dn, pw_ref, sc_ref, gx_ref, du_ref, dpw_ref, dsc_ref, dgx_ref, uext, dyx, qs, dps):
        i = pl.program_id(0)
        u0, u2 = _edge_masked(up, un, i, nt)
        uext[pl.ds(0, HALO), :] = u0
        uext[pl.ds(HALO, tm), :] = um[...]
        uext[pl.ds(HALO + tm, HALO), :] = u2
        d0, d2 = _edge_masked(dp, dn, i, nt)
        dyx[pl.ds(0, HALO), :] = d0
        dyx[pl.ds(HALO, tm), :] = dm[...]
        dyx[pl.ds(HALO + tm, HALO), :] = d2
        pos = i * tm - off + lax.broadcasted_iota(jnp.int32, (te, 1), 0)
        pooled = _pooled(uext, off, te, pos, s)
        mixed = jnp.concatenate([_dot(pooled[g].astype(BF16), pw_ref[g], "nn") for g in range(4)], axis=1)
        sc = sc_ref[...]
        yc = mixed * sc
        dyc, dgx_rows = _rms_bwd(dyx[pl.ds(off, te), :], yc, gx_ref[...])
        dmixed = dyc * sc

        @pl.when(i == 0)
        def _():
            for r in (dpw_ref, dsc_ref, dgx_ref):
                r[...] = jnp.zeros_like(r)

        dsc_ref[...] += _sum8((dyc * mixed)[off:off + tm])
        dgx_ref[...] += _sum8(dgx_rows[off:off + tm])
        for gi, win in enumerate(POOL_WINDOWS):
            lanes = pl.ds(gi * C_G, C_G)
            dmg = dmixed[:, gi * C_G:(gi + 1) * C_G].astype(BF16)
            dpw_ref[gi] += _dot(pooled[gi][off:off + tm].astype(BF16), dmg[off:off + tm], "tn")
            dpl = _dot(dmg, pw_ref[gi], "nt")
            dps[:, lanes] = dpl
            qs[:, lanes] = dpl / _pool_counts(pos, win, s)
        for gi, win in enumerate(POOL_WINDOWS):
            lanes = pl.ds(gi * C_G, C_G)
            acc = jnp.zeros((tm, C_G), F32) - dps[pl.ds(off, tm), lanes]
            for o in range(-(win // 2) + 1, win // 2 + 1):
                acc = acc + qs[pl.ds(off + o, tm), lanes]
            du_ref[:, lanes] = acc.astype(BF16)

    vec = _const_spec((1, C_C))
    acc8 = _const_spec((8, C_C))
    sh8 = jax.ShapeDtypeStruct((8, C_C), F32)
    return pl.pallas_call(body, name=name, grid=(nt,),
                          in_specs=_halo_specs(tm, s, C_C, ucol) + _halo_specs(tm, s, C_C, dcol)
                          + [_const_spec((4, C_G, C_G)), vec, vec],
                          out_specs=[_row_spec(tm, C_C), _const_spec((4, C_G, C_G)), acc8, acc8],
                          out_shape=[jax.ShapeDtypeStruct((s, C_C), BF16), jax.ShapeDtypeStruct((4, C_G, C_G), F32),
                                     sh8, sh8],
                          scratch_shapes=[pltpu.VMEM((tm + 2 * HALO, C_C), F32), pltpu.VMEM((tm + 2 * HALO, C_C), F32),
                                          pltpu.VMEM((te, C_C), F32), pltpu.VMEM((te, C_C), F32)],
                          compiler_params=_params("arbitrary"))(proj, proj, proj, dy, dy, dy, pw, scale, gmix)


def rope_tables(s):
    pos = jnp.arange(s, dtype=F32)
    inv = ROPE_THETA ** (-jnp.arange(0, ROT_DIM, 2, dtype=F32) / ROT_DIM)
    ang = pos[:, None] * inv[None, :]
    cos, sin = jnp.cos(ang), jnp.sin(ang)
    half = ROT_DIM // 2
    rest = HEAD_DIM - ROT_DIM
    c = jnp.concatenate([cos, cos, jnp.ones((s, rest), F32)], axis=1)
    sa = jnp.concatenate([-sin, jnp.zeros((s, HEAD_DIM - half), F32)], axis=1)
    sb = jnp.concatenate([jnp.zeros((s, half), F32), sin, jnp.zeros((s, rest), F32)], axis=1)
    return tuple(jnp.concatenate([t, t], axis=1) for t in (c, sa, sb))


DIL_TILE = 256
N_CHUNKS = C_B // LANES


def _dilated_shape(s, d, dtype):
    return jax.ShapeDtypeStruct((s, C_B) if d == 1 else (d, s // d, C_B), dtype)


def _dilated_spec(tm, d):
    return _row_spec(tm, C_B) if d == 1 else pl.BlockSpec((d, tm // d, C_B), lambda i: (0, i, 0))


def _as_dilated(t, d):
    return t if d == 1 else t.reshape(d, t.shape[0] // d, t.shape[1])


def _dil_scratch(tm):
    return pltpu.VMEM((N_CHUNKS, tm, LANES), F32)


def _store_dilated(o_ref, scr, d, tm):
    for r in range(d):
        for j in range(N_CHUNKS):
            o_ref[r, :, pl.ds(j * LANES, LANES)] = scr[j, pl.ds(r, tm // d, stride=d), :].astype(o_ref.dtype)


def _load_dilated(in_ref, scr, d, tm):
    for r in range(d):
        for j in range(N_CHUNKS):
            scr[j, pl.ds(r, tm // d, stride=d), :] = in_ref[r, :, pl.ds(j * LANES, LANES)].astype(F32)


def rope_fwd(name, proj, tabs):
    s = proj.shape[0]
    tm = DIL_TILE
    scale = HEAD_DIM ** -0.5

    def body(q_ref, k_ref, v_ref, c_ref, sa_ref, sb_ref, *rest):
        outs, scr = rest[:9], rest[9:]
        c, sa, sb = c_ref[...], sa_ref[...], sb_ref[...]
        for j in range(N_CHUNKS):
            lanes = pl.ds(j * LANES, LANES)
            for which, (src, mul) in enumerate(((q_ref, scale), (k_ref, 1.0))):
                t = src[:, lanes]
                r = t * c + pltpu.roll(t, LANES - ROT_DIM // 2, 1) * sa + pltpu.roll(t, ROT_DIM // 2, 1) * sb
                scr[which][j] = r * mul
            scr[2][j] = v_ref[:, lanes]
            for which in range(3):
                outs[which][:, lanes] = scr[which][j].astype(BF16)
        for pi, d in enumerate(DILATIONS[1:]):
            for which in range(3):
                _store_dilated(outs[3 * (pi + 1) + which], scr[which], d, tm)

    tab = _row_spec(tm, LANES)
    res = pl.pallas_call(body, name=name, grid=(s // tm,),
                         in_specs=[_row_spec(tm, C_B, 1), _row_spec(tm, C_B, 2), _row_spec(tm, C_B, 3), tab, tab, tab],
                         out_specs=[_dilated_spec(tm, d) for d in DILATIONS for _ in range(3)],
                         out_shape=[_dilated_shape(s, d, BF16) for d in DILATIONS for _ in range(3)],
                         scratch_shapes=[_dil_scratch(tm)] * 3,
                         compiler_params=_params("parallel"))(proj, proj, proj, *tabs)
    return [tuple(t.reshape(s, C_B) for t in res[3 * pi:3 * pi + 3]) for pi in range(len(DILATIONS))]


def rope_bwd(name, dqs, dks, dvs, tabs):
    s = dqs[0].shape[0]
    tm = DIL_TILE
    scale = HEAD_DIM ** -0.5
    n_dil = len(DILATIONS)

    def body(*refs):
        ins = [refs[n_dil * which:n_dil * (which + 1)] for which in range(3)]
        c_ref, sa_ref, sb_ref, o_ref = refs[3 * n_dil:3 * n_dil + 4]
        scr = refs[3 * n_dil + 4:]
        for which in range(3):
            for pi, d in enumerate(DILATIONS[1:]):
                _load_dilated(ins[which][pi + 1], scr[which * (n_dil - 1) + pi], d, tm)
        c, sa, sb = c_ref[...], sa_ref[...], sb_ref[...]
        for j in range(N_CHUNKS):
            lanes = pl.ds(j * LANES, LANES)
            for which, mul in ((0, scale), (1, 1.0), (2, None)):
                d = ins[which][0][:, lanes].astype(F32)
                for pi in range(n_dil - 1):
                    d = d + scr[which * (n_dil - 1) + pi][j]
                if mul is not None:
                    d = (d * c + pltpu.roll(d * sa, ROT_DIM // 2, 1) + pltpu.roll(d * sb, LANES - ROT_DIM // 2, 1)) * mul
                o_ref[:, pl.ds(which * C_B + j * LANES, LANES)] = d.astype(BF16)

    tab = _row_spec(tm, LANES)
    args = [_as_dilated(t, d) for ts in (dqs, dks, dvs) for t, d in zip(ts, DILATIONS)]
    return pl.pallas_call(body, name=name, grid=(s // tm,),
                          in_specs=[_dilated_spec(tm, d) for _ in range(3) for d in DILATIONS] + [tab, tab, tab],
                          out_specs=_row_spec(tm, 3 * C_B), out_shape=jax.ShapeDtypeStruct((s, 3 * C_B), BF16),
                          scratch_shapes=[_dil_scratch(tm)] * (3 * (n_dil - 1)),
                          compiler_params=_params("parallel"))(*args, *tabs)


def _window_specs(tq, s, cols):
    r = tq // ATT_HALF
    last = s // ATT_HALF - 1
    return [pl.BlockSpec((ATT_HALF, cols), lambda i: (jnp.maximum(i * r - 1, 0), 0)),
            pl.BlockSpec((tq, cols), lambda i: (i, 0)),
            pl.BlockSpec((ATT_HALF, cols), lambda i: (jnp.minimum((i + 1) * r, last), 0))]


def _fill_window(win, prev_ref, main_ref, next_ref, tq):
    win[pl.ds(0, ATT_HALF), :] = prev_ref[...]
    win[pl.ds(ATT_HALF, tq), :] = main_ref[...]
    win[pl.ds(ATT_HALF + tq, ATT_HALF), :] = next_ref[...]


def _band_valid(tile, window, j0, seg, tile_is_rows):
    shape = (2 * tile, window) if tile_is_rows else (2 * window, tile)
    ri = lax.broadcasted_iota(jnp.int32, shape, 0)
    ci = lax.broadcasted_iota(jnp.int32, shape, 1)
    per_head = tile if tile_is_rows else window
    ri = jnp.where(ri >= per_head, ri - per_head, ri)
    ti, wi = (ri, ci) if tile_is_rows else (ci, ri)
    jw = j0 - ATT_HALF + wi
    return (jnp.abs(wi - ATT_HALF - ti) <= ATT_HALF) & (jw >= 0) & (jw < seg)


def _first_head():
    return lax.broadcasted_iota(jnp.int32, (1, LANES), 1) < HEAD_DIM


def _stack_heads(t, first):
    z = jnp.zeros_like(t)
    return jnp.concatenate([jnp.where(first, t, z), jnp.where(first, z, t)], axis=0)


def _unstack_heads(t2, first, rows):
    return jnp.where(first, t2[:rows], t2[rows:])


def _head_columns(ref, pair, rows):
    return jnp.concatenate([ref[rows, pl.ds(pair * LANES, 1)], ref[rows, pl.ds(pair * LANES + HEAD_DIM, 1)]], axis=0)


def _sub_tiles(tile):
    sub = min(tile, 2 * ATT_HALF)
    return sub, tile // sub, sub + 2 * ATT_HALF


def band_attn_fwd(name, q, k, v, seg):
    s, c = q.shape
    tq = _pick(seg, ATT_TILES)
    per_seg = seg // tq

    wrows = tq + 2 * ATT_HALF

    sub, n_sub, wsub = _sub_tiles(tq)

    def body(q_ref, kp, km, kn, vp, vm, vn, o_ref, lse_ref, kw, vw):
        j0 = (pl.program_id(0) % per_seg) * tq
        _fill_window(kw, kp, km, kn, tq)
        _fill_window(vw, vp, vm, vn, tq)
        first = _first_head()
        for h in range(n_sub):
            rows, wnd = pl.ds(h * sub, sub), pl.ds(h * sub, wsub)
            valid = _band_valid(sub, wsub, j0 + h * sub, seg, True)
            for pair in range(c // LANES):
                lanes = pl.ds(pair * LANES, LANES)
                sc = jnp.where(valid, _dot(_stack_heads(q_ref[rows, lanes], first), kw[wnd, lanes], "nt"), NEG)
                m = jnp.max(sc, axis=-1, keepdims=True)
                p = jnp.exp(sc - m)
                l = jnp.sum(p, axis=-1, keepdims=True)
                o = _dot(p.astype(BF16), vw[wnd, lanes], "nn") / l
                o_ref[rows, lanes] = _unstack_heads(o, first, sub).astype(BF16)
                lse_ref[rows, lanes] = _unstack_heads(m + jnp.log(l), first, sub)

    win = _window_specs(tq, s, c)
    tile = _row_spec(tq, c)
    return pl.pallas_call(body, name=name, grid=(s // tq,), in_specs=[tile] + win + win, out_specs=[tile, tile],
                          out_shape=[jax.ShapeDtypeStruct((s, c), BF16), jax.ShapeDtypeStruct((s, c), F32)],
                          scratch_shapes=[pltpu.VMEM((wrows, c), BF16)] * 2,
                          compiler_params=_params("parallel"))(q, k, k, k, v, v, v)


def band_attn_dq(name, q, k, v, do, lse, delta, seg):
    s, c = q.shape
    tq = _pick(seg, ATT_TILES)
    per_seg = seg // tq

    wrows = tq + 2 * ATT_HALF

    sub, n_sub, wsub = _sub_tiles(tq)

    def body(q_ref, kp, km, kn, vp, vm, vn, do_ref, lse_ref, dl_ref, dq_ref, kw, vw):
        j0 = (pl.program_id(0) % per_seg) * tq
        _fill_window(kw, kp, km, kn, tq)
        _fill_window(vw, vp, vm, vn, tq)
        first = _first_head()
        for h in range(n_sub):
            rows, wnd = pl.ds(h * sub, sub), pl.ds(h * sub, wsub)
            valid = _band_valid(sub, wsub, j0 + h * sub, seg, True)
            for pair in range(c // LANES):
                lanes = pl.ds(pair * LANES, LANES)
                sc = _dot(_stack_heads(q_ref[rows, lanes], first), kw[wnd, lanes], "nt")
                p = jnp.where(valid, jnp.exp(sc - _head_columns(lse_ref, pair, rows)), 0.0)
                dp = _dot(_stack_heads(do_ref[rows, lanes], first), vw[wnd, lanes], "nt")
                ds = p * (dp - _head_columns(dl_ref, pair, rows))
                dq = _dot(ds.astype(BF16), kw[wnd, lanes], "nn")
                dq_ref[rows, lanes] = _unstack_heads(dq, first, sub).astype(BF16)

    win = _window_specs(tq, s, c)
    tile = _row_spec(tq, c)
    return pl.pallas_call(body, name=name, grid=(s // tq,), in_specs=[tile] + win + win + [tile, tile, tile],
                          out_specs=tile, out_shape=jax.ShapeDtypeStruct((s, c), BF16),
                          scratch_shapes=[pltpu.VMEM((wrows, c), BF16)] * 2,
                          compiler_params=_params("parallel"))(q, k, k, k, v, v, v, do, lse, delta)


def band_attn_dkv(name, q, k, v, do, lse, delta, seg):
    s, c = q.shape
    tk = _pick(seg, ATT_TILES)
    per_seg = seg // tk

    wrows = tk + 2 * ATT_HALF
    sub, n_sub, wsub = _sub_tiles(tk)

    def body(k_ref, v_ref, qp, qm, qn, dop, dom, don, lp, lm, ln, dlp, dlm, dln, dk_ref, dv_ref, qw, dow, lsew, dlw):
        j0 = (pl.program_id(0) % per_seg) * tk
        _fill_window(qw, qp, qm, qn, tk)
        _fill_window(dow, dop, dom, don, tk)
        _fill_window(lsew, lp, lm, ln, tk)
        _fill_window(dlw, dlp, dlm, dln, tk)
        first = _first_head()
        top = lax.broadcasted_iota(jnp.int32, (2 * sub, 1), 0) < sub

        def head_rows(win, wnd, lanes):
            t = win[wnd, lanes].T
            return jnp.where(top, t[0:1, :], t[HEAD_DIM:HEAD_DIM + 1, :])

        for h in range(n_sub):
            rows, wnd = pl.ds(h * sub, sub), pl.ds(h * sub, wsub)
            valid = _band_valid(sub, wsub, j0 + h * sub, seg, True)
            for pair in range(c // LANES):
                lanes = pl.ds(pair * LANES, LANES)
                q2, do2 = qw[wnd, lanes], dow[wnd, lanes]
                sc = _dot(_stack_heads(k_ref[rows, lanes], first), q2, "nt")
                p = jnp.where(valid, jnp.exp(sc - head_rows(lsew, wnd, lanes)), 0.0)
                dv = _dot(p.astype(BF16), do2, "nn")
                dp = _dot(_stack_heads(v_ref[rows, lanes], first), do2, "nt")
                ds = p * (dp - head_rows(dlw, wnd, lanes))
                dk = _dot(ds.astype(BF16), q2, "nn")
                dv_ref[rows, lanes] = _unstack_heads(dv, first, sub).astype(BF16)
                dk_ref[rows, lanes] = _unstack_heads(dk, first, sub).astype(BF16)

    win = _window_specs(tk, s, c)
    tile = _row_spec(tk, c)
    osh = jax.ShapeDtypeStruct((s, c), BF16)
    return pl.pallas_call(body, name=name, grid=(s // tk,), in_specs=[tile, tile] + win * 4,
                          out_specs=[tile, tile], out_shape=[osh, osh],
                          scratch_shapes=[pltpu.VMEM((wrows, c), BF16)] * 2 + [pltpu.VMEM((wrows, c), F32)] * 2,
                          compiler_params=_params("parallel"))(k, v, q, q, q, do, do, do, lse, lse, lse,
                                                               delta, delta, delta)


def attn_combine_fwd(name, os_, lses, gmix):
    s, c = os_[0].shape
    tm = DIL_TILE
    n_dil = len(DILATIONS)

    def body(*refs):
        o_refs, l_refs, gx_ref = refs[:n_dil], refs[n_dil:2 * n_dil], refs[2 * n_dil]
        yb_ref, y_ref = refs[2 * n_dil + 1:2 * n_dil + 3]
        lse_refs = refs[2 * n_dil + 3:3 * n_dil + 3]
        scr = refs[3 * n_dil + 3:]
        scr_o, scr_l, scr_lse = scr[:n_dil - 1], scr[n_dil - 1:2 * (n_dil - 1)], scr[-1]
        for pi, d in enumerate(DILATIONS[1:]):
            _load_dilated(o_refs[pi + 1], scr_o[pi], d, tm)
            _load_dilated(l_refs[pi + 1], scr_l[pi], d, tm)
        sumsq = jnp.zeros((tm, 1), F32)
        for j in range(N_CHUNKS):
            lanes = pl.ds(j * LANES, LANES)
            ls = [l_refs[0][:, lanes]] + [t[j] for t in scr_l]
            vals = [o_refs[0][:, lanes].astype(F32)] + [t[j] for t in scr_o]
            m = functools.reduce(jnp.maximum, ls)
            es = [jnp.exp(l - m) for l in ls]
            den = functools.reduce(lambda a, b: a + b, es)
            yb = functools.reduce(lambda a, b: a + b, [e * v for e, v in zip(es, vals)]) / den
            yb_ref[:, lanes] = yb
            lse = m + jnp.log(den)
            lse_refs[0][:, lanes] = lse
            scr_lse[j] = lse
            sumsq = sumsq + jnp.sum(yb * yb, axis=-1, keepdims=True)
        r = lax.rsqrt(sumsq / c + EPS)
        y_ref[...] = (yb_ref[...] * r * gx_ref[...]).astype(BF16)
        for pi, d in enumerate(DILATIONS[1:]):
            _store_dilated(lse_refs[pi + 1], scr_lse, d, tm)

    row = _row_spec(tm, c)
    dil = [_dilated_spec(tm, d) for d in DILATIONS]
    res = pl.pallas_call(body, name=name, grid=(s // tm,), in_specs=dil + dil + [_const_spec((1, c))],
                         out_specs=[row, row] + dil,
                         out_shape=[jax.ShapeDtypeStruct((s, c), F32), jax.ShapeDtypeStruct((s, c), BF16)]
                         + [_dilated_shape(s, d, F32) for d in DILATIONS],
                         scratch_shapes=[_dil_scratch(tm)] * (2 * (n_dil - 1) + 1),
                         compiler_params=_params("parallel"))(
                             *[_as_dilated(t, d) for t, d in zip(os_, DILATIONS)],
                             *[_as_dilated(t, d) for t, d in zip(lses, DILATIONS)], gmix)
    return res[0], res[1], [t.reshape(s, c) for t in res[2:]]


def attn_combine_bwd(name, dy, yb, gmix):
    s, c = yb.shape
    tm = DIL_TILE
    half = c // 2
    n_dil = len(DILATIONS)

    def body(d1, d2, yb_ref, gx_ref, *rest):
        dyb_refs, dl_refs, dgx_ref = rest[:n_dil], rest[n_dil:2 * n_dil], rest[2 * n_dil]
        scr_dyb, scr_dl = rest[2 * n_dil + 1:]
        i = pl.program_id(0)
        d = jnp.concatenate([d1[...], d2[...]], axis=1)
        yb_ = yb_ref[...]
        dyb, dgx_rows = _rms_bwd(d, yb_, gx_ref[...])
        dyb_refs[0][...] = dyb.astype(BF16)

        @pl.when(i == 0)
        def _():
            dgx_ref[...] = jnp.zeros_like(dgx_ref)

        dgx_ref[...] += _sum8(dgx_rows)
        ri = lax.broadcasted_iota(jnp.int32, (LANES, LANES), 0) // HEAD_DIM
        ci = lax.broadcasted_iota(jnp.int32, (LANES, LANES), 1) // HEAD_DIM
        same_head = jnp.where(ri == ci, 1.0, 0.0).astype(BF16)
        prod = dyb * yb_
        for j in range(N_CHUNKS):
            pj = prod[:, j * LANES:(j + 1) * LANES]
            hi = pj.astype(BF16)
            lo = (pj - hi.astype(F32)).astype(BF16)
            delta = _dot(hi, same_head, "nn") + _dot(lo, same_head, "nn")
            dl_refs[0][:, pl.ds(j * LANES, LANES)] = delta
            scr_dl[j] = delta
            scr_dyb[j] = dyb[:, j * LANES:(j + 1) * LANES]
        for pi, dil in enumerate(DILATIONS[1:]):
            _store_dilated(dyb_refs[pi + 1], scr_dyb, dil, tm)
            _store_dilated(dl_refs[pi + 1], scr_dl, dil, tm)

    row = _row_spec(tm, c)
    dil_specs = [_dilated_spec(tm, d) for d in DILATIONS]
    res = pl.pallas_call(body, name=name, grid=(s // tm,),
                         in_specs=[_row_spec(tm, half, 1), _row_spec(tm, half, 2), row, _const_spec((1, c))],
                         out_specs=dil_specs + dil_specs + [_const_spec((8, c))],
                         out_shape=[_dilated_shape(s, d, BF16) for d in DILATIONS]
                         + [_dilated_shape(s, d, F32) for d in DILATIONS] + [jax.ShapeDtypeStruct((8, c), F32)],
                         scratch_shapes=[_dil_scratch(tm)] * 2,
                         compiler_params=_params("arbitrary"))(dy, dy, yb, gmix)
    return ([t.reshape(s, c) for t in res[:n_dil]], [t.reshape(s, c) for t in res[n_dil:2 * n_dil]], res[2 * n_dil])


def _ew_rows(rows, cols, n_bufs):
    budget = 24 * 1024 * 1024 // (8 * n_bufs * cols)
    return _pick(rows, tuple(t for t in (2048, 1024, 512, 256, 128, 64, 32, 16, 8) if t <= max(budget, 8)))


def elementwise(name, fn, ins, out_dtypes):
    shape = ins[0].shape
    cols = shape[-1]
    rows = math.prod(shape[:-1])
    tr = _ew_rows(rows, cols, len(ins) + len(out_dtypes))
    n_in = len(ins)

    def body(*refs):
        outs = fn(*[r[...] for r in refs[:n_in]])
        for o_ref, o in zip(refs[n_in:], outs):
            o_ref[...] = o.astype(o_ref.dtype)

    spec = _row_spec(tr, cols)
    res = pl.pallas_call(body, name=name, grid=(rows // tr,), in_specs=[spec] * n_in, out_specs=[spec] * len(out_dtypes),
                         out_shape=[jax.ShapeDtypeStruct((rows, cols), dt) for dt in out_dtypes],
                         compiler_params=_params("parallel"))(*[t.reshape(rows, cols) for t in ins])
    return [r.reshape(shape) for r in res]


def _adamw(g, w, m, v):
    m = ADAM_B1 * m + (1.0 - ADAM_B1) * g
    v = ADAM_B2 * v + (1.0 - ADAM_B2) * (g * g)
    m_hat = m / (1.0 - ADAM_B1 ** ADAM_STEP)
    v_hat = v / (1.0 - ADAM_B2 ** ADAM_STEP)
    delta = -ADAM_LR * (m_hat / (jnp.sqrt(v_hat) + ADAM_EPS) + ADAM_WD * w)
    return delta, m, v


def _my_group():
    return 2 * lax.axis_index("x") + lax.axis_index("y")


def _my_core():
    return lax.axis_index("c")


def cast_into_gathered(w, layer):
    _, r, c = w.shape
    tr = _ew_rows(r, c, 2)

    def body(w_ref, o_ref):
        o_ref[...] = w_ref[...].astype(BF16)

    return pl.pallas_call(body, name="cast_into_gathered", grid=(r // tr,),
                          in_specs=[pl.BlockSpec((None, tr, c), lambda i: (layer, i, 0))],
                          out_specs=pl.BlockSpec((None, None, tr, c), lambda i: (_my_group(), 0, i, 0)),
                          out_shape=jax.ShapeDtypeStruct((N_GROUPS, 1, r, c), BF16),
                          compiler_params=_params("parallel"))(w)


def sum_core_halves(name, grad, theirs):
    g, r, c = grad.shape
    h = r // 2
    tr = _ew_rows(h, c, 3)
    nh = h // tr

    def body(a_ref, b_ref, o_ref):
        o_ref[...] = (a_ref[...].astype(F32) + b_ref[...].astype(F32)).astype(BF16)

    blk = pl.BlockSpec((None, tr, c), lambda gi, i: (gi, i, 0))
    return pl.pallas_call(body, name=name, grid=(g, nh),
                          in_specs=[pl.BlockSpec((None, tr, c), lambda gi, i: (gi, _my_core() * nh + i, 0)), blk],
                          out_specs=blk, out_shape=jax.ShapeDtypeStruct((g, h, c), BF16),
                          compiler_params=_params("parallel", "parallel"))(grad, theirs)


def sum_chip_partials(name, parts, recv, layer, n_layers, into):
    _, h, c = parts.shape
    tr = _ew_rows(h, c, 5)
    nh = h // tr

    def body(o_ref, r0, r1, r2, *rest):
        rest[-1][...] = ((o_ref[...].astype(F32) + r0[...].astype(F32)) + r1[...].astype(F32)) + r2[...].astype(F32)

    in_specs = [pl.BlockSpec((None, tr, c), lambda i: (_my_group(), i, 0))]
    in_specs += [pl.BlockSpec((None, tr, c), lambda i, j=j: (j, i, 0)) for j in range(3)]
    args = [parts, recv, recv, recv]
    aliases = {}
    if into is not None:
        in_specs.append(ANY)
        args.append(into)
        aliases = {4: 0}
    return pl.pallas_call(body, name=name, grid=(nh,), in_specs=in_specs,
                          out_specs=pl.BlockSpec((None, tr, c), lambda i: (layer, _my_core() * nh + i, 0)),
                          input_output_aliases=aliases, out_shape=jax.ShapeDtypeStruct((n_layers, 2 * h, c), F32),
                          compiler_params=_params("parallel"))(*args)


def small_update(name, gall, w, m, v):
    _, rows, cols = gall.shape
    tr = _ew_rows(rows, cols, 16)

    def body(g_ref, w_ref, m_ref, v_ref, go, do, mo, vo):
        g = g_ref[0]
        for dev in range(1, 8):
            g = g + g_ref[dev]
        delta, mn, vn = _adamw(g, w_ref[...], m_ref[...], v_ref[...])
        go[...] = g
        do[...] = delta
        mo[...] = mn
        vo[...] = vn

    spec = _row_spec(tr, cols)
    sh = jax.ShapeDtypeStruct((rows, cols), F32)
    return pl.pallas_call(body, name=name, grid=(rows // tr,),
                          in_specs=[pl.BlockSpec((8, tr, cols), lambda i: (0, i, 0)), spec, spec, spec],
                          out_specs=[spec] * 4, out_shape=[sh] * 4, compiler_params=_params("parallel"))(gall, w, m, v)


def _place():
    x, y, c = lax.axis_index("x"), lax.axis_index("y"), lax.axis_index("c")
    chips = [(1 - x, y), (x, 1 - y), (1 - x, 1 - y)]
    return x, y, c, chips


def _remote(src, dst, send_sems, recv_sems, k, to):
    return pltpu.make_async_remote_copy(src_ref=src, dst_ref=dst, send_sem=send_sems.at[k], recv_sem=recv_sems.at[k],
                                        device_id=to, device_id_type=MESH)


ANY = pl.BlockSpec(memory_space=pl.ANY)


def allgather_small(name, block):
    m_per, n = block.shape

    def body(x_ref, out_ref, send_sems, recv_sems, local_sem):
        x, y, c, chips = _place()
        me, sibling = (x, y, c), (x, y, 1 - c)

        def rows(px, py, pc):
            return out_ref.at[pl.ds((4 * px + 2 * py + pc) * m_per, m_per), :]

        def copy(k, blk, to, src=None):
            return _remote(rows(*blk) if src is None else src, rows(*blk), send_sems, recv_sems, k, to)

        mine = pltpu.make_async_copy(x_ref, rows(*me), local_sem)
        mine.start()
        first = [copy(0, me, sibling, src=x_ref)]
        first += [copy(1 + j, me, (*chip, c), src=x_ref) for j, chip in enumerate(chips)]
        for cp in first:
            cp.start()
        passed = [copy(4 + j, (*chip, c), sibling) for j, chip in enumerate(chips)]
        for j, chip in enumerate(chips):
            copy(1 + j, (*chip, c), me).wait_recv()
            passed[j].start()
        copy(0, sibling, me).wait_recv()
        for j, chip in enumerate(chips):
            copy(4 + j, (*chip, 1 - c), me).wait_recv()
        for cp in first + passed:
            cp.wait_send()
        mine.wait()

    return pl.pallas_call(body, name=name, out_shape=jax.ShapeDtypeStruct((8 * m_per, n), block.dtype),
                          in_specs=[pl.BlockSpec(memory_space=pltpu.VMEM)], out_specs=pl.BlockSpec(memory_space=pltpu.VMEM),
                          scratch_shapes=[pltpu.SemaphoreType.DMA((7,)), pltpu.SemaphoreType.DMA((7,)),
                                          pltpu.SemaphoreType.DMA],
                          compiler_params=pltpu.CompilerParams(vmem_limit_bytes=VMEM_LIMIT))(block)


def gather_weights(bufs):
    n = len(bufs)

    def body(*refs):
        start, finish = _gather_plan(refs[n:2 * n], *refs[2 * n:])
        start()
        finish()

    return pl.pallas_call(body, name="gather_weights",
                          out_shape=[jax.ShapeDtypeStruct(t.shape, t.dtype) for t in bufs],
                          in_specs=[ANY] * n, out_specs=[ANY] * n, input_output_aliases={a: a for a in range(n)},
                          scratch_shapes=_gather_sems(n))(*bufs)


def _gather_sems(n):
    return [pltpu.SemaphoreType.DMA((6 * n,)), pltpu.SemaphoreType.DMA((6 * n,))]


def _gather_plan(outs, send_sems, recv_sems):
    n = len(outs)
    x, y, c, chips = _place()
    g0 = 2 * x + y
    sibling = (x, y, 1 - c)
    groups = [2 * cx + cy for cx, cy in chips]

    def part(a, g, cc):
        h = outs[a].shape[2] // 2
        return outs[a].at[g, :, pl.ds(cc * h, h), :]

    def ici(a, j):
        return _remote(part(a, g0, c), part(a, g0, c), send_sems, recv_sems, 3 * a + j, (*chips[j], c))

    def ici_arrival(a, j):
        return _remote(part(a, groups[j], c), part(a, groups[j], c), send_sems, recv_sems, 3 * a + j, (*chips[j], c))

    def passed(a, j, cc):
        return _remote(part(a, groups[j], cc), part(a, groups[j], cc), send_sems, recv_sems, 3 * n + 3 * a + j, sibling)

    def start():
        for a in range(n):
            for j in range(3):
                ici(a, j).start()

    def finish():
        for a in range(n):
            for j in range(3):
                ici_arrival(a, j).wait_recv()
                passed(a, j, c).start()
        for a in range(n):
            for j in range(3):
                passed(a, j, 1 - c).wait_recv()
        for a in range(n):
            for j in range(3):
                ici(a, j).wait_send()
                passed(a, j, c).wait_send()

    return start, finish


def exchange_core_halves(name, grads):
    n = len(grads)
    halves = [t.shape[1] // 2 for t in grads]

    def body(*refs):
        ins, theirs = refs[:n], refs[n:2 * n]
        send_sems, recv_sems = refs[2 * n:]
        x, y, c, _ = _place()
        sibling = (x, y, 1 - c)

        def give(a, g):
            return _remote(ins[a].at[g, pl.ds((1 - c) * halves[a], halves[a]), :], theirs[a].at[g], send_sems,
                           recv_sems, N_GROUPS * a + g, sibling)

        for a in range(n):
            for g in range(N_GROUPS):
                give(a, g).start()
        for a in range(n):
            for g in range(N_GROUPS):
                give(a, g).wait_recv()
        for a in range(n):
            for g in range(N_GROUPS):
                give(a, g).wait_send()

    shapes = [jax.ShapeDtypeStruct((t.shape[0], h, t.shape[2]), t.dtype) for t, h in zip(grads, halves)]
    k = N_GROUPS * n
    return pl.pallas_call(body, name=name, out_shape=shapes, in_specs=[ANY] * n, out_specs=[ANY] * n,
                          scratch_shapes=[pltpu.SemaphoreType.DMA((k,)), pltpu.SemaphoreType.DMA((k,))])(*grads)


def exchange_chip_partials(name, parts):
    n = len(parts)

    def body(*refs):
        start, finish = _chip_exchange_plan(refs[:n], refs[n:2 * n], *refs[2 * n:])
        start()
        finish()

    return pl.pallas_call(body, name=name, out_shape=_chip_exchange_shapes(parts), in_specs=[ANY] * n,
                          out_specs=[ANY] * n, scratch_shapes=_chip_exchange_sems(n))(*parts)


def _chip_exchange_shapes(parts):
    return [jax.ShapeDtypeStruct((3,) + t.shape[1:], t.dtype) for t in parts]


def _chip_exchange_sems(n):
    return [pltpu.SemaphoreType.DMA((3 * n,)), pltpu.SemaphoreType.DMA((3 * n,))]


def _chip_exchange_plan(ins, recv, send_sems, recv_sems):
    n = len(ins)
    x, y, c, chips = _place()

    def give(a, j):
        return _remote(ins[a].at[2 * chips[j][0] + chips[j][1]], recv[a].at[j], send_sems, recv_sems, 3 * a + j,
                       (*chips[j], c))

    def start():
        for a in range(n):
            for j in range(3):
                give(a, j).start()

    def finish():
        for a in range(n):
            for j in range(3):
                give(a, j).wait_recv()
        for a in range(n):
            for j in range(3):
                give(a, j).wait_send()

    return start, finish


def share_reduced_halves(bufs):
    n = len(bufs)
    n_layers = bufs[0].shape[0]
    halves = [t.shape[1] // 2 for t in bufs]

    def body(*refs):
        outs = refs[n:2 * n]
        send_sems, recv_sems = refs[2 * n:]
        x, y, c, _ = _place()
        sibling = (x, y, 1 - c)

        def give(l, a, cc):
            part = outs[a].at[l, pl.ds(cc * halves[a], halves[a]), :]
            return _remote(part, part, send_sems, recv_sems, l * n + a, sibling)

        for l in range(n_layers):
            for a in range(n):
                give(l, a, c).start()
        for l in range(n_layers):
            for a in range(n):
                give(l, a, 1 - c).wait_recv()
        for l in range(n_layers):
            for a in range(n):
                give(l, a, c).wait_send()

    k = n_layers * n
    return pl.pallas_call(body, name="share_reduced_halves", out_shape=[jax.ShapeDtypeStruct(t.shape, t.dtype) for t in bufs],
                          in_specs=[ANY] * n, out_specs=[ANY] * n, input_output_aliases={a: a for a in range(n)},
                          scratch_shapes=[pltpu.SemaphoreType.DMA((k,)), pltpu.SemaphoreType.DMA((k,))])(*bufs)


def reduce_in_chip(grads):
    theirs = exchange_core_halves("exchange_core_halves", grads)
    return [sum_core_halves("sum_core_halves", g, t) for g, t in zip(grads, theirs)]


def reduce_over_chips(parts, recv, layer, n_layers, into):
    return [sum_chip_partials("sum_chip_partials", p, r, layer, n_layers, None if into is None else into[a])
            for a, (p, r) in enumerate(zip(parts, recv))]


def _pack(parts):
    flat = jnp.concatenate([p.reshape(-1) for p in parts])
    pad = (-flat.shape[0]) % (8 * LANES)
    if pad:
        flat = jnp.concatenate([flat, jnp.zeros((pad,), flat.dtype)])
    return flat.reshape(-1, LANES)


def _unpack(buf, shapes):
    flat = buf.reshape(-1)
    out, at = [], 0
    for sh in shapes:
        size = math.prod(sh)
        out.append(flat[at:at + size].reshape(sh))
        at += size
    return out


def kernel(x, w_in, conv_w, conv_b, conv_ln_g, conv_ln_b, pool_w, pool_scale, g_mix, w_out, g_pre_mix, g_post_mix, g_pre_ffn, g_post_ffn, w_gate, w_up, w_down, loss_target, m_w_in, m_conv_w, m_conv_b, m_conv_ln_g, m_conv_ln_b, m_pool_w, m_pool_scale, m_g_mix, m_w_out, m_g_pre_mix, m_g_post_mix, m_g_pre_ffn, m_g_post_ffn, m_w_gate, m_w_up, m_w_down, v_w_in, v_conv_w, v_conv_b, v_conv_ln_g, v_conv_ln_b, v_pool_w, v_pool_scale, v_g_mix, v_w_out, v_g_pre_mix, v_g_post_mix, v_g_pre_ffn, v_g_post_ffn, v_w_gate, v_w_up, v_w_down):
    xs, target = x[0], loss_target[0]
    s, d = xs.shape
    n_layers = w_in.shape[0]
    my_group = 2 * lax.axis_index("x") + lax.axis_index("y")

    big = [w_in, w_out, w_gate, w_up, w_down]
    gathered = [[cast_into_gathered(w, l) for w in big] for l in range(n_layers)]
    gathered[0][:1] = gather_weights(gathered[0][:1])
    cw_pad = jnp.pad(conv_w, ((0, 0), (0, 32 - CONV_WIDTH), (0, 0)))
    cw_all = allgather_small("allgather_conv_w", cw_pad.reshape(n_layers * 32, LANES))
    cw_full = cw_all.reshape(N_GROUPS, 2, n_layers, 32, LANES)[:, 0].transpose(1, 2, 0, 3).reshape(n_layers, 32, C_A)
    pw_bf16 = pool_w.astype(BF16)
    tabs = rope_tables(s)
    gmix_a, gmix_b, gmix_c = g_mix[:, :C_A], g_mix[:, C_A:C_A + C_B], g_mix[:, C_A + C_B:]
    row = lambda t: t.reshape(1, -1)

    saved = []
    (h,) = norm_fwd("norm_first", xs, None, None, g_pre_mix[0])
    xin = xs
    for l in range(n_layers):
        if l == 0:
            proj, gathered[0][1:] = mm_act_wcols("proj_gather", h, gathered[0][0], 0, F32, gathered[0][1:])
        else:
            proj = mm_act_wcols("proj", h, gathered[l][0], 0, F32)
        win_g, wout_g, wgate_g, wup_g, wdown_g = gathered[l]
        ya, conv_out = conv_fwd("conv_fwd", proj, cw_full[l], row(conv_b[l]), row(conv_ln_g[l]), row(conv_ln_b[l]),
                                row(gmix_a[l]))
        qkv_d = rope_fwd("rope_fwd", proj, tabs)
        os_, lses = [], []
        for dil, (qd, kd, vd) in zip(DILATIONS, qkv_d):
            o, lse = band_attn_fwd(f"band_attn_fwd_d{dil}", qd, kd, vd, s // dil)
            os_.append(o)
            lses.append(lse)
        yb, ybn, lse_joint = attn_combine_fwd("attn_combine_fwd", os_, lses, row(gmix_b[l]))
        yc = pool_fwd("pool_fwd", proj, pw_bf16[l], row(pool_scale[l]), row(gmix_c[l]))
        y = jnp.concatenate([ya, ybn, yc], axis=1)
        z = mm_act_wrows("mix_out", y, wout_g, 0, F32)
        x1, h2 = norm_fwd("norm_mid", xin, z, g_post_mix[l], g_pre_ffn[l])
        if l + 1 < n_layers:
            (gt, up, act), gathered[l + 1] = ffn_up("ffn_up_gather", h2, wgate_g, wup_g, 0, gathered[l + 1])
        else:
            (gt, up, act), _ = ffn_up("ffn_up", h2, wgate_g, wup_g, 0)
        f = mm_act_wrows("ffn_down", act, wdown_g, 0, F32)
        if l + 1 < n_layers:
            x2, h_next = norm_fwd("norm_next", x1, f, g_post_ffn[l], g_pre_mix[l + 1])
        else:
            (x2,), h_next = norm_fwd("norm_last", x1, f, g_post_ffn[l], None), None
        saved.append(dict(xin=xin, h=h, proj=proj, conv=conv_out, qkv_d=qkv_d, yb=yb, lse=lse_joint, y=y, z=z, x1=x1, h2=h2,
                          gt=gt, up=up, act=act, f=f))
        xin, h = x2, h_next

    dx, sq = loss_head(xin, target)
    loss = lax.psum(0.5 * jnp.sum(sq) / d, ("x", "y", "c"))

    small = {k: [None] * n_layers for k in ("cw", "cb", "lg", "lb", "pw", "ps", "gmix", "gpre_mix", "gpost_mix",
                                            "gpre_ffn", "gpost_ffn")}
    reduced = None
    df, dg = norm_bwd("norm_bwd_top", dx, None, None, None, saved[-1]["f"], g_post_ffn[n_layers - 1])
    small["gpost_ffn"][n_layers - 1] = dg
    pending = None
    for l in reversed(range(n_layers)):
        sv = saved[l]
        win_g, wout_g, wgate_g, wup_g, wdown_g = gathered[l]
        dgt, dup = ffn_down_bwd("ffn_down_bwd", df, wdown_g, 0, sv["gt"], sv["up"])
        d_wdown = mm_plain("wgrad_rows_ffn", "tn", sv["act"], df, BF16)
        if pending is None:
            dh2, _ = ffn_in_bwd("ffn_in_bwd", dgt, dup, wgate_g, wup_g, 0)
        else:
            dh2, recv = ffn_in_bwd("ffn_in_bwd_exchange", dgt, dup, wgate_g, wup_g, 0, pending[1])
            reduced = reduce_over_chips(pending[1], recv, pending[0], n_layers, reduced)
        d_wgate = mm_wgrad_cols("wgrad_cols_ffn", sv["h2"], dgt, wgate_g.shape[3])
        d_wup = mm_wgrad_cols("wgrad_cols_ffn", sv["h2"], dup, wup_g.shape[3])
        dx1, small["gpre_ffn"][l], dz, small["gpost_mix"][l] = norm_bwd(
            "norm_bwd_mid", dx, dh2, sv["x1"], g_pre_ffn[l], sv["z"], g_post_mix[l])
        dy = mm_act_wrows_t("mix_out_bwd", dz, wout_g, 0, F32)
        d_wout = mm_plain("wgrad_rows_mix", "tn", sv["y"], dz, BF16)
        proj = sv["proj"]
        if l == 0:
            ffn_parts = reduce_in_chip([d_wgate, d_wup, d_wdown.reshape(N_GROUPS, -1, d_wdown.shape[1])])
        (dag, dcw, small["cb"][l], small["lg"][l], small["lb"][l], dgx_a), ffn_recv = conv_bwd(
            "conv_bwd_exchange" if l == 0 else "conv_bwd", proj, sv["conv"], dy, cw_full[l], row(conv_ln_g[l]),
            row(conv_ln_b[l]), row(gmix_a[l]), ffn_parts if l == 0 else ())
        small["cw"][l] = dcw.reshape(32, 8, C_A).sum(axis=1)
        dybs, deltas, dgx_b = attn_combine_bwd("attn_combine_bwd", dy, sv["yb"], row(gmix_b[l]))
        dqs, dks, dvs = [], [], []
        for dil, (qd, kd, vd), do_d, lse_d, delta_d in zip(DILATIONS, sv["qkv_d"], dybs, sv["lse"], deltas):
            dqs.append(band_attn_dq(f"band_attn_dq_d{dil}", qd, kd, vd, do_d, lse_d, delta_d, s // dil))
            dk, dv = band_attn_dkv(f"band_attn_dkv_d{dil}", qd, kd, vd, do_d, lse_d, delta_d, s // dil)
            dks.append(dk)
            dvs.append(dv)
        dqkv = rope_bwd("rope_bwd", dqs, dks, dvs, tabs)
        du, small["pw"][l], small["ps"][l], dgx_c = pool_bwd("pool_bwd", proj, dy, pw_bf16[l], row(pool_scale[l]),
                                                              row(gmix_c[l]))
        small["gmix"][l] = jnp.concatenate([dgx_a, dgx_b, dgx_c], axis=1)
        dproj = jnp.concatenate([dag, dqkv, du], axis=1)
        dh1 = mm_act_wcols_t("proj_bwd", dproj, win_g, 0, F32)
        d_win = mm_wgrad_cols("wgrad_cols_proj", sv["h"], dproj, win_g.shape[3])
        if l > 0:
            dx, small["gpre_mix"][l], df, small["gpost_ffn"][l - 1] = norm_bwd(
                "norm_bwd_next", dx1, dh1, sv["xin"], g_pre_mix[l], saved[l - 1]["f"], g_post_ffn[l - 1])
        else:
            dx, small["gpre_mix"][l] = norm_bwd("norm_bwd_first", dx1, dh1, sv["xin"], g_pre_mix[l], None, None)
        d_wout = d_wout.reshape(N_GROUPS, -1, d_wout.shape[1])
        d_wdown = d_wdown.reshape(N_GROUPS, -1, d_wdown.shape[1])
        if l > 0:
            pending = (l, reduce_in_chip([d_win, d_wout, d_wgate, d_wup, d_wdown]))
    rest_parts = reduce_in_chip([d_win, d_wout])
    rest_recv = exchange_chip_partials("exchange_chip_partials", rest_parts)
    into = [None] * 5 if reduced is None else reduced
    reduced = (reduce_over_chips(rest_parts, rest_recv, 0, n_layers, into[:2])
               + reduce_over_chips(ffn_parts, ffn_recv, 0, n_layers, into[2:]))

    big_grads = share_reduced_halves(reduced)
    big_m = [m_w_in, m_w_out, m_w_gate, m_w_up, m_w_down]
    big_v = [v_w_in, v_w_out, v_w_gate, v_w_up, v_w_down]
    big_upd = [elementwise("adamw", _adamw, [g, w, m, v], [F32, F32, F32])
               for g, w, m, v in zip(big_grads, big, big_m, big_v)]

    def stack8(k):
        return jnp.stack([t.sum(axis=0) if t.shape[0] == 8 and t.ndim == 2 else t for t in small[k]])

    order = ("cw", "cb", "lg", "lb", "pw", "ps", "gmix", "gpre_mix", "gpost_mix", "gpre_ffn", "gpost_ffn")
    partial = [jnp.stack(small["cw"])] + [stack8(k) for k in order[1:]]
    rep_w = [conv_b, conv_ln_g, conv_ln_b, pool_w, pool_scale, g_mix, g_pre_mix, g_post_mix, g_pre_ffn, g_post_ffn]
    rep_m = [m_conv_b, m_conv_ln_g, m_conv_ln_b, m_pool_w, m_pool_scale, m_g_mix, m_g_pre_mix, m_g_post_mix,
             m_g_pre_ffn, m_g_post_ffn]
    rep_v = [v_conv_b, v_conv_ln_g, v_conv_ln_b, v_pool_w, v_pool_scale, v_g_mix, v_g_pre_mix, v_g_post_mix,
             v_g_pre_ffn, v_g_post_ffn]
    packed = _pack(partial)
    rows = packed.shape[0]
    gall = allgather_small("allgather_small_grads", packed).reshape(8, rows, LANES)
    cw_hole = jnp.zeros((n_layers, 32, C_A), F32)
    rep_out = small_update("small_update", gall, _pack([cw_hole] + rep_w), _pack([cw_hole] + rep_m),
                           _pack([cw_hole] + rep_v))
    shapes = [cw_hole.shape] + [t.shape for t in rep_w]
    rep_g, rep_d, rep_mn, rep_vn = ([t for t in _unpack(buf, shapes)[1:]] for buf in rep_out)
    cw_rows = n_layers * 32
    gall_cw = gall[:, :cw_rows * C_A // LANES].reshape(8, cw_rows, C_A // LANES, LANES)
    gall_cw = lax.dynamic_index_in_dim(gall_cw, my_group, axis=2, keepdims=False)
    pad_cw = lambda t: jnp.pad(t, ((0, 0), (0, 32 - CONV_WIDTH), (0, 0))).reshape(cw_rows, LANES)
    cw_out = small_update("conv_w_update", gall_cw, pad_cw(conv_w), pad_cw(m_conv_w), pad_cw(v_conv_w))
    cw_g, cw_d, cw_mn, cw_vn = (t.reshape(n_layers, 32, LANES)[:, :CONV_WIDTH] for t in cw_out)

    def assemble(bigs, cw, reps):
        return [bigs[0], cw] + list(reps[:6]) + [bigs[1]] + list(reps[6:]) + list(bigs[2:])

    grads = assemble(big_grads, cw_g, rep_g)
    deltas = assemble([u[0] for u in big_upd], cw_d, rep_d)
    new_m = assemble([u[1] for u in big_upd], cw_mn, rep_mn)
    new_v = assemble([u[2] for u in big_upd], cw_vn, rep_vn)
    return (loss, dx[None], *grads, *deltas, *new_m, *new_v)
```
